```python
import math
import jax
import jax.numpy as jnp
from jax import lax
import numpy as np

D_MODEL = 1024
BATCH = 8
SEQ = 2048
DEPTH = 4
DEC_BATCH = 8
DEC_SEQ = 8192
PAST_LEN = 128

N_MIXERS = 4
N_GROUPS = DEPTH // N_MIXERS
N_MOD = 6
NORM_EPS = 1e-6

HY_WIDTH = D_MODEL
HY_BANDS = 16
HY_EMB = 1 + 2 * HY_BANDS
HY_FILTER_ORDER = 64
HY_TARGET = 1e-2
HY_DECAY_SHORT = 0.3
HY_DECAY_LONG = 1.5

RET_HEADS = 4
RET_DK = D_MODEL // RET_HEADS
RET_DV = 2 * RET_DK
RET_CHUNK = 128
RET_SPLITS = [RET_HEADS * RET_DK, 2 * RET_HEADS * RET_DK, 2 * RET_HEADS * RET_DK + RET_HEADS * RET_DV]
RET_IN = 2 * RET_HEADS * RET_DK + 2 * RET_HEADS * RET_DV

SWA_HQ = 16
SWA_HKV = 4
SWA_GROUP = SWA_HQ // SWA_HKV
SWA_DH = 64
WINDOW = 128
ATTN_BLOCK = 128
ROPE_THETA = 10000.0
NEG_INF = -1e30
SWA_SPLITS = [SWA_HQ * SWA_DH, (SWA_HQ + SWA_HKV) * SWA_DH]
SWA_IN = (SWA_HQ + 2 * SWA_HKV) * SWA_DH

HG_HEADS = 8
HG_DK = 128
HG_DV = D_MODEL // HG_HEADS
HG_CHUNK = 64
HG_SPLITS = [HG_HEADS * HG_DK, HG_HEADS * (HG_DK + HG_DV), HG_HEADS * (2 * HG_DK + HG_DV), HG_HEADS * (3 * HG_DK + HG_DV)]
HG_IN = HG_HEADS * (3 * HG_DK + 2 * HG_DV)

D_FF = 2816

kernel_name = 'bidir_hybrid_encoder_two_groups'


def rms_norm(x, gain=None):
    xf = x.astype(jnp.float32)
    y = xf * lax.rsqrt(jnp.mean(xf * xf, axis=-1, keepdims=True) + NORM_EPS)
    if gain is not None:
        y = y * gain.astype(jnp.float32)
    return y.astype(x.dtype)


def dwconv3(x, w, b):
    xp = jnp.pad(x, ((0, 0), (1, 1), (0, 0)))
    return xp[:, :-2] * w[0] + xp[:, 1:-1] * w[1] + xp[:, 2:] * w[2] + b


def rope(x):
    L, dh = x.shape[1], x.shape[-1]
    inv = ROPE_THETA ** (-jnp.arange(0, dh, 2, dtype=jnp.float32) / dh)
    ang = jnp.arange(L, dtype=jnp.float32)[:, None] * inv[None, :]
    cos = jnp.cos(ang)[None, :, None, :]
    sin = jnp.sin(ang)[None, :, None, :]
    xf = x.astype(jnp.float32)
    x1, x2 = xf[..., : dh // 2], xf[..., dh // 2:]
    return jnp.concatenate([x1 * cos - x2 * sin, x1 * sin + x2 * cos], axis=-1).astype(x.dtype)


def chunk_recurrence(q, k, v, log_f, chunk):
    B, H, T, DK = q.shape
    DV = v.shape[-1]
    n = T // chunk
    q = q.astype(jnp.float32).reshape(B, H, n, chunk, DK)
    k = k.astype(jnp.float32).reshape(B, H, n, chunk, DK)
    v = v.astype(jnp.float32).reshape(B, H, n, chunk, DV)
    g = jnp.broadcast_to(log_f.astype(jnp.float32), (B, H, T, DK)).reshape(B, H, n, chunk, DK)
    b = jnp.cumsum(g, axis=3)
    b_last = b[:, :, :, -1:]
    q_dec = q * jnp.exp(b)
    k_inv = k * jnp.exp(-b)
    k_end = k * jnp.exp(b_last - b)
    causal = jnp.tril(jnp.ones((chunk, chunk), dtype=bool))
    scores = jnp.where(causal, jnp.einsum('bhntd,bhnsd->bhnts', q_dec, k_inv), 0.0)
    o_intra = jnp.einsum('bhnts,bhnse->bhnte', scores, v)

    def step(S, xs):
        qd, ke, vc, dl = xs
        o = jnp.einsum('bhtd,bhde->bhte', qd, S)
        S = dl[..., None] * S + jnp.einsum('bhtd,bhte->bhde', ke, vc)
        return S, o

    xs = (jnp.moveaxis(q_dec, 2, 0), jnp.moveaxis(k_end, 2, 0), jnp.moveaxis(v, 2, 0),
          jnp.moveaxis(jnp.exp(b_last[:, :, :, 0]), 2, 0))
    _, o_inter = lax.scan(step, jnp.zeros((B, H, DK, DV), jnp.float32), xs)
    return (o_intra + jnp.moveaxis(o_inter, 0, 2)).reshape(B, H, T, DV)


def bidir_recurrence(q, k_fwd, k_bwd, v, g_fwd, g_bwd, chunk):
    def flip(a):
        return jnp.flip(a, axis=2)
    fwd = chunk_recurrence(q, k_fwd, v, g_fwd, chunk)
    bwd = chunk_recurrence(flip(q), flip(k_bwd), flip(v), flip(g_bwd), chunk)
    return fwd + flip(bwd)


def hyena_filters(L, w1, b1, w2, b2, w3, freq, decay):
    t = jnp.linspace(0.0, 1.0, L, dtype=jnp.float32)[:, None]
    ang = (2.0 * math.pi / L) * jnp.arange(L, dtype=jnp.float32)[:, None]
    bands = jnp.linspace(1e-4, HY_BANDS - 1, HY_BANDS, dtype=jnp.float32)[None, :]
    z = jnp.concatenate([t, jnp.cos(bands * ang), -jnp.sin(bands * ang)], axis=-1)
    fr = freq.astype(jnp.float32)
    h = jnp.sin(fr * (z @ w1.astype(jnp.float32) + b1.astype(jnp.float32)))
    h = jnp.sin(fr * (h @ w2.astype(jnp.float32) + b2.astype(jnp.float32)))
    h = (h @ w3.astype(jnp.float32)).reshape(L, 2, HY_WIDTH)
    h = h * jnp.exp(-t[:, :, None] * jnp.abs(decay.astype(jnp.float32))[None])
    taps = jnp.concatenate([h[:, 0], jnp.zeros((1, HY_WIDTH), jnp.float32), h[:0:-1, 1]], axis=0)
    return taps / jnp.sum(jnp.abs(taps), axis=0, keepdims=True)


def long_conv(u, taps):
    L = u.shape[1]
    uf = jnp.fft.rfft(u.astype(jnp.float32), n=2 * L, axis=1)
    tf = jnp.fft.rfft(taps, axis=0)
    return jnp.fft.irfft(uf * tf[None], n=2 * L, axis=1)[:, :L]


def hyena_mixer(h, w_in, conv_w, conv_b, w1, b1, w2, b2, w3, freq, decay, skip, w_out):
    z = dwconv3(h @ w_in, conv_w, conv_b)
    x0, x1, v = jnp.split(z, 3, axis=-1)
    u = x1 * v
    taps = hyena_filters(h.shape[1], w1, b1, w2, b2, w3, freq, decay)
    y = long_conv(u, taps) + u.astype(jnp.float32) * skip.astype(jnp.float32)
    return (y.astype(h.dtype) * x0) @ w_out


def retention_mixer(h, w_in, decay_raw, w_out):
    B, L, _ = h.shape
    q, k, v, g = jnp.split(h @ w_in, RET_SPLITS, axis=-1)
    q = rope(q.reshape(B, L, RET_HEADS, RET_DK)).transpose(0, 2, 1, 3)
    k = (rope(k.reshape(B, L, RET_HEADS, RET_DK)) * (RET_DK ** -0.5)).transpose(0, 2, 1, 3)
    v = v.reshape(B, L, RET_HEADS, RET_DV).transpose(0, 2, 1, 3)
    log_gamma = -jnp.exp(decay_raw.astype(jnp.float32))
    o = bidir_recurrence(q, k, k, v, log_gamma[0].reshape(1, RET_HEADS, 1, 1),
                         log_gamma[1].reshape(1, RET_HEADS, 1, 1), RET_CHUNK)
    o = rms_norm(o.transpose(0, 2, 1, 3)).reshape(B, L, RET_HEADS * RET_DV).astype(h.dtype)
    return (o * jax.nn.silu(g)) @ w_out


def banded_window_attention(q, k, v, sink):
    B, L = q.shape[0], q.shape[1]
    nb = L // ATTN_BLOCK
    span = ATTN_BLOCK + 2 * WINDOW
    scale = SWA_DH ** -0.5
    kp = jnp.pad(k, ((0, 0), (WINDOW, WINDOW), (0, 0), (0, 0)))
    vp = jnp.pad(v, ((0, 0), (WINDOW, WINDOW), (0, 0), (0, 0)))
    qb = jnp.moveaxis(q.reshape(B, nb, ATTN_BLOCK, SWA_HKV, SWA_GROUP, SWA_DH), 1, 0)
    starts = jnp.arange(nb, dtype=jnp.int32) * ATTN_BLOCK
    rel = jnp.arange(ATTN_BLOCK)[:, None] - (jnp.arange(span)[None, :] - WINDOW)
    band = jnp.abs(rel) <= WINDOW
    sink_l = sink.astype(jnp.float32).reshape(1, SWA_HKV, SWA_GROUP, 1, 1)

    def block(args):
        qj, start = args
        kj = lax.dynamic_slice_in_dim(kp, start, span, axis=1)
        vj = lax.dynamic_slice_in_dim(vp, start, span, axis=1)
        kpos = start - WINDOW + jnp.arange(span)
        valid = band & ((kpos >= 0) & (kpos < L))[None, :]
        s = jnp.einsum('bqhgd,bkhd->bhgqk', qj.astype(jnp.float32), kj.astype(jnp.float32)) * scale
        s = jnp.where(valid, s, NEG_INF)
        sk = jnp.broadcast_to(sink_l, s.shape[:-1] + (1,))
        p = jax.nn.softmax(jnp.concatenate([s, sk], axis=-1), axis=-1)[..., :-1]
        return jnp.einsum('bhgqk,bkhd->bqhgd', p, vj.astype(jnp.float32))

    o = lax.map(block, (qb, starts))
    return jnp.moveaxis(o, 0, 1).reshape(B, L, SWA_HQ * SWA_DH)


def swa_mixer(h, w_in, q_gain, k_gain, sink, w_out):
    B, L, _ = h.shape
    q, k, v = jnp.split(h @ w_in, SWA_SPLITS, axis=-1)
    q = rope(rms_norm(q.reshape(B, L, SWA_HQ, SWA_DH), q_gain))
    k = rope(rms_norm(k.reshape(B, L, SWA_HKV, SWA_DH), k_gain))
    v = v.reshape(B, L, SWA_HKV, SWA_DH)
    o = banded_window_attention(q, k, v, sink).astype(h.dtype)
    return o @ w_out


def hgrn_mixer(h, w_in, lb_table, layer, gain, w_out):
    B, L, _ = h.shape
    q, i, ff, fb, gate = jnp.split(h @ w_in, HG_SPLITS, axis=-1)

    def heads(a):
        return a.reshape(B, L, HG_HEADS, -1).transpose(0, 2, 1, 3)

    sm = jax.nn.softmax(lb_table.astype(jnp.float32), axis=1)
    lb = (jnp.cumsum(sm, axis=1) - sm)[:, layer]
    f_f = lb[0] + (1.0 - lb[0]) * jax.nn.sigmoid(ff.astype(jnp.float32))
    f_b = lb[1] + (1.0 - lb[1]) * jax.nn.sigmoid(fb.astype(jnp.float32))
    o = bidir_recurrence(heads(jax.nn.silu(q)), heads(1.0 - f_f), heads(1.0 - f_b), heads(i),
                         heads(jnp.log(f_f)), heads(jnp.log(f_b)), HG_CHUNK)
    o = rms_norm(o.transpose(0, 2, 1, 3), gain).reshape(B, L, HG_HEADS * HG_DV).astype(h.dtype)
    return (o * jax.nn.silu(gate)) @ w_out


def conv_ffn(h, w_gate, w_val, conv_w, conv_b, w_down):
    a = dwconv3(h @ w_gate, conv_w, conv_b)
    return (jax.nn.silu(a) * (h @ w_val)) @ w_down


def setup_inputs(seed: int = 0) -> dict:
    key = jax.random.key(seed)
    keys = jax.random.split(key, 36)
    counter = iter(range(36))

    def nrm(shape, scale=1.0):
        return scale * jax.random.normal(keys[next(counter)], shape, jnp.float32)

    G, W, D = N_GROUPS, HY_WIDTH, D_MODEL
    hy_decay0 = jnp.linspace(math.log(HY_TARGET) / HY_DECAY_LONG, math.log(HY_TARGET) / HY_DECAY_SHORT, W, dtype=jnp.float32)
    ret_decay0 = jnp.log(-jnp.log(1.0 - 2.0 ** (-5.0 - jnp.arange(RET_HEADS, dtype=jnp.float32))))
    return {
        'x_prompt': nrm((BATCH, SEQ, D)),
        'x_sample': nrm((DEC_BATCH, DEC_SEQ, D)),
        'c_prompt': nrm((BATCH, D)),
        'c_sample': nrm((DEC_BATCH, D)),
        'ada_w': nrm((DEPTH, D, N_MOD * D), D ** -0.5),
        'ada_b': nrm((DEPTH, N_MOD * D), 0.02),
        'norm_g': 1.0 + nrm((DEPTH, 2, D), 0.02),
        'hy_w_in': nrm((G, D, 3 * W), D ** -0.5),
        'hy_conv_w': nrm((G, 3, 3 * W), 3 ** -0.5),
        'hy_conv_b': nrm((G, 3 * W), 0.02),
        'hy_w1': nrm((G, HY_EMB, HY_FILTER_ORDER), HY_EMB ** -0.5),
        'hy_b1': nrm((G, HY_FILTER_ORDER), 0.02),
        'hy_w2': nrm((G, HY_FILTER_ORDER, HY_FILTER_ORDER), HY_FILTER_ORDER ** -0.5),
        'hy_b2': nrm((G, HY_FILTER_ORDER), 0.02),
        'hy_w3': nrm((G, HY_FILTER_ORDER, 2 * W), HY_FILTER_ORDER ** -0.5),
        'hy_freq': 1.0 + nrm((G, HY_FILTER_ORDER), 0.01),
        'hy_decay': hy_decay0[None, None, :] + nrm((G, 2, W), 0.01),
        'hy_skip': nrm((G, W)),
        'hy_w_out': nrm((G, W, D), W ** -0.5),
        'ret_w_in': nrm((G, D, RET_IN), D ** -0.5),
        'ret_decay': ret_decay0[None, None, :] + nrm((G, 2, RET_HEADS), 0.01),
        'ret_w_out': nrm((G, RET_HEADS * RET_DV, D), (RET_HEADS * RET_DV) ** -0.5),
        'swa_w_in': nrm((G, D, SWA_IN), D ** -0.5),
        'swa_q_gain': 1.0 + nrm((G, SWA_DH), 0.02),
        'swa_k_gain': 1.0 + nrm((G, SWA_DH), 0.02),
        'swa_sink': nrm((G, SWA_HQ), 0.5),
        'swa_w_out': nrm((G, SWA_HQ * SWA_DH, D), (SWA_HQ * SWA_DH) ** -0.5),
        'hg_w_in': nrm((G, D, HG_IN), D ** -0.5),
        'hg_lb': nrm((2, DEPTH, HG_HEADS * HG_DK), 0.1),
        'hg_gain': 1.0 + nrm((G, HG_DV), 0.02),
        'hg_w_out': nrm((G, HG_HEADS * HG_DV, D), (HG_HEADS * HG_DV) ** -0.5),
        'ffn_w_gate': nrm((DEPTH, D, D_FF), D ** -0.5),
        'ffn_w_val': nrm((DEPTH, D, D_FF), D ** -0.5),
        'ffn_conv_w': nrm((DEPTH, 3, D_FF), 3 ** -0.5),
        'ffn_conv_b': nrm((DEPTH, D_FF), 0.02),
        'ffn_w_down': nrm((DEPTH, D_FF, D), D_FF ** -0.5),
    }


def reference(x_prompt, x_sample, c_prompt, c_sample, ada_w, ada_b, norm_g,
              hy_w_in, hy_conv_w, hy_conv_b, hy_w1, hy_b1, hy_w2, hy_b2, hy_w3, hy_freq, hy_decay, hy_skip, hy_w_out,
              ret_w_in, ret_decay, ret_w_out,
              swa_w_in, swa_q_gain, swa_k_gain, swa_sink, swa_w_out,
              hg_w_in, hg_lb, hg_gain, hg_w_out,
              ffn_w_gate, ffn_w_val, ffn_conv_w, ffn_conv_b, ffn_w_down):

    def trunk(x, c):
        cs = jax.nn.silu(c)
        for layer in range(DEPTH):
            kind, j = layer % N_MIXERS, layer // N_MIXERS
            mod = (cs @ ada_w[layer] + ada_b[layer])[:, None, :]
            sh1, sc1, g1, sh2, sc2, g2 = jnp.split(mod, N_MOD, axis=-1)
            h = rms_norm(x, norm_g[layer, 0]) * (1.0 + sc1) + sh1
            if kind == 0:
                m = hyena_mixer(h, hy_w_in[j], hy_conv_w[j], hy_conv_b[j], hy_w1[j], hy_b1[j], hy_w2[j], hy_b2[j],
                                hy_w3[j], hy_freq[j], hy_decay[j], hy_skip[j], hy_w_out[j])
            elif kind == 1:
                m = retention_mixer(h, ret_w_in[j], ret_decay[j], ret_w_out[j])
            elif kind == 2:
                m = swa_mixer(h, swa_w_in[j], swa_q_gain[j], swa_k_gain[j], swa_sink[j], swa_w_out[j])
            else:
                m = hgrn_mixer(h, hg_w_in[j], hg_lb, layer, hg_gain[j], hg_w_out[j])
            x = x + g1 * m
            h = rms_norm(x, norm_g[layer, 1]) * (1.0 + sc2) + sh2
            x = x + g2 * conv_ffn(h, ffn_w_gate[layer], ffn_w_val[layer], ffn_conv_w[layer], ffn_conv_b[layer], ffn_w_down[layer])
        return x

    y_prompt = trunk(x_prompt, c_prompt)
    y_sample = trunk(x_sample, c_sample)
    return (y_prompt, y_sample)
```

```python
import functools
import math

import numpy as np
import jax
import jax.numpy as jnp
from jax import lax
from jax.experimental import pallas as pl
from jax.experimental.pallas import tpu as pltpu

F32 = jnp.float32
BF16 = jnp.bfloat16

NORM_EPS = 1e-6
N_MIXERS = 4
N_MOD = 6
HY_BANDS = 16
RET_HEADS = 4
RET_CHUNK = 128
SWA_HQ = 16
SWA_HKV = 4
SWA_DH = 64
WINDOW = 128
ATTN_BLOCK = 128
ROPE_THETA = 10000.0
NEG_INF = -1e30
HG_HEADS = 8
HG_CHUNK = 64

LANES = 128
SUBLANES = 8
BF16_ROWS = 16
VMEM_LIMIT_CAP = 60 * 1024 * 1024
VMEM_SLACK = 8 * 1024 * 1024

NT_DIMS = (((1,), (1,)), ((), ()))
TN_DIMS = (((0,), (0,)), ((), ()))


def _nbytes(shape, dtype):
    item = jnp.dtype(dtype).itemsize
    sub = SUBLANES * 4 // item
    dims = list(shape)
    dims[-1] = -(-dims[-1] // LANES) * LANES
    if len(dims) > 1:
        dims[-2] = -(-dims[-2] // sub) * sub
    return int(np.prod(dims)) * item


def _cparams(semantics, pipelined, resident):
    need = 2 * sum(_nbytes(s, d) for s, d in pipelined) + sum(_nbytes(s, d) for s, d in resident)
    return pltpu.CompilerParams(dimension_semantics=semantics,
                                vmem_limit_bytes=min(need + VMEM_SLACK, VMEM_LIMIT_CAP))


def _dot(a, b):
    return jnp.dot(a, b, preferred_element_type=F32)


def _split(x):
    hi = x.astype(BF16)
    lo = (x - hi.astype(F32)).astype(BF16)
    return hi, lo


def _dot_tab(t_hi, t_lo, d):
    d_hi, d_lo = _split(d)
    return _dot(t_hi, d_hi) + _dot(t_lo, d_hi) + _dot(t_hi, d_lo)


def _dot3(a, b):
    a_hi, a_lo = _split(a)
    b_hi, b_lo = _split(b)
    return _dot(a_hi, b_hi) + _dot(a_lo, b_hi) + _dot(a_hi, b_lo)


def _cumdot(tri, g):
    g0 = g.astype(BF16)
    r1 = g - g0.astype(F32)
    g1 = r1.astype(BF16)
    g2 = (r1 - g1.astype(F32)).astype(BF16)
    return _dot(tri, g0) + _dot(tri, g1) + _dot(tri, g2)


def _silu(x):
    return x * jax.nn.sigmoid(x)


def _norm_mod(x, g, sc, sh):
    xf = x.astype(F32)
    y = xf * lax.rsqrt(jnp.mean(xf * xf, axis=-1, keepdims=True) + NORM_EPS)
    return (y * g) * (1.0 + sc) + sh


def _row_tile(L):
    return min(L, 1024)


def _ada_kernel(c_ref, w_ref, b_ref, o_ref):
    cs = _silu(c_ref[...]).astype(BF16)
    o_ref[...] = _dot(cs, w_ref[...].astype(BF16)) + b_ref[...]


def _ada_mod(c_all, ada_w, ada_b):
    depth, D, N = ada_w.shape
    R = c_all.shape[0]
    tn = 1024
    return pl.pallas_call(
        _ada_kernel,
        grid=(depth, N // tn),
        in_specs=[pl.BlockSpec((R, D), lambda l, j: (0, 0)),
                  pl.BlockSpec((None, D, tn), lambda l, j: (l, 0, j)),
                  pl.BlockSpec((None, 1, tn), lambda l, j: (l, 0, j))],
        out_specs=pl.BlockSpec((None, R, tn), lambda l, j: (l, 0, j)),
        out_shape=jax.ShapeDtypeStruct((depth, R, N), F32),
        compiler_params=_cparams(("parallel", "parallel"),
                                 [((D, tn), F32), ((R, D), F32), ((R, tn), F32)], []),
        name="ada_mod",
    )(c_all, ada_w, ada_b.reshape(depth, 1, N))


def _fill_h(h_ref, x_ref, g_ref, sc_ref, sh_ref):
    h_ref[...] = _norm_mod(x_ref[...], g_ref[...], sc_ref[...], sh_ref[...]).astype(BF16)


def _fill_h_halo(h_ref, x_ref, xp_ref, xn_ref, g_ref, sc_ref, sh_ref, tm):
    i = pl.program_id(1)
    last = pl.num_programs(1) - 1
    g, sc, sh = g_ref[...], sc_ref[...], sh_ref[...]
    h_ref[BF16_ROWS:BF16_ROWS + tm, :] = _norm_mod(x_ref[...], g, sc, sh).astype(BF16)
    hp = _norm_mod(xp_ref[...], g, sc, sh)
    hn = _norm_mod(xn_ref[...], g, sc, sh)
    h_ref[0:BF16_ROWS, :] = jnp.where(i > 0, hp, 0.0).astype(BF16)
    h_ref[BF16_ROWS + tm:, :] = jnp.where(i < last, hn, 0.0).astype(BF16)


def _conv3_rows(z_ref, cw_ref, cb_ref, tm):
    o = BF16_ROWS
    return (z_ref[o - 1:o - 1 + tm, :] * cw_ref[0:1, :] + z_ref[o:o + tm, :] * cw_ref[1:2, :]
            + z_ref[o + 1:o + 1 + tm, :] * cw_ref[2:3, :] + cb_ref[...])


def _proj_plain_kernel(x_ref, g_ref, sc_ref, sh_ref, w_ref, o_ref, h_ref):
    @pl.when(pl.program_id(2) == 0)
    def _():
        _fill_h(h_ref, x_ref, g_ref, sc_ref, sh_ref)

    o_ref[...] = _dot(h_ref[...], w_ref[...]).astype(o_ref.dtype)


def _proj_conv_kernel(x_ref, xp_ref, xn_ref, g_ref, sc_ref, sh_ref, w_ref, cw_ref, cb_ref,
                      o_ref, h_ref, z_ref, *, tm):
    @pl.when(pl.program_id(2) == 0)
    def _():
        _fill_h_halo(h_ref, x_ref, xp_ref, xn_ref, g_ref, sc_ref, sh_ref, tm)

    z_ref[...] = _dot(h_ref[...], w_ref[...])
    o_ref[...] = _conv3_rows(z_ref, cw_ref, cb_ref, tm).astype(o_ref.dtype)


def _proj_rope_kernel(x_ref, g_ref, sc_ref, sh_ref, w_ref, hg_ref, flag_ref, cos_ref, sin_ref,
                      o_ref, h_ref, *, dh):
    @pl.when(pl.program_id(2) == 0)
    def _():
        _fill_h(h_ref, x_ref, g_ref, sc_ref, sh_ref)

    z = _dot(h_ref[...], w_ref[...])
    cos, sin = cos_ref[...], sin_ref[...]
    for s in range(z.shape[1] // LANES):
        sl = slice(s * LANES, (s + 1) * LANES)
        zs = z[:, sl]
        ms = jnp.sum(zs * zs, axis=-1, keepdims=True) * (1.0 / dh)
        yn = zs * lax.rsqrt(ms + NORM_EPS) * hg_ref[:, sl]
        yr = yn * cos + pltpu.roll(yn, LANES // 2, axis=1) * sin
        o_ref[:, sl] = jnp.where(flag_ref[:, sl] > 0.0, yr, zs).astype(o_ref.dtype)


def _mod_specs(D):
    return [pl.BlockSpec((1, D), lambda b, i, j: (0, 0)),
            pl.BlockSpec((None, 1, D), lambda b, i, j: (b, 0, 0)),
            pl.BlockSpec((None, 1, D), lambda b, i, j: (b, 0, 0))]


def _halo_specs(tm, D, L):
    hb = tm // BF16_ROWS
    nhb = L // BF16_ROWS
    return [pl.BlockSpec((None, tm, D), lambda b, i, j: (b, i, 0)),
            pl.BlockSpec((None, BF16_ROWS, D), lambda b, i, j: (b, jnp.maximum(i * hb - 1, 0), 0)),
            pl.BlockSpec((None, BF16_ROWS, D), lambda b, i, j: (b, jnp.minimum((i + 1) * hb, nhb - 1), 0))]


def _proj_in(x, g, sc, sh, w, *, tn):
    B, L, D = x.shape
    N = w.shape[1]
    tm = _row_tile(L)
    return pl.pallas_call(
        _proj_plain_kernel,
        grid=(B, L // tm, N // tn),
        in_specs=[pl.BlockSpec((None, tm, D), lambda b, i, j: (b, i, 0))] + _mod_specs(D)
        + [pl.BlockSpec((D, tn), lambda b, i, j: (0, j))],
        out_specs=pl.BlockSpec((None, tm, tn), lambda b, i, j: (b, i, j)),
        out_shape=jax.ShapeDtypeStruct((B, L, N), BF16),
        scratch_shapes=[pltpu.VMEM((tm, D), BF16)],
        compiler_params=_cparams(("parallel", "parallel", "arbitrary"),
                                 [((tm, D), F32), ((D, tn), BF16), ((tm, tn), BF16)],
                                 [((tm, D), BF16), ((tm, tn), F32)]),
        name="proj_in",
    )(x, g, sc, sh, w)


def _proj_in_conv(x, g, sc, sh, w, cw, cb, *, tn):
    B, L, D = x.shape
    N = w.shape[1]
    tm = _row_tile(L)
    te = tm + 2 * BF16_ROWS
    return pl.pallas_call(
        functools.partial(_proj_conv_kernel, tm=tm),
        grid=(B, L // tm, N // tn),
        in_specs=_halo_specs(tm, D, L) + _mod_specs(D)
        + [pl.BlockSpec((D, tn), lambda b, i, j: (0, j)),
           pl.BlockSpec((3, tn), lambda b, i, j: (0, j)),
           pl.BlockSpec((1, tn), lambda b, i, j: (0, j))],
        out_specs=pl.BlockSpec((None, tm, tn), lambda b, i, j: (b, i, j)),
        out_shape=jax.ShapeDtypeStruct((B, L, N), BF16),
        scratch_shapes=[pltpu.VMEM((te, D), BF16), pltpu.VMEM((te, tn), F32)],
        compiler_params=_cparams(("parallel", "parallel", "arbitrary"),
                                 [((tm, D), F32), ((D, tn), BF16), ((tm, tn), BF16)],
                                 [((te, D), BF16), ((te, tn), F32), ((te, tn), F32)]),
        name="proj_in_conv",
    )(x, x, x, g, sc, sh, w, cw, cb.reshape(1, N))


def _proj_in_rope(x, g, sc, sh, w, head_gain, flag, cos, sin, *, tn, dh):
    B, L, D = x.shape
    N = w.shape[1]
    tm = _row_tile(L)
    return pl.pallas_call(
        functools.partial(_proj_rope_kernel, dh=dh),
        grid=(B, L // tm, N // tn),
        in_specs=[pl.BlockSpec((None, tm, D), lambda b, i, j: (b, i, 0))] + _mod_specs(D)
        + [pl.BlockSpec((D, tn), lambda b, i, j: (0, j)),
           pl.BlockSpec((1, tn), lambda b, i, j: (0, j)),
           pl.BlockSpec((1, tn), lambda b, i, j: (0, j)),
           pl.BlockSpec((tm, LANES), lambda b, i, j: (i, 0)),
           pl.BlockSpec((tm, LANES), lambda b, i, j: (i, 0))],
        out_specs=pl.BlockSpec((None, tm, tn), lambda b, i, j: (b, i, j)),
        out_shape=jax.ShapeDtypeStruct((B, L, N), BF16),
        scratch_shapes=[pltpu.VMEM((tm, D), BF16)],
        compiler_params=_cparams(("parallel", "parallel", "arbitrary"),
                                 [((tm, D), F32), ((D, tn), BF16), ((tm, tn), BF16), ((tm, LANES), F32),
                                  ((tm, LANES), F32)],
                                 [((tm, D), BF16), ((tm, tn), F32), ((tm, tn), F32)]),
        name="proj_in_rope",
    )(x, g, sc, sh, w, head_gain, flag, cos, sin)


def _proj_out_kernel(a_ref, w_ref, x_ref, gate_ref, o_ref):
    o_ref[...] = x_ref[...] + gate_ref[...] * _dot(a_ref[...], w_ref[...])


def _proj_out(a, w, x, gate):
    B, L, K = a.shape
    D = w.shape[1]
    tm = _row_tile(L)
    return pl.pallas_call(
        _proj_out_kernel,
        grid=(B, L // tm),
        in_specs=[pl.BlockSpec((None, tm, K), lambda b, i: (b, i, 0)),
                  pl.BlockSpec((K, D), lambda b, i: (0, 0)),
                  pl.BlockSpec((None, tm, D), lambda b, i: (b, i, 0)),
                  pl.BlockSpec((None, 1, D), lambda b, i: (b, 0, 0))],
        out_specs=pl.BlockSpec((None, tm, D), lambda b, i: (b, i, 0)),
        out_shape=jax.ShapeDtypeStruct((B, L, D), F32),
        compiler_params=_cparams(("parallel", "parallel"),
                                 [((tm, K), BF16), ((K, D), BF16), ((tm, D), F32), ((tm, D), F32)],
                                 [((tm, D), F32)]),
        name="proj_out",
    )(a, w, x, gate)


def _ffn_kernel(x_ref, xp_ref, xn_ref, g_ref, sc_ref, sh_ref, gate_ref, wg_ref, wv_ref, cw_ref, cb_ref,
                wd_ref, o_ref, h_ref, z_ref, acc_ref, *, tm):
    j = pl.program_id(2)

    @pl.when(j == 0)
    def _():
        _fill_h_halo(h_ref, x_ref, xp_ref, xn_ref, g_ref, sc_ref, sh_ref, tm)
        acc_ref[...] = jnp.zeros_like(acc_ref)

    z_ref[...] = _dot(h_ref[...], wg_ref[...])
    a = _conv3_rows(z_ref, cw_ref, cb_ref, tm)
    v = _dot(h_ref[BF16_ROWS:BF16_ROWS + tm, :], wv_ref[...])
    acc_ref[...] += _dot((_silu(a) * v).astype(BF16), wd_ref[...])

    @pl.when(j == pl.num_programs(2) - 1)
    def _():
        o_ref[...] = x_ref[...] + gate_ref[...] * acc_ref[...]


def _ffn(x, g, sc, sh, gate, wg, wv, cw, cb, wd):
    B, L, D = x.shape
    F = wg.shape[1]
    tm = _row_tile(L)
    te = tm + 2 * BF16_ROWS
    tf = 256
    return pl.pallas_call(
        functools.partial(_ffn_kernel, tm=tm),
        grid=(B, L // tm, F // tf),
        in_specs=_halo_specs(tm, D, L) + _mod_specs(D)
        + [pl.BlockSpec((None, 1, D), lambda b, i, j: (b, 0, 0)),
           pl.BlockSpec((D, tf), lambda b, i, j: (0, j)),
           pl.BlockSpec((D, tf), lambda b, i, j: (0, j)),
           pl.BlockSpec((3, tf), lambda b, i, j: (0, j)),
           pl.BlockSpec((1, tf), lambda b, i, j: (0, j)),
           pl.BlockSpec((tf, D), lambda b, i, j: (j, 0))],
        out_specs=pl.BlockSpec((None, tm, D), lambda b, i, j: (b, i, 0)),
        out_shape=jax.ShapeDtypeStruct((B, L, D), F32),
        scratch_shapes=[pltpu.VMEM((te, D), BF16), pltpu.VMEM((te, tf), F32), pltpu.VMEM((tm, D), F32)],
        compiler_params=_cparams(("parallel", "parallel", "arbitrary"),
                                 [((tm, D), F32), ((tm, D), F32), ((D, tf), BF16), ((D, tf), BF16),
                                  ((tf, D), BF16)],
                                 [((te, D), BF16), ((te, tf), F32), ((tm, D), F32), ((te, tf), F32),
                                  ((tm, tf), F32), ((tm, D), F32)]),
        name="ffn",
    )(x, x, x, g, sc, sh, gate, wg, wv, cw, cb.reshape(1, F), wd)


def _fft_dims(L):
    N = 2 * L
    p = N.bit_length() - 1
    n1 = 1 << ((p + 1) // 2)
    return N, n1, N // n1


def _k1_pad(n1):
    return n1 // 2 + 8


def _fft_tables(L):
    N, n1, n2 = _fft_dims(L)
    k1p = _k1_pad(n1)
    k1 = np.arange(k1p)
    valid = (k1 <= n1 // 2).astype(np.float64)
    weight = np.where((k1 == 0) | (k1 == n1 // 2), 1.0, 2.0) * valid
    m1 = np.arange(n1 // 2)
    n2v = np.arange(n2)
    n_idx = n2 * m1[None, :] + n2v[:, None]
    phi = 2.0 * np.pi * k1[None, :, None] * n_idx[:, None, :] / N
    s1 = np.concatenate([np.cos(phi) * valid[None, :, None], -np.sin(phi) * valid[None, :, None]], axis=1)
    phit = np.transpose(phi, (0, 2, 1))
    gl = np.concatenate([np.cos(phit) * weight[None, None, :], -np.sin(phit) * weight[None, None, :]],
                        axis=2) / N
    th = 2.0 * np.pi * np.outer(n2v, n2v) / n2
    c, s = np.cos(th), np.sin(th)
    f2 = np.block([[c, s], [-s, c]])
    f2c = np.block([[c, -s], [s, c]])

    def hl(a):
        a32 = jnp.asarray(a, F32)
        hi = a32.astype(BF16)
        return hi, (a32 - hi.astype(F32)).astype(BF16)

    return dict(N=N, n1=n1, n2=n2, k1p=k1p, s1=hl(s1), gl=hl(gl), f2=hl(f2), f2c=hl(f2c))


def _hy_positions(L):
    idx = np.concatenate([np.arange(L), np.array([0]), np.arange(L - 1, 0, -1)])
    t = idx / (L - 1)
    ang = (2.0 * np.pi / L) * idx
    bands = np.linspace(1e-4, HY_BANDS - 1, HY_BANDS)
    z = np.concatenate([t[:, None], np.cos(bands[None, :] * ang[:, None]), -np.sin(bands[None, :] * ang[:, None])],
                       axis=1)
    zp = np.zeros((2 * L, LANES))
    zp[:, : z.shape[1]] = z
    return jnp.asarray(zp, F32)


def _hy_mlp_kernel(z_ref, w1_ref, b1_ref, w2_ref, b2_ref, w3_ref, fr_ref, dec_ref, taps_ref, asum_ref, *, L, tr):
    i = pl.program_id(0)
    z = z_ref[...]
    fr = fr_ref[...]
    h = jnp.sin(fr * (_dot3(z, w1_ref[...]) + b1_ref[...]))
    h = jnp.sin(fr * (_dot3(h, w2_ref[...]) + b2_ref[...]))
    h = _dot3(h, w3_ref[...])
    t = z[:, 0:1]
    taps = h * jnp.exp(-t * jnp.abs(dec_ref[...]))
    row = i * tr + lax.broadcasted_iota(jnp.int32, (tr, 1), 0)
    taps = jnp.where(row == L, 0.0, taps)
    taps_ref[...] = taps

    @pl.when(i == 0)
    def _():
        asum_ref[...] = jnp.zeros_like(asum_ref)

    asum_ref[...] += jnp.sum(jnp.abs(taps), axis=0, keepdims=True)


def _hy_filter_taps(L, w1, b1, w2, b2, w3, freq, decay):
    W = decay.shape[1]
    E, O = w1.shape
    tr = 512
    half = L // tr
    z = _hy_positions(L)
    w1p = jnp.zeros((LANES, O), F32).at[:E].set(w1)
    return pl.pallas_call(
        functools.partial(_hy_mlp_kernel, L=L, tr=tr),
        grid=(2 * L // tr,),
        in_specs=[pl.BlockSpec((tr, LANES), lambda i: (i, 0)),
                  pl.BlockSpec((LANES, O), lambda i: (0, 0)),
                  pl.BlockSpec((1, O), lambda i: (0, 0)),
                  pl.BlockSpec((O, O), lambda i: (0, 0)),
                  pl.BlockSpec((1, O), lambda i: (0, 0)),
                  pl.BlockSpec((O, W), lambda i: (0, i // half)),
                  pl.BlockSpec((1, O), lambda i: (0, 0)),
                  pl.BlockSpec((None, 1, W), lambda i: (i // half, 0, 0))],
        out_specs=[pl.BlockSpec((tr, W), lambda i: (i, 0)),
                   pl.BlockSpec((1, W), lambda i: (0, 0))],
        out_shape=[jax.ShapeDtypeStruct((2 * L, W), F32), jax.ShapeDtypeStruct((1, W), F32)],
        compiler_params=_cparams(("arbitrary",), [((tr, W), F32), ((O, W), F32), ((tr, LANES), F32)],
                                 [((tr, W), F32), ((tr, W), F32)]),
        name="hyena_filter_mlp",
    )(z, w1p, b1.reshape(1, O), w2, b2.reshape(1, O), w3, freq.reshape(1, O), decay.reshape(2, 1, W))


def _fft_stage1(src_ref, src_off, a_ref, s1h_ref, s1l_ref, n1, n2, k1p, sign=None, src_off2=None):
    def body(j, carry):
        d = src_ref[pl.ds(src_off + j, n1 // 2, stride=n2), :]
        r = _dot_tab(s1h_ref[j], s1l_ref[j], d)
        if src_off2 is not None:
            d2 = src_ref[pl.ds(src_off2 + j, n1 // 2, stride=n2), :]
            r = r + sign * _dot_tab(s1h_ref[j], s1l_ref[j], d2)
        a_ref[pl.ds(pl.multiple_of(j * 2 * k1p, 8), 2 * k1p), :] = r
        return carry

    lax.fori_loop(0, n2, body, 0)


def _load_k1(a_ref, k, n2, k1p):
    br = a_ref[pl.ds(k, n2, stride=2 * k1p), :]
    bi = a_ref[pl.ds(k1p + k, n2, stride=2 * k1p), :]
    return jnp.concatenate([br, bi], axis=0)


def _hy_spec_kernel(taps_ref, asum_ref, s1h_ref, s1l_ref, f2h_ref, f2l_ref, h_ref, a_ref, *, L, n1, n2, k1p):
    r = lax.broadcasted_iota(jnp.int32, (2 * k1p, 1), 0)
    k1 = jnp.where(r < k1p, r, r - k1p)
    sign = (1 - 2 * (k1 & 1)).astype(F32)
    _fft_stage1(taps_ref, 0, a_ref, s1h_ref, s1l_ref, n1, n2, k1p, sign=sign, src_off2=L)
    inv = 1.0 / asum_ref[...]

    def body(k, carry):
        x = _dot_tab(f2h_ref[...], f2l_ref[...], _load_k1(a_ref, k, n2, k1p))
        h_ref[k] = x * inv
        return carry

    lax.fori_loop(0, n1 // 2 + 1, body, 0)


def _hy_spectrum(taps, asum, tabs):
    n1, n2, k1p = tabs["n1"], tabs["n2"], tabs["k1p"]
    N, W = taps.shape
    L = N // 2
    cb = LANES
    k1v = n1 // 2 + 1
    s1h, s1l = tabs["s1"]
    f2h, f2l = tabs["f2"]
    const3 = lambda c: (0, 0, 0)
    const2 = lambda c: (0, 0)
    return pl.pallas_call(
        functools.partial(_hy_spec_kernel, L=L, n1=n1, n2=n2, k1p=k1p),
        grid=(W // cb,),
        in_specs=[pl.BlockSpec((N, cb), lambda c: (0, c)),
                  pl.BlockSpec((1, cb), lambda c: (0, c)),
                  pl.BlockSpec(s1h.shape, const3), pl.BlockSpec(s1l.shape, const3),
                  pl.BlockSpec(f2h.shape, const2), pl.BlockSpec(f2l.shape, const2)],
        out_specs=pl.BlockSpec((k1v, 2 * n2, cb), lambda c: (0, 0, c)),
        out_shape=jax.ShapeDtypeStruct((k1v, 2 * n2, W), F32),
        scratch_shapes=[pltpu.VMEM((n2 * 2 * k1p, cb), F32)],
        compiler_params=_cparams(("parallel",),
                                 [((N, cb), F32), (s1h.shape, BF16), (s1l.shape, BF16), ((k1v, 2 * n2, cb), F32)],
                                 [((n2 * 2 * k1p, cb), F32)]),
        name="hyena_filter_spectrum",
    )(taps, asum, s1h, s1l, f2h, f2l)


def _hy_conv_kernel(x0_ref, x1_ref, v_ref, skip_ref, h_ref, s1h_ref, s1l_ref, f2h_ref, f2l_ref,
                    f2ch_ref, f2cl_ref, glh_ref, gll_ref, o_ref, u_ref, a_ref, *, n1, n2, k1p):
    u_ref[...] = x1_ref[...].astype(F32) * v_ref[...].astype(F32)
    _fft_stage1(u_ref, 0, a_ref, s1h_ref, s1l_ref, n1, n2, k1p)

    def mid(k, carry):
        x = _dot_tab(f2h_ref[...], f2l_ref[...], _load_k1(a_ref, k, n2, k1p))
        xr, xi = x[:n2], x[n2:]
        hr, hi = h_ref[k, :n2, :], h_ref[k, n2:, :]
        y = jnp.concatenate([xr * hr - xi * hi, xr * hi + xi * hr], axis=0)
        c = _dot_tab(f2ch_ref[...], f2cl_ref[...], y)
        a_ref[pl.ds(k, n2, stride=2 * k1p), :] = c[:n2]
        a_ref[pl.ds(k1p + k, n2, stride=2 * k1p), :] = c[n2:]
        return carry

    lax.fori_loop(0, n1 // 2 + 1, mid, 0)
    skip = skip_ref[...]

    def last(j, carry):
        rhs = a_ref[pl.ds(pl.multiple_of(j * 2 * k1p, 8), 2 * k1p), :]
        y = _dot_tab(glh_ref[j], gll_ref[j], rhs)
        rows = pl.ds(j, n1 // 2, stride=n2)
        u_ref[rows, :] = y + u_ref[rows, :] * skip
        return carry

    lax.fori_loop(0, n2, last, 0)
    o_ref[...] = (u_ref[...].astype(F32) * x0_ref[...].astype(F32)).astype(o_ref.dtype)


def _hy_conv(z, skip, spec, tabs):
    n1, n2, k1p = tabs["n1"], tabs["n2"], tabs["k1p"]
    B, L, W3 = z.shape
    W = W3 // 3
    cb = LANES
    nc = W // cb
    k1v = n1 // 2 + 1
    tables = [*tabs["s1"], *tabs["f2"], *tabs["f2c"], *tabs["gl"]]

    def const_spec(t):
        return pl.BlockSpec(t.shape, (lambda c, b: (0, 0, 0)) if t.ndim == 3 else (lambda c, b: (0, 0)),
                            pipeline_mode=pl.Buffered(1))

    return pl.pallas_call(
        functools.partial(_hy_conv_kernel, n1=n1, n2=n2, k1p=k1p),
        grid=(nc, B),
        in_specs=[pl.BlockSpec((None, L, cb), lambda c, b: (b, 0, c), pipeline_mode=pl.Buffered(1)),
                  pl.BlockSpec((None, L, cb), lambda c, b: (b, 0, nc + c), pipeline_mode=pl.Buffered(1)),
                  pl.BlockSpec((None, L, cb), lambda c, b: (b, 0, 2 * nc + c), pipeline_mode=pl.Buffered(1)),
                  pl.BlockSpec((1, cb), lambda c, b: (0, c)),
                  pl.BlockSpec((k1v, 2 * n2, cb), lambda c, b: (0, 0, c), pipeline_mode=pl.Buffered(1))]
        + [const_spec(t) for t in tables],
        out_specs=pl.BlockSpec((None, L, cb), lambda c, b: (b, 0, c)),
        out_shape=jax.ShapeDtypeStruct((B, L, W), BF16),
        scratch_shapes=[pltpu.VMEM((L, cb), F32), pltpu.VMEM((n2 * 2 * k1p, cb), F32)],
        compiler_params=_cparams(("parallel", "arbitrary"),
                                 [((L, cb), BF16)],
                                 [((L, cb), BF16)] * 3
                                 + [((L, cb), F32), ((n2 * 2 * k1p, cb), F32), ((k1v, 2 * n2, cb), F32)]
                                 + [(t.shape, BF16) for t in tables]),
        name="hyena_long_conv",
    )(z, z, z, skip.reshape(1, W), spec, *tables)


def _rope_half(x, cos, sin):
    half = x.shape[1] // 2
    x1, x2 = x[:, :half], x[:, half:]
    return jnp.concatenate([x1 * cos - x2 * sin, x1 * sin + x2 * cos], axis=1)


def _ret_fwd_kernel(lg_ref, q_ref, k_ref, v_ref, cos_ref, sin_ref, o_ref, s_ref, *, T, C, H, DK, DV):
    @pl.when(pl.program_id(1) == 0)
    def _():
        s_ref[...] = jnp.zeros_like(s_ref)

    row = lax.broadcasted_iota(jnp.int32, (C, C), 0)
    col = lax.broadcasted_iota(jnp.int32, (C, C), 1)
    rel = (row - col).astype(F32)
    ridx = lax.broadcasted_iota(jnp.int32, (C, 1), 0).astype(F32)
    for h in range(H):
        lgf = lg_ref[0, h]
        lgb = lg_ref[1, h]
        dmat = jnp.where(rel > 0.0, jnp.exp(lgf * jnp.maximum(rel, 0.0)),
                         jnp.where(rel < 0.0, jnp.exp(lgb * jnp.maximum(-rel, 0.0)), 2.0))
        q_scale = jnp.exp(lgf * (ridx + 1.0))
        k_scale = jnp.exp(lgf * (C - 1.0 - ridx))
        s_decay = jnp.exp(jnp.full((1, 1), C, F32) * lgf)
        ck = slice(h * DK, (h + 1) * DK)
        cv = slice(h * DV, (h + 1) * DV)

        def chunk(c, carry):
            rows = pl.ds(pl.multiple_of(c * C, C), C)
            cos, sin = cos_ref[rows, :], sin_ref[rows, :]
            q = _rope_half(q_ref[rows, ck].astype(F32), cos, sin)
            k = _rope_half(k_ref[rows, ck].astype(F32), cos, sin) * (DK ** -0.5)
            v = v_ref[rows, cv]
            s = lax.dot_general(q.astype(BF16), k.astype(BF16), NT_DIMS, preferred_element_type=F32) * dmat
            st = s_ref[h]
            o_ref[rows, cv] = _dot(s.astype(BF16), v) + _dot((q * q_scale).astype(BF16), st.astype(BF16))
            s_ref[h] = s_decay * st + lax.dot_general((k * k_scale).astype(BF16), v, TN_DIMS,
                                                      preferred_element_type=F32)
            return carry

        lax.fori_loop(0, T // C, chunk, 0)


def _ret_bwd_kernel(lg_ref, q_ref, k_ref, v_ref, g_ref, cos_ref, sin_ref, o1_ref, o_ref, s_ref,
                    *, T, C, H, DK, DV):
    @pl.when(pl.program_id(1) == 0)
    def _():
        s_ref[...] = jnp.zeros_like(s_ref)

    ridx = lax.broadcasted_iota(jnp.int32, (C, 1), 0).astype(F32)
    nc = T // C
    for h in range(H):
        lgb = lg_ref[1, h]
        q_scale = jnp.exp(lgb * (C - ridx))
        k_scale = jnp.exp(lgb * ridx)
        s_decay = jnp.exp(jnp.full((1, 1), C, F32) * lgb)
        ck = slice(h * DK, (h + 1) * DK)
        cv = slice(h * DV, (h + 1) * DV)

        def chunk(cc, carry):
            rows = pl.ds(pl.multiple_of((nc - 1 - cc) * C, C), C)
            cos, sin = cos_ref[rows, :], sin_ref[rows, :]
            q = _rope_half(q_ref[rows, ck].astype(F32), cos, sin)
            k = _rope_half(k_ref[rows, ck].astype(F32), cos, sin) * (DK ** -0.5)
            v = v_ref[rows, cv]
            st = s_ref[h]
            o = o1_ref[rows, cv] + _dot((q * q_scale).astype(BF16), st.astype(BF16))
            y = o * lax.rsqrt(jnp.mean(o * o, axis=-1, keepdims=True) + NORM_EPS)
            o_ref[rows, cv] = (y * _silu(g_ref[rows, cv].astype(F32))).astype(o_ref.dtype)
            s_ref[h] = s_decay * st + lax.dot_general((k * k_scale).astype(BF16), v, TN_DIMS,
                                                      preferred_element_type=F32)
            return carry

        lax.fori_loop(0, nc, chunk, 0)


def _retention(z, log_gamma, cos, sin):
    B, L, _ = z.shape
    H, C = RET_HEADS, RET_CHUNK
    DK = cos.shape[1] * 2
    DV = 2 * DK
    T = min(L, 512)
    nT = L // T
    kw = dict(T=T, C=C, H=H, DK=DK, DV=DV)
    smem = pl.BlockSpec(memory_space=pltpu.SMEM)
    blocks = [((T, H * DK), BF16)] * 2 + [((T, H * DV), BF16)] * 2 + [((T, H * DV), F32)] * 2
    state = [((H, DK, DV), F32), ((DK, DV), F32), ((DK, DV), F32)]
    o1 = pl.pallas_call(
        functools.partial(_ret_fwd_kernel, **kw),
        grid=(B, nT),
        in_specs=[smem,
                  pl.BlockSpec((None, T, H * DK), lambda b, i: (b, i, 0)),
                  pl.BlockSpec((None, T, H * DK), lambda b, i: (b, i, 1)),
                  pl.BlockSpec((None, T, H * DV), lambda b, i: (b, i, 1)),
                  pl.BlockSpec((T, DK // 2), lambda b, i: (i, 0)),
                  pl.BlockSpec((T, DK // 2), lambda b, i: (i, 0))],
        out_specs=pl.BlockSpec((None, T, H * DV), lambda b, i: (b, i, 0)),
        out_shape=jax.ShapeDtypeStruct((B, L, H * DV), F32),
        scratch_shapes=[pltpu.VMEM((H, DK, DV), F32)],
        compiler_params=_cparams(("parallel", "arbitrary"), blocks, state),
        name="retention_fwd",
    )(log_gamma, z, z, z, cos, sin)
    return pl.pallas_call(
        functools.partial(_ret_bwd_kernel, **kw),
        grid=(B, nT),
        in_specs=[smem,
                  pl.BlockSpec((None, T, H * DK), lambda b, i: (b, nT - 1 - i, 0)),
                  pl.BlockSpec((None, T, H * DK), lambda b, i: (b, nT - 1 - i, 1)),
                  pl.BlockSpec((None, T, H * DV), lambda b, i: (b, nT - 1 - i, 1)),
                  pl.BlockSpec((None, T, H * DV), lambda b, i: (b, nT - 1 - i, 2)),
                  pl.BlockSpec((T, DK // 2), lambda b, i: (nT - 1 - i, 0)),
                  pl.BlockSpec((T, DK // 2), lambda b, i: (nT - 1 - i, 0)),
                  pl.BlockSpec((None, T, H * DV), lambda b, i: (b, nT - 1 - i, 0))],
        out_specs=pl.BlockSpec((None, T, H * DV), lambda b, i: (b, nT - 1 - i, 0)),
        out_shape=jax.ShapeDtypeStruct((B, L, H * DV), BF16),
        scratch_shapes=[pltpu.VMEM((H, DK, DV), F32)],
        compiler_params=_cparams(("parallel", "arbitrary"), blocks, state),
        name="retention_bwd",
    )(log_gamma, z, z, z, z, cos, sin, o1)


def _swa_kernel(sink_ref, q_ref, kp_ref, kc_ref, kn_ref, vp_ref, vc_ref, vn_ref, o_ref, *, HKV, G, BLK):
    i = pl.program_id(1)
    last = pl.num_programs(1) - 1
    R = G * BLK
    r = lax.broadcasted_iota(jnp.int32, (R, 3 * BLK), 0) % BLK
    c = lax.broadcasted_iota(jnp.int32, (R, 3 * BLK), 1)
    rel = r - (c - BLK)
    lo = jnp.where(i == 0, BLK, 0)
    hi = jnp.where(i == last, 2 * BLK, 3 * BLK)
    valid = (jnp.abs(rel) <= WINDOW) & (c >= lo) & (c < hi)
    grp = lax.broadcasted_iota(jnp.int32, (R, 1), 0) // BLK
    for j in range(HKV):
        sl = slice(j * LANES, (j + 1) * LANES)
        k = jnp.concatenate([kp_ref[:, sl], kc_ref[:, sl], kn_ref[:, sl]], axis=0)
        v = jnp.concatenate([vp_ref[:, sl], vc_ref[:, sl], vn_ref[:, sl]], axis=0)
        q = jnp.concatenate([q_ref[:, (j * G + g) * LANES:(j * G + g + 1) * LANES] for g in range(G)], axis=0)
        s = lax.dot_general(q, k, NT_DIMS, preferred_element_type=F32)
        s = jnp.where(valid, s, NEG_INF)
        sink = jnp.zeros((R, 1), F32)
        for g in range(G):
            sink = jnp.where(grp == g, sink_ref[0, j * G + g], sink)
        m = jnp.maximum(jnp.max(s, axis=-1, keepdims=True), sink)
        p = jnp.exp(s - m)
        denom = jnp.sum(p, axis=-1, keepdims=True) + jnp.exp(sink - m)
        o = _dot(p.astype(BF16), v) / denom
        for g in range(G):
            o_ref[:, (j * G + g) * LANES:(j * G + g + 1) * LANES] = o[g * BLK:(g + 1) * BLK].astype(o_ref.dtype)


def _swa_attention(z, sink):
    B, L, _ = z.shape
    BLK = ATTN_BLOCK
    nb = L // BLK
    G = SWA_HQ // SWA_HKV
    qw = SWA_HQ * LANES
    kvw = SWA_HKV * LANES
    kcol = qw // kvw
    prev = lambda b, i: (b, jnp.maximum(i - 1, 0), kcol)
    cur = lambda b, i: (b, i, kcol)
    nxt = lambda b, i: (b, jnp.minimum(i + 1, nb - 1), kcol)
    vprev = lambda b, i: (b, jnp.maximum(i - 1, 0), kcol + 1)
    vcur = lambda b, i: (b, i, kcol + 1)
    vnxt = lambda b, i: (b, jnp.minimum(i + 1, nb - 1), kcol + 1)
    kv = lambda f: pl.BlockSpec((None, BLK, kvw), f)
    return pl.pallas_call(
        functools.partial(_swa_kernel, HKV=SWA_HKV, G=G, BLK=BLK),
        grid=(B, nb),
        in_specs=[pl.BlockSpec(memory_space=pltpu.SMEM),
                  pl.BlockSpec((None, BLK, qw), lambda b, i: (b, i, 0)),
                  kv(prev), kv(cur), kv(nxt), kv(vprev), kv(vcur), kv(vnxt)],
        out_specs=pl.BlockSpec((None, BLK, qw), lambda b, i: (b, i, 0)),
        out_shape=jax.ShapeDtypeStruct((B, L, qw), BF16),
        compiler_params=_cparams(("parallel", "parallel"),
                                 [((BLK, qw), BF16)] * 2 + [((BLK, kvw), BF16)] * 6,
                                 [((G * BLK, 3 * BLK), F32)] * 4),
        name="swa_attention",
    )(sink.reshape(1, SWA_HQ), z, z, z, z, z, z, z)


def _swa_layout(w_in, q_gain, k_gain, w_out, L):
    D = w_in.shape[0]
    dh, hq, hkv = SWA_DH, SWA_HQ, SWA_HKV
    hf = dh // 2
    q_end, k_end = hq * dh, (hq + hkv) * dh

    def rot_slots(w, n):
        w = w.reshape(D, n, 2, hf)
        return jnp.pad(w, ((0, 0), (0, 0), (0, 0), (0, LANES // 2 - hf))).reshape(D, n * LANES)

    def val_slots(w, n):
        return jnp.pad(w.reshape(D, n, dh), ((0, 0), (0, 0), (0, LANES - dh))).reshape(D, n * LANES)

    w = jnp.concatenate([rot_slots(w_in[:, :q_end], hq), rot_slots(w_in[:, q_end:k_end], hkv),
                         val_slots(w_in[:, k_end:], hkv)], axis=1)

    def gain_slot(g):
        return jnp.pad(g.reshape(2, hf), ((0, 0), (0, LANES // 2 - hf))).reshape(LANES)

    head_gain = jnp.concatenate([jnp.tile(gain_slot(q_gain) * (dh ** -0.5), hq), jnp.tile(gain_slot(k_gain), hkv),
                                 jnp.ones((hkv * LANES,), F32)]).reshape(1, -1)
    flag = jnp.concatenate([jnp.ones(((hq + hkv) * LANES,), F32), jnp.zeros((hkv * LANES,), F32)]).reshape(1, -1)
    wo = jnp.pad(w_out.reshape(hq, dh, -1), ((0, 0), (0, LANES - dh), (0, 0))).reshape(hq * LANES, -1)

    inv = ROPE_THETA ** (-np.arange(0, dh, 2) / dh)
    ang = np.arange(L)[:, None] * inv[None, :]
    zero = np.zeros((L, LANES // 2 - hf))
    cos = np.concatenate([np.cos(ang), zero, np.cos(ang), zero], axis=1)
    sin = np.concatenate([-np.sin(ang), zero, np.sin(ang), zero], axis=1)
    return w, head_gain, flag, wo, jnp.asarray(cos, F32), jnp.asarray(sin, F32)


def _hg_gates(f_ref, rows, lb):
    f = lb + (1.0 - lb) * jax.nn.sigmoid(f_ref[rows, :].astype(F32))
    return f, jnp.log(f)


def _hg_fwd_kernel(lb_ref, q_ref, i_ref, f_ref, o_ref, st_ref, *, T, C, H, DK, DV):
    @pl.when(pl.program_id(1) == 0)
    def _():
        st_ref[...] = jnp.zeros_like(st_ref)

    row = lax.broadcasted_iota(jnp.int32, (C, C), 0)
    col = lax.broadcasted_iota(jnp.int32, (C, C), 1)
    causal = row >= col
    tri = causal.astype(BF16)
    lb = lb_ref[...]

    def chunk(c, carry):
        rows = pl.ds(pl.multiple_of(c * C, C), C)
        f, gl = _hg_gates(f_ref, rows, lb)
        b = _cumdot(tri, gl)
        b_last = jnp.sum(gl, axis=0, keepdims=True)
        qd = (_silu(q_ref[rows, :].astype(F32)) * jnp.exp(b)).astype(BF16)
        kf = 1.0 - f
        k_inv = (kf * jnp.exp(-b)).astype(BF16)
        k_end = (kf * jnp.exp(b_last - b)).astype(BF16)
        s_decay = jnp.exp(b_last)
        v = i_ref[rows, :]
        for h in range(H):
            ck = slice(h * DK, (h + 1) * DK)
            cv = slice(h * DV, (h + 1) * DV)
            s = lax.dot_general(qd[:, ck], k_inv[:, ck], NT_DIMS, preferred_element_type=F32)
            s = jnp.where(causal, s, 0.0)
            st = st_ref[h]
            o_ref[rows, cv] = _dot(s.astype(BF16), v[:, cv]) + lax.dot_general(
                qd[:, ck], st.astype(BF16), NT_DIMS, preferred_element_type=F32)
            st_ref[h] = st * s_decay[:, ck] + lax.dot_general(v[:, cv], k_end[:, ck], TN_DIMS,
                                                              preferred_element_type=F32)
        return carry

    lax.fori_loop(0, T // C, chunk, 0)


def _hg_bwd_kernel(lb_ref, gain_ref, q_ref, i_ref, f_ref, gate_ref, o1_ref, o_ref, st_ref, *, T, C, H, DK, DV):
    @pl.when(pl.program_id(1) == 0)
    def _():
        st_ref[...] = jnp.zeros_like(st_ref)

    row = lax.broadcasted_iota(jnp.int32, (C, C), 0)
    col = lax.broadcasted_iota(jnp.int32, (C, C), 1)
    anti = col >= row
    tri = anti.astype(BF16)
    lb = lb_ref[...]
    gain = gain_ref[...]
    nc = T // C

    def chunk(cc, carry):
        rows = pl.ds(pl.multiple_of((nc - 1 - cc) * C, C), C)
        f, gl = _hg_gates(f_ref, rows, lb)
        a = _cumdot(tri, gl)
        a_first = jnp.sum(gl, axis=0, keepdims=True)
        qd = (_silu(q_ref[rows, :].astype(F32)) * jnp.exp(a)).astype(BF16)
        kb = 1.0 - f
        k_inv = (kb * jnp.exp(-a)).astype(BF16)
        k_end = (kb * jnp.exp(a_first - a)).astype(BF16)
        s_decay = jnp.exp(a_first)
        v = i_ref[rows, :]
        for h in range(H):
            ck = slice(h * DK, (h + 1) * DK)
            cv = slice(h * DV, (h + 1) * DV)
            s = lax.dot_general(qd[:, ck], k_inv[:, ck], NT_DIMS, preferred_element_type=F32)
            s = jnp.where(anti, s, 0.0)
            st = st_ref[h]
            o = o1_ref[rows, cv] + _dot(s.astype(BF16), v[:, cv]) + lax.dot_general(
                qd[:, ck], st.astype(BF16), NT_DIMS, preferred_element_type=F32)
            y = o * lax.rsqrt(jnp.mean(o * o, axis=-1, keepdims=True) + NORM_EPS) * gain
            o_ref[rows, cv] = (y * _silu(gate_ref[rows, cv].astype(F32))).astype(o_ref.dtype)
            st_ref[h] = st * s_decay[:, ck] + lax.dot_general(v[:, cv], k_end[:, ck], TN_DIMS,
                                                              preferred_element_type=F32)
        return carry

    lax.fori_loop(0, nc, chunk, 0)


def _hgrn(z, lb, gain):
    B, L, _ = z.shape
    H, C = HG_HEADS, HG_CHUNK
    HD = lb.shape[1]
    DK = HD // H
    DV = gain.shape[0]
    T = min(L, 512)
    nT = L // T
    kw = dict(T=T, C=C, H=H, DK=DK, DV=DV)
    blocks = [((T, HD), BF16)] * 5 + [((T, H * DV), F32)] * 2
    state = [((H, DV, DK), F32)] + [((C, HD), F32)] * 12
    lb3 = lb.reshape(2, 1, HD)
    o1 = pl.pallas_call(
        functools.partial(_hg_fwd_kernel, **kw),
        grid=(B, nT),
        in_specs=[pl.BlockSpec((None, 1, HD), lambda b, i: (0, 0, 0)),
                  pl.BlockSpec((None, T, HD), lambda b, i: (b, i, 0)),
                  pl.BlockSpec((None, T, H * DV), lambda b, i: (b, i, 1)),
                  pl.BlockSpec((None, T, HD), lambda b, i: (b, i, 2))],
        out_specs=pl.BlockSpec((None, T, H * DV), lambda b, i: (b, i, 0)),
        out_shape=jax.ShapeDtypeStruct((B, L, H * DV), F32),
        scratch_shapes=[pltpu.VMEM((H, DV, DK), F32)],
        compiler_params=_cparams(("parallel", "arbitrary"), blocks, state),
        name="hgrn_fwd",
    )(lb3, z, z, z)
    return pl.pallas_call(
        functools.partial(_hg_bwd_kernel, **kw),
        grid=(B, nT),
        in_specs=[pl.BlockSpec((None, 1, HD), lambda b, i: (1, 0, 0)),
                  pl.BlockSpec((1, DV), lambda b, i: (0, 0)),
                  pl.BlockSpec((None, T, HD), lambda b, i: (b, nT - 1 - i, 0)),
                  pl.BlockSpec((None, T, H * DV), lambda b, i: (b, nT - 1 - i, 1)),
                  pl.BlockSpec((None, T, HD), lambda b, i: (b, nT - 1 - i, 3)),
                  pl.BlockSpec((None, T, H * DV), lambda b, i: (b, nT - 1 - i, 4)),
                  pl.BlockSpec((None, T, H * DV), lambda b, i: (b, nT - 1 - i, 0))],
        out_specs=pl.BlockSpec((None, T, H * DV), lambda b, i: (b, nT - 1 - i, 0)),
        out_shape=jax.ShapeDtypeStruct((B, L, H * DV), BF16),
        scratch_shapes=[pltpu.VMEM((H, DV, DK), F32)],
        compiler_params=_cparams(("parallel", "arbitrary"), blocks, state),
        name="hgrn_bwd",
    )(lb3, gain.reshape(1, DV), z, z, z, z, o1)


def _ret_rope_tables(L, dk):
    inv = ROPE_THETA ** (-np.arange(0, dk, 2) / dk)
    ang = np.arange(L)[:, None] * inv[None, :]
    return jnp.asarray(np.cos(ang), F32), jnp.asarray(np.sin(ang), F32)


def kernel(x_prompt, x_sample, c_prompt, c_sample, ada_w, ada_b, norm_g, hy_w_in, hy_conv_w, hy_conv_b, hy_w1, hy_b1, hy_w2, hy_b2, hy_w3, hy_freq, hy_decay, hy_skip, hy_w_out, ret_w_in, ret_decay, ret_w_out, swa_w_in, swa_q_gain, swa_k_gain, swa_sink, swa_w_out, hg_w_in, hg_lb, hg_gain, hg_w_out, ffn_w_gate, ffn_w_val, ffn_conv_w, ffn_conv_b, ffn_w_down):
    depth, D = norm_g.shape[0], norm_g.shape[2]
    groups = [(x_prompt, c_prompt), (x_sample, c_sample)]
    mods = _ada_mod(jnp.concatenate([c for _, c in groups], axis=0), ada_w, ada_b)

    bf = lambda w: w.astype(BF16)
    hy_w_in_b, hy_w_out_b = bf(hy_w_in), bf(hy_w_out)
    ret_w_in_b, ret_w_out_b = bf(ret_w_in), bf(ret_w_out)
    hg_w_in_b, hg_w_out_b = bf(hg_w_in), bf(hg_w_out)
    wg_b, wv_b, wd_b = bf(ffn_w_gate), bf(ffn_w_val), bf(ffn_w_down)
    hg_sm = jax.nn.softmax(hg_lb.astype(F32), axis=1)
    hg_lower = jnp.cumsum(hg_sm, axis=1) - hg_sm
    ret_log_gamma = -jnp.exp(ret_decay.astype(F32))

    outs = []
    row0 = 0
    for x, c in groups:
        B, L, _ = x.shape
        for layer in range(depth):
            kind, j = layer % N_MIXERS, layer // N_MIXERS
            mod = mods[layer, row0:row0 + B].reshape(B, N_MOD, 1, D)
            sh1, sc1, g1, sh2, sc2, g2 = (mod[:, m] for m in range(N_MOD))
            gn1 = norm_g[layer, 0].reshape(1, D)
            gn2 = norm_g[layer, 1].reshape(1, D)
            if kind == 0:
                tabs = _fft_tables(L)
                z = _proj_in_conv(x, gn1, sc1, sh1, hy_w_in_b[j], hy_conv_w[j], hy_conv_b[j], tn=1024)
                taps, asum = _hy_filter_taps(L, hy_w1[j], hy_b1[j], hy_w2[j], hy_b2[j], hy_w3[j], hy_freq[j],
                                             hy_decay[j])
                spec = _hy_spectrum(taps, asum, tabs)
                a = _hy_conv(z, hy_skip[j], spec, tabs)
                x = _proj_out(a, hy_w_out_b[j], x, g1)
            elif kind == 1:
                z = _proj_in(x, gn1, sc1, sh1, ret_w_in_b[j], tn=1024)
                cos, sin = _ret_rope_tables(L, D // RET_HEADS)
                a = _retention(z, ret_log_gamma[j], cos, sin)
                x = _proj_out(a, ret_w_out_b[j], x, g1)
            elif kind == 2:
                w, head_gain, flag, wo, cos, sin = _swa_layout(swa_w_in[j], swa_q_gain[j], swa_k_gain[j],
                                                               swa_w_out[j], L)
                z = _proj_in_rope(x, gn1, sc1, sh1, bf(w), head_gain, flag, cos, sin, tn=512, dh=SWA_DH)
                a = _swa_attention(z, swa_sink[j])
                x = _proj_out(a, bf(wo), x, g1)
            else:
                z = _proj_in(x, gn1, sc1, sh1, hg_w_in_b[j], tn=1024)
                a = _hgrn(z, hg_lower[:, layer], hg_gain[j])
                x = _proj_out(a, hg_w_out_b[j], x, g1)
            x = _ffn(x, gn2, sc2, sh2, g2, wg_b[layer], wv_b[layer], ffn_conv_w[layer], ffn_conv_b[layer],
                     wd_b[layer])
        outs.append(x)
        row0 += B
    return tuple(outs)
```

```python
import functools
import math

import numpy as np
import jax
import jax.numpy as jnp
from jax import lax
from jax.experimental import pallas as pl
from jax.experimental.pallas import tpu as pltpu

F32 = jnp.float32
BF16 = jnp.bfloat16

NORM_EPS = 1e-6
N_MIXERS = 4
N_MOD = 6
HY_BANDS = 16
RET_HEADS = 4
RET_CHUNK = 128
SWA_HQ = 16
SWA_HKV = 4
SWA_DH = 64
WINDOW = 128
ATTN_BLOCK = 128
ROPE_THETA = 10000.0
NEG_INF = -1e30
HG_HEADS = 8
HG_CHUNK = 64

LANES = 128
SUBLANES = 8
BF16_ROWS = 16
VMEM_LIMIT_CAP = 60 * 1024 * 1024
VMEM_SLACK = 8 * 1024 * 1024

NT_DIMS = (((1,), (1,)), ((), ()))
TN_DIMS = (((0,), (0,)), ((), ()))


def _nbytes(shape, dtype):
    item = jnp.dtype(dtype).itemsize
    sub = SUBLANES * 4 // item
    dims = list(shape)
    dims[-1] = -(-dims[-1] // LANES) * LANES
    if len(dims) > 1:
        dims[-2] = -(-dims[-2] // sub) * sub
    return int(np.prod(dims)) * item


def _cparams(semantics, pipelined, resident):
    need = 2 * sum(_nbytes(s, d) for s, d in pipelined) + sum(_nbytes(s, d) for s, d in resident)
    return pltpu.CompilerParams(dimension_semantics=semantics,
                                vmem_limit_bytes=min(need + VMEM_SLACK, VMEM_LIMIT_CAP))


def _dot(a, b):
    return jnp.dot(a, b, preferred_element_type=F32)


def _split(x):
    hi = x.astype(BF16)
    lo = (x - hi.astype(F32)).astype(BF16)
    return hi, lo


def _dot_tab(t_cat, d):
    d_hi, d_lo = _split(d)
    return _dot(t_cat, jnp.concatenate([d_hi, d_hi, d_lo], axis=0))


def _dot3(a, b):
    a_hi, a_lo = _split(a)
    b_hi, b_lo = _split(b)
    return _dot(a_hi, b_hi) + _dot(a_lo, b_hi) + _dot(a_hi, b_lo)


def _cumdot(tri, g):
    g0 = g.astype(BF16)
    r1 = g - g0.astype(F32)
    g1 = r1.astype(BF16)
    g2 = (r1 - g1.astype(F32)).astype(BF16)
    return _dot(tri, g0) + _dot(tri, g1) + _dot(tri, g2)


def _silu(x):
    return x * jax.nn.sigmoid(x)


def _norm_mod(x, g, sc, sh):
    xf = x.astype(F32)
    y = xf * lax.rsqrt(jnp.mean(xf * xf, axis=-1, keepdims=True) + NORM_EPS)
    return (y * g) * (1.0 + sc) + sh


def _row_tile(L):
    return min(L, 1024)


def _ada_kernel(c_ref, w_ref, b_ref, o_ref):
    cs = _silu(c_ref[...]).astype(BF16)
    o_ref[...] = _dot(cs, w_ref[...].astype(BF16)) + b_ref[...]


def _ada_mod(c_all, ada_w, ada_b):
    depth, D, N = ada_w.shape
    R = c_all.shape[0]
    tn = 1024
    return pl.pallas_call(
        _ada_kernel,
        grid=(depth, N // tn),
        in_specs=[pl.BlockSpec((R, D), lambda l, j: (0, 0)),
                  pl.BlockSpec((None, D, tn), lambda l, j: (l, 0, j)),
                  pl.BlockSpec((None, 1, tn), lambda l, j: (l, 0, j))],
        out_specs=pl.BlockSpec((None, R, tn), lambda l, j: (l, 0, j)),
        out_shape=jax.ShapeDtypeStruct((depth, R, N), F32),
        compiler_params=_cparams(("parallel", "parallel"),
                                 [((D, tn), F32), ((R, D), F32), ((R, tn), F32)], []),
        name="ada_mod",
    )(c_all, ada_w, ada_b.reshape(depth, 1, N))


def _fill_h(h_ref, x_ref, g_ref, sc_ref, sh_ref):
    h_ref[...] = _norm_mod(x_ref[...], g_ref[...], sc_ref[...], sh_ref[...]).astype(BF16)


def _fill_h_halo(h_ref, x_ref, xp_ref, xn_ref, g_ref, sc_ref, sh_ref, tm):
    i = pl.program_id(1)
    last = pl.num_programs(1) - 1
    g, sc, sh = g_ref[...], sc_ref[...], sh_ref[...]
    h_ref[BF16_ROWS:BF16_ROWS + tm, :] = _norm_mod(x_ref[...], g, sc, sh).astype(BF16)
    hp = _norm_mod(xp_ref[...], g, sc, sh)
    hn = _norm_mod(xn_ref[...], g, sc, sh)
    h_ref[0:BF16_ROWS, :] = jnp.where(i > 0, hp, 0.0).astype(BF16)
    h_ref[BF16_ROWS + tm:, :] = jnp.where(i < last, hn, 0.0).astype(BF16)


def _conv3_rows(z_ref, cw, cb, tm):
    o = BF16_ROWS
    return (z_ref[o - 1:o - 1 + tm, :] * cw[0:1, :] + z_ref[o:o + tm, :] * cw[1:2, :]
            + z_ref[o + 1:o + 1 + tm, :] * cw[2:3, :] + cb)


def _proj_plain_kernel(x_ref, g_ref, sc_ref, sh_ref, w_ref, o_ref, h_ref):
    @pl.when(pl.program_id(2) == 0)
    def _():
        _fill_h(h_ref, x_ref, g_ref, sc_ref, sh_ref)

    o_ref[...] = _dot(h_ref[...], w_ref[...]).astype(o_ref.dtype)


def _proj_conv_kernel(x_ref, xp_ref, xn_ref, g_ref, sc_ref, sh_ref, w_ref, cw_ref, cb_ref,
                      o_ref, h_ref, z_ref, *, tm):
    @pl.when(pl.program_id(2) == 0)
    def _():
        _fill_h_halo(h_ref, x_ref, xp_ref, xn_ref, g_ref, sc_ref, sh_ref, tm)

    z_ref[...] = _dot(h_ref[...], w_ref[...])
    o_ref[...] = _conv3_rows(z_ref, cw_ref[...], cb_ref[...], tm).astype(o_ref.dtype)


def _proj_rope_kernel(x_ref, g_ref, sc_ref, sh_ref, w_ref, flag_ref, ones_ref, cos_ref, sin_ref,
                      o_ref, h_ref, *, dh):
    @pl.when(pl.program_id(2) == 0)
    def _():
        _fill_h(h_ref, x_ref, g_ref, sc_ref, sh_ref)

    z = _dot(h_ref[...], w_ref[...])
    ms = _dot((z * z).astype(BF16), ones_ref[...]) * (1.0 / dh)
    zn = z * jnp.where(flag_ref[...] > 0.0, lax.rsqrt(ms + NORM_EPS), 1.0)
    cos, sin = cos_ref[...], sin_ref[...]
    for s in range(z.shape[1] // LANES):
        sl = slice(s * LANES, (s + 1) * LANES)
        zs = zn[:, sl]
        o_ref[:, sl] = (zs * cos + pltpu.roll(zs, LANES // 2, axis=1) * sin).astype(o_ref.dtype)


def _mod_specs(D):
    return [pl.BlockSpec((1, D), lambda b, i, j: (0, 0)),
            pl.BlockSpec((None, 1, D), lambda b, i, j: (b, 0, 0)),
            pl.BlockSpec((None, 1, D), lambda b, i, j: (b, 0, 0))]


def _halo_specs(tm, D, L):
    hb = tm // BF16_ROWS
    nhb = L // BF16_ROWS
    return [pl.BlockSpec((None, tm, D), lambda b, i, j: (b, i, 0)),
            pl.BlockSpec((None, BF16_ROWS, D), lambda b, i, j: (b, jnp.maximum(i * hb - 1, 0), 0)),
            pl.BlockSpec((None, BF16_ROWS, D), lambda b, i, j: (b, jnp.minimum((i + 1) * hb, nhb - 1), 0))]


def _proj_in(x, g, sc, sh, w, *, tn):
    B, L, D = x.shape
    N = w.shape[1]
    tm = _row_tile(L)
    return pl.pallas_call(
        _proj_plain_kernel,
        grid=(B, L // tm, N // tn),
        in_specs=[pl.BlockSpec((None, tm, D), lambda b, i, j: (b, i, 0))] + _mod_specs(D)
        + [pl.BlockSpec((D, tn), lambda b, i, j: (0, j))],
        out_specs=pl.BlockSpec((None, tm, tn), lambda b, i, j: (b, i, j)),
        out_shape=jax.ShapeDtypeStruct((B, L, N), BF16),
        scratch_shapes=[pltpu.VMEM((tm, D), BF16)],
        compiler_params=_cparams(("parallel", "parallel", "arbitrary"),
                                 [((tm, D), F32), ((D, tn), BF16), ((tm, tn), BF16)],
                                 [((tm, D), BF16), ((tm, tn), F32)]),
        name="proj_in",
    )(x, g, sc, sh, w)


def _proj_in_conv(x, g, sc, sh, w, cw, cb, *, tn):
    B, L, D = x.shape
    N = w.shape[1]
    tm = _row_tile(L)
    te = tm + 2 * BF16_ROWS
    return pl.pallas_call(
        functools.partial(_proj_conv_kernel, tm=tm),
        grid=(B, L // tm, N // tn),
        in_specs=_halo_specs(tm, D, L) + _mod_specs(D)
        + [pl.BlockSpec((D, tn), lambda b, i, j: (0, j)),
           pl.BlockSpec((3, tn), lambda b, i, j: (0, j)),
           pl.BlockSpec((1, tn), lambda b, i, j: (0, j))],
        out_specs=pl.BlockSpec((None, tm, tn), lambda b, i, j: (b, i, j)),
        out_shape=jax.ShapeDtypeStruct((B, L, N), BF16),
        scratch_shapes=[pltpu.VMEM((te, D), BF16), pltpu.VMEM((te, tn), F32)],
        compiler_params=_cparams(("parallel", "parallel", "arbitrary"),
                                 [((tm, D), F32), ((D, tn), BF16), ((tm, tn), BF16)],
                                 [((te, D), BF16), ((te, tn), F32), ((te, tn), F32)]),
        name="proj_in_conv",
    )(x, x, x, g, sc, sh, w, cw, cb.reshape(1, N))


def _proj_in_rope(x, g, sc, sh, w, flag, cos, sin, *, tn, dh, q_tiles):
    B, L, D = x.shape
    N = w.shape[1]
    tm = _row_tile(L)
    slot = lax.broadcasted_iota(jnp.int32, (tn, tn), 0) // LANES == lax.broadcasted_iota(jnp.int32, (tn, tn), 1) // LANES
    table = lambda b, i, j: (jnp.maximum(j - (q_tiles - 1), 0), i, 0)
    return pl.pallas_call(
        functools.partial(_proj_rope_kernel, dh=dh),
        grid=(B, L // tm, N // tn),
        in_specs=[pl.BlockSpec((None, tm, D), lambda b, i, j: (b, i, 0))] + _mod_specs(D)
        + [pl.BlockSpec((D, tn), lambda b, i, j: (0, j)),
           pl.BlockSpec((1, tn), lambda b, i, j: (0, j)),
           pl.BlockSpec((tn, tn), lambda b, i, j: (0, 0)),
           pl.BlockSpec((None, tm, LANES), table),
           pl.BlockSpec((None, tm, LANES), table)],
        out_specs=pl.BlockSpec((None, tm, tn), lambda b, i, j: (b, i, j)),
        out_shape=jax.ShapeDtypeStruct((B, L, N), BF16),
        scratch_shapes=[pltpu.VMEM((tm, D), BF16)],
        compiler_params=_cparams(("parallel", "parallel", "arbitrary"),
                                 [((tm, D), F32), ((D, tn), BF16), ((tm, tn), BF16), ((tm, LANES), F32),
                                  ((tm, LANES), F32), ((tn, tn), BF16)],
                                 [((tm, D), BF16), ((tm, tn), F32), ((tm, tn), F32), ((tm, tn), F32)]),
        name="proj_in_rope",
    )(x, g, sc, sh, w, flag, slot.astype(BF16), cos, sin)


def _proj_out_kernel(a_ref, w_ref, x_ref, gate_ref, o_ref):
    o_ref[...] = x_ref[...] + gate_ref[...] * _dot(a_ref[...], w_ref[...])


def _proj_out(a, w, x, gate):
    B, L, K = a.shape
    D = w.shape[1]
    tm = _row_tile(L)
    return pl.pallas_call(
        _proj_out_kernel,
        grid=(B, L // tm),
        in_specs=[pl.BlockSpec((None, tm, K), lambda b, i: (b, i, 0)),
                  pl.BlockSpec((K, D), lambda b, i: (0, 0)),
                  pl.BlockSpec((None, tm, D), lambda b, i: (b, i, 0)),
                  pl.BlockSpec((None, 1, D), lambda b, i: (b, 0, 0))],
        out_specs=pl.BlockSpec((None, tm, D), lambda b, i: (b, i, 0)),
        out_shape=jax.ShapeDtypeStruct((B, L, D), F32),
        compiler_params=_cparams(("parallel", "parallel"),
                                 [((tm, K), BF16), ((K, D), BF16), ((tm, D), F32), ((tm, D), F32)],
                                 [((tm, D), F32)]),
        name="proj_out",
    )(a, w, x, gate)


def _ffn_kernel(x_ref, xp_ref, xn_ref, g_ref, sc_ref, sh_ref, gate_ref, wg_ref, wv_ref, cw_ref, cb_ref,
                wd_ref, o_ref, h_ref, z_ref, *, tm, tf):
    _fill_h_halo(h_ref, x_ref, xp_ref, xn_ref, g_ref, sc_ref, sh_ref, tm)
    acc = None
    for c in range(wg_ref.shape[1] // tf):
        cols = slice(c * tf, (c + 1) * tf)
        z_ref[...] = _dot(h_ref[...], wg_ref[:, cols])
        a = _conv3_rows(z_ref, cw_ref[:, cols], cb_ref[:, cols], tm)
        v = _dot(h_ref[BF16_ROWS:BF16_ROWS + tm, :], wv_ref[:, cols])
        part = _dot((_silu(a) * v).astype(BF16), wd_ref[cols, :])
        acc = part if acc is None else acc + part
    o_ref[...] = x_ref[...] + gate_ref[...] * acc


def _ffn(x, g, sc, sh, gate, wg, wv, cw, cb, wd):
    B, L, D = x.shape
    F = wg.shape[1]
    tm = min(L, 512)
    te = tm + 2 * BF16_ROWS
    tf = F // 2
    const = lambda b, i, j: (0, 0)
    resident = lambda shape: pl.BlockSpec(shape, const, pipeline_mode=pl.Buffered(1))
    return pl.pallas_call(
        functools.partial(_ffn_kernel, tm=tm, tf=tf),
        grid=(B, L // tm, 1),
        in_specs=_halo_specs(tm, D, L) + _mod_specs(D)
        + [pl.BlockSpec((None, 1, D), lambda b, i, j: (b, 0, 0)),
           resident((D, F)), resident((D, F)), resident((3, F)), resident((1, F)), resident((F, D))],
        out_specs=pl.BlockSpec((None, tm, D), lambda b, i, j: (b, i, 0)),
        out_shape=jax.ShapeDtypeStruct((B, L, D), F32),
        scratch_shapes=[pltpu.VMEM((te, D), BF16), pltpu.VMEM((te, tf), F32)],
        compiler_params=_cparams(("parallel", "parallel", "arbitrary"),
                                 [((tm, D), F32), ((tm, D), F32)],
                                 [((D, F), BF16), ((D, F), BF16), ((F, D), BF16), ((te, D), BF16),
                                  ((te, tf), F32), ((tm, tf), F32), ((tm, tf), F32), ((tm, tf), BF16),
                                  ((tm, D), F32), ((tm, D), F32)]),
        name="ffn",
    )(x, x, x, g, sc, sh, gate, wg, wv, cw, cb.reshape(1, F), wd)


def _fft_dims(L):
    N = 2 * L
    p = N.bit_length() - 1
    n1 = 1 << ((p + 1) // 2)
    return N, n1, N // n1


def _k1_pad(n1):
    return n1 // 2 + 8


def _fft_tables(L):
    N, n1, n2 = _fft_dims(L)
    k1p = _k1_pad(n1)
    k1 = np.arange(k1p)
    valid = (k1 <= n1 // 2).astype(np.float64)
    weight = np.where((k1 == 0) | (k1 == n1 // 2), 1.0, 2.0) * valid
    m1 = np.arange(n1 // 2)
    n2v = np.arange(n2)
    n_idx = n2 * m1[None, :] + n2v[:, None]
    phi = 2.0 * np.pi * k1[None, :, None] * n_idx[:, None, :] / N
    s1 = np.concatenate([np.cos(phi) * valid[None, :, None], -np.sin(phi) * valid[None, :, None]], axis=1)
    phit = np.transpose(phi, (0, 2, 1))
    gl = np.concatenate([np.cos(phit) * weight[None, None, :], -np.sin(phit) * weight[None, None, :]],
                        axis=2) / N
    th = 2.0 * np.pi * np.outer(n2v, n2v) / n2
    c, s = np.cos(th), np.sin(th)
    f2 = np.block([[c, s], [-s, c]])
    f2c = np.block([[c, -s], [s, c]])

    def cat(a):
        a32 = jnp.asarray(a, F32)
        hi = a32.astype(BF16)
        lo = (a32 - hi.astype(F32)).astype(BF16)
        return jnp.concatenate([hi, lo, hi], axis=-1)

    return dict(N=N, n1=n1, n2=n2, k1p=k1p, s1=cat(s1), gl=cat(gl), f2=cat(f2), f2c=cat(f2c))


def _hy_positions(L):
    idx = np.concatenate([np.arange(L), np.array([0]), np.arange(L - 1, 0, -1)])
    t = idx / (L - 1)
    ang = (2.0 * np.pi / L) * idx
    bands = np.linspace(1e-4, HY_BANDS - 1, HY_BANDS)
    z = np.concatenate([t[:, None], np.cos(bands[None, :] * ang[:, None]), -np.sin(bands[None, :] * ang[:, None])],
                       axis=1)
    zp = np.zeros((2 * L, LANES))
    zp[:, : z.shape[1]] = z
    return jnp.asarray(zp, F32)


def _hy_mlp_kernel(z_ref, w1_ref, b1_ref, w2_ref, b2_ref, w3_ref, fr_ref, dec_ref, taps_ref, asum_ref, *, L, tr):
    i = pl.program_id(0)
    z = z_ref[...]
    fr = fr_ref[...]
    h = jnp.sin(fr * (_dot3(z, w1_ref[...]) + b1_ref[...]))
    h = jnp.sin(fr * (_dot3(h, w2_ref[...]) + b2_ref[...]))
    h = _dot3(h, w3_ref[...])
    t = z[:, 0:1]
    taps = h * jnp.exp(-t * jnp.abs(dec_ref[...]))
    row = i * tr + lax.broadcasted_iota(jnp.int32, (tr, 1), 0)
    taps = jnp.where(row == L, 0.0, taps)
    taps_ref[...] = taps

    @pl.when(i == 0)
    def _():
        asum_ref[...] = jnp.zeros_like(asum_ref)

    asum_ref[...] += jnp.sum(jnp.abs(taps), axis=0, keepdims=True)


def _hy_filter_taps(L, w1, b1, w2, b2, w3, freq, decay):
    W = decay.shape[1]
    E, O = w1.shape
    tr = 512
    half = L // tr
    z = _hy_positions(L)
    w1p = jnp.zeros((LANES, O), F32).at[:E].set(w1)
    return pl.pallas_call(
        functools.partial(_hy_mlp_kernel, L=L, tr=tr),
        grid=(2 * L // tr,),
        in_specs=[pl.BlockSpec((tr, LANES), lambda i: (i, 0)),
                  pl.BlockSpec((LANES, O), lambda i: (0, 0)),
                  pl.BlockSpec((1, O), lambda i: (0, 0)),
                  pl.BlockSpec((O, O), lambda i: (0, 0)),
                  pl.BlockSpec((1, O), lambda i: (0, 0)),
                  pl.BlockSpec((O, W), lambda i: (0, i // half)),
                  pl.BlockSpec((1, O), lambda i: (0, 0)),
                  pl.BlockSpec((None, 1, W), lambda i: (i // half, 0, 0))],
        out_specs=[pl.BlockSpec((tr, W), lambda i: (i, 0)),
                   pl.BlockSpec((1, W), lambda i: (0, 0))],
        out_shape=[jax.ShapeDtypeStruct((2 * L, W), F32), jax.ShapeDtypeStruct((1, W), F32)],
        compiler_params=_cparams(("arbitrary",), [((tr, W), F32), ((O, W), F32), ((tr, LANES), F32)],
                                 [((tr, W), F32), ((tr, W), F32)]),
        name="hyena_filter_mlp",
    )(z, w1p, b1.reshape(1, O), w2, b2.reshape(1, O), w3, freq.reshape(1, O), decay.reshape(2, 1, W))


FFT_UNROLL = 4


def _fft_stage1(src_ref, src_off, a_ref, s1_ref, n1, n2, k1p, sign=None, src_off2=None):
    def body(j, carry):
        d = src_ref[pl.ds(src_off + j, n1 // 2, stride=n2), :]
        r = _dot_tab(s1_ref[j], d)
        if src_off2 is not None:
            d2 = src_ref[pl.ds(src_off2 + j, n1 // 2, stride=n2), :]
            r = r + sign * _dot_tab(s1_ref[j], d2)
        a_ref[pl.ds(pl.multiple_of(j * 2 * k1p, 8), 2 * k1p), :] = r
        return carry

    lax.fori_loop(0, n2, body, 0, unroll=FFT_UNROLL)


def _load_k1(a_ref, k, n2, k1p):
    br = a_ref[pl.ds(k, n2, stride=2 * k1p), :]
    bi = a_ref[pl.ds(k1p + k, n2, stride=2 * k1p), :]
    return jnp.concatenate([br, bi], axis=0)


def _hy_spec_kernel(taps_ref, asum_ref, s1_ref, f2_ref, h_ref, a_ref, *, L, n1, n2, k1p):
    r = lax.broadcasted_iota(jnp.int32, (2 * k1p, 1), 0)
    k1 = jnp.where(r < k1p, r, r - k1p)
    sign = (1 - 2 * (k1 & 1)).astype(F32)
    _fft_stage1(taps_ref, 0, a_ref, s1_ref, n1, n2, k1p, sign=sign, src_off2=L)
    inv = 1.0 / asum_ref[...]

    def body(k, carry):
        h_ref[k] = _dot_tab(f2_ref[...], _load_k1(a_ref, k, n2, k1p)) * inv
        return carry

    lax.fori_loop(0, n1 // 2, body, 0, unroll=2)
    body(n1 // 2, 0)


def _hy_spectrum(taps, asum, tabs):
    n1, n2, k1p = tabs["n1"], tabs["n2"], tabs["k1p"]
    N, W = taps.shape
    L = N // 2
    cb = LANES
    k1v = n1 // 2 + 1
    s1, f2 = tabs["s1"], tabs["f2"]
    return pl.pallas_call(
        functools.partial(_hy_spec_kernel, L=L, n1=n1, n2=n2, k1p=k1p),
        grid=(W // cb,),
        in_specs=[pl.BlockSpec((N, cb), lambda c: (0, c)),
                  pl.BlockSpec((1, cb), lambda c: (0, c)),
                  pl.BlockSpec(s1.shape, lambda c: (0, 0, 0), pipeline_mode=pl.Buffered(1)),
                  pl.BlockSpec(f2.shape, lambda c: (0, 0), pipeline_mode=pl.Buffered(1))],
        out_specs=pl.BlockSpec((k1v, 2 * n2, cb), lambda c: (0, 0, c)),
        out_shape=jax.ShapeDtypeStruct((k1v, 2 * n2, W), F32),
        scratch_shapes=[pltpu.VMEM((n2 * 2 * k1p, cb), F32)],
        compiler_params=_cparams(("parallel",),
                                 [((N, cb), F32), ((k1v, 2 * n2, cb), F32)],
                                 [((n2 * 2 * k1p, cb), F32), (s1.shape, BF16), (f2.shape, BF16)]),
        name="hyena_filter_spectrum",
    )(taps, asum, s1, f2)


def _hy_conv_kernel(x0_ref, x1_ref, v_ref, skip_ref, h_ref, s1_ref, f2_ref, f2c_ref, gl_ref,
                    o_ref, u_ref, a_ref, *, n1, n2, k1p):
    u_ref[...] = x1_ref[...].astype(F32) * v_ref[...].astype(F32)
    _fft_stage1(u_ref, 0, a_ref, s1_ref, n1, n2, k1p)

    def mid(k, carry):
        x = _dot_tab(f2_ref[...], _load_k1(a_ref, k, n2, k1p))
        xr, xi = x[:n2], x[n2:]
        hr, hi = h_ref[k, :n2, :], h_ref[k, n2:, :]
        y = jnp.concatenate([xr * hr - xi * hi, xr * hi + xi * hr], axis=0)
        c = _dot_tab(f2c_ref[...], y)
        a_ref[pl.ds(k, n2, stride=2 * k1p), :] = c[:n2]
        a_ref[pl.ds(k1p + k, n2, stride=2 * k1p), :] = c[n2:]
        return carry

    lax.fori_loop(0, n1 // 2, mid, 0, unroll=2)
    mid(n1 // 2, 0)
    skip = skip_ref[...]

    def last(j, carry):
        rhs = a_ref[pl.ds(pl.multiple_of(j * 2 * k1p, 8), 2 * k1p), :]
        y = _dot_tab(gl_ref[j], rhs)
        rows = pl.ds(j, n1 // 2, stride=n2)
        u_ref[rows, :] = y + u_ref[rows, :] * skip
        return carry

    lax.fori_loop(0, n2, last, 0, unroll=FFT_UNROLL)
    o_ref[...] = (u_ref[...].astype(F32) * x0_ref[...].astype(F32)).astype(o_ref.dtype)


def _hy_conv(z, skip, spec, tabs):
    n1, n2, k1p = tabs["n1"], tabs["n2"], tabs["k1p"]
    B, L, W3 = z.shape
    W = W3 // 3
    cb = LANES
    nc = W // cb
    k1v = n1 // 2 + 1
    tables = [tabs["s1"], tabs["f2"], tabs["f2c"], tabs["gl"]]

    def const_spec(t):
        return pl.BlockSpec(t.shape, (lambda c, b: (0, 0, 0)) if t.ndim == 3 else (lambda c, b: (0, 0)),
                            pipeline_mode=pl.Buffered(1))

    return pl.pallas_call(
        functools.partial(_hy_conv_kernel, n1=n1, n2=n2, k1p=k1p),
        grid=(nc, B),
        in_specs=[pl.BlockSpec((None, L, cb), lambda c, b: (b, 0, c), pipeline_mode=pl.Buffered(1)),
                  pl.BlockSpec((None, L, cb), lambda c, b: (b, 0, nc + c), pipeline_mode=pl.Buffered(1)),
                  pl.BlockSpec((None, L, cb), lambda c, b: (b, 0, 2 * nc + c), pipeline_mode=pl.Buffered(1)),
                  pl.BlockSpec((1, cb), lambda c, b: (0, c)),
                  pl.BlockSpec((k1v, 2 * n2, cb), lambda c, b: (0, 0, c), pipeline_mode=pl.Buffered(1))]
        + [const_spec(t) for t in tables],
        out_specs=pl.BlockSpec((None, L, cb), lambda c, b: (b, 0, c)),
        out_shape=jax.ShapeDtypeStruct((B, L, W), BF16),
        scratch_shapes=[pltpu.VMEM((L, cb), F32), pltpu.VMEM((n2 * 2 * k1p, cb), F32)],
        compiler_params=_cparams(("parallel", "arbitrary"),
                                 [((L, cb), BF16)],
                                 [((L, cb), BF16)] * 3
                                 + [((L, cb), F32), ((n2 * 2 * k1p, cb), F32), ((k1v, 2 * n2, cb), F32)]
                                 + [(t.shape, BF16) for t in tables]),
        name="hyena_long_conv",
    )(z, z, z, skip.reshape(1, W), spec, *tables)


def _rope_half(x, cos, sin):
    half = x.shape[1] // 2
    x1, x2 = x[:, :half], x[:, half:]
    return jnp.concatenate([x1 * cos - x2 * sin, x1 * sin + x2 * cos], axis=1)


def _ret_fwd_kernel(lg_ref, q_ref, k_ref, v_ref, cos_ref, sin_ref, o_ref, s_ref, *, T, C, H, DK, DV):
    @pl.when(pl.program_id(1) == 0)
    def _():
        s_ref[...] = jnp.zeros_like(s_ref)

    row = lax.broadcasted_iota(jnp.int32, (C, C), 0)
    col = lax.broadcasted_iota(jnp.int32, (C, C), 1)
    rel = (row - col).astype(F32)
    ridx = lax.broadcasted_iota(jnp.int32, (C, 1), 0).astype(F32)
    for h in range(H):
        lgf = lg_ref[0, h]
        lgb = lg_ref[1, h]
        dmat = jnp.where(rel > 0.0, jnp.exp(lgf * jnp.maximum(rel, 0.0)),
                         jnp.where(rel < 0.0, jnp.exp(lgb * jnp.maximum(-rel, 0.0)), 2.0))
        q_scale = jnp.exp(lgf * (ridx + 1.0))
        k_scale = jnp.exp(lgf * (C - 1.0 - ridx))
        s_decay = jnp.exp(jnp.full((1, 1), C, F32) * lgf)
        ck = slice(h * DK, (h + 1) * DK)
        cv = slice(h * DV, (h + 1) * DV)

        def chunk(c, carry):
            rows = pl.ds(pl.multiple_of(c * C, C), C)
            cos, sin = cos_ref[rows, :], sin_ref[rows, :]
            q = _rope_half(q_ref[rows, ck].astype(F32), cos, sin)
            k = _rope_half(k_ref[rows, ck].astype(F32), cos, sin) * (DK ** -0.5)
            v = v_ref[rows, cv]
            s = lax.dot_general(q.astype(BF16), k.astype(BF16), NT_DIMS, preferred_element_type=F32) * dmat
            st = s_ref[h]
            o_ref[rows, cv] = _dot(s.astype(BF16), v) + _dot((q * q_scale).astype(BF16), st.astype(BF16))
            s_ref[h] = s_decay * st + lax.dot_general((k * k_scale).astype(BF16), v, TN_DIMS,
                                                      preferred_element_type=F32)
            return carry

        lax.fori_loop(0, T // C, chunk, 0)


def _ret_bwd_kernel(lg_ref, q_ref, k_ref, v_ref, g_ref, cos_ref, sin_ref, o1_ref, o_ref, s_ref,
                    *, T, C, H, DK, DV):
    @pl.when(pl.program_id(1) == 0)
    def _():
        s_ref[...] = jnp.zeros_like(s_ref)

    ridx = lax.broadcasted_iota(jnp.int32, (C, 1), 0).astype(F32)
    nc = T // C
    for h in range(H):
        lgb = lg_ref[1, h]
        q_scale = jnp.exp(lgb * (C - ridx))
        k_scale = jnp.exp(lgb * ridx)
        s_decay = jnp.exp(jnp.full((1, 1), C, F32) * lgb)
        ck = slice(h * DK, (h + 1) * DK)
        cv = slice(h * DV, (h + 1) * DV)

        def chunk(cc, carry):
            rows = pl.ds(pl.multiple_of((nc - 1 - cc) * C, C), C)
            cos, sin = cos_ref[rows, :], sin_ref[rows, :]
            q = _rope_half(q_ref[rows, ck].astype(F32), cos, sin)
            k = _rope_half(k_ref[rows, ck].astype(F32), cos, sin) * (DK ** -0.5)
            v = v_ref[rows, cv]
            st = s_ref[h]
            o = o1_ref[rows, cv] + _dot((q * q_scale).astype(BF16), st.astype(BF16))
            y = o * lax.rsqrt(jnp.mean(o * o, axis=-1, keepdims=True) + NORM_EPS)
            o_ref[rows, cv] = (y * _silu(g_ref[rows, cv].astype(F32))).astype(o_ref.dtype)
            s_ref[h] = s_decay * st + lax.dot_general((k * k_scale).astype(BF16), v, TN_DIMS,
                                                      preferred_element_type=F32)
            return carry

        lax.fori_loop(0, nc, chunk, 0)


def _retention(z, log_gamma, cos, sin):
    B, L, _ = z.shape
    H, C = RET_HEADS, RET_CHUNK
    DK = cos.shape[1] * 2
    DV = 2 * DK
    T = min(L, 512)
    nT = L // T
    kw = dict(T=T, C=C, H=H, DK=DK, DV=DV)
    smem = pl.BlockSpec(memory_space=pltpu.SMEM)
    blocks = [((T, H * DK), BF16)] * 2 + [((T, H * DV), BF16)] * 2 + [((T, H * DV), F32)] * 2
    state = [((H, DK, DV), F32), ((DK, DV), F32), ((DK, DV), F32)]
    o1 = pl.pallas_call(
        functools.partial(_ret_fwd_kernel, **kw),
        grid=(B, nT),
        in_specs=[smem,
                  pl.BlockSpec((None, T, H * DK), lambda b, i: (b, i, 0)),
                  pl.BlockSpec((None, T, H * DK), lambda b, i: (b, i, 1)),
                  pl.BlockSpec((None, T, H * DV), lambda b, i: (b, i, 1)),
                  pl.BlockSpec((T, DK // 2), lambda b, i: (i, 0)),
                  pl.BlockSpec((T, DK // 2), lambda b, i: (i, 0))],
        out_specs=pl.BlockSpec((None, T, H * DV), lambda b, i: (b, i, 0)),
        out_shape=jax.ShapeDtypeStruct((B, L, H * DV), F32),
        scratch_shapes=[pltpu.VMEM((H, DK, DV), F32)],
        compiler_params=_cparams(("parallel", "arbitrary"), blocks, state),
        name="retention_fwd",
    )(log_gamma, z, z, z, cos, sin)
    return pl.pallas_call(
        functools.partial(_ret_bwd_kernel, **kw),
        grid=(B, nT),
        in_specs=[smem,
                  pl.BlockSpec((None, T, H * DK), lambda b, i: (b, nT - 1 - i, 0)),
                  pl.BlockSpec((None, T, H * DK), lambda b, i: (b, nT - 1 - i, 1)),
                  pl.BlockSpec((None, T, H * DV), lambda b, i: (b, nT - 1 - i, 1)),
                  pl.BlockSpec((None, T, H * DV), lambda b, i: (b, nT - 1 - i, 2)),
                  pl.BlockSpec((T, DK // 2), lambda b, i: (nT - 1 - i, 0)),
                  pl.BlockSpec((T, DK // 2), lambda b, i: (nT - 1 - i, 0)),
                  pl.BlockSpec((None, T, H * DV), lambda b, i: (b, nT - 1 - i, 0))],
        out_specs=pl.BlockSpec((None, T, H * DV), lambda b, i: (b, nT - 1 - i, 0)),
        out_shape=jax.ShapeDtypeStruct((B, L, H * DV), BF16),
        scratch_shapes=[pltpu.VMEM((H, DK, DV), F32)],
        compiler_params=_cparams(("parallel", "arbitrary"), blocks, state),
        name="retention_bwd",
    )(log_gamma, z, z, z, z, cos, sin, o1)


def _swa_kernel(sink_ref, q_ref, kp_ref, kc_ref, kn_ref, vp_ref, vc_ref, vn_ref, o_ref, *, HKV, G, BLK):
    i = pl.program_id(1)
    last = pl.num_programs(1) - 1
    R = G * BLK
    r = lax.broadcasted_iota(jnp.int32, (R, 3 * BLK), 0) % BLK
    c = lax.broadcasted_iota(jnp.int32, (R, 3 * BLK), 1)
    rel = r - (c - BLK)
    lo = jnp.where(i == 0, BLK, 0)
    hi = jnp.where(i == last, 2 * BLK, 3 * BLK)
    valid = (jnp.abs(rel) <= WINDOW) & (c >= lo) & (c < hi)
    grp = lax.broadcasted_iota(jnp.int32, (R, 1), 0) // BLK
    for j in range(HKV):
        sl = slice(j * LANES, (j + 1) * LANES)
        k = jnp.concatenate([kp_ref[:, sl], kc_ref[:, sl], kn_ref[:, sl]], axis=0)
        v = jnp.concatenate([vp_ref[:, sl], vc_ref[:, sl], vn_ref[:, sl]], axis=0)
        q = jnp.concatenate([q_ref[:, (j * G + g) * LANES:(j * G + g + 1) * LANES] for g in range(G)], axis=0)
        s = lax.dot_general(q, k, NT_DIMS, preferred_element_type=F32)
        s = jnp.where(valid, s, NEG_INF)
        sink = jnp.zeros((R, 1), F32)
        for g in range(G):
            sink = jnp.where(grp == g, sink_ref[0, j * G + g], sink)
        m = jnp.maximum(jnp.max(s, axis=-1, keepdims=True), sink)
        p = jnp.exp(s - m)
        denom = jnp.sum(p, axis=-1, keepdims=True) + jnp.exp(sink - m)
        o = _dot(p.astype(BF16), v) / denom
        for g in range(G):
            o_ref[:, (j * G + g) * LANES:(j * G + g + 1) * LANES] = o[g * BLK:(g + 1) * BLK].astype(o_ref.dtype)


def _swa_attention(z, sink):
    B, L, _ = z.shape
    BLK = ATTN_BLOCK
    nb = L // BLK
    G = SWA_HQ // SWA_HKV
    qw = SWA_HQ * LANES
    kvw = SWA_HKV * LANES
    kcol = qw // kvw
    prev = lambda b, i: (b, jnp.maximum(i - 1, 0), kcol)
    cur = lambda b, i: (b, i, kcol)
    nxt = lambda b, i: (b, jnp.minimum(i + 1, nb - 1), kcol)
    vprev = lambda b, i: (b, jnp.maximum(i - 1, 0), kcol + 1)
    vcur = lambda b, i: (b, i, kcol + 1)
    vnxt = lambda b, i: (b, jnp.minimum(i + 1, nb - 1), kcol + 1)
    kv = lambda f: pl.BlockSpec((None, BLK, kvw), f)
    return pl.pallas_call(
        functools.partial(_swa_kernel, HKV=SWA_HKV, G=G, BLK=BLK),
        grid=(B, nb),
        in_specs=[pl.BlockSpec(memory_space=pltpu.SMEM),
                  pl.BlockSpec((None, BLK, qw), lambda b, i: (b, i, 0)),
                  kv(prev), kv(cur), kv(nxt), kv(vprev), kv(vcur), kv(vnxt)],
        out_specs=pl.BlockSpec((None, BLK, qw), lambda b, i: (b, i, 0)),
        out_shape=jax.ShapeDtypeStruct((B, L, qw), BF16),
        compiler_params=_cparams(("parallel", "parallel"),
                                 [((BLK, qw), BF16)] * 2 + [((BLK, kvw), BF16)] * 6,
                                 [((G * BLK, 3 * BLK), F32)] * 4),
        name="swa_attention",
    )(sink.reshape(1, SWA_HQ), z, z, z, z, z, z, z)


def _swa_layout(w_in, q_gain, k_gain, w_out, L):
    D = w_in.shape[0]
    dh, hq, hkv = SWA_DH, SWA_HQ, SWA_HKV
    hf = dh // 2
    q_end, k_end = hq * dh, (hq + hkv) * dh

    def rot_slots(w, n):
        w = w.reshape(D, n, 2, hf)
        return jnp.pad(w, ((0, 0), (0, 0), (0, 0), (0, LANES // 2 - hf))).reshape(D, n * LANES)

    def val_slots(w, n):
        return jnp.pad(w.reshape(D, n, dh), ((0, 0), (0, 0), (0, LANES - dh))).reshape(D, n * LANES)

    w = jnp.concatenate([rot_slots(w_in[:, :q_end], hq), rot_slots(w_in[:, q_end:k_end], hkv),
                         val_slots(w_in[:, k_end:], hkv)], axis=1)

    def gain_slot(g):
        return jnp.pad(g.reshape(2, hf), ((0, 0), (0, LANES // 2 - hf))).reshape(1, LANES)

    flag = jnp.concatenate([jnp.ones(((hq + hkv) * LANES,), F32), jnp.zeros((hkv * LANES,), F32)]).reshape(1, -1)
    wo = jnp.pad(w_out.reshape(hq, dh, -1), ((0, 0), (0, LANES - dh), (0, 0))).reshape(hq * LANES, -1)

    inv = ROPE_THETA ** (-np.arange(0, dh, 2) / dh)
    ang = np.arange(L)[:, None] * inv[None, :]
    zero = np.zeros((L, LANES // 2 - hf))
    cos = jnp.asarray(np.concatenate([np.cos(ang), zero, np.cos(ang), zero], axis=1), F32)
    sin = jnp.asarray(np.concatenate([-np.sin(ang), zero, np.sin(ang), zero], axis=1), F32)
    gq = gain_slot(q_gain) * (dh ** -0.5)
    gk = gain_slot(k_gain)
    roll = lambda g: jnp.roll(g, LANES // 2, axis=1)
    cos3 = jnp.stack([cos * gq, cos * gk, jnp.ones_like(cos)])
    sin3 = jnp.stack([sin * roll(gq), sin * roll(gk), jnp.zeros_like(sin)])
    return w, flag, wo, cos3, sin3


def _hg_gates(f_ref, rows, lb):
    f = lb + (1.0 - lb) * jax.nn.sigmoid(f_ref[rows, :].astype(F32))
    return f, jnp.log(f)


def _hg_fwd_kernel(lb_ref, q_ref, i_ref, f_ref, o_ref, st_ref, *, T, C, H, DK, DV):
    @pl.when(pl.program_id(1) == 0)
    def _():
        st_ref[...] = jnp.zeros_like(st_ref)

    row = lax.broadcasted_iota(jnp.int32, (C, C), 0)
    col = lax.broadcasted_iota(jnp.int32, (C, C), 1)
    causal = row >= col
    tri = causal.astype(BF16)
    lb = lb_ref[...]

    def chunk(c, carry):
        rows = pl.ds(pl.multiple_of(c * C, C), C)
        f, gl = _hg_gates(f_ref, rows, lb)
        b = _cumdot(tri, gl)
        b_last = jnp.sum(gl, axis=0, keepdims=True)
        qd = (_silu(q_ref[rows, :].astype(F32)) * jnp.exp(b)).astype(BF16)
        kf = 1.0 - f
        k_inv = (kf * jnp.exp(-b)).astype(BF16)
        k_end = (kf * jnp.exp(b_last - b)).astype(BF16)
        s_decay = jnp.exp(b_last)
        v = i_ref[rows, :]
        for h in range(H):
            ck = slice(h * DK, (h + 1) * DK)
            cv = slice(h * DV, (h + 1) * DV)
            s = lax.dot_general(qd[:, ck], k_inv[:, ck], NT_DIMS, preferred_element_type=F32)
            s = jnp.where(causal, s, 0.0)
            st = st_ref[h]
            o_ref[rows, cv] = _dot(s.astype(BF16), v[:, cv]) + lax.dot_general(
                qd[:, ck], st.astype(BF16), NT_DIMS, preferred_element_type=F32)
            st_ref[h] = st * s_decay[:, ck] + lax.dot_general(v[:, cv], k_end[:, ck], TN_DIMS,
                                                              preferred_element_type=F32)
        return carry

    lax.fori_loop(0, T // C, chunk, 0)


def _hg_bwd_kernel(lb_ref, gain_ref, q_ref, i_ref, f_ref, gate_ref, o1_ref, o_ref, st_ref, *, T, C, H, DK, DV):
    @pl.when(pl.program_id(1) == 0)
    def _():
        st_ref[...] = jnp.zeros_like(st_ref)

    row = lax.broadcasted_iota(jnp.int32, (C, C), 0)
    col = lax.broadcasted_iota(jnp.int32, (C, C), 1)
    anti = col >= row
    tri = anti.astype(BF16)
    lb = lb_ref[...]
    gain = gain_ref[...]
    nc = T // C

    def chunk(cc, carry):
        rows = pl.ds(pl.multiple_of((nc - 1 - cc) * C, C), C)
        f, gl = _hg_gates(f_ref, rows, lb)
        a = _cumdot(tri, gl)
        a_first = jnp.sum(gl, axis=0, keepdims=True)
        qd = (_silu(q_ref[rows, :].astype(F32)) * jnp.exp(a)).astype(BF16)
        kb = 1.0 - f
        k_inv = (kb * jnp.exp(-a)).astype(BF16)
        k_end = (kb * jnp.exp(a_first - a)).astype(BF16)
        s_decay = jnp.exp(a_first)
        v = i_ref[rows, :]
        for h in range(H):
            ck = slice(h * DK, (h + 1) * DK)
            cv = slice(h * DV, (h + 1) * DV)
            s = lax.dot_general(qd[:, ck], k_inv[:, ck], NT_DIMS, preferred_element_type=F32)
            s = jnp.where(anti, s, 0.0)
            st = st_ref[h]
            o = o1_ref[rows, cv] + _dot(s.astype(BF16), v[:, cv]) + lax.dot_general(
                qd[:, ck], st.astype(BF16), NT_DIMS, preferred_element_type=F32)
            y = o * lax.rsqrt(jnp.mean(o * o, axis=-1, keepdims=True) + NORM_EPS) * gain
            o_ref[rows, cv] = (y * _silu(gate_ref[rows, cv].astype(F32))).astype(o_ref.dtype)
            st_ref[h] = st * s_decay[:, ck] + lax.dot_general(v[:, cv], k_end[:, ck], TN_DIMS,
                                                              preferred_element_type=F32)
        return carry

    lax.fori_loop(0, nc, chunk, 0)


def _hgrn(z, lb, gain):
    B, L, _ = z.shape
    H, C = HG_HEADS, HG_CHUNK
    HD = lb.shape[1]
    DK = HD // H
    DV = gain.shape[0]
    T = min(L, 512)
    nT = L // T
    kw = dict(T=T, C=C, H=H, DK=DK, DV=DV)
    blocks = [((T, HD), BF16)] * 5 + [((T, H * DV), F32)] * 2
    state = [((H, DV, DK), F32)] + [((C, HD), F32)] * 12
    lb3 = lb.reshape(2, 1, HD)
    o1 = pl.pallas_call(
        functools.partial(_hg_fwd_kernel, **kw),
        grid=(B, nT),
        in_specs=[pl.BlockSpec((None, 1, HD), lambda b, i: (0, 0, 0)),
                  pl.BlockSpec((None, T, HD), lambda b, i: (b, i, 0)),
                  pl.BlockSpec((None, T, H * DV), lambda b, i: (b, i, 1)),
                  pl.BlockSpec((None, T, HD), lambda b, i: (b, i, 2))],
        out_specs=pl.BlockSpec((None, T, H * DV), lambda b, i: (b, i, 0)),
        out_shape=jax.ShapeDtypeStruct((B, L, H * DV), F32),
        scratch_shapes=[pltpu.VMEM((H, DV, DK), F32)],
        compiler_params=_cparams(("parallel", "arbitrary"), blocks, state),
        name="hgrn_fwd",
    )(lb3, z, z, z)
    return pl.pallas_call(
        functools.partial(_hg_bwd_kernel, **kw),
        grid=(B, nT),
        in_specs=[pl.BlockSpec((None, 1, HD), lambda b, i: (1, 0, 0)),
                  pl.BlockSpec((1, DV), lambda b, i: (0, 0)),
                  pl.BlockSpec((None, T, HD), lambda b, i: (b, nT - 1 - i, 0)),
                  pl.BlockSpec((None, T, H * DV), lambda b, i: (b, nT - 1 - i, 1)),
                  pl.BlockSpec((None, T, HD), lambda b, i: (b, nT - 1 - i, 3)),
                  pl.BlockSpec((None, T, H * DV), lambda b, i: (b, nT - 1 - i, 4)),
                  pl.BlockSpec((None, T, H * DV), lambda b, i: (b, nT - 1 - i, 0))],
        out_specs=pl.BlockSpec((None, T, H * DV), lambda b, i: (b, nT - 1 - i, 0)),
        out_shape=jax.ShapeDtypeStruct((B, L, H * DV), BF16),
        scratch_shapes=[pltpu.VMEM((H, DV, DK), F32)],
        compiler_params=_cparams(("parallel", "arbitrary"), blocks, state),
        name="hgrn_bwd",
    )(lb3, gain.reshape(1, DV), z, z, z, z, o1)


def _ret_rope_tables(L, dk):
    inv = ROPE_THETA ** (-np.arange(0, dk, 2) / dk)
    ang = np.arange(L)[:, None] * inv[None, :]
    return jnp.asarray(np.cos(ang), F32), jnp.asarray(np.sin(ang), F32)


def kernel(x_prompt, x_sample, c_prompt, c_sample, ada_w, ada_b, norm_g, hy_w_in, hy_conv_w, hy_conv_b, hy_w1, hy_b1, hy_w2, hy_b2, hy_w3, hy_freq, hy_decay, hy_skip, hy_w_out, ret_w_in, ret_decay, ret_w_out, swa_w_in, swa_q_gain, swa_k_gain, swa_sink, swa_w_out, hg_w_in, hg_lb, hg_gain, hg_w_out, ffn_w_gate, ffn_w_val, ffn_conv_w, ffn_conv_b, ffn_w_down):
    depth, D = norm_g.shape[0], norm_g.shape[2]
    groups = [(x_prompt, c_prompt), (x_sample, c_sample)]
    mods = _ada_mod(jnp.concatenate([c for _, c in groups], axis=0), ada_w, ada_b)

    bf = lambda w: w.astype(BF16)
    hy_w_in_b, hy_w_out_b = bf(hy_w_in), bf(hy_w_out)
    ret_w_in_b, ret_w_out_b = bf(ret_w_in), bf(ret_w_out)
    hg_w_in_b, hg_w_out_b = bf(hg_w_in), bf(hg_w_out)
    wg_b, wv_b, wd_b = bf(ffn_w_gate), bf(ffn_w_val), bf(ffn_w_down)
    hg_sm = jax.nn.softmax(hg_lb.astype(F32), axis=1)
    hg_lower = jnp.cumsum(hg_sm, axis=1) - hg_sm
    ret_log_gamma = -jnp.exp(ret_decay.astype(F32))

    outs = []
    row0 = 0
    for x, c in groups:
        B, L, _ = x.shape
        for layer in range(depth):
            kind, j = layer % N_MIXERS, layer // N_MIXERS
            mod = mods[layer, row0:row0 + B].reshape(B, N_MOD, 1, D)
            sh1, sc1, g1, sh2, sc2, g2 = (mod[:, m] for m in range(N_MOD))
            gn1 = norm_g[layer, 0].reshape(1, D)
            gn2 = norm_g[layer, 1].reshape(1, D)
            if kind == 0:
                tabs = _fft_tables(L)
                z = _proj_in_conv(x, gn1, sc1, sh1, hy_w_in_b[j], hy_conv_w[j], hy_conv_b[j], tn=1024)
                taps, asum = _hy_filter_taps(L, hy_w1[j], hy_b1[j], hy_w2[j], hy_b2[j], hy_w3[j], hy_freq[j],
                                             hy_decay[j])
                spec = _hy_spectrum(taps, asum, tabs)
                a = _hy_conv(z, hy_skip[j], spec, tabs)
                x = _proj_out(a, hy_w_out_b[j], x, g1)
            elif kind == 1:
                z = _proj_in(x, gn1, sc1, sh1, ret_w_in_b[j], tn=1024)
                cos, sin = _ret_rope_tables(L, D // RET_HEADS)
                a = _retention(z, ret_log_gamma[j], cos, sin)
                x = _proj_out(a, ret_w_out_b[j], x, g1)
            elif kind == 2:
                w, flag, wo, cos, sin = _swa_layout(swa_w_in[j], swa_q_gain[j], swa_k_gain[j], swa_w_out[j], L)
                z = _proj_in_rope(x, gn1, sc1, sh1, bf(w), flag, cos, sin, tn=SWA_HKV * LANES, dh=SWA_DH,
                                  q_tiles=SWA_HQ // SWA_HKV)
                a = _swa_attention(z, swa_sink[j])
                x = _proj_out(a, bf(wo), x, g1)
            else:
                z = _proj_in(x, gn1, sc1, sh1, hg_w_in_b[j], tn=1024)
                a = _hgrn(z, hg_lower[:, layer], hg_gain[j])
                x = _proj_out(a, hg_w_out_b[j], x, g1)
            x = _ffn(x, gn2, sc2, sh2, g2, wg_b[layer], wv_b[layer], ffn_conv_w[layer], ffn_conv_b[layer],
                     wd_b[layer])
        outs.append(x)
        row0 += B
    return tuple(outs)
```

```python
import functools
import math

import numpy as np
import jax
import jax.numpy as jnp
from jax import lax
from jax.experimental import pallas as pl
from jax.experimental.pallas import tpu as pltpu

F32 = jnp.float32
BF16 = jnp.bfloat16

NORM_EPS = 1e-6
N_MIXERS = 4
N_MOD = 6
HY_BANDS = 16
RET_HEADS = 4
RET_CHUNK = 128
SWA_HQ = 16
SWA_HKV = 4
SWA_DH = 64
WINDOW = 128
ATTN_BLOCK = 128
ROPE_THETA = 10000.0
NEG_INF = -1e30
HG_HEADS = 8
HG_CHUNK = 128

LANES = 128
SUBLANES = 8
BF16_ROWS = 16
VMEM_LIMIT_CAP = 60 * 1024 * 1024
VMEM_SLACK = 8 * 1024 * 1024

NT_DIMS = (((1,), (1,)), ((), ()))
TN_DIMS = (((0,), (0,)), ((), ()))


def _nbytes(shape, dtype):
    item = jnp.dtype(dtype).itemsize
    sub = SUBLANES * 4 // item
    dims = list(shape)
    dims[-1] = -(-dims[-1] // LANES) * LANES
    if len(dims) > 1:
        dims[-2] = -(-dims[-2] // sub) * sub
    return int(np.prod(dims)) * item


def _cparams(semantics, pipelined, resident):
    need = 2 * sum(_nbytes(s, d) for s, d in pipelined) + sum(_nbytes(s, d) for s, d in resident)
    return pltpu.CompilerParams(dimension_semantics=semantics,
                                vmem_limit_bytes=min(need + VMEM_SLACK, VMEM_LIMIT_CAP))


def _dot(a, b):
    return jnp.dot(a, b, preferred_element_type=F32)


def _split(x):
    hi = x.astype(BF16)
    lo = (x - hi.astype(F32)).astype(BF16)
    return hi, lo


def _dot_tab(t_cat, d):
    d_hi, d_lo = _split(d)
    return _dot(t_cat, jnp.concatenate([d_hi, d_hi, d_lo], axis=0))


def _dot3(a, b):
    a_hi, a_lo = _split(a)
    b_hi, b_lo = _split(b)
    return _dot(a_hi, b_hi) + _dot(a_lo, b_hi) + _dot(a_hi, b_lo)


def _cumdot(tri, g):
    g0 = g.astype(BF16)
    r1 = g - g0.astype(F32)
    g1 = r1.astype(BF16)
    g2 = (r1 - g1.astype(F32)).astype(BF16)
    return _dot(tri, g0) + _dot(tri, g1) + _dot(tri, g2)


def _silu(x):
    return x * jax.nn.sigmoid(x)


def _norm_mod(x, g, sc, sh):
    xf = x.astype(F32)
    y = xf * lax.rsqrt(jnp.mean(xf * xf, axis=-1, keepdims=True) + NORM_EPS)
    return (y * g) * (1.0 + sc) + sh


def _row_tile(L):
    return min(L, 1024)


def _ada_kernel(c_ref, w_ref, b_ref, o_ref):
    cs = _silu(c_ref[...]).astype(BF16)
    o_ref[...] = _dot(cs, w_ref[...].astype(BF16)) + b_ref[...]


def _ada_mod(c_all, ada_w, ada_b):
    depth, D, N = ada_w.shape
    R = c_all.shape[0]
    tn = 1024
    return pl.pallas_call(
        _ada_kernel,
        grid=(depth, N // tn),
        in_specs=[pl.BlockSpec((R, D), lambda l, j: (0, 0)),
                  pl.BlockSpec((None, D, tn), lambda l, j: (l, 0, j)),
                  pl.BlockSpec((None, 1, tn), lambda l, j: (l, 0, j))],
        out_specs=pl.BlockSpec((None, R, tn), lambda l, j: (l, 0, j)),
        out_shape=jax.ShapeDtypeStruct((depth, R, N), F32),
        compiler_params=_cparams(("parallel", "parallel"),
                                 [((D, tn), F32), ((R, D), F32), ((R, tn), F32)], []),
        name="ada_mod",
    )(c_all, ada_w, ada_b.reshape(depth, 1, N))


def _fill_h(h_ref, x_ref, g_ref, sc_ref, sh_ref):
    h_ref[...] = _norm_mod(x_ref[...], g_ref[...], sc_ref[...], sh_ref[...]).astype(BF16)


def _fill_h_halo(h_ref, x_ref, xp_ref, xn_ref, g_ref, sc_ref, sh_ref, tm):
    i = pl.program_id(1)
    last = pl.num_programs(1) - 1
    g, sc, sh = g_ref[...], sc_ref[...], sh_ref[...]
    h_ref[BF16_ROWS:BF16_ROWS + tm, :] = _norm_mod(x_ref[...], g, sc, sh).astype(BF16)
    hp = _norm_mod(xp_ref[...], g, sc, sh)
    hn = _norm_mod(xn_ref[...], g, sc, sh)
    h_ref[0:BF16_ROWS, :] = jnp.where(i > 0, hp, 0.0).astype(BF16)
    h_ref[BF16_ROWS + tm:, :] = jnp.where(i < last, hn, 0.0).astype(BF16)


def _conv3_rows(z_ref, cw, cb, tm):
    o = BF16_ROWS
    return (z_ref[o - 1:o - 1 + tm, :] * cw[0:1, :] + z_ref[o:o + tm, :] * cw[1:2, :]
            + z_ref[o + 1:o + 1 + tm, :] * cw[2:3, :] + cb)


def _proj_plain_kernel(x_ref, g_ref, sc_ref, sh_ref, w_ref, o_ref, h_ref):
    @pl.when(pl.program_id(2) == 0)
    def _():
        _fill_h(h_ref, x_ref, g_ref, sc_ref, sh_ref)

    o_ref[...] = _dot(h_ref[...], w_ref[...]).astype(o_ref.dtype)


def _proj_conv_kernel(x_ref, xp_ref, xn_ref, g_ref, sc_ref, sh_ref, w_ref, cw_ref, cb_ref,
                      o_ref, h_ref, z_ref, *, tm):
    @pl.when(pl.program_id(2) == 0)
    def _():
        _fill_h_halo(h_ref, x_ref, xp_ref, xn_ref, g_ref, sc_ref, sh_ref, tm)

    z_ref[...] = _dot(h_ref[...], w_ref[...])
    o_ref[...] = _conv3_rows(z_ref, cw_ref[...], cb_ref[...], tm).astype(o_ref.dtype)


def _proj_rope_kernel(x_ref, g_ref, sc_ref, sh_ref, w_ref, flag_ref, ones_ref, cos_ref, sin_ref,
                      o_ref, h_ref, *, dh):
    @pl.when(pl.program_id(2) == 0)
    def _():
        _fill_h(h_ref, x_ref, g_ref, sc_ref, sh_ref)

    z = _dot(h_ref[...], w_ref[...])
    ms = _dot((z * z).astype(BF16), ones_ref[...]) * (1.0 / dh)
    zn = z * jnp.where(flag_ref[...] > 0.0, lax.rsqrt(ms + NORM_EPS), 1.0)
    cos, sin = cos_ref[...], sin_ref[...]
    for s in range(z.shape[1] // LANES):
        sl = slice(s * LANES, (s + 1) * LANES)
        zs = zn[:, sl]
        o_ref[:, sl] = (zs * cos + pltpu.roll(zs, LANES // 2, axis=1) * sin).astype(o_ref.dtype)


def _mod_specs(D):
    return [pl.BlockSpec((1, D), lambda b, i, j: (0, 0)),
            pl.BlockSpec((None, 1, D), lambda b, i, j: (b, 0, 0)),
            pl.BlockSpec((None, 1, D), lambda b, i, j: (b, 0, 0))]


def _halo_specs(tm, D, L):
    hb = tm // BF16_ROWS
    nhb = L // BF16_ROWS
    return [pl.BlockSpec((None, tm, D), lambda b, i, j: (b, i, 0)),
            pl.BlockSpec((None, BF16_ROWS, D), lambda b, i, j: (b, jnp.maximum(i * hb - 1, 0), 0)),
            pl.BlockSpec((None, BF16_ROWS, D), lambda b, i, j: (b, jnp.minimum((i + 1) * hb, nhb - 1), 0))]


def _proj_in(x, g, sc, sh, w, *, tn):
    B, L, D = x.shape
    N = w.shape[1]
    tm = _row_tile(L)
    return pl.pallas_call(
        _proj_plain_kernel,
        grid=(B, L // tm, N // tn),
        in_specs=[pl.BlockSpec((None, tm, D), lambda b, i, j: (b, i, 0))] + _mod_specs(D)
        + [pl.BlockSpec((D, tn), lambda b, i, j: (0, j))],
        out_specs=pl.BlockSpec((None, tm, tn), lambda b, i, j: (b, i, j)),
        out_shape=jax.ShapeDtypeStruct((B, L, N), BF16),
        scratch_shapes=[pltpu.VMEM((tm, D), BF16)],
        compiler_params=_cparams(("parallel", "parallel", "arbitrary"),
                                 [((tm, D), F32), ((D, tn), BF16), ((tm, tn), BF16)],
                                 [((tm, D), BF16), ((tm, tn), F32)]),
        name="proj_in",
    )(x, g, sc, sh, w)


def _proj_in_conv(x, g, sc, sh, w, cw, cb, *, tn):
    B, L, D = x.shape
    N = w.shape[1]
    tm = _row_tile(L)
    te = tm + 2 * BF16_ROWS
    return pl.pallas_call(
        functools.partial(_proj_conv_kernel, tm=tm),
        grid=(B, L // tm, N // tn),
        in_specs=_halo_specs(tm, D, L) + _mod_specs(D)
        + [pl.BlockSpec((D, tn), lambda b, i, j: (0, j)),
           pl.BlockSpec((3, tn), lambda b, i, j: (0, j)),
           pl.BlockSpec((1, tn), lambda b, i, j: (0, j))],
        out_specs=pl.BlockSpec((None, tm, tn), lambda b, i, j: (b, i, j)),
        out_shape=jax.ShapeDtypeStruct((B, L, N), BF16),
        scratch_shapes=[pltpu.VMEM((te, D), BF16), pltpu.VMEM((te, tn), F32)],
        compiler_params=_cparams(("parallel", "parallel", "arbitrary"),
                                 [((tm, D), F32), ((D, tn), BF16), ((tm, tn), BF16)],
                                 [((te, D), BF16), ((te, tn), F32), ((te, tn), F32)]),
        name="proj_in_conv",
    )(x, x, x, g, sc, sh, w, cw, cb.reshape(1, N))


def _proj_in_rope(x, g, sc, sh, w, flag, cos, sin, *, tn, dh, q_tiles):
    B, L, D = x.shape
    N = w.shape[1]
    tm = _row_tile(L)
    slot = lax.broadcasted_iota(jnp.int32, (tn, tn), 0) // LANES == lax.broadcasted_iota(jnp.int32, (tn, tn), 1) // LANES
    table = lambda b, i, j: (jnp.maximum(j - (q_tiles - 1), 0), i, 0)
    return pl.pallas_call(
        functools.partial(_proj_rope_kernel, dh=dh),
        grid=(B, L // tm, N // tn),
        in_specs=[pl.BlockSpec((None, tm, D), lambda b, i, j: (b, i, 0))] + _mod_specs(D)
        + [pl.BlockSpec((D, tn), lambda b, i, j: (0, j)),
           pl.BlockSpec((1, tn), lambda b, i, j: (0, j)),
           pl.BlockSpec((tn, tn), lambda b, i, j: (0, 0)),
           pl.BlockSpec((None, tm, LANES), table),
           pl.BlockSpec((None, tm, LANES), table)],
        out_specs=pl.BlockSpec((None, tm, tn), lambda b, i, j: (b, i, j)),
        out_shape=jax.ShapeDtypeStruct((B, L, N), BF16),
        scratch_shapes=[pltpu.VMEM((tm, D), BF16)],
        compiler_params=_cparams(("parallel", "parallel", "arbitrary"),
                                 [((tm, D), F32), ((D, tn), BF16), ((tm, tn), BF16), ((tm, LANES), F32),
                                  ((tm, LANES), F32), ((tn, tn), BF16)],
                                 [((tm, D), BF16), ((tm, tn), F32), ((tm, tn), F32), ((tm, tn), F32)]),
        name="proj_in_rope",
    )(x, g, sc, sh, w, flag, slot.astype(BF16), cos, sin)


def _proj_out_kernel(a_ref, w_ref, x_ref, gate_ref, o_ref):
    o_ref[...] = x_ref[...] + gate_ref[...] * _dot(a_ref[...], w_ref[...])


def _proj_out(a, w, x, gate):
    B, L, K = a.shape
    D = w.shape[1]
    tm = _row_tile(L)
    return pl.pallas_call(
        _proj_out_kernel,
        grid=(B, L // tm),
        in_specs=[pl.BlockSpec((None, tm, K), lambda b, i: (b, i, 0)),
                  pl.BlockSpec((K, D), lambda b, i: (0, 0)),
                  pl.BlockSpec((None, tm, D), lambda b, i: (b, i, 0)),
                  pl.BlockSpec((None, 1, D), lambda b, i: (b, 0, 0))],
        out_specs=pl.BlockSpec((None, tm, D), lambda b, i: (b, i, 0)),
        out_shape=jax.ShapeDtypeStruct((B, L, D), F32),
        compiler_params=_cparams(("parallel", "parallel"),
                                 [((tm, K), BF16), ((K, D), BF16), ((tm, D), F32), ((tm, D), F32)],
                                 [((tm, D), F32)]),
        name="proj_out",
    )(a, w, x, gate)


def _ffn_kernel(x_ref, xp_ref, xn_ref, g_ref, sc_ref, sh_ref, gate_ref, wg_ref, wv_ref, cw_ref, cb_ref,
                wd_ref, o_ref, h_ref, z_ref, *, tm, tf):
    _fill_h_halo(h_ref, x_ref, xp_ref, xn_ref, g_ref, sc_ref, sh_ref, tm)
    acc = None
    for c in range(wg_ref.shape[1] // tf):
        cols = slice(c * tf, (c + 1) * tf)
        z_ref[...] = _dot(h_ref[...], wg_ref[:, cols])
        a = _conv3_rows(z_ref, cw_ref[:, cols], cb_ref[:, cols], tm)
        v = _dot(h_ref[BF16_ROWS:BF16_ROWS + tm, :], wv_ref[:, cols])
        part = _dot((_silu(a) * v).astype(BF16), wd_ref[cols, :])
        acc = part if acc is None else acc + part
    o_ref[...] = x_ref[...] + gate_ref[...] * acc


def _ffn(x, g, sc, sh, gate, wg, wv, cw, cb, wd):
    B, L, D = x.shape
    F = wg.shape[1]
    tm = min(L, 512)
    te = tm + 2 * BF16_ROWS
    tf = F // 2
    const = lambda b, i, j: (0, 0)
    resident = lambda shape: pl.BlockSpec(shape, const, pipeline_mode=pl.Buffered(1))
    return pl.pallas_call(
        functools.partial(_ffn_kernel, tm=tm, tf=tf),
        grid=(B, L // tm, 1),
        in_specs=_halo_specs(tm, D, L) + _mod_specs(D)
        + [pl.BlockSpec((None, 1, D), lambda b, i, j: (b, 0, 0)),
           resident((D, F)), resident((D, F)), resident((3, F)), resident((1, F)), resident((F, D))],
        out_specs=pl.BlockSpec((None, tm, D), lambda b, i, j: (b, i, 0)),
        out_shape=jax.ShapeDtypeStruct((B, L, D), F32),
        scratch_shapes=[pltpu.VMEM((te, D), BF16), pltpu.VMEM((te, tf), F32)],
        compiler_params=_cparams(("parallel", "parallel", "arbitrary"),
                                 [((tm, D), F32), ((tm, D), F32)],
                                 [((D, F), BF16), ((D, F), BF16), ((F, D), BF16), ((te, D), BF16),
                                  ((te, tf), F32), ((tm, tf), F32), ((tm, tf), F32), ((tm, tf), BF16),
                                  ((tm, D), F32), ((tm, D), F32)]),
        name="ffn",
    )(x, x, x, g, sc, sh, gate, wg, wv, cw, cb.reshape(1, F), wd)


def _fft_dims(L):
    N = 2 * L
    p = N.bit_length() - 1
    n1 = 1 << ((p + 1) // 2)
    return N, n1, N // n1


def _k1_pad(n1):
    return n1 // 2 + 8


def _fft_tables(L):
    N, n1, n2 = _fft_dims(L)
    k1p = _k1_pad(n1)
    k1 = np.arange(k1p)
    valid = (k1 <= n1 // 2).astype(np.float64)
    weight = np.where((k1 == 0) | (k1 == n1 // 2), 1.0, 2.0) * valid
    m1 = np.arange(n1 // 2)
    n2v = np.arange(n2)
    n_idx = n2 * m1[None, :] + n2v[:, None]
    phi = 2.0 * np.pi * k1[None, :, None] * n_idx[:, None, :] / N
    s1 = np.concatenate([np.cos(phi) * valid[None, :, None], -np.sin(phi) * valid[None, :, None]], axis=1)
    phit = np.transpose(phi, (0, 2, 1))
    gl = np.concatenate([np.cos(phit) * weight[None, None, :], -np.sin(phit) * weight[None, None, :]],
                        axis=2) / N
    th = 2.0 * np.pi * np.outer(n2v, n2v) / n2
    c, s = np.cos(th), np.sin(th)
    f2 = np.block([[c, s], [-s, c]])
    f2c = np.block([[c, -s], [s, c]])

    def cat(a):
        a32 = jnp.asarray(a, F32)
        hi = a32.astype(BF16)
        lo = (a32 - hi.astype(F32)).astype(BF16)
        return jnp.concatenate([hi, lo, hi], axis=-1)

    return dict(N=N, n1=n1, n2=n2, k1p=k1p, s1=cat(s1), gl=cat(gl), f2=cat(f2), f2c=cat(f2c))


def _hy_positions(L):
    idx = np.concatenate([np.arange(L), np.array([0]), np.arange(L - 1, 0, -1)])
    t = idx / (L - 1)
    ang = (2.0 * np.pi / L) * idx
    bands = np.linspace(1e-4, HY_BANDS - 1, HY_BANDS)
    z = np.concatenate([t[:, None], np.cos(bands[None, :] * ang[:, None]), -np.sin(bands[None, :] * ang[:, None])],
                       axis=1)
    zp = np.zeros((2 * L, LANES))
    zp[:, : z.shape[1]] = z
    return jnp.asarray(zp, F32)


def _hy_mlp_kernel(z_ref, w1_ref, b1_ref, w2_ref, b2_ref, w3_ref, fr_ref, dec_ref, taps_ref, asum_ref, *, L, tr):
    i = pl.program_id(0)
    z = z_ref[...]
    fr = fr_ref[...]
    h = jnp.sin(fr * (_dot3(z, w1_ref[...]) + b1_ref[...]))
    h = jnp.sin(fr * (_dot3(h, w2_ref[...]) + b2_ref[...]))
    h = _dot3(h, w3_ref[...])
    t = z[:, 0:1]
    taps = h * jnp.exp(-t * jnp.abs(dec_ref[...]))
    row = i * tr + lax.broadcasted_iota(jnp.int32, (tr, 1), 0)
    taps = jnp.where(row == L, 0.0, taps)
    taps_ref[...] = taps

    @pl.when(i == 0)
    def _():
        asum_ref[...] = jnp.zeros_like(asum_ref)

    asum_ref[...] += jnp.sum(jnp.abs(taps), axis=0, keepdims=True)


def _hy_filter_taps(L, w1, b1, w2, b2, w3, freq, decay):
    W = decay.shape[1]
    E, O = w1.shape
    tr = 512
    half = L // tr
    z = _hy_positions(L)
    w1p = jnp.zeros((LANES, O), F32).at[:E].set(w1)
    return pl.pallas_call(
        functools.partial(_hy_mlp_kernel, L=L, tr=tr),
        grid=(2 * L // tr,),
        in_specs=[pl.BlockSpec((tr, LANES), lambda i: (i, 0)),
                  pl.BlockSpec((LANES, O), lambda i: (0, 0)),
                  pl.BlockSpec((1, O), lambda i: (0, 0)),
                  pl.BlockSpec((O, O), lambda i: (0, 0)),
                  pl.BlockSpec((1, O), lambda i: (0, 0)),
                  pl.BlockSpec((O, W), lambda i: (0, i // half)),
                  pl.BlockSpec((1, O), lambda i: (0, 0)),
                  pl.BlockSpec((None, 1, W), lambda i: (i // half, 0, 0))],
        out_specs=[pl.BlockSpec((tr, W), lambda i: (i, 0)),
                   pl.BlockSpec((1, W), lambda i: (0, 0))],
        out_shape=[jax.ShapeDtypeStruct((2 * L, W), F32), jax.ShapeDtypeStruct((1, W), F32)],
        compiler_params=_cparams(("arbitrary",), [((tr, W), F32), ((O, W), F32), ((tr, LANES), F32)],
                                 [((tr, W), F32), ((tr, W), F32)]),
        name="hyena_filter_mlp",
    )(z, w1p, b1.reshape(1, O), w2, b2.reshape(1, O), w3, freq.reshape(1, O), decay.reshape(2, 1, W))


FFT_UNROLL = 4


def _fft_stage1(src_ref, src_off, a_ref, s1_ref, n1, n2, k1p, sign=None, src_off2=None):
    def body(j, carry):
        d = src_ref[pl.ds(src_off + j, n1 // 2, stride=n2), :]
        r = _dot_tab(s1_ref[j], d)
        if src_off2 is not None:
            d2 = src_ref[pl.ds(src_off2 + j, n1 // 2, stride=n2), :]
            r = r + sign * _dot_tab(s1_ref[j], d2)
        a_ref[pl.ds(pl.multiple_of(j * 2 * k1p, 8), 2 * k1p), :] = r
        return carry

    lax.fori_loop(0, n2, body, 0, unroll=FFT_UNROLL)


def _load_k1(a_ref, k, n2, k1p):
    br = a_ref[pl.ds(k, n2, stride=2 * k1p), :]
    bi = a_ref[pl.ds(k1p + k, n2, stride=2 * k1p), :]
    return jnp.concatenate([br, bi], axis=0)


def _hy_spec_kernel(taps_ref, asum_ref, s1_ref, f2_ref, h_ref, a_ref, *, L, n1, n2, k1p):
    r = lax.broadcasted_iota(jnp.int32, (2 * k1p, 1), 0)
    k1 = jnp.where(r < k1p, r, r - k1p)
    sign = (1 - 2 * (k1 & 1)).astype(F32)
    _fft_stage1(taps_ref, 0, a_ref, s1_ref, n1, n2, k1p, sign=sign, src_off2=L)
    inv = 1.0 / asum_ref[...]

    def body(k, carry):
        h_ref[k] = _dot_tab(f2_ref[...], _load_k1(a_ref, k, n2, k1p)) * inv
        return carry

    lax.fori_loop(0, n1 // 2, body, 0, unroll=2)
    body(n1 // 2, 0)


def _hy_spectrum(taps, asum, tabs):
    n1, n2, k1p = tabs["n1"], tabs["n2"], tabs["k1p"]
    N, W = taps.shape
    L = N // 2
    cb = LANES
    k1v = n1 // 2 + 1
    s1, f2 = tabs["s1"], tabs["f2"]
    return pl.pallas_call(
        functools.partial(_hy_spec_kernel, L=L, n1=n1, n2=n2, k1p=k1p),
        grid=(W // cb,),
        in_specs=[pl.BlockSpec((N, cb), lambda c: (0, c)),
                  pl.BlockSpec((1, cb), lambda c: (0, c)),
                  pl.BlockSpec(s1.shape, lambda c: (0, 0, 0), pipeline_mode=pl.Buffered(1)),
                  pl.BlockSpec(f2.shape, lambda c: (0, 0), pipeline_mode=pl.Buffered(1))],
        out_specs=pl.BlockSpec((k1v, 2 * n2, cb), lambda c: (0, 0, c)),
        out_shape=jax.ShapeDtypeStruct((k1v, 2 * n2, W), F32),
        scratch_shapes=[pltpu.VMEM((n2 * 2 * k1p, cb), F32)],
        compiler_params=_cparams(("parallel",),
                                 [((N, cb), F32), ((k1v, 2 * n2, cb), F32)],
                                 [((n2 * 2 * k1p, cb), F32), (s1.shape, BF16), (f2.shape, BF16)]),
        name="hyena_filter_spectrum",
    )(taps, asum, s1, f2)


def _hy_conv_kernel(x0_ref, x1_ref, v_ref, skip_ref, h_ref, s1_ref, f2_ref, f2c_ref, gl_ref,
                    o_ref, u_ref, a_ref, *, n1, n2, k1p):
    u_ref[...] = x1_ref[...].astype(F32) * v_ref[...].astype(F32)
    _fft_stage1(u_ref, 0, a_ref, s1_ref, n1, n2, k1p)

    cb = u_ref.shape[1]

    def mid(ks):
        x = _dot_tab(f2_ref[...], jnp.concatenate([_load_k1(a_ref, k, n2, k1p) for k in ks], axis=1))
        xr, xi = x[:n2], x[n2:]
        hr = jnp.concatenate([h_ref[k, :n2, :] for k in ks], axis=1)
        hi = jnp.concatenate([h_ref[k, n2:, :] for k in ks], axis=1)
        y = jnp.concatenate([xr * hr - xi * hi, xr * hi + xi * hr], axis=0)
        c = _dot_tab(f2c_ref[...], y)
        for p, k in enumerate(ks):
            a_ref[pl.ds(k, n2, stride=2 * k1p), :] = c[:n2, p * cb:(p + 1) * cb]
            a_ref[pl.ds(k1p + k, n2, stride=2 * k1p), :] = c[n2:, p * cb:(p + 1) * cb]

    def mid_pair(kk, carry):
        mid((2 * kk, 2 * kk + 1))
        return carry

    lax.fori_loop(0, n1 // 4, mid_pair, 0, unroll=2)
    mid((n1 // 2,))
    skip = skip_ref[...]

    def last(j, carry):
        rhs = a_ref[pl.ds(pl.multiple_of(j * 2 * k1p, 8), 2 * k1p), :]
        y = _dot_tab(gl_ref[j], rhs)
        rows = pl.ds(j, n1 // 2, stride=n2)
        u_ref[rows, :] = y + u_ref[rows, :] * skip
        return carry

    lax.fori_loop(0, n2, last, 0, unroll=FFT_UNROLL)
    o_ref[...] = (u_ref[...].astype(F32) * x0_ref[...].astype(F32)).astype(o_ref.dtype)


def _hy_conv(z, skip, spec, tabs):
    n1, n2, k1p = tabs["n1"], tabs["n2"], tabs["k1p"]
    B, L, W3 = z.shape
    W = W3 // 3
    cb = LANES
    nc = W // cb
    k1v = n1 // 2 + 1
    tables = [tabs["s1"], tabs["f2"], tabs["f2c"], tabs["gl"]]

    def const_spec(t):
        return pl.BlockSpec(t.shape, (lambda c, b: (0, 0, 0)) if t.ndim == 3 else (lambda c, b: (0, 0)),
                            pipeline_mode=pl.Buffered(1))

    return pl.pallas_call(
        functools.partial(_hy_conv_kernel, n1=n1, n2=n2, k1p=k1p),
        grid=(nc, B),
        in_specs=[pl.BlockSpec((None, L, cb), lambda c, b: (b, 0, c), pipeline_mode=pl.Buffered(1)),
                  pl.BlockSpec((None, L, cb), lambda c, b: (b, 0, nc + c), pipeline_mode=pl.Buffered(1)),
                  pl.BlockSpec((None, L, cb), lambda c, b: (b, 0, 2 * nc + c), pipeline_mode=pl.Buffered(1)),
                  pl.BlockSpec((1, cb), lambda c, b: (0, c)),
                  pl.BlockSpec((k1v, 2 * n2, cb), lambda c, b: (0, 0, c), pipeline_mode=pl.Buffered(1))]
        + [const_spec(t) for t in tables],
        out_specs=pl.BlockSpec((None, L, cb), lambda c, b: (b, 0, c)),
        out_shape=jax.ShapeDtypeStruct((B, L, W), BF16),
        scratch_shapes=[pltpu.VMEM((L, cb), F32), pltpu.VMEM((n2 * 2 * k1p, cb), F32)],
        compiler_params=_cparams(("parallel", "arbitrary"),
                                 [((L, cb), BF16)],
                                 [((L, cb), BF16)] * 3
                                 + [((L, cb), F32), ((n2 * 2 * k1p, cb), F32), ((k1v, 2 * n2, cb), F32)]
                                 + [(t.shape, BF16) for t in tables]),
        name="hyena_long_conv",
    )(z, z, z, skip.reshape(1, W), spec, *tables)


def _rope_half(x, cos, sin):
    half = x.shape[1] // 2
    x1, x2 = x[:, :half], x[:, half:]
    return jnp.concatenate([x1 * cos - x2 * sin, x1 * sin + x2 * cos], axis=1)


def _ret_fwd_kernel(lg_ref, q_ref, k_ref, v_ref, cos_ref, sin_ref, o_ref, s_ref, *, T, C, H, DK, DV):
    @pl.when(pl.program_id(1) == 0)
    def _():
        s_ref[...] = jnp.zeros_like(s_ref)

    row = lax.broadcasted_iota(jnp.int32, (C, C), 0)
    col = lax.broadcasted_iota(jnp.int32, (C, C), 1)
    rel = (row - col).astype(F32)
    ridx = lax.broadcasted_iota(jnp.int32, (C, 1), 0).astype(F32)
    consts = []
    for h in range(H):
        lgf = lg_ref[0, h]
        lgb = lg_ref[1, h]
        dmat = jnp.where(rel > 0.0, jnp.exp(lgf * jnp.maximum(rel, 0.0)),
                         jnp.where(rel < 0.0, jnp.exp(lgb * jnp.maximum(-rel, 0.0)), 2.0))
        consts.append((dmat, jnp.exp(lgf * (ridx + 1.0)), jnp.exp(lgf * (C - 1.0 - ridx)),
                       jnp.exp(jnp.full((1, 1), C, F32) * lgf)))

    def chunk(c, carry):
        rows = pl.ds(pl.multiple_of(c * C, C), C)
        cos, sin = cos_ref[rows, :], sin_ref[rows, :]
        for h in range(H):
            dmat, q_scale, k_scale, s_decay = consts[h]
            ck = slice(h * DK, (h + 1) * DK)
            cv = slice(h * DV, (h + 1) * DV)
            q = _rope_half(q_ref[rows, ck].astype(F32), cos, sin)
            k = _rope_half(k_ref[rows, ck].astype(F32), cos, sin) * (DK ** -0.5)
            v = v_ref[rows, cv]
            s = lax.dot_general(q.astype(BF16), k.astype(BF16), NT_DIMS, preferred_element_type=F32) * dmat
            st = s_ref[h]
            o_ref[rows, cv] = _dot(s.astype(BF16), v) + _dot((q * q_scale).astype(BF16), st.astype(BF16))
            s_ref[h] = s_decay * st + lax.dot_general((k * k_scale).astype(BF16), v, TN_DIMS,
                                                      preferred_element_type=F32)
        return carry

    lax.fori_loop(0, T // C, chunk, 0)


def _ret_bwd_kernel(lg_ref, q_ref, k_ref, v_ref, g_ref, cos_ref, sin_ref, o1_ref, o_ref, s_ref,
                    *, T, C, H, DK, DV):
    @pl.when(pl.program_id(1) == 0)
    def _():
        s_ref[...] = jnp.zeros_like(s_ref)

    ridx = lax.broadcasted_iota(jnp.int32, (C, 1), 0).astype(F32)
    nc = T // C
    consts = []
    for h in range(H):
        lgb = lg_ref[1, h]
        consts.append((jnp.exp(lgb * (C - ridx)), jnp.exp(lgb * ridx), jnp.exp(jnp.full((1, 1), C, F32) * lgb)))

    def chunk(cc, carry):
        rows = pl.ds(pl.multiple_of((nc - 1 - cc) * C, C), C)
        cos, sin = cos_ref[rows, :], sin_ref[rows, :]
        for h in range(H):
            q_scale, k_scale, s_decay = consts[h]
            ck = slice(h * DK, (h + 1) * DK)
            cv = slice(h * DV, (h + 1) * DV)
            q = _rope_half(q_ref[rows, ck].astype(F32), cos, sin)
            k = _rope_half(k_ref[rows, ck].astype(F32), cos, sin) * (DK ** -0.5)
            v = v_ref[rows, cv]
            st = s_ref[h]
            o = o1_ref[rows, cv] + _dot((q * q_scale).astype(BF16), st.astype(BF16))
            y = o * lax.rsqrt(jnp.mean(o * o, axis=-1, keepdims=True) + NORM_EPS)
            o_ref[rows, cv] = (y * _silu(g_ref[rows, cv].astype(F32))).astype(o_ref.dtype)
            s_ref[h] = s_decay * st + lax.dot_general((k * k_scale).astype(BF16), v, TN_DIMS,
                                                      preferred_element_type=F32)
        return carry

    lax.fori_loop(0, nc, chunk, 0)


def _retention(z, log_gamma, cos, sin):
    B, L, _ = z.shape
    H, C = RET_HEADS, RET_CHUNK
    DK = cos.shape[1] * 2
    DV = 2 * DK
    T = min(L, 512)
    nT = L // T
    kw = dict(T=T, C=C, H=H, DK=DK, DV=DV)
    smem = pl.BlockSpec(memory_space=pltpu.SMEM)
    blocks = [((T, H * DK), BF16)] * 2 + [((T, H * DV), BF16)] * 2 + [((T, H * DV), F32)] * 2
    state = [((H, DK, DV), F32), ((DK, DV), F32), ((DK, DV), F32)]
    o1 = pl.pallas_call(
        functools.partial(_ret_fwd_kernel, **kw),
        grid=(B, nT),
        in_specs=[smem,
                  pl.BlockSpec((None, T, H * DK), lambda b, i: (b, i, 0)),
                  pl.BlockSpec((None, T, H * DK), lambda b, i: (b, i, 1)),
                  pl.BlockSpec((None, T, H * DV), lambda b, i: (b, i, 1)),
                  pl.BlockSpec((T, DK // 2), lambda b, i: (i, 0)),
                  pl.BlockSpec((T, DK // 2), lambda b, i: (i, 0))],
        out_specs=pl.BlockSpec((None, T, H * DV), lambda b, i: (b, i, 0)),
        out_shape=jax.ShapeDtypeStruct((B, L, H * DV), F32),
        scratch_shapes=[pltpu.VMEM((H, DK, DV), F32)],
        compiler_params=_cparams(("parallel", "arbitrary"), blocks, state),
        name="retention_fwd",
    )(log_gamma, z, z, z, cos, sin)
    return pl.pallas_call(
        functools.partial(_ret_bwd_kernel, **kw),
        grid=(B, nT),
        in_specs=[smem,
                  pl.BlockSpec((None, T, H * DK), lambda b, i: (b, nT - 1 - i, 0)),
                  pl.BlockSpec((None, T, H * DK), lambda b, i: (b, nT - 1 - i, 1)),
                  pl.BlockSpec((None, T, H * DV), lambda b, i: (b, nT - 1 - i, 1)),
                  pl.BlockSpec((None, T, H * DV), lambda b, i: (b, nT - 1 - i, 2)),
                  pl.BlockSpec((T, DK // 2), lambda b, i: (nT - 1 - i, 0)),
                  pl.BlockSpec((T, DK // 2), lambda b, i: (nT - 1 - i, 0)),
                  pl.BlockSpec((None, T, H * DV), lambda b, i: (b, nT - 1 - i, 0))],
        out_specs=pl.BlockSpec((None, T, H * DV), lambda b, i: (b, nT - 1 - i, 0)),
        out_shape=jax.ShapeDtypeStruct((B, L, H * DV), BF16),
        scratch_shapes=[pltpu.VMEM((H, DK, DV), F32)],
        compiler_params=_cparams(("parallel", "arbitrary"), blocks, state),
        name="retention_bwd",
    )(log_gamma, z, z, z, z, cos, sin, o1)


def _swa_kernel(sink_ref, q_ref, kp_ref, kc_ref, kn_ref, vp_ref, vc_ref, vn_ref, o_ref, *, HKV, G, BLK):
    i = pl.program_id(1)
    last = pl.num_programs(1) - 1
    R = G * BLK
    r = lax.broadcasted_iota(jnp.int32, (R, 3 * BLK), 0) % BLK
    c = lax.broadcasted_iota(jnp.int32, (R, 3 * BLK), 1)
    rel = r - (c - BLK)
    lo = jnp.where(i == 0, BLK, 0)
    hi = jnp.where(i == last, 2 * BLK, 3 * BLK)
    valid = (jnp.abs(rel) <= WINDOW) & (c >= lo) & (c < hi)
    grp = lax.broadcasted_iota(jnp.int32, (R, 1), 0) // BLK
    for j in range(HKV):
        sl = slice(j * LANES, (j + 1) * LANES)
        k = jnp.concatenate([kp_ref[:, sl], kc_ref[:, sl], kn_ref[:, sl]], axis=0)
        v = jnp.concatenate([vp_ref[:, sl], vc_ref[:, sl], vn_ref[:, sl]], axis=0)
        q = jnp.concatenate([q_ref[:, (j * G + g) * LANES:(j * G + g + 1) * LANES] for g in range(G)], axis=0)
        s = lax.dot_general(q, k, NT_DIMS, preferred_element_type=F32)
        s = jnp.where(valid, s, NEG_INF)
        sink = jnp.zeros((R, 1), F32)
        for g in range(G):
            sink = jnp.where(grp == g, sink_ref[0, j * G + g], sink)
        m = jnp.maximum(jnp.max(s, axis=-1, keepdims=True), sink)
        p = jnp.exp(s - m)
        denom = jnp.sum(p, axis=-1, keepdims=True) + jnp.exp(sink - m)
        o = _dot(p.astype(BF16), v) / denom
        for g in range(G):
            o_ref[:, (j * G + g) * LANES:(j * G + g + 1) * LANES] = o[g * BLK:(g + 1) * BLK].astype(o_ref.dtype)


def _swa_attention(z, sink):
    B, L, _ = z.shape
    BLK = ATTN_BLOCK
    nb = L // BLK
    G = SWA_HQ // SWA_HKV
    qw = SWA_HQ * LANES
    kvw = SWA_HKV * LANES
    kcol = qw // kvw
    prev = lambda b, i: (b, jnp.maximum(i - 1, 0), kcol)
    cur = lambda b, i: (b, i, kcol)
    nxt = lambda b, i: (b, jnp.minimum(i + 1, nb - 1), kcol)
    vprev = lambda b, i: (b, jnp.maximum(i - 1, 0), kcol + 1)
    vcur = lambda b, i: (b, i, kcol + 1)
    vnxt = lambda b, i: (b, jnp.minimum(i + 1, nb - 1), kcol + 1)
    kv = lambda f: pl.BlockSpec((None, BLK, kvw), f)
    return pl.pallas_call(
        functools.partial(_swa_kernel, HKV=SWA_HKV, G=G, BLK=BLK),
        grid=(B, nb),
        in_specs=[pl.BlockSpec(memory_space=pltpu.SMEM),
                  pl.BlockSpec((None, BLK, qw), lambda b, i: (b, i, 0)),
                  kv(prev), kv(cur), kv(nxt), kv(vprev), kv(vcur), kv(vnxt)],
        out_specs=pl.BlockSpec((None, BLK, qw), lambda b, i: (b, i, 0)),
        out_shape=jax.ShapeDtypeStruct((B, L, qw), BF16),
        compiler_params=_cparams(("parallel", "parallel"),
                                 [((BLK, qw), BF16)] * 2 + [((BLK, kvw), BF16)] * 6,
                                 [((G * BLK, 3 * BLK), F32)] * 4),
        name="swa_attention",
    )(sink.reshape(1, SWA_HQ), z, z, z, z, z, z, z)


def _swa_layout(w_in, q_gain, k_gain, w_out, L):
    D = w_in.shape[0]
    dh, hq, hkv = SWA_DH, SWA_HQ, SWA_HKV
    hf = dh // 2
    q_end, k_end = hq * dh, (hq + hkv) * dh

    def rot_slots(w, n):
        w = w.reshape(D, n, 2, hf)
        return jnp.pad(w, ((0, 0), (0, 0), (0, 0), (0, LANES // 2 - hf))).reshape(D, n * LANES)

    def val_slots(w, n):
        return jnp.pad(w.reshape(D, n, dh), ((0, 0), (0, 0), (0, LANES - dh))).reshape(D, n * LANES)

    w = jnp.concatenate([rot_slots(w_in[:, :q_end], hq), rot_slots(w_in[:, q_end:k_end], hkv),
                         val_slots(w_in[:, k_end:], hkv)], axis=1)

    def gain_slot(g):
        return jnp.pad(g.reshape(2, hf), ((0, 0), (0, LANES // 2 - hf))).reshape(1, LANES)

    flag = jnp.concatenate([jnp.ones(((hq + hkv) * LANES,), F32), jnp.zeros((hkv * LANES,), F32)]).reshape(1, -1)
    wo = jnp.pad(w_out.reshape(hq, dh, -1), ((0, 0), (0, LANES - dh), (0, 0))).reshape(hq * LANES, -1)

    inv = ROPE_THETA ** (-np.arange(0, dh, 2) / dh)
    ang = np.arange(L)[:, None] * inv[None, :]
    zero = np.zeros((L, LANES // 2 - hf))
    cos = jnp.asarray(np.concatenate([np.cos(ang), zero, np.cos(ang), zero], axis=1), F32)
    sin = jnp.asarray(np.concatenate([-np.sin(ang), zero, np.sin(ang), zero], axis=1), F32)
    gq = gain_slot(q_gain) * (dh ** -0.5)
    gk = gain_slot(k_gain)
    roll = lambda g: jnp.roll(g, LANES // 2, axis=1)
    cos3 = jnp.stack([cos * gq, cos * gk, jnp.ones_like(cos)])
    sin3 = jnp.stack([sin * roll(gq), sin * roll(gk), jnp.zeros_like(sin)])
    return w, flag, wo, cos3, sin3


def _hg_decays(f_ref, q_ref, rows, lb, tri, before_mid):
    f = lb + (1.0 - lb) * jax.nn.sigmoid(f_ref[rows, :].astype(F32))
    gl = jnp.log(f)
    cum = _cumdot(tri, gl)
    total = jnp.sum(gl, axis=0, keepdims=True)
    mid = jnp.sum(jnp.where(before_mid, gl, 0.0), axis=0, keepdims=True)
    q_dec = _silu(q_ref[rows, :].astype(F32)) * jnp.exp(cum - mid)
    k_inv = (1.0 - f) * jnp.exp(mid - cum)
    q_full = (q_dec * jnp.exp(mid)).astype(BF16)
    k_end = (k_inv * jnp.exp(total - mid)).astype(BF16)
    return q_dec.astype(BF16), k_inv.astype(BF16), q_full, k_end, jnp.exp(total)


def _hg_fwd_kernel(lb_ref, q_ref, i_ref, f_ref, o_ref, st_ref, *, T, C, H, DK, DV):
    @pl.when(pl.program_id(1) == 0)
    def _():
        st_ref[...] = jnp.zeros_like(st_ref)

    row = lax.broadcasted_iota(jnp.int32, (C, C), 0)
    col = lax.broadcasted_iota(jnp.int32, (C, C), 1)
    causal = row >= col
    tri = causal.astype(BF16)
    before_mid = lax.broadcasted_iota(jnp.int32, (C, 1), 0) < C // 2
    lb = lb_ref[...]

    def chunk(c, carry):
        rows = pl.ds(pl.multiple_of(c * C, C), C)
        q_dec, k_inv, q_full, k_end, s_decay = _hg_decays(f_ref, q_ref, rows, lb, tri, before_mid)
        v = i_ref[rows, :]
        for h in range(H):
            ck = slice(h * DK, (h + 1) * DK)
            cv = slice(h * DV, (h + 1) * DV)
            s = lax.dot_general(q_dec[:, ck], k_inv[:, ck], NT_DIMS, preferred_element_type=F32)
            s = jnp.where(causal, s, 0.0)
            st = st_ref[h]
            o_ref[rows, cv] = _dot(s.astype(BF16), v[:, cv]) + lax.dot_general(
                q_full[:, ck], st.astype(BF16), NT_DIMS, preferred_element_type=F32)
            st_ref[h] = st * s_decay[:, ck] + lax.dot_general(v[:, cv], k_end[:, ck], TN_DIMS,
                                                              preferred_element_type=F32)
        return carry

    lax.fori_loop(0, T // C, chunk, 0)


def _hg_bwd_kernel(lb_ref, gain_ref, q_ref, i_ref, f_ref, gate_ref, o1_ref, o_ref, st_ref, *, T, C, H, DK, DV):
    @pl.when(pl.program_id(1) == 0)
    def _():
        st_ref[...] = jnp.zeros_like(st_ref)

    row = lax.broadcasted_iota(jnp.int32, (C, C), 0)
    col = lax.broadcasted_iota(jnp.int32, (C, C), 1)
    anti = col >= row
    tri = anti.astype(BF16)
    after_mid = lax.broadcasted_iota(jnp.int32, (C, 1), 0) >= C // 2
    lb = lb_ref[...]
    gain = gain_ref[...]
    nc = T // C

    def chunk(cc, carry):
        rows = pl.ds(pl.multiple_of((nc - 1 - cc) * C, C), C)
        q_dec, k_inv, q_full, k_end, s_decay = _hg_decays(f_ref, q_ref, rows, lb, tri, after_mid)
        v = i_ref[rows, :]
        for h in range(H):
            ck = slice(h * DK, (h + 1) * DK)
            cv = slice(h * DV, (h + 1) * DV)
            s = lax.dot_general(q_dec[:, ck], k_inv[:, ck], NT_DIMS, preferred_element_type=F32)
            s = jnp.where(anti, s, 0.0)
            st = st_ref[h]
            o = o1_ref[rows, cv] + _dot(s.astype(BF16), v[:, cv]) + lax.dot_general(
                q_full[:, ck], st.astype(BF16), NT_DIMS, preferred_element_type=F32)
            y = o * lax.rsqrt(jnp.mean(o * o, axis=-1, keepdims=True) + NORM_EPS) * gain
            o_ref[rows, cv] = (y * _silu(gate_ref[rows, cv].astype(F32))).astype(o_ref.dtype)
            st_ref[h] = st * s_decay[:, ck] + lax.dot_general(v[:, cv], k_end[:, ck], TN_DIMS,
                                                              preferred_element_type=F32)
        return carry

    lax.fori_loop(0, nc, chunk, 0)


def _hgrn(z, lb, gain):
    B, L, _ = z.shape
    H, C = HG_HEADS, HG_CHUNK
    HD = lb.shape[1]
    DK = HD // H
    DV = gain.shape[0]
    T = min(L, 512)
    nT = L // T
    kw = dict(T=T, C=C, H=H, DK=DK, DV=DV)
    blocks = [((T, HD), BF16)] * 5 + [((T, H * DV), F32)] * 2
    state = [((H, DV, DK), F32)] + [((C, HD), F32)] * 12
    lb3 = lb.reshape(2, 1, HD)
    o1 = pl.pallas_call(
        functools.partial(_hg_fwd_kernel, **kw),
        grid=(B, nT),
        in_specs=[pl.BlockSpec((None, 1, HD), lambda b, i: (0, 0, 0)),
                  pl.BlockSpec((None, T, HD), lambda b, i: (b, i, 0)),
                  pl.BlockSpec((None, T, H * DV), lambda b, i: (b, i, 1)),
                  pl.BlockSpec((None, T, HD), lambda b, i: (b, i, 2))],
        out_specs=pl.BlockSpec((None, T, H * DV), lambda b, i: (b, i, 0)),
        out_shape=jax.ShapeDtypeStruct((B, L, H * DV), F32),
        scratch_shapes=[pltpu.VMEM((H, DV, DK), F32)],
        compiler_params=_cparams(("parallel", "arbitrary"), blocks, state),
        name="hgrn_fwd",
    )(lb3, z, z, z)
    return pl.pallas_call(
        functools.partial(_hg_bwd_kernel, **kw),
        grid=(B, nT),
        in_specs=[pl.BlockSpec((None, 1, HD), lambda b, i: (1, 0, 0)),
                  pl.BlockSpec((1, DV), lambda b, i: (0, 0)),
                  pl.BlockSpec((None, T, HD), lambda b, i: (b, nT - 1 - i, 0)),
                  pl.BlockSpec((None, T, H * DV), lambda b, i: (b, nT - 1 - i, 1)),
                  pl.BlockSpec((None, T, HD), lambda b, i: (b, nT - 1 - i, 3)),
                  pl.BlockSpec((None, T, H * DV), lambda b, i: (b, nT - 1 - i, 4)),
                  pl.BlockSpec((None, T, H * DV), lambda b, i: (b, nT - 1 - i, 0))],
        out_specs=pl.BlockSpec((None, T, H * DV), lambda b, i: (b, nT - 1 - i, 0)),
        out_shape=jax.ShapeDtypeStruct((B, L, H * DV), BF16),
        scratch_shapes=[pltpu.VMEM((H, DV, DK), F32)],
        compiler_params=_cparams(("parallel", "arbitrary"), blocks, state),
        name="hgrn_bwd",
    )(lb3, gain.reshape(1, DV), z, z, z, z, o1)


def _ret_rope_tables(L, dk):
    inv = ROPE_THETA ** (-np.arange(0, dk, 2) / dk)
    ang = np.arange(L)[:, None] * inv[None, :]
    return jnp.asarray(np.cos(ang), F32), jnp.asarray(np.sin(ang), F32)


def kernel(x_prompt, x_sample, c_prompt, c_sample, ada_w, ada_b, norm_g, hy_w_in, hy_conv_w, hy_conv_b, hy_w1, hy_b1, hy_w2, hy_b2, hy_w3, hy_freq, hy_decay, hy_skip, hy_w_out, ret_w_in, ret_decay, ret_w_out, swa_w_in, swa_q_gain, swa_k_gain, swa_sink, swa_w_out, hg_w_in, hg_lb, hg_gain, hg_w_out, ffn_w_gate, ffn_w_val, ffn_conv_w, ffn_conv_b, ffn_w_down):
    depth, D = norm_g.shape[0], norm_g.shape[2]
    groups = [(x_prompt, c_prompt), (x_sample, c_sample)]
    mods = _ada_mod(jnp.concatenate([c for _, c in groups], axis=0), ada_w, ada_b)

    bf = lambda w: w.astype(BF16)
    hy_w_in_b, hy_w_out_b = bf(hy_w_in), bf(hy_w_out)
    ret_w_in_b, ret_w_out_b = bf(ret_w_in), bf(ret_w_out)
    hg_w_in_b, hg_w_out_b = bf(hg_w_in), bf(hg_w_out)
    wg_b, wv_b, wd_b = bf(ffn_w_gate), bf(ffn_w_val), bf(ffn_w_down)
    hg_sm = jax.nn.softmax(hg_lb.astype(F32), axis=1)
    hg_lower = jnp.cumsum(hg_sm, axis=1) - hg_sm
    ret_log_gamma = -jnp.exp(ret_decay.astype(F32))

    outs = []
    row0 = 0
    for x, c in groups:
        B, L, _ = x.shape
        for layer in range(depth):
            kind, j = layer % N_MIXERS, layer // N_MIXERS
            mod = mods[layer, row0:row0 + B].reshape(B, N_MOD, 1, D)
            sh1, sc1, g1, sh2, sc2, g2 = (mod[:, m] for m in range(N_MOD))
            gn1 = norm_g[layer, 0].reshape(1, D)
            gn2 = norm_g[layer, 1].reshape(1, D)
            if kind == 0:
                tabs = _fft_tables(L)
                z = _proj_in_conv(x, gn1, sc1, sh1, hy_w_in_b[j], hy_conv_w[j], hy_conv_b[j], tn=1024)
                taps, asum = _hy_filter_taps(L, hy_w1[j], hy_b1[j], hy_w2[j], hy_b2[j], hy_w3[j], hy_freq[j],
                                             hy_decay[j])
                spec = _hy_spectrum(taps, asum, tabs)
                a = _hy_conv(z, hy_skip[j], spec, tabs)
                x = _proj_out(a, hy_w_out_b[j], x, g1)
            elif kind == 1:
                z = _proj_in(x, gn1, sc1, sh1, ret_w_in_b[j], tn=1024)
                cos, sin = _ret_rope_tables(L, D // RET_HEADS)
                a = _retention(z, ret_log_gamma[j], cos, sin)
                x = _proj_out(a, ret_w_out_b[j], x, g1)
            elif kind == 2:
                w, flag, wo, cos, sin = _swa_layout(swa_w_in[j], swa_q_gain[j], swa_k_gain[j], swa_w_out[j], L)
                z = _proj_in_rope(x, gn1, sc1, sh1, bf(w), flag, cos, sin, tn=SWA_HKV * LANES, dh=SWA_DH,
                                  q_tiles=SWA_HQ // SWA_HKV)
                a = _swa_attention(z, swa_sink[j])
                x = _proj_out(a, bf(wo), x, g1)
            else:
                z = _proj_in(x, gn1, sc1, sh1, hg_w_in_b[j], tn=1024)
                a = _hgrn(z, hg_lower[:, layer], hg_gain[j])
                x = _proj_out(a, hg_w_out_b[j], x, g1)
            x = _ffn(x, gn2, sc2, sh2, g2, wg_b[layer], wv_b[layer], ffn_conv_w[layer], ffn_conv_b[layer],
                     wd_b[layer])
        outs.append(x)
        row0 += B
    return tuple(outs)
```

```python
import functools
import math

import numpy as np
import jax
import jax.numpy as jnp
from jax import lax
from jax.experimental import pallas as pl
from jax.experimental.pallas import tpu as pltpu

F32 = jnp.float32
BF16 = jnp.bfloat16

NORM_EPS = 1e-6
N_MIXERS = 4
N_MOD = 6
HY_BANDS = 16
RET_HEADS = 4
RET_CHUNK = 128
SWA_HQ = 16
SWA_HKV = 4
SWA_DH = 64
WINDOW = 128
ATTN_BLOCK = 128
ROPE_THETA = 10000.0
NEG_INF = -1e30
HG_HEADS = 8
HG_CHUNK = 128

LANES = 128
SUBLANES = 8
BF16_ROWS = 16
VMEM_LIMIT_CAP = 60 * 1024 * 1024
VMEM_SLACK = 8 * 1024 * 1024

NT_DIMS = (((1,), (1,)), ((), ()))
TN_DIMS = (((0,), (0,)), ((), ()))


def _nbytes(shape, dtype):
    item = jnp.dtype(dtype).itemsize
    sub = SUBLANES * 4 // item
    dims = list(shape)
    dims[-1] = -(-dims[-1] // LANES) * LANES
    if len(dims) > 1:
        dims[-2] = -(-dims[-2] // sub) * sub
    return int(np.prod(dims)) * item


def _cparams(semantics, pipelined, resident):
    need = 2 * sum(_nbytes(s, d) for s, d in pipelined) + sum(_nbytes(s, d) for s, d in resident)
    return pltpu.CompilerParams(dimension_semantics=semantics,
                                vmem_limit_bytes=min(need + VMEM_SLACK, VMEM_LIMIT_CAP))


def _dot(a, b):
    return jnp.dot(a, b, preferred_element_type=F32)


def _split(x):
    hi = x.astype(BF16)
    lo = (x - hi.astype(F32)).astype(BF16)
    return hi, lo


def _dot_tab(t_cat, d):
    d_hi, d_lo = _split(d)
    return _dot(t_cat, jnp.concatenate([d_hi, d_hi, d_lo], axis=0))


def _dot3(a, b):
    a_hi, a_lo = _split(a)
    b_hi, b_lo = _split(b)
    return _dot(a_hi, b_hi) + _dot(a_lo, b_hi) + _dot(a_hi, b_lo)


def _cumdot(tri, g):
    g0 = g.astype(BF16)
    r1 = g - g0.astype(F32)
    g1 = r1.astype(BF16)
    g2 = (r1 - g1.astype(F32)).astype(BF16)
    return _dot(tri, g0) + _dot(tri, g1) + _dot(tri, g2)


def _silu(x):
    return x * jax.nn.sigmoid(x)


def _norm_mod(x, g, sc, sh):
    xf = x.astype(F32)
    y = xf * lax.rsqrt(jnp.mean(xf * xf, axis=-1, keepdims=True) + NORM_EPS)
    return (y * g) * (1.0 + sc) + sh


def _row_tile(L):
    return min(L, 1024)


def _ada_kernel(c_ref, w_ref, b_ref, o_ref):
    cs = _silu(c_ref[...]).astype(BF16)
    o_ref[...] = _dot(cs, w_ref[...].astype(BF16)) + b_ref[...]


def _ada_mod(c_all, ada_w, ada_b):
    depth, D, N = ada_w.shape
    R = c_all.shape[0]
    tn = 1024
    return pl.pallas_call(
        _ada_kernel,
        grid=(depth, N // tn),
        in_specs=[pl.BlockSpec((R, D), lambda l, j: (0, 0)),
                  pl.BlockSpec((None, D, tn), lambda l, j: (l, 0, j)),
                  pl.BlockSpec((None, 1, tn), lambda l, j: (l, 0, j))],
        out_specs=pl.BlockSpec((None, R, tn), lambda l, j: (l, 0, j)),
        out_shape=jax.ShapeDtypeStruct((depth, R, N), F32),
        compiler_params=_cparams(("parallel", "parallel"),
                                 [((D, tn), F32), ((R, D), F32), ((R, tn), F32)], []),
        name="ada_mod",
    )(c_all, ada_w, ada_b.reshape(depth, 1, N))


def _fill_h(h_ref, x_ref, g_ref, sc_ref, sh_ref):
    h_ref[...] = _norm_mod(x_ref[...], g_ref[...], sc_ref[...], sh_ref[...]).astype(BF16)


def _fill_h_halo(h_ref, x_ref, xp_ref, xn_ref, g_ref, sc_ref, sh_ref, tm):
    i = pl.program_id(1)
    last = pl.num_programs(1) - 1
    g, sc, sh = g_ref[...], sc_ref[...], sh_ref[...]
    h_ref[BF16_ROWS:BF16_ROWS + tm, :] = _norm_mod(x_ref[...], g, sc, sh).astype(BF16)
    hp = _norm_mod(xp_ref[...], g, sc, sh)
    hn = _norm_mod(xn_ref[...], g, sc, sh)
    h_ref[0:BF16_ROWS, :] = jnp.where(i > 0, hp, 0.0).astype(BF16)
    h_ref[BF16_ROWS + tm:, :] = jnp.where(i < last, hn, 0.0).astype(BF16)


def _conv3_rows(z_ref, cw, cb, tm):
    o = BF16_ROWS
    return (z_ref[o - 1:o - 1 + tm, :] * cw[0:1, :] + z_ref[o:o + tm, :] * cw[1:2, :]
            + z_ref[o + 1:o + 1 + tm, :] * cw[2:3, :] + cb)


def _proj_plain_kernel(x_ref, g_ref, sc_ref, sh_ref, w_ref, o_ref, h_ref):
    @pl.when(pl.program_id(2) == 0)
    def _():
        _fill_h(h_ref, x_ref, g_ref, sc_ref, sh_ref)

    o_ref[...] = _dot(h_ref[...], w_ref[...]).astype(o_ref.dtype)


def _proj_conv_kernel(x_ref, xp_ref, xn_ref, g_ref, sc_ref, sh_ref, w_ref, cw_ref, cb_ref,
                      o_ref, h_ref, z_ref, *, tm):
    @pl.when(pl.program_id(2) == 0)
    def _():
        _fill_h_halo(h_ref, x_ref, xp_ref, xn_ref, g_ref, sc_ref, sh_ref, tm)

    z_ref[...] = _dot(h_ref[...], w_ref[...])
    o_ref[...] = _conv3_rows(z_ref, cw_ref[...], cb_ref[...], tm).astype(o_ref.dtype)


def _proj_rope_kernel(x_ref, g_ref, sc_ref, sh_ref, w_ref, flag_ref, ones_ref, cos_ref, sin_ref,
                      o_ref, h_ref, *, dh):
    @pl.when(pl.program_id(2) == 0)
    def _():
        _fill_h(h_ref, x_ref, g_ref, sc_ref, sh_ref)

    z = _dot(h_ref[...], w_ref[...])
    ms = _dot((z * z).astype(BF16), ones_ref[...]) * (1.0 / dh)
    zn = z * jnp.where(flag_ref[...] > 0.0, lax.rsqrt(ms + NORM_EPS), 1.0)
    cos, sin = cos_ref[...], sin_ref[...]
    for s in range(z.shape[1] // LANES):
        sl = slice(s * LANES, (s + 1) * LANES)
        zs = zn[:, sl]
        o_ref[:, sl] = (zs * cos + pltpu.roll(zs, LANES // 2, axis=1) * sin).astype(o_ref.dtype)


def _mod_specs(D):
    return [pl.BlockSpec((1, D), lambda b, i, j: (0, 0)),
            pl.BlockSpec((None, 1, D), lambda b, i, j: (b, 0, 0)),
            pl.BlockSpec((None, 1, D), lambda b, i, j: (b, 0, 0))]


def _halo_specs(tm, D, L):
    hb = tm // BF16_ROWS
    nhb = L // BF16_ROWS
    return [pl.BlockSpec((None, tm, D), lambda b, i, j: (b, i, 0)),
            pl.BlockSpec((None, BF16_ROWS, D), lambda b, i, j: (b, jnp.maximum(i * hb - 1, 0), 0)),
            pl.BlockSpec((None, BF16_ROWS, D), lambda b, i, j: (b, jnp.minimum((i + 1) * hb, nhb - 1), 0))]


def _proj_in(x, g, sc, sh, w, *, tn):
    B, L, D = x.shape
    N = w.shape[1]
    tm = _row_tile(L)
    return pl.pallas_call(
        _proj_plain_kernel,
        grid=(B, L // tm, N // tn),
        in_specs=[pl.BlockSpec((None, tm, D), lambda b, i, j: (b, i, 0))] + _mod_specs(D)
        + [pl.BlockSpec((D, tn), lambda b, i, j: (0, j))],
        out_specs=pl.BlockSpec((None, tm, tn), lambda b, i, j: (b, i, j)),
        out_shape=jax.ShapeDtypeStruct((B, L, N), BF16),
        scratch_shapes=[pltpu.VMEM((tm, D), BF16)],
        compiler_params=_cparams(("parallel", "parallel", "arbitrary"),
                                 [((tm, D), F32), ((D, tn), BF16), ((tm, tn), BF16)],
                                 [((tm, D), BF16), ((tm, tn), F32)]),
        name="proj_in",
    )(x, g, sc, sh, w)


def _proj_in_conv(x, g, sc, sh, w, cw, cb, *, tn):
    B, L, D = x.shape
    N = w.shape[1]
    tm = _row_tile(L)
    te = tm + 2 * BF16_ROWS
    return pl.pallas_call(
        functools.partial(_proj_conv_kernel, tm=tm),
        grid=(B, L // tm, N // tn),
        in_specs=_halo_specs(tm, D, L) + _mod_specs(D)
        + [pl.BlockSpec((D, tn), lambda b, i, j: (0, j)),
           pl.BlockSpec((3, tn), lambda b, i, j: (0, j)),
           pl.BlockSpec((1, tn), lambda b, i, j: (0, j))],
        out_specs=pl.BlockSpec((None, tm, tn), lambda b, i, j: (b, i, j)),
        out_shape=jax.ShapeDtypeStruct((B, L, N), BF16),
        scratch_shapes=[pltpu.VMEM((te, D), BF16), pltpu.VMEM((te, tn), F32)],
        compiler_params=_cparams(("parallel", "parallel", "arbitrary"),
                                 [((tm, D), F32), ((D, tn), BF16), ((tm, tn), BF16)],
                                 [((te, D), BF16), ((te, tn), F32), ((te, tn), F32)]),
        name="proj_in_conv",
    )(x, x, x, g, sc, sh, w, cw, cb.reshape(1, N))


def _proj_in_rope(x, g, sc, sh, w, flag, cos, sin, *, tn, dh, q_tiles):
    B, L, D = x.shape
    N = w.shape[1]
    tm = _row_tile(L)
    slot = lax.broadcasted_iota(jnp.int32, (tn, tn), 0) // LANES == lax.broadcasted_iota(jnp.int32, (tn, tn), 1) // LANES
    table = lambda b, i, j: (jnp.maximum(j - (q_tiles - 1), 0), i, 0)
    return pl.pallas_call(
        functools.partial(_proj_rope_kernel, dh=dh),
        grid=(B, L // tm, N // tn),
        in_specs=[pl.BlockSpec((None, tm, D), lambda b, i, j: (b, i, 0))] + _mod_specs(D)
        + [pl.BlockSpec((D, tn), lambda b, i, j: (0, j)),
           pl.BlockSpec((1, tn), lambda b, i, j: (0, j)),
           pl.BlockSpec((tn, tn), lambda b, i, j: (0, 0)),
           pl.BlockSpec((None, tm, LANES), table),
           pl.BlockSpec((None, tm, LANES), table)],
        out_specs=pl.BlockSpec((None, tm, tn), lambda b, i, j: (b, i, j)),
        out_shape=jax.ShapeDtypeStruct((B, L, N), BF16),
        scratch_shapes=[pltpu.VMEM((tm, D), BF16)],
        compiler_params=_cparams(("parallel", "parallel", "arbitrary"),
                                 [((tm, D), F32), ((D, tn), BF16), ((tm, tn), BF16), ((tm, LANES), F32),
                                  ((tm, LANES), F32), ((tn, tn), BF16)],
                                 [((tm, D), BF16), ((tm, tn), F32), ((tm, tn), F32), ((tm, tn), F32)]),
        name="proj_in_rope",
    )(x, g, sc, sh, w, flag, slot.astype(BF16), cos, sin)


def _proj_out_kernel(a_ref, w_ref, x_ref, gate_ref, o_ref):
    o_ref[...] = x_ref[...] + gate_ref[...] * _dot(a_ref[...], w_ref[...])


def _proj_out(a, w, x, gate):
    B, L, K = a.shape
    D = w.shape[1]
    tm = _row_tile(L)
    return pl.pallas_call(
        _proj_out_kernel,
        grid=(B, L // tm),
        in_specs=[pl.BlockSpec((None, tm, K), lambda b, i: (b, i, 0)),
                  pl.BlockSpec((K, D), lambda b, i: (0, 0)),
                  pl.BlockSpec((None, tm, D), lambda b, i: (b, i, 0)),
                  pl.BlockSpec((None, 1, D), lambda b, i: (b, 0, 0))],
        out_specs=pl.BlockSpec((None, tm, D), lambda b, i: (b, i, 0)),
        out_shape=jax.ShapeDtypeStruct((B, L, D), F32),
        compiler_params=_cparams(("parallel", "parallel"),
                                 [((tm, K), BF16), ((K, D), BF16), ((tm, D), F32), ((tm, D), F32)],
                                 [((tm, D), F32)]),
        name="proj_out",
    )(a, w, x, gate)


def _ffn_kernel(x_ref, xp_ref, xn_ref, g_ref, sc_ref, sh_ref, gate_ref, wg_ref, wv_ref, cw_ref, cb_ref,
                wd_ref, o_ref, h_ref, z_ref, *, tm, tf):
    _fill_h_halo(h_ref, x_ref, xp_ref, xn_ref, g_ref, sc_ref, sh_ref, tm)
    acc = None
    for c in range(wg_ref.shape[1] // tf):
        cols = slice(c * tf, (c + 1) * tf)
        z_ref[...] = _dot(h_ref[...], wg_ref[:, cols])
        a = _conv3_rows(z_ref, cw_ref[:, cols], cb_ref[:, cols], tm)
        v = _dot(h_ref[BF16_ROWS:BF16_ROWS + tm, :], wv_ref[:, cols])
        part = _dot((_silu(a) * v).astype(BF16), wd_ref[cols, :])
        acc = part if acc is None else acc + part
    o_ref[...] = x_ref[...] + gate_ref[...] * acc


def _ffn(x, g, sc, sh, gate, wg, wv, cw, cb, wd):
    B, L, D = x.shape
    F = wg.shape[1]
    tm = min(L, 512)
    te = tm + 2 * BF16_ROWS
    tf = F // 2
    const = lambda b, i, j: (0, 0)
    resident = lambda shape: pl.BlockSpec(shape, const, pipeline_mode=pl.Buffered(1))
    return pl.pallas_call(
        functools.partial(_ffn_kernel, tm=tm, tf=tf),
        grid=(B, L // tm, 1),
        in_specs=_halo_specs(tm, D, L) + _mod_specs(D)
        + [pl.BlockSpec((None, 1, D), lambda b, i, j: (b, 0, 0)),
           resident((D, F)), resident((D, F)), resident((3, F)), resident((1, F)), resident((F, D))],
        out_specs=pl.BlockSpec((None, tm, D), lambda b, i, j: (b, i, 0)),
        out_shape=jax.ShapeDtypeStruct((B, L, D), F32),
        scratch_shapes=[pltpu.VMEM((te, D), BF16), pltpu.VMEM((te, tf), F32)],
        compiler_params=_cparams(("parallel", "parallel", "arbitrary"),
                                 [((tm, D), F32), ((tm, D), F32)],
                                 [((D, F), BF16), ((D, F), BF16), ((F, D), BF16), ((te, D), BF16),
                                  ((te, tf), F32), ((tm, tf), F32), ((tm, tf), F32), ((tm, tf), BF16),
                                  ((tm, D), F32), ((tm, D), F32)]),
        name="ffn",
    )(x, x, x, g, sc, sh, gate, wg, wv, cw, cb.reshape(1, F), wd)


def _fft_dims(L):
    N = 2 * L
    p = N.bit_length() - 1
    n1 = 1 << ((p + 1) // 2)
    return N, n1, N // n1


def _k1_pad(n1):
    return n1 // 2 + 8


def _fft_tables(L):
    N, n1, n2 = _fft_dims(L)
    k1p = _k1_pad(n1)
    k1 = np.arange(k1p)
    valid = (k1 <= n1 // 2).astype(np.float64)
    weight = np.where((k1 == 0) | (k1 == n1 // 2), 1.0, 2.0) * valid
    m1 = np.arange(n1 // 2)
    n2v = np.arange(n2)
    n_idx = n2 * m1[None, :] + n2v[:, None]
    phi = 2.0 * np.pi * k1[None, :, None] * n_idx[:, None, :] / N
    s1 = np.concatenate([np.cos(phi) * valid[None, :, None], -np.sin(phi) * valid[None, :, None]], axis=1)
    phit = np.transpose(phi, (0, 2, 1))
    gl = np.concatenate([np.cos(phit) * weight[None, None, :], -np.sin(phit) * weight[None, None, :]],
                        axis=2) / N
    th = 2.0 * np.pi * np.outer(n2v, n2v) / n2
    c, s = np.cos(th), np.sin(th)
    f2 = np.block([[c, s], [-s, c]])
    f2c = np.block([[c, -s], [s, c]])

    def cat(a):
        a32 = jnp.asarray(a, F32)
        hi = a32.astype(BF16)
        lo = (a32 - hi.astype(F32)).astype(BF16)
        return jnp.concatenate([hi, lo, hi], axis=-1)

    return dict(N=N, n1=n1, n2=n2, k1p=k1p, s1=cat(s1), gl=cat(gl), f2=cat(f2), f2c=cat(f2c))


def _hy_positions(L):
    idx = np.concatenate([np.arange(L), np.array([0]), np.arange(L - 1, 0, -1)])
    t = idx / (L - 1)
    ang = (2.0 * np.pi / L) * idx
    bands = np.linspace(1e-4, HY_BANDS - 1, HY_BANDS)
    z = np.concatenate([t[:, None], np.cos(bands[None, :] * ang[:, None]), -np.sin(bands[None, :] * ang[:, None])],
                       axis=1)
    zp = np.zeros((2 * L, LANES))
    zp[:, : z.shape[1]] = z
    return jnp.asarray(zp, F32)


def _hy_mlp_kernel(z_ref, w1_ref, b1_ref, w2_ref, b2_ref, w3_ref, fr_ref, dec_ref, taps_ref, asum_ref, *, L, tr):
    i = pl.program_id(0)
    z = z_ref[...]
    fr = fr_ref[...]
    h = jnp.sin(fr * (_dot3(z, w1_ref[...]) + b1_ref[...]))
    h = jnp.sin(fr * (_dot3(h, w2_ref[...]) + b2_ref[...]))
    h = _dot3(h, w3_ref[...])
    t = z[:, 0:1]
    taps = h * jnp.exp(-t * jnp.abs(dec_ref[...]))
    row = i * tr + lax.broadcasted_iota(jnp.int32, (tr, 1), 0)
    taps = jnp.where(row == L, 0.0, taps)
    taps_ref[...] = taps

    @pl.when(i == 0)
    def _():
        asum_ref[...] = jnp.zeros_like(asum_ref)

    asum_ref[...] += jnp.sum(jnp.abs(taps), axis=0, keepdims=True)


def _hy_filter_taps(L, w1, b1, w2, b2, w3, freq, decay):
    W = decay.shape[1]
    E, O = w1.shape
    tr = 512
    half = L // tr
    z = _hy_positions(L)
    w1p = jnp.zeros((LANES, O), F32).at[:E].set(w1)
    return pl.pallas_call(
        functools.partial(_hy_mlp_kernel, L=L, tr=tr),
        grid=(2 * L // tr,),
        in_specs=[pl.BlockSpec((tr, LANES), lambda i: (i, 0)),
                  pl.BlockSpec((LANES, O), lambda i: (0, 0)),
                  pl.BlockSpec((1, O), lambda i: (0, 0)),
                  pl.BlockSpec((O, O), lambda i: (0, 0)),
                  pl.BlockSpec((1, O), lambda i: (0, 0)),
                  pl.BlockSpec((O, W), lambda i: (0, i // half)),
                  pl.BlockSpec((1, O), lambda i: (0, 0)),
                  pl.BlockSpec((None, 1, W), lambda i: (i // half, 0, 0))],
        out_specs=[pl.BlockSpec((tr, W), lambda i: (i, 0)),
                   pl.BlockSpec((1, W), lambda i: (0, 0))],
        out_shape=[jax.ShapeDtypeStruct((2 * L, W), F32), jax.ShapeDtypeStruct((1, W), F32)],
        compiler_params=_cparams(("arbitrary",), [((tr, W), F32), ((O, W), F32), ((tr, LANES), F32)],
                                 [((tr, W), F32), ((tr, W), F32)]),
        name="hyena_filter_mlp",
    )(z, w1p, b1.reshape(1, O), w2, b2.reshape(1, O), w3, freq.reshape(1, O), decay.reshape(2, 1, W))


FFT_UNROLL = 16


def _fft_stage1(src_ref, src_off, a_ref, s1_ref, n1, n2, k1p, sign=None, src_off2=None):
    def body(j, carry):
        d = src_ref[pl.ds(src_off + j, n1 // 2, stride=n2), :]
        r = _dot_tab(s1_ref[j], d)
        if src_off2 is not None:
            d2 = src_ref[pl.ds(src_off2 + j, n1 // 2, stride=n2), :]
            r = r + sign * _dot_tab(s1_ref[j], d2)
        a_ref[pl.ds(pl.multiple_of(j * 2 * k1p, 8), 2 * k1p), :] = r
        return carry

    lax.fori_loop(0, n2, body, 0, unroll=FFT_UNROLL)


def _load_k1(a_ref, k, n2, k1p):
    br = a_ref[pl.ds(k, n2, stride=2 * k1p), :]
    bi = a_ref[pl.ds(k1p + k, n2, stride=2 * k1p), :]
    return jnp.concatenate([br, bi], axis=0)


def _hy_spec_kernel(taps_ref, asum_ref, s1_ref, f2_ref, h_ref, a_ref, *, L, n1, n2, k1p):
    r = lax.broadcasted_iota(jnp.int32, (2 * k1p, 1), 0)
    k1 = jnp.where(r < k1p, r, r - k1p)
    sign = (1 - 2 * (k1 & 1)).astype(F32)
    _fft_stage1(taps_ref, 0, a_ref, s1_ref, n1, n2, k1p, sign=sign, src_off2=L)
    inv = 1.0 / asum_ref[...]

    def body(k, carry):
        h_ref[k] = _dot_tab(f2_ref[...], _load_k1(a_ref, k, n2, k1p)) * inv
        return carry

    lax.fori_loop(0, n1 // 2, body, 0, unroll=2)
    body(n1 // 2, 0)


def _hy_spectrum(taps, asum, tabs):
    n1, n2, k1p = tabs["n1"], tabs["n2"], tabs["k1p"]
    N, W = taps.shape
    L = N // 2
    cb = LANES
    k1v = n1 // 2 + 1
    s1, f2 = tabs["s1"], tabs["f2"]
    return pl.pallas_call(
        functools.partial(_hy_spec_kernel, L=L, n1=n1, n2=n2, k1p=k1p),
        grid=(W // cb,),
        in_specs=[pl.BlockSpec((N, cb), lambda c: (0, c)),
                  pl.BlockSpec((1, cb), lambda c: (0, c)),
                  pl.BlockSpec(s1.shape, lambda c: (0, 0, 0), pipeline_mode=pl.Buffered(1)),
                  pl.BlockSpec(f2.shape, lambda c: (0, 0), pipeline_mode=pl.Buffered(1))],
        out_specs=pl.BlockSpec((k1v, 2 * n2, cb), lambda c: (0, 0, c)),
        out_shape=jax.ShapeDtypeStruct((k1v, 2 * n2, W), F32),
        scratch_shapes=[pltpu.VMEM((n2 * 2 * k1p, cb), F32)],
        compiler_params=_cparams(("parallel",),
                                 [((N, cb), F32), ((k1v, 2 * n2, cb), F32)],
                                 [((n2 * 2 * k1p, cb), F32), (s1.shape, BF16), (f2.shape, BF16)]),
        name="hyena_filter_spectrum",
    )(taps, asum, s1, f2)


def _hy_conv_kernel(x0_ref, x1_ref, v_ref, skip_ref, h_ref, s1_ref, f2_ref, f2c_ref, gl_ref,
                    o_ref, u_ref, a_ref, *, n1, n2, k1p):
    u_ref[...] = x1_ref[...].astype(F32) * v_ref[...].astype(F32)
    _fft_stage1(u_ref, 0, a_ref, s1_ref, n1, n2, k1p)

    cb = u_ref.shape[1]

    def mid(groups):
        loaded = [jnp.concatenate([_load_k1(a_ref, k, n2, k1p) for k in ks], axis=1) for ks in groups]
        results = []
        for ks, rhs in zip(groups, loaded):
            x = _dot_tab(f2_ref[...], rhs)
            xr, xi = x[:n2], x[n2:]
            hr = jnp.concatenate([h_ref[k, :n2, :] for k in ks], axis=1)
            hi = jnp.concatenate([h_ref[k, n2:, :] for k in ks], axis=1)
            y = jnp.concatenate([xr * hr - xi * hi, xr * hi + xi * hr], axis=0)
            results.append(_dot_tab(f2c_ref[...], y))
        for ks, c in zip(groups, results):
            for p, k in enumerate(ks):
                a_ref[pl.ds(k, n2, stride=2 * k1p), :] = c[:n2, p * cb:(p + 1) * cb]
                a_ref[pl.ds(k1p + k, n2, stride=2 * k1p), :] = c[n2:, p * cb:(p + 1) * cb]

    def mid_oct(ko, carry):
        k = 8 * ko
        mid(((k, k + 1, k + 2, k + 3), (k + 4, k + 5, k + 6, k + 7)))
        return carry

    lax.fori_loop(0, n1 // 16, mid_oct, 0)
    mid(((n1 // 2,),))
    skip = skip_ref[...]

    def last(j, carry):
        rhs = a_ref[pl.ds(pl.multiple_of(j * 2 * k1p, 8), 2 * k1p), :]
        y = _dot_tab(gl_ref[j], rhs)
        rows = pl.ds(j, n1 // 2, stride=n2)
        u_ref[rows, :] = y + u_ref[rows, :] * skip
        return carry

    lax.fori_loop(0, n2, last, 0, unroll=FFT_UNROLL)
    o_ref[...] = (u_ref[...].astype(F32) * x0_ref[...].astype(F32)).astype(o_ref.dtype)


def _hy_conv(z, skip, spec, tabs):
    n1, n2, k1p = tabs["n1"], tabs["n2"], tabs["k1p"]
    B, L, W3 = z.shape
    W = W3 // 3
    cb = LANES
    nc = W // cb
    k1v = n1 // 2 + 1
    tables = [tabs["s1"], tabs["f2"], tabs["f2c"], tabs["gl"]]

    def const_spec(t):
        return pl.BlockSpec(t.shape, (lambda c, b: (0, 0, 0)) if t.ndim == 3 else (lambda c, b: (0, 0)),
                            pipeline_mode=pl.Buffered(1))

    return pl.pallas_call(
        functools.partial(_hy_conv_kernel, n1=n1, n2=n2, k1p=k1p),
        grid=(nc, B),
        in_specs=[pl.BlockSpec((None, L, cb), lambda c, b: (b, 0, c), pipeline_mode=pl.Buffered(1)),
                  pl.BlockSpec((None, L, cb), lambda c, b: (b, 0, nc + c), pipeline_mode=pl.Buffered(1)),
                  pl.BlockSpec((None, L, cb), lambda c, b: (b, 0, 2 * nc + c), pipeline_mode=pl.Buffered(1)),
                  pl.BlockSpec((1, cb), lambda c, b: (0, c)),
                  pl.BlockSpec((k1v, 2 * n2, cb), lambda c, b: (0, 0, c), pipeline_mode=pl.Buffered(1))]
        + [const_spec(t) for t in tables],
        out_specs=pl.BlockSpec((None, L, cb), lambda c, b: (b, 0, c)),
        out_shape=jax.ShapeDtypeStruct((B, L, W), BF16),
        scratch_shapes=[pltpu.VMEM((L, cb), F32), pltpu.VMEM((n2 * 2 * k1p, cb), F32)],
        compiler_params=_cparams(("parallel", "arbitrary"),
                                 [((L, cb), BF16)],
                                 [((L, cb), BF16)] * 3
                                 + [((L, cb), F32), ((n2 * 2 * k1p, cb), F32), ((k1v, 2 * n2, cb), F32)]
                                 + [(t.shape, BF16) for t in tables]),
        name="hyena_long_conv",
    )(z, z, z, skip.reshape(1, W), spec, *tables)


def _rope_half(x, cos, sin):
    half = x.shape[1] // 2
    x1, x2 = x[:, :half], x[:, half:]
    return jnp.concatenate([x1 * cos - x2 * sin, x1 * sin + x2 * cos], axis=1)


def _ret_fwd_kernel(lg_ref, q_ref, k_ref, v_ref, cos_ref, sin_ref, o_ref, s_ref, *, T, C, H, DK, DV):
    @pl.when(pl.program_id(1) == 0)
    def _():
        s_ref[...] = jnp.zeros_like(s_ref)

    row = lax.broadcasted_iota(jnp.int32, (C, C), 0)
    col = lax.broadcasted_iota(jnp.int32, (C, C), 1)
    rel = (row - col).astype(F32)
    ridx = lax.broadcasted_iota(jnp.int32, (C, 1), 0).astype(F32)
    consts = []
    for h in range(H):
        lgf = lg_ref[0, h]
        lgb = lg_ref[1, h]
        dmat = jnp.where(rel > 0.0, jnp.exp(lgf * jnp.maximum(rel, 0.0)),
                         jnp.where(rel < 0.0, jnp.exp(lgb * jnp.maximum(-rel, 0.0)), 2.0))
        consts.append((dmat, jnp.exp(lgf * (ridx + 1.0)), jnp.exp(lgf * (C - 1.0 - ridx)),
                       jnp.exp(jnp.full((1, 1), C, F32) * lgf)))

    def chunk(c, carry):
        rows = pl.ds(pl.multiple_of(c * C, C), C)
        cos, sin = cos_ref[rows, :], sin_ref[rows, :]
        for h in range(H):
            dmat, q_scale, k_scale, s_decay = consts[h]
            ck = slice(h * DK, (h + 1) * DK)
            cv = slice(h * DV, (h + 1) * DV)
            q = _rope_half(q_ref[rows, ck].astype(F32), cos, sin)
            k = _rope_half(k_ref[rows, ck].astype(F32), cos, sin) * (DK ** -0.5)
            v = v_ref[rows, cv]
            s = lax.dot_general(q.astype(BF16), k.astype(BF16), NT_DIMS, preferred_element_type=F32) * dmat
            st = s_ref[h]
            o_ref[rows, cv] = _dot(s.astype(BF16), v) + _dot((q * q_scale).astype(BF16), st.astype(BF16))
            s_ref[h] = s_decay * st + lax.dot_general((k * k_scale).astype(BF16), v, TN_DIMS,
                                                      preferred_element_type=F32)
        return carry

    lax.fori_loop(0, T // C, chunk, 0)


def _ret_bwd_kernel(lg_ref, q_ref, k_ref, v_ref, g_ref, cos_ref, sin_ref, o1_ref, o_ref, s_ref,
                    *, T, C, H, DK, DV):
    @pl.when(pl.program_id(1) == 0)
    def _():
        s_ref[...] = jnp.zeros_like(s_ref)

    ridx = lax.broadcasted_iota(jnp.int32, (C, 1), 0).astype(F32)
    nc = T // C
    consts = []
    for h in range(H):
        lgb = lg_ref[1, h]
        consts.append((jnp.exp(lgb * (C - ridx)), jnp.exp(lgb * ridx), jnp.exp(jnp.full((1, 1), C, F32) * lgb)))

    def chunk(cc, carry):
        rows = pl.ds(pl.multiple_of((nc - 1 - cc) * C, C), C)
        cos, sin = cos_ref[rows, :], sin_ref[rows, :]
        for h in range(H):
            q_scale, k_scale, s_decay = consts[h]
            ck = slice(h * DK, (h + 1) * DK)
            cv = slice(h * DV, (h + 1) * DV)
            q = _rope_half(q_ref[rows, ck].astype(F32), cos, sin)
            k = _rope_half(k_ref[rows, ck].astype(F32), cos, sin) * (DK ** -0.5)
            v = v_ref[rows, cv]
            st = s_ref[h]
            o = o1_ref[rows, cv] + _dot((q * q_scale).astype(BF16), st.astype(BF16))
            y = o * lax.rsqrt(jnp.mean(o * o, axis=-1, keepdims=True) + NORM_EPS)
            o_ref[rows, cv] = (y * _silu(g_ref[rows, cv].astype(F32))).astype(o_ref.dtype)
            s_ref[h] = s_decay * st + lax.dot_general((k * k_scale).astype(BF16), v, TN_DIMS,
                                                      preferred_element_type=F32)
        return carry

    lax.fori_loop(0, nc, chunk, 0)


def _retention(z, log_gamma, cos, sin):
    B, L, _ = z.shape
    H, C = RET_HEADS, RET_CHUNK
    DK = cos.shape[1] * 2
    DV = 2 * DK
    T = min(L, 512)
    nT = L // T
    kw = dict(T=T, C=C, H=H, DK=DK, DV=DV)
    smem = pl.BlockSpec(memory_space=pltpu.SMEM)
    blocks = [((T, H * DK), BF16)] * 2 + [((T, H * DV), BF16)] * 2 + [((T, H * DV), F32)] * 2
    state = [((H, DK, DV), F32), ((DK, DV), F32), ((DK, DV), F32)]
    o1 = pl.pallas_call(
        functools.partial(_ret_fwd_kernel, **kw),
        grid=(B, nT),
        in_specs=[smem,
                  pl.BlockSpec((None, T, H * DK), lambda b, i: (b, i, 0)),
                  pl.BlockSpec((None, T, H * DK), lambda b, i: (b, i, 1)),
                  pl.BlockSpec((None, T, H * DV), lambda b, i: (b, i, 1)),
                  pl.BlockSpec((T, DK // 2), lambda b, i: (i, 0)),
                  pl.BlockSpec((T, DK // 2), lambda b, i: (i, 0))],
        out_specs=pl.BlockSpec((None, T, H * DV), lambda b, i: (b, i, 0)),
        out_shape=jax.ShapeDtypeStruct((B, L, H * DV), F32),
        scratch_shapes=[pltpu.VMEM((H, DK, DV), F32)],
        compiler_params=_cparams(("parallel", "arbitrary"), blocks, state),
        name="retention_fwd",
    )(log_gamma, z, z, z, cos, sin)
    return pl.pallas_call(
        functools.partial(_ret_bwd_kernel, **kw),
        grid=(B, nT),
        in_specs=[smem,
                  pl.BlockSpec((None, T, H * DK), lambda b, i: (b, nT - 1 - i, 0)),
                  pl.BlockSpec((None, T, H * DK), lambda b, i: (b, nT - 1 - i, 1)),
                  pl.BlockSpec((None, T, H * DV), lambda b, i: (b, nT - 1 - i, 1)),
                  pl.BlockSpec((None, T, H * DV), lambda b, i: (b, nT - 1 - i, 2)),
                  pl.BlockSpec((T, DK // 2), lambda b, i: (nT - 1 - i, 0)),
                  pl.BlockSpec((T, DK // 2), lambda b, i: (nT - 1 - i, 0)),
                  pl.BlockSpec((None, T, H * DV), lambda b, i: (b, nT - 1 - i, 0))],
        out_specs=pl.BlockSpec((None, T, H * DV), lambda b, i: (b, nT - 1 - i, 0)),
        out_shape=jax.ShapeDtypeStruct((B, L, H * DV), BF16),
        scratch_shapes=[pltpu.VMEM((H, DK, DV), F32)],
        compiler_params=_cparams(("parallel", "arbitrary"), blocks, state),
        name="retention_bwd",
    )(log_gamma, z, z, z, z, cos, sin, o1)


def _swa_kernel(sink_ref, q_ref, kp_ref, kc_ref, kn_ref, vp_ref, vc_ref, vn_ref, o_ref, *, HKV, G, BLK):
    i = pl.program_id(1)
    last = pl.num_programs(1) - 1
    R = G * BLK
    r = lax.broadcasted_iota(jnp.int32, (R, 3 * BLK), 0) % BLK
    c = lax.broadcasted_iota(jnp.int32, (R, 3 * BLK), 1)
    rel = r - (c - BLK)
    lo = jnp.where(i == 0, BLK, 0)
    hi = jnp.where(i == last, 2 * BLK, 3 * BLK)
    valid = (jnp.abs(rel) <= WINDOW) & (c >= lo) & (c < hi)
    grp = lax.broadcasted_iota(jnp.int32, (R, 1), 0) // BLK
    for j in range(HKV):
        sl = slice(j * LANES, (j + 1) * LANES)
        k = jnp.concatenate([kp_ref[:, sl], kc_ref[:, sl], kn_ref[:, sl]], axis=0)
        v = jnp.concatenate([vp_ref[:, sl], vc_ref[:, sl], vn_ref[:, sl]], axis=0)
        q = jnp.concatenate([q_ref[:, (j * G + g) * LANES:(j * G + g + 1) * LANES] for g in range(G)], axis=0)
        s = lax.dot_general(q, k, NT_DIMS, preferred_element_type=F32)
        s = jnp.where(valid, s, NEG_INF)
        sink = jnp.zeros((R, 1), F32)
        for g in range(G):
            sink = jnp.where(grp == g, sink_ref[0, j * G + g], sink)
        m = jnp.maximum(jnp.max(s, axis=-1, keepdims=True), sink)
        p = jnp.exp(s - m)
        denom = jnp.sum(p, axis=-1, keepdims=True) + jnp.exp(sink - m)
        o = _dot(p.astype(BF16), v) / denom
        for g in range(G):
            o_ref[:, (j * G + g) * LANES:(j * G + g + 1) * LANES] = o[g * BLK:(g + 1) * BLK].astype(o_ref.dtype)


def _swa_attention(z, sink):
    B, L, _ = z.shape
    BLK = ATTN_BLOCK
    nb = L // BLK
    G = SWA_HQ // SWA_HKV
    qw = SWA_HQ * LANES
    kvw = SWA_HKV * LANES
    kcol = qw // kvw
    prev = lambda b, i: (b, jnp.maximum(i - 1, 0), kcol)
    cur = lambda b, i: (b, i, kcol)
    nxt = lambda b, i: (b, jnp.minimum(i + 1, nb - 1), kcol)
    vprev = lambda b, i: (b, jnp.maximum(i - 1, 0), kcol + 1)
    vcur = lambda b, i: (b, i, kcol + 1)
    vnxt = lambda b, i: (b, jnp.minimum(i + 1, nb - 1), kcol + 1)
    kv = lambda f: pl.BlockSpec((None, BLK, kvw), f)
    return pl.pallas_call(
        functools.partial(_swa_kernel, HKV=SWA_HKV, G=G, BLK=BLK),
        grid=(B, nb),
        in_specs=[pl.BlockSpec(memory_space=pltpu.SMEM),
                  pl.BlockSpec((None, BLK, qw), lambda b, i: (b, i, 0)),
                  kv(prev), kv(cur), kv(nxt), kv(vprev), kv(vcur), kv(vnxt)],
        out_specs=pl.BlockSpec((None, BLK, qw), lambda b, i: (b, i, 0)),
        out_shape=jax.ShapeDtypeStruct((B, L, qw), BF16),
        compiler_params=_cparams(("parallel", "parallel"),
                                 [((BLK, qw), BF16)] * 2 + [((BLK, kvw), BF16)] * 6,
                                 [((G * BLK, 3 * BLK), F32)] * 4),
        name="swa_attention",
    )(sink.reshape(1, SWA_HQ), z, z, z, z, z, z, z)


def _swa_layout(w_in, q_gain, k_gain, w_out, L):
    D = w_in.shape[0]
    dh, hq, hkv = SWA_DH, SWA_HQ, SWA_HKV
    hf = dh // 2
    q_end, k_end = hq * dh, (hq + hkv) * dh

    def rot_slots(w, n):
        w = w.reshape(D, n, 2, hf)
        return jnp.pad(w, ((0, 0), (0, 0), (0, 0), (0, LANES // 2 - hf))).reshape(D, n * LANES)

    def val_slots(w, n):
        return jnp.pad(w.reshape(D, n, dh), ((0, 0), (0, 0), (0, LANES - dh))).reshape(D, n * LANES)

    w = jnp.concatenate([rot_slots(w_in[:, :q_end], hq), rot_slots(w_in[:, q_end:k_end], hkv),
                         val_slots(w_in[:, k_end:], hkv)], axis=1)

    def gain_slot(g):
        return jnp.pad(g.reshape(2, hf), ((0, 0), (0, LANES // 2 - hf))).reshape(1, LANES)

    flag = jnp.concatenate([jnp.ones(((hq + hkv) * LANES,), F32), jnp.zeros((hkv * LANES,), F32)]).reshape(1, -1)
    wo = jnp.pad(w_out.reshape(hq, dh, -1), ((0, 0), (0, LANES - dh), (0, 0))).reshape(hq * LANES, -1)

    inv = ROPE_THETA ** (-np.arange(0, dh, 2) / dh)
    ang = np.arange(L)[:, None] * inv[None, :]
    zero = np.zeros((L, LANES // 2 - hf))
    cos = jnp.asarray(np.concatenate([np.cos(ang), zero, np.cos(ang), zero], axis=1), F32)
    sin = jnp.asarray(np.concatenate([-np.sin(ang), zero, np.sin(ang), zero], axis=1), F32)
    gq = gain_slot(q_gain) * (dh ** -0.5)
    gk = gain_slot(k_gain)
    roll = lambda g: jnp.roll(g, LANES // 2, axis=1)
    cos3 = jnp.stack([cos * gq, cos * gk, jnp.ones_like(cos)])
    sin3 = jnp.stack([sin * roll(gq), sin * roll(gk), jnp.zeros_like(sin)])
    return w, flag, wo, cos3, sin3


def _hg_decays(f_ref, q_ref, rows, lb, tri, before_mid):
    f = lb + (1.0 - lb) * jax.nn.sigmoid(f_ref[rows, :].astype(F32))
    gl = jnp.log(f)
    cum = _cumdot(tri, gl)
    total = jnp.sum(gl, axis=0, keepdims=True)
    mid = jnp.sum(jnp.where(before_mid, gl, 0.0), axis=0, keepdims=True)
    q_dec = _silu(q_ref[rows, :].astype(F32)) * jnp.exp(cum - mid)
    k_inv = (1.0 - f) * jnp.exp(mid - cum)
    q_full = (q_dec * jnp.exp(mid)).astype(BF16)
    k_end = (k_inv * jnp.exp(total - mid)).astype(BF16)
    return q_dec.astype(BF16), k_inv.astype(BF16), q_full, k_end, jnp.exp(total)


def _hg_fwd_kernel(lb_ref, q_ref, i_ref, f_ref, o_ref, st_ref, *, T, C, H, DK, DV):
    @pl.when(pl.program_id(1) == 0)
    def _():
        st_ref[...] = jnp.zeros_like(st_ref)

    row = lax.broadcasted_iota(jnp.int32, (C, C), 0)
    col = lax.broadcasted_iota(jnp.int32, (C, C), 1)
    causal = row >= col
    tri = causal.astype(BF16)
    before_mid = lax.broadcasted_iota(jnp.int32, (C, 1), 0) < C // 2
    lb = lb_ref[...]

    def chunk(c, carry):
        rows = pl.ds(pl.multiple_of(c * C, C), C)
        q_dec, k_inv, q_full, k_end, s_decay = _hg_decays(f_ref, q_ref, rows, lb, tri, before_mid)
        v = i_ref[rows, :]
        for h in range(H):
            ck = slice(h * DK, (h + 1) * DK)
            cv = slice(h * DV, (h + 1) * DV)
            s = lax.dot_general(q_dec[:, ck], k_inv[:, ck], NT_DIMS, preferred_element_type=F32)
            s = jnp.where(causal, s, 0.0)
            st = st_ref[h]
            o_ref[rows, cv] = _dot(s.astype(BF16), v[:, cv]) + lax.dot_general(
                q_full[:, ck], st.astype(BF16), NT_DIMS, preferred_element_type=F32)
            st_ref[h] = st * s_decay[:, ck] + lax.dot_general(v[:, cv], k_end[:, ck], TN_DIMS,
                                                              preferred_element_type=F32)
        return carry

    lax.fori_loop(0, T // C, chunk, 0)


def _hg_bwd_kernel(lb_ref, gain_ref, q_ref, i_ref, f_ref, gate_ref, o1_ref, o_ref, st_ref, *, T, C, H, DK, DV):
    @pl.when(pl.program_id(1) == 0)
    def _():
        st_ref[...] = jnp.zeros_like(st_ref)

    row = lax.broadcasted_iota(jnp.int32, (C, C), 0)
    col = lax.broadcasted_iota(jnp.int32, (C, C), 1)
    anti = col >= row
    tri = anti.astype(BF16)
    after_mid = lax.broadcasted_iota(jnp.int32, (C, 1), 0) >= C // 2
    lb = lb_ref[...]
    gain = gain_ref[...]
    nc = T // C

    def chunk(cc, carry):
        rows = pl.ds(pl.multiple_of((nc - 1 - cc) * C, C), C)
        q_dec, k_inv, q_full, k_end, s_decay = _hg_decays(f_ref, q_ref, rows, lb, tri, after_mid)
        v = i_ref[rows, :]
        for h in range(H):
            ck = slice(h * DK, (h + 1) * DK)
            cv = slice(h * DV, (h + 1) * DV)
            s = lax.dot_general(q_dec[:, ck], k_inv[:, ck], NT_DIMS, preferred_element_type=F32)
            s = jnp.where(anti, s, 0.0)
            st = st_ref[h]
            o = o1_ref[rows, cv] + _dot(s.astype(BF16), v[:, cv]) + lax.dot_general(
                q_full[:, ck], st.astype(BF16), NT_DIMS, preferred_element_type=F32)
            y = o * lax.rsqrt(jnp.mean(o * o, axis=-1, keepdims=True) + NORM_EPS) * gain
            o_ref[rows, cv] = (y * _silu(gate_ref[rows, cv].astype(F32))).astype(o_ref.dtype)
            st_ref[h] = st * s_decay[:, ck] + lax.dot_general(v[:, cv], k_end[:, ck], TN_DIMS,
                                                              preferred_element_type=F32)
        return carry

    lax.fori_loop(0, nc, chunk, 0)


def _hgrn(z, lb, gain):
    B, L, _ = z.shape
    H, C = HG_HEADS, HG_CHUNK
    HD = lb.shape[1]
    DK = HD // H
    DV = gain.shape[0]
    T = min(L, 512)
    nT = L // T
    kw = dict(T=T, C=C, H=H, DK=DK, DV=DV)
    blocks = [((T, HD), BF16)] * 5 + [((T, H * DV), F32)] * 2
    state = [((H, DV, DK), F32)] + [((C, HD), F32)] * 12
    lb3 = lb.reshape(2, 1, HD)
    o1 = pl.pallas_call(
        functools.partial(_hg_fwd_kernel, **kw),
        grid=(B, nT),
        in_specs=[pl.BlockSpec((None, 1, HD), lambda b, i: (0, 0, 0)),
                  pl.BlockSpec((None, T, HD), lambda b, i: (b, i, 0)),
                  pl.BlockSpec((None, T, H * DV), lambda b, i: (b, i, 1)),
                  pl.BlockSpec((None, T, HD), lambda b, i: (b, i, 2))],
        out_specs=pl.BlockSpec((None, T, H * DV), lambda b, i: (b, i, 0)),
        out_shape=jax.ShapeDtypeStruct((B, L, H * DV), F32),
        scratch_shapes=[pltpu.VMEM((H, DV, DK), F32)],
        compiler_params=_cparams(("parallel", "arbitrary"), blocks, state),
        name="hgrn_fwd",
    )(lb3, z, z, z)
    return pl.pallas_call(
        functools.partial(_hg_bwd_kernel, **kw),
        grid=(B, nT),
        in_specs=[pl.BlockSpec((None, 1, HD), lambda b, i: (1, 0, 0)),
                  pl.BlockSpec((1, DV), lambda b, i: (0, 0)),
                  pl.BlockSpec((None, T, HD), lambda b, i: (b, nT - 1 - i, 0)),
                  pl.BlockSpec((None, T, H * DV), lambda b, i: (b, nT - 1 - i, 1)),
                  pl.BlockSpec((None, T, HD), lambda b, i: (b, nT - 1 - i, 3)),
                  pl.BlockSpec((None, T, H * DV), lambda b, i: (b, nT - 1 - i, 4)),
                  pl.BlockSpec((None, T, H * DV), lambda b, i: (b, nT - 1 - i, 0))],
        out_specs=pl.BlockSpec((None, T, H * DV), lambda b, i: (b, nT - 1 - i, 0)),
        out_shape=jax.ShapeDtypeStruct((B, L, H * DV), BF16),
        scratch_shapes=[pltpu.VMEM((H, DV, DK), F32)],
        compiler_params=_cparams(("parallel", "arbitrary"), blocks, state),
        name="hgrn_bwd",
    )(lb3, gain.reshape(1, DV), z, z, z, z, o1)


def _ret_rope_tables(L, dk):
    inv = ROPE_THETA ** (-np.arange(0, dk, 2) / dk)
    ang = np.arange(L)[:, None] * inv[None, :]
    return jnp.asarray(np.cos(ang), F32), jnp.asarray(np.sin(ang), F32)


def kernel(x_prompt, x_sample, c_prompt, c_sample, ada_w, ada_b, norm_g, hy_w_in, hy_conv_w, hy_conv_b, hy_w1, hy_b1, hy_w2, hy_b2, hy_w3, hy_freq, hy_decay, hy_skip, hy_w_out, ret_w_in, ret_decay, ret_w_out, swa_w_in, swa_q_gain, swa_k_gain, swa_sink, swa_w_out, hg_w_in, hg_lb, hg_gain, hg_w_out, ffn_w_gate, ffn_w_val, ffn_conv_w, ffn_conv_b, ffn_w_down):
    depth, D = norm_g.shape[0], norm_g.shape[2]
    groups = [(x_prompt, c_prompt), (x_sample, c_sample)]
    mods = _ada_mod(jnp.concatenate([c for _, c in groups], axis=0), ada_w, ada_b)

    bf = lambda w: w.astype(BF16)
    hy_w_in_b, hy_w_out_b = bf(hy_w_in), bf(hy_w_out)
    ret_w_in_b, ret_w_out_b = bf(ret_w_in), bf(ret_w_out)
    hg_w_in_b, hg_w_out_b = bf(hg_w_in), bf(hg_w_out)
    wg_b, wv_b, wd_b = bf(ffn_w_gate), bf(ffn_w_val), bf(ffn_w_down)
    hg_sm = jax.nn.softmax(hg_lb.astype(F32), axis=1)
    hg_lower = jnp.cumsum(hg_sm, axis=1) - hg_sm
    ret_log_gamma = -jnp.exp(ret_decay.astype(F32))

    outs = []
    row0 = 0
    for x, c in groups:
        B, L, _ = x.shape
        for layer in range(depth):
            kind, j = layer % N_MIXERS, layer // N_MIXERS
            mod = mods[layer, row0:row0 + B].reshape(B, N_MOD, 1, D)
            sh1, sc1, g1, sh2, sc2, g2 = (mod[:, m] for m in range(N_MOD))
            gn1 = norm_g[layer, 0].reshape(1, D)
            gn2 = norm_g[layer, 1].reshape(1, D)
            if kind == 0:
                tabs = _fft_tables(L)
                z = _proj_in_conv(x, gn1, sc1, sh1, hy_w_in_b[j], hy_conv_w[j], hy_conv_b[j], tn=1024)
                taps, asum = _hy_filter_taps(L, hy_w1[j], hy_b1[j], hy_w2[j], hy_b2[j], hy_w3[j], hy_freq[j],
                                             hy_decay[j])
                spec = _hy_spectrum(taps, asum, tabs)
                a = _hy_conv(z, hy_skip[j], spec, tabs)
                x = _proj_out(a, hy_w_out_b[j], x, g1)
            elif kind == 1:
                z = _proj_in(x, gn1, sc1, sh1, ret_w_in_b[j], tn=1024)
                cos, sin = _ret_rope_tables(L, D // RET_HEADS)
                a = _retention(z, ret_log_gamma[j], cos, sin)
                x = _proj_out(a, ret_w_out_b[j], x, g1)
            elif kind == 2:
                w, flag, wo, cos, sin = _swa_layout(swa_w_in[j], swa_q_gain[j], swa_k_gain[j], swa_w_out[j], L)
                z = _proj_in_rope(x, gn1, sc1, sh1, bf(w), flag, cos, sin, tn=SWA_HKV * LANES, dh=SWA_DH,
                                  q_tiles=SWA_HQ // SWA_HKV)
                a = _swa_attention(z, swa_sink[j])
                x = _proj_out(a, bf(wo), x, g1)
            else:
                z = _proj_in(x, gn1, sc1, sh1, hg_w_in_b[j], tn=1024)
                a = _hgrn(z, hg_lower[:, layer], hg_gain[j])
                x = _proj_out(a, hg_w_out_b[j], x, g1)
            x = _ffn(x, gn2, sc2, sh2, g2, wg_b[layer], wv_b[layer], ffn_conv_w[layer], ffn_conv_b[layer],
                     wd_b[layer])
        outs.append(x)
        row0 += B
    return tuple(outs)
```

```python
import functools
import math

import numpy as np
import jax
import jax.numpy as jnp
from jax import lax
from jax.experimental import pallas as pl
from jax.experimental.pallas import tpu as pltpu

F32 = jnp.float32
BF16 = jnp.bfloat16

NORM_EPS = 1e-6
N_MIXERS = 4
N_MOD = 6
HY_BANDS = 16
RET_HEADS = 4
RET_CHUNK = 128
SWA_HQ = 16
SWA_HKV = 4
SWA_DH = 64
WINDOW = 128
ATTN_BLOCK = 128
ROPE_THETA = 10000.0
NEG_INF = -1e30
HG_HEADS = 8
HG_CHUNK = 128

LANES = 128
SUBLANES = 8
BF16_ROWS = 16
VMEM_LIMIT_CAP = 60 * 1024 * 1024
VMEM_SLACK = 8 * 1024 * 1024

NT_DIMS = (((1,), (1,)), ((), ()))
TN_DIMS = (((0,), (0,)), ((), ()))


def _nbytes(shape, dtype):
    item = jnp.dtype(dtype).itemsize
    sub = SUBLANES * 4 // item
    dims = list(shape)
    dims[-1] = -(-dims[-1] // LANES) * LANES
    if len(dims) > 1:
        dims[-2] = -(-dims[-2] // sub) * sub
    return int(np.prod(dims)) * item


def _cparams(semantics, pipelined, resident):
    need = 2 * sum(_nbytes(s, d) for s, d in pipelined) + sum(_nbytes(s, d) for s, d in resident)
    return pltpu.CompilerParams(dimension_semantics=semantics,
                                vmem_limit_bytes=min(need + VMEM_SLACK, VMEM_LIMIT_CAP))


def _dot(a, b):
    return jnp.dot(a, b, preferred_element_type=F32)


def _split(x):
    hi = x.astype(BF16)
    lo = (x - hi.astype(F32)).astype(BF16)
    return hi, lo


def _dot_tab(t_cat, d):
    d_hi, d_lo = _split(d)
    return _dot(t_cat, jnp.concatenate([d_hi, d_hi, d_lo], axis=0))


def _dot3(a, b):
    a_hi, a_lo = _split(a)
    b_hi, b_lo = _split(b)
    return _dot(a_hi, b_hi) + _dot(a_lo, b_hi) + _dot(a_hi, b_lo)


def _cumdot(tri2, g):
    g_hi, g_lo = _split(g)
    return _dot(tri2, jnp.concatenate([g_hi, g_lo], axis=0))


def _silu(x):
    return x * jax.nn.sigmoid(x)


def _norm_mod(x, g, sc, sh):
    xf = x.astype(F32)
    y = xf * lax.rsqrt(jnp.mean(xf * xf, axis=-1, keepdims=True) + NORM_EPS)
    return (y * g) * (1.0 + sc) + sh


def _row_tile(L):
    return min(L, 1024)


def _ada_kernel(c_ref, w_ref, b_ref, o_ref):
    cs = _silu(c_ref[...]).astype(BF16)
    o_ref[...] = _dot(cs, w_ref[...].astype(BF16)) + b_ref[...]


def _ada_mod(c_all, ada_w, ada_b):
    depth, D, N = ada_w.shape
    R = c_all.shape[0]
    tn = 1024
    return pl.pallas_call(
        _ada_kernel,
        grid=(depth, N // tn),
        in_specs=[pl.BlockSpec((R, D), lambda l, j: (0, 0)),
                  pl.BlockSpec((None, D, tn), lambda l, j: (l, 0, j)),
                  pl.BlockSpec((None, 1, tn), lambda l, j: (l, 0, j))],
        out_specs=pl.BlockSpec((None, R, tn), lambda l, j: (l, 0, j)),
        out_shape=jax.ShapeDtypeStruct((depth, R, N), F32),
        compiler_params=_cparams(("parallel", "parallel"),
                                 [((D, tn), F32), ((R, D), F32), ((R, tn), F32)], []),
        name="ada_mod",
    )(c_all, ada_w, ada_b.reshape(depth, 1, N))


def _fill_h(h_ref, x_ref, g_ref, sc_ref, sh_ref):
    h_ref[...] = _norm_mod(x_ref[...], g_ref[...], sc_ref[...], sh_ref[...]).astype(BF16)


def _fill_h_halo(h_ref, x_ref, xp_ref, xn_ref, g_ref, sc_ref, sh_ref, tm):
    i = pl.program_id(1)
    last = pl.num_programs(1) - 1
    g, sc, sh = g_ref[...], sc_ref[...], sh_ref[...]
    h_ref[BF16_ROWS:BF16_ROWS + tm, :] = _norm_mod(x_ref[...], g, sc, sh).astype(BF16)
    hp = _norm_mod(xp_ref[...], g, sc, sh)
    hn = _norm_mod(xn_ref[...], g, sc, sh)
    h_ref[0:BF16_ROWS, :] = jnp.where(i > 0, hp, 0.0).astype(BF16)
    h_ref[BF16_ROWS + tm:, :] = jnp.where(i < last, hn, 0.0).astype(BF16)


def _conv3_rows(z_ref, cw, cb, tm):
    o = BF16_ROWS
    return (z_ref[o - 1:o - 1 + tm, :] * cw[0:1, :] + z_ref[o:o + tm, :] * cw[1:2, :]
            + z_ref[o + 1:o + 1 + tm, :] * cw[2:3, :] + cb)


def _proj_plain_kernel(x_ref, g_ref, sc_ref, sh_ref, w_ref, o_ref, h_ref):
    @pl.when(pl.program_id(2) == 0)
    def _():
        _fill_h(h_ref, x_ref, g_ref, sc_ref, sh_ref)

    o_ref[...] = _dot(h_ref[...], w_ref[...]).astype(o_ref.dtype)


def _proj_halfrope_kernel(x_ref, g_ref, sc_ref, sh_ref, w_ref, cos_ref, sin_ref, o_ref, h_ref, *, rope_tiles, dk):
    j = pl.program_id(2)

    @pl.when(j == 0)
    def _():
        _fill_h(h_ref, x_ref, g_ref, sc_ref, sh_ref)

    z = _dot(h_ref[...], w_ref[...])

    @pl.when(j < rope_tiles)
    def _():
        cos, sin = cos_ref[...], sin_ref[...]
        half = dk // 2
        for c0 in range(0, z.shape[1], dk):
            x1, x2 = z[:, c0:c0 + half], z[:, c0 + half:c0 + dk]
            o_ref[:, c0:c0 + half] = (x1 * cos - x2 * sin).astype(o_ref.dtype)
            o_ref[:, c0 + half:c0 + dk] = (x1 * sin + x2 * cos).astype(o_ref.dtype)

    @pl.when(j >= rope_tiles)
    def _():
        o_ref[...] = z.astype(o_ref.dtype)


def _proj_conv_kernel(x_ref, xp_ref, xn_ref, g_ref, sc_ref, sh_ref, w_ref, cw_ref, cb_ref,
                      o_ref, h_ref, z_ref, *, tm):
    @pl.when(pl.program_id(2) == 0)
    def _():
        _fill_h_halo(h_ref, x_ref, xp_ref, xn_ref, g_ref, sc_ref, sh_ref, tm)

    z_ref[...] = _dot(h_ref[...], w_ref[...])
    o_ref[...] = _conv3_rows(z_ref, cw_ref[...], cb_ref[...], tm).astype(o_ref.dtype)


def _proj_rope_kernel(x_ref, g_ref, sc_ref, sh_ref, w_ref, flag_ref, ones_ref, cos_ref, sin_ref,
                      o_ref, h_ref, *, dh):
    @pl.when(pl.program_id(2) == 0)
    def _():
        _fill_h(h_ref, x_ref, g_ref, sc_ref, sh_ref)

    z = _dot(h_ref[...], w_ref[...])
    ms = _dot((z * z).astype(BF16), ones_ref[...]) * (1.0 / dh)
    zn = z * jnp.where(flag_ref[...] > 0.0, lax.rsqrt(ms + NORM_EPS), 1.0)
    cos, sin = cos_ref[...], sin_ref[...]
    for s in range(z.shape[1] // LANES):
        sl = slice(s * LANES, (s + 1) * LANES)
        zs = zn[:, sl]
        o_ref[:, sl] = (zs * cos + pltpu.roll(zs, LANES // 2, axis=1) * sin).astype(o_ref.dtype)


def _mod_specs(D):
    return [pl.BlockSpec((1, D), lambda b, i, j: (0, 0)),
            pl.BlockSpec((None, 1, D), lambda b, i, j: (b, 0, 0)),
            pl.BlockSpec((None, 1, D), lambda b, i, j: (b, 0, 0))]


def _halo_specs(tm, D, L):
    hb = tm // BF16_ROWS
    nhb = L // BF16_ROWS
    return [pl.BlockSpec((None, tm, D), lambda b, i, j: (b, i, 0)),
            pl.BlockSpec((None, BF16_ROWS, D), lambda b, i, j: (b, jnp.maximum(i * hb - 1, 0), 0)),
            pl.BlockSpec((None, BF16_ROWS, D), lambda b, i, j: (b, jnp.minimum((i + 1) * hb, nhb - 1), 0))]


def _proj_in(x, g, sc, sh, w, *, tn):
    B, L, D = x.shape
    N = w.shape[1]
    tm = _row_tile(L)
    return pl.pallas_call(
        _proj_plain_kernel,
        grid=(B, L // tm, N // tn),
        in_specs=[pl.BlockSpec((None, tm, D), lambda b, i, j: (b, i, 0))] + _mod_specs(D)
        + [pl.BlockSpec((D, tn), lambda b, i, j: (0, j))],
        out_specs=pl.BlockSpec((None, tm, tn), lambda b, i, j: (b, i, j)),
        out_shape=jax.ShapeDtypeStruct((B, L, N), BF16),
        scratch_shapes=[pltpu.VMEM((tm, D), BF16)],
        compiler_params=_cparams(("parallel", "parallel", "arbitrary"),
                                 [((tm, D), F32), ((D, tn), BF16), ((tm, tn), BF16)],
                                 [((tm, D), BF16), ((tm, tn), F32)]),
        name="proj_in",
    )(x, g, sc, sh, w)


def _proj_in_halfrope(x, g, sc, sh, w, cos, sin, *, tn, rope_tiles, dk):
    B, L, D = x.shape
    N = w.shape[1]
    tm = _row_tile(L)
    half = dk // 2
    table = lambda b, i, j: (jnp.minimum(j, rope_tiles - 1), i, 0)
    return pl.pallas_call(
        functools.partial(_proj_halfrope_kernel, rope_tiles=rope_tiles, dk=dk),
        grid=(B, L // tm, N // tn),
        in_specs=[pl.BlockSpec((None, tm, D), lambda b, i, j: (b, i, 0))] + _mod_specs(D)
        + [pl.BlockSpec((D, tn), lambda b, i, j: (0, j)),
           pl.BlockSpec((None, tm, half), table),
           pl.BlockSpec((None, tm, half), table)],
        out_specs=pl.BlockSpec((None, tm, tn), lambda b, i, j: (b, i, j)),
        out_shape=jax.ShapeDtypeStruct((B, L, N), BF16),
        scratch_shapes=[pltpu.VMEM((tm, D), BF16)],
        compiler_params=_cparams(("parallel", "parallel", "arbitrary"),
                                 [((tm, D), F32), ((D, tn), BF16), ((tm, tn), BF16), ((tm, half), F32),
                                  ((tm, half), F32)],
                                 [((tm, D), BF16), ((tm, tn), F32), ((tm, tn), F32)]),
        name="proj_in_halfrope",
    )(x, g, sc, sh, w, cos, sin)


def _proj_in_conv(x, g, sc, sh, w, cw, cb, *, tn):
    B, L, D = x.shape
    N = w.shape[1]
    tm = _row_tile(L)
    te = tm + 2 * BF16_ROWS
    return pl.pallas_call(
        functools.partial(_proj_conv_kernel, tm=tm),
        grid=(B, L // tm, N // tn),
        in_specs=_halo_specs(tm, D, L) + _mod_specs(D)
        + [pl.BlockSpec((D, tn), lambda b, i, j: (0, j)),
           pl.BlockSpec((3, tn), lambda b, i, j: (0, j)),
           pl.BlockSpec((1, tn), lambda b, i, j: (0, j))],
        out_specs=pl.BlockSpec((None, tm, tn), lambda b, i, j: (b, i, j)),
        out_shape=jax.ShapeDtypeStruct((B, L, N), BF16),
        scratch_shapes=[pltpu.VMEM((te, D), BF16), pltpu.VMEM((te, tn), F32)],
        compiler_params=_cparams(("parallel", "parallel", "arbitrary"),
                                 [((tm, D), F32), ((D, tn), BF16), ((tm, tn), BF16)],
                                 [((te, D), BF16), ((te, tn), F32), ((te, tn), F32)]),
        name="proj_in_conv",
    )(x, x, x, g, sc, sh, w, cw, cb.reshape(1, N))


def _proj_in_rope(x, g, sc, sh, w, flag, cos, sin, *, tn, dh, q_tiles):
    B, L, D = x.shape
    N = w.shape[1]
    tm = _row_tile(L)
    slot = lax.broadcasted_iota(jnp.int32, (tn, tn), 0) // LANES == lax.broadcasted_iota(jnp.int32, (tn, tn), 1) // LANES
    table = lambda b, i, j: (jnp.maximum(j - (q_tiles - 1), 0), i, 0)
    return pl.pallas_call(
        functools.partial(_proj_rope_kernel, dh=dh),
        grid=(B, L // tm, N // tn),
        in_specs=[pl.BlockSpec((None, tm, D), lambda b, i, j: (b, i, 0))] + _mod_specs(D)
        + [pl.BlockSpec((D, tn), lambda b, i, j: (0, j)),
           pl.BlockSpec((1, tn), lambda b, i, j: (0, j)),
           pl.BlockSpec((tn, tn), lambda b, i, j: (0, 0)),
           pl.BlockSpec((None, tm, LANES), table),
           pl.BlockSpec((None, tm, LANES), table)],
        out_specs=pl.BlockSpec((None, tm, tn), lambda b, i, j: (b, i, j)),
        out_shape=jax.ShapeDtypeStruct((B, L, N), BF16),
        scratch_shapes=[pltpu.VMEM((tm, D), BF16)],
        compiler_params=_cparams(("parallel", "parallel", "arbitrary"),
                                 [((tm, D), F32), ((D, tn), BF16), ((tm, tn), BF16), ((tm, LANES), F32),
                                  ((tm, LANES), F32), ((tn, tn), BF16)],
                                 [((tm, D), BF16), ((tm, tn), F32), ((tm, tn), F32), ((tm, tn), F32)]),
        name="proj_in_rope",
    )(x, g, sc, sh, w, flag, slot.astype(BF16), cos, sin)


def _proj_out_kernel(a_ref, w_ref, x_ref, gate_ref, o_ref):
    o_ref[...] = x_ref[...] + gate_ref[...] * _dot(a_ref[...], w_ref[...])


def _proj_out(a, w, x, gate):
    B, L, K = a.shape
    D = w.shape[1]
    tm = _row_tile(L)
    return pl.pallas_call(
        _proj_out_kernel,
        grid=(B, L // tm),
        in_specs=[pl.BlockSpec((None, tm, K), lambda b, i: (b, i, 0)),
                  pl.BlockSpec((K, D), lambda b, i: (0, 0)),
                  pl.BlockSpec((None, tm, D), lambda b, i: (b, i, 0)),
                  pl.BlockSpec((None, 1, D), lambda b, i: (b, 0, 0))],
        out_specs=pl.BlockSpec((None, tm, D), lambda b, i: (b, i, 0)),
        out_shape=jax.ShapeDtypeStruct((B, L, D), F32),
        compiler_params=_cparams(("parallel", "parallel"),
                                 [((tm, K), BF16), ((K, D), BF16), ((tm, D), F32), ((tm, D), F32)],
                                 [((tm, D), F32)]),
        name="proj_out",
    )(a, w, x, gate)


def _ffn_kernel(x_ref, xp_ref, xn_ref, g_ref, sc_ref, sh_ref, gate_ref, wg_ref, wv_ref, cw_ref, cb_ref,
                wd_ref, o_ref, h_ref, z_ref, *, tm, tf):
    _fill_h_halo(h_ref, x_ref, xp_ref, xn_ref, g_ref, sc_ref, sh_ref, tm)
    pieces = [slice(c * tf, (c + 1) * tf) for c in range(wg_ref.shape[1] // tf)]
    vals = []
    for c, cols in enumerate(pieces):
        z_ref[c] = _dot(h_ref[...], wg_ref[:, cols])
        vals.append(_dot(h_ref[BF16_ROWS:BF16_ROWS + tm, :], wv_ref[:, cols]))
    acc = None
    for c, cols in enumerate(pieces):
        a = _conv3_rows(z_ref.at[c], cw_ref[:, cols], cb_ref[:, cols], tm)
        part = _dot((_silu(a) * vals[c]).astype(BF16), wd_ref[cols, :])
        acc = part if acc is None else acc + part
    o_ref[...] = x_ref[...] + gate_ref[...] * acc


def _ffn(x, g, sc, sh, gate, wg, wv, cw, cb, wd):
    B, L, D = x.shape
    F = wg.shape[1]
    tm = min(L, 512)
    te = tm + 2 * BF16_ROWS
    tf = F // 2
    const = lambda b, i, j: (0, 0)
    resident = lambda shape: pl.BlockSpec(shape, const, pipeline_mode=pl.Buffered(1))
    return pl.pallas_call(
        functools.partial(_ffn_kernel, tm=tm, tf=tf),
        grid=(B, L // tm, 1),
        in_specs=_halo_specs(tm, D, L) + _mod_specs(D)
        + [pl.BlockSpec((None, 1, D), lambda b, i, j: (b, 0, 0)),
           resident((D, F)), resident((D, F)), resident((3, F)), resident((1, F)), resident((F, D))],
        out_specs=pl.BlockSpec((None, tm, D), lambda b, i, j: (b, i, 0)),
        out_shape=jax.ShapeDtypeStruct((B, L, D), F32),
        scratch_shapes=[pltpu.VMEM((te, D), BF16), pltpu.VMEM((F // tf, te, tf), F32)],
        compiler_params=_cparams(("parallel", "parallel", "arbitrary"),
                                 [((tm, D), F32), ((tm, D), F32)],
                                 [((D, F), BF16), ((D, F), BF16), ((F, D), BF16), ((te, D), BF16),
                                  ((te, F), F32), ((tm, F), F32), ((tm, tf), F32), ((tm, tf), BF16),
                                  ((tm, D), F32), ((tm, D), F32)]),
        name="ffn",
    )(x, x, x, g, sc, sh, gate, wg, wv, cw, cb.reshape(1, F), wd)


def _fft_dims(L):
    N = 2 * L
    p = N.bit_length() - 1
    n1 = 1 << ((p + 1) // 2)
    return N, n1, N // n1


def _k1_pad(n1):
    return n1 // 2 + 8


def _fft_tables(L):
    N, n1, n2 = _fft_dims(L)
    k1p = _k1_pad(n1)
    k1 = np.arange(k1p)
    valid = (k1 <= n1 // 2).astype(np.float64)
    weight = np.where((k1 == 0) | (k1 == n1 // 2), 1.0, 2.0) * valid
    m1 = np.arange(n1 // 2)
    n2v = np.arange(n2)
    n_idx = n2 * m1[None, :] + n2v[:, None]
    phi = 2.0 * np.pi * k1[None, :, None] * n_idx[:, None, :] / N
    s1 = np.concatenate([np.cos(phi) * valid[None, :, None], -np.sin(phi) * valid[None, :, None]], axis=1)
    phit = np.transpose(phi, (0, 2, 1))
    gl = np.concatenate([np.cos(phit) * weight[None, None, :], -np.sin(phit) * weight[None, None, :]],
                        axis=2) / N
    th = 2.0 * np.pi * np.outer(n2v, n2v) / n2
    c, s = np.cos(th), np.sin(th)
    f2 = np.block([[c, s], [-s, c]])
    f2c = np.block([[c, -s], [s, c]])

    def cat(a):
        a32 = jnp.asarray(a, F32)
        hi = a32.astype(BF16)
        lo = (a32 - hi.astype(F32)).astype(BF16)
        return jnp.concatenate([hi, lo, hi], axis=-1)

    return dict(N=N, n1=n1, n2=n2, k1p=k1p, s1=cat(s1), gl=cat(gl), f2=cat(f2), f2c=cat(f2c))


def _hy_positions(L):
    idx = np.concatenate([np.arange(L), np.array([0]), np.arange(L - 1, 0, -1)])
    t = idx / (L - 1)
    ang = (2.0 * np.pi / L) * idx
    bands = np.linspace(1e-4, HY_BANDS - 1, HY_BANDS)
    z = np.concatenate([t[:, None], np.cos(bands[None, :] * ang[:, None]), -np.sin(bands[None, :] * ang[:, None])],
                       axis=1)
    zp = np.zeros((2 * L, LANES))
    zp[:, : z.shape[1]] = z
    return jnp.asarray(zp, F32)


def _hy_mlp_kernel(z_ref, w1_ref, b1_ref, w2_ref, b2_ref, w3_ref, fr_ref, dec_ref, taps_ref, asum_ref, *, L, tr):
    i = pl.program_id(0)
    z = z_ref[...]
    fr = fr_ref[...]
    h = jnp.sin(fr * (_dot3(z, w1_ref[...]) + b1_ref[...]))
    h = jnp.sin(fr * (_dot3(h, w2_ref[...]) + b2_ref[...]))
    h = _dot3(h, w3_ref[...])
    t = z[:, 0:1]
    taps = h * jnp.exp(-t * jnp.abs(dec_ref[...]))
    row = i * tr + lax.broadcasted_iota(jnp.int32, (tr, 1), 0)
    taps = jnp.where(row == L, 0.0, taps)
    taps_ref[...] = taps

    @pl.when(i == 0)
    def _():
        asum_ref[...] = jnp.zeros_like(asum_ref)

    asum_ref[...] += jnp.sum(jnp.abs(taps), axis=0, keepdims=True)


def _hy_filter_taps(L, w1, b1, w2, b2, w3, freq, decay):
    W = decay.shape[1]
    E, O = w1.shape
    tr = 512
    half = L // tr
    z = _hy_positions(L)
    w1p = jnp.zeros((LANES, O), F32).at[:E].set(w1)
    return pl.pallas_call(
        functools.partial(_hy_mlp_kernel, L=L, tr=tr),
        grid=(2 * L // tr,),
        in_specs=[pl.BlockSpec((tr, LANES), lambda i: (i, 0)),
                  pl.BlockSpec((LANES, O), lambda i: (0, 0)),
                  pl.BlockSpec((1, O), lambda i: (0, 0)),
                  pl.BlockSpec((O, O), lambda i: (0, 0)),
                  pl.BlockSpec((1, O), lambda i: (0, 0)),
                  pl.BlockSpec((O, W), lambda i: (0, i // half)),
                  pl.BlockSpec((1, O), lambda i: (0, 0)),
                  pl.BlockSpec((None, 1, W), lambda i: (i // half, 0, 0))],
        out_specs=[pl.BlockSpec((tr, W), lambda i: (i, 0)),
                   pl.BlockSpec((1, W), lambda i: (0, 0))],
        out_shape=[jax.ShapeDtypeStruct((2 * L, W), F32), jax.ShapeDtypeStruct((1, W), F32)],
        compiler_params=_cparams(("arbitrary",), [((tr, W), F32), ((O, W), F32), ((tr, LANES), F32)],
                                 [((tr, W), F32), ((tr, W), F32)]),
        name="hyena_filter_mlp",
    )(z, w1p, b1.reshape(1, O), w2, b2.reshape(1, O), w3, freq.reshape(1, O), decay.reshape(2, 1, W))


FFT_UNROLL = 16


def _fft_stage1(src_ref, src_off, a_ref, s1_ref, n1, n2, k1p, sign=None, src_off2=None):
    def body(j, carry):
        d = src_ref[pl.ds(src_off + j, n1 // 2, stride=n2), :]
        r = _dot_tab(s1_ref[j], d)
        if src_off2 is not None:
            d2 = src_ref[pl.ds(src_off2 + j, n1 // 2, stride=n2), :]
            r = r + sign * _dot_tab(s1_ref[j], d2)
        a_ref[pl.ds(pl.multiple_of(j * 2 * k1p, 8), 2 * k1p), :] = r
        return carry

    lax.fori_loop(0, n2, body, 0, unroll=FFT_UNROLL)


def _load_k1(a_ref, k, n2, k1p):
    br = a_ref[pl.ds(k, n2, stride=2 * k1p), :]
    bi = a_ref[pl.ds(k1p + k, n2, stride=2 * k1p), :]
    return jnp.concatenate([br, bi], axis=0)


def _hy_spec_kernel(taps_ref, asum_ref, s1_ref, f2_ref, h_ref, a_ref, *, L, n1, n2, k1p):
    r = lax.broadcasted_iota(jnp.int32, (2 * k1p, 1), 0)
    k1 = jnp.where(r < k1p, r, r - k1p)
    sign = (1 - 2 * (k1 & 1)).astype(F32)
    _fft_stage1(taps_ref, 0, a_ref, s1_ref, n1, n2, k1p, sign=sign, src_off2=L)
    inv = 1.0 / asum_ref[...]

    def body(k, carry):
        h_ref[k] = _dot_tab(f2_ref[...], _load_k1(a_ref, k, n2, k1p)) * inv
        return carry

    lax.fori_loop(0, n1 // 2, body, 0, unroll=2)
    body(n1 // 2, 0)


def _hy_spectrum(taps, asum, tabs):
    n1, n2, k1p = tabs["n1"], tabs["n2"], tabs["k1p"]
    N, W = taps.shape
    L = N // 2
    cb = LANES
    k1v = n1 // 2 + 1
    s1, f2 = tabs["s1"], tabs["f2"]
    return pl.pallas_call(
        functools.partial(_hy_spec_kernel, L=L, n1=n1, n2=n2, k1p=k1p),
        grid=(W // cb,),
        in_specs=[pl.BlockSpec((N, cb), lambda c: (0, c)),
                  pl.BlockSpec((1, cb), lambda c: (0, c)),
                  pl.BlockSpec(s1.shape, lambda c: (0, 0, 0), pipeline_mode=pl.Buffered(1)),
                  pl.BlockSpec(f2.shape, lambda c: (0, 0), pipeline_mode=pl.Buffered(1))],
        out_specs=pl.BlockSpec((k1v, 2 * n2, cb), lambda c: (0, 0, c)),
        out_shape=jax.ShapeDtypeStruct((k1v, 2 * n2, W), F32),
        scratch_shapes=[pltpu.VMEM((n2 * 2 * k1p, cb), F32)],
        compiler_params=_cparams(("parallel",),
                                 [((N, cb), F32), ((k1v, 2 * n2, cb), F32)],
                                 [((n2 * 2 * k1p, cb), F32), (s1.shape, BF16), (f2.shape, BF16)]),
        name="hyena_filter_spectrum",
    )(taps, asum, s1, f2)


def _hy_conv_kernel(x0_ref, x1_ref, v_ref, skip_ref, h_ref, s1_ref, f2_ref, f2c_ref, gl_ref,
                    o_ref, u_ref, a_ref, *, n1, n2, k1p):
    u_ref[...] = x1_ref[...].astype(F32) * v_ref[...].astype(F32)
    _fft_stage1(u_ref, 0, a_ref, s1_ref, n1, n2, k1p)

    cb = u_ref.shape[1]

    def mid(groups):
        loaded = [jnp.concatenate([_load_k1(a_ref, k, n2, k1p) for k in ks], axis=1) for ks in groups]
        results = []
        for ks, rhs in zip(groups, loaded):
            x = _dot_tab(f2_ref[...], rhs)
            xr, xi = x[:n2], x[n2:]
            hr = jnp.concatenate([h_ref[k, :n2, :] for k in ks], axis=1)
            hi = jnp.concatenate([h_ref[k, n2:, :] for k in ks], axis=1)
            y = jnp.concatenate([xr * hr - xi * hi, xr * hi + xi * hr], axis=0)
            results.append(_dot_tab(f2c_ref[...], y))
        for ks, c in zip(groups, results):
            for p, k in enumerate(ks):
                a_ref[pl.ds(k, n2, stride=2 * k1p), :] = c[:n2, p * cb:(p + 1) * cb]
                a_ref[pl.ds(k1p + k, n2, stride=2 * k1p), :] = c[n2:, p * cb:(p + 1) * cb]

    def mid_oct(ko, carry):
        k = 8 * ko
        mid(((k, k + 1, k + 2, k + 3), (k + 4, k + 5, k + 6, k + 7)))
        return carry

    lax.fori_loop(0, n1 // 16, mid_oct, 0)
    mid(((n1 // 2,),))
    skip = skip_ref[...]

    def last(j, carry):
        rhs = a_ref[pl.ds(pl.multiple_of(j * 2 * k1p, 8), 2 * k1p), :]
        y = _dot_tab(gl_ref[j], rhs)
        rows = pl.ds(j, n1 // 2, stride=n2)
        u_ref[rows, :] = y + u_ref[rows, :] * skip
        return carry

    lax.fori_loop(0, n2, last, 0, unroll=FFT_UNROLL)
    o_ref[...] = (u_ref[...].astype(F32) * x0_ref[...].astype(F32)).astype(o_ref.dtype)


def _hy_conv(z, skip, spec, tabs):
    n1, n2, k1p = tabs["n1"], tabs["n2"], tabs["k1p"]
    B, L, W3 = z.shape
    W = W3 // 3
    cb = LANES
    nc = W // cb
    k1v = n1 // 2 + 1
    tables = [tabs["s1"], tabs["f2"], tabs["f2c"], tabs["gl"]]

    def const_spec(t):
        return pl.BlockSpec(t.shape, (lambda c, b: (0, 0, 0)) if t.ndim == 3 else (lambda c, b: (0, 0)),
                            pipeline_mode=pl.Buffered(1))

    return pl.pallas_call(
        functools.partial(_hy_conv_kernel, n1=n1, n2=n2, k1p=k1p),
        grid=(nc, B),
        in_specs=[pl.BlockSpec((None, L, cb), lambda c, b: (b, 0, c), pipeline_mode=pl.Buffered(1)),
                  pl.BlockSpec((None, L, cb), lambda c, b: (b, 0, nc + c), pipeline_mode=pl.Buffered(1)),
                  pl.BlockSpec((None, L, cb), lambda c, b: (b, 0, 2 * nc + c), pipeline_mode=pl.Buffered(1)),
                  pl.BlockSpec((1, cb), lambda c, b: (0, c)),
                  pl.BlockSpec((k1v, 2 * n2, cb), lambda c, b: (0, 0, c), pipeline_mode=pl.Buffered(1))]
        + [const_spec(t) for t in tables],
        out_specs=pl.BlockSpec((None, L, cb), lambda c, b: (b, 0, c)),
        out_shape=jax.ShapeDtypeStruct((B, L, W), BF16),
        scratch_shapes=[pltpu.VMEM((L, cb), F32), pltpu.VMEM((n2 * 2 * k1p, cb), F32)],
        compiler_params=_cparams(("parallel", "arbitrary"),
                                 [((L, cb), BF16)],
                                 [((L, cb), BF16)] * 3
                                 + [((L, cb), F32), ((n2 * 2 * k1p, cb), F32), ((k1v, 2 * n2, cb), F32)]
                                 + [(t.shape, BF16) for t in tables]),
        name="hyena_long_conv",
    )(z, z, z, skip.reshape(1, W), spec, *tables)


def _ret_fwd_kernel(lg_ref, q_ref, k_ref, v_ref, o_ref, s_ref, *, T, C, H, DK, DV):
    @pl.when(pl.program_id(1) == 0)
    def _():
        s_ref[...] = jnp.zeros_like(s_ref)

    row = lax.broadcasted_iota(jnp.int32, (C, C), 0)
    col = lax.broadcasted_iota(jnp.int32, (C, C), 1)
    rel = (row - col).astype(F32)
    ridx = lax.broadcasted_iota(jnp.int32, (C, 1), 0).astype(F32)
    consts = []
    for h in range(H):
        lgf = lg_ref[0, h]
        lgb = lg_ref[1, h]
        dmat = jnp.where(rel > 0.0, jnp.exp(lgf * jnp.maximum(rel, 0.0)),
                         jnp.where(rel < 0.0, jnp.exp(lgb * jnp.maximum(-rel, 0.0)), 2.0))
        consts.append((dmat, jnp.exp(lgf * (ridx + 1.0)), jnp.exp(lgf * (C - 1.0 - ridx)),
                       jnp.exp(jnp.full((1, 1), C, F32) * lgf)))

    def chunk(c, carry):
        rows = pl.ds(pl.multiple_of(c * C, C), C)
        for h in range(H):
            dmat, q_scale, k_scale, s_decay = consts[h]
            ck = slice(h * DK, (h + 1) * DK)
            cv = slice(h * DV, (h + 1) * DV)
            q = q_ref[rows, ck]
            k = k_ref[rows, ck]
            v = v_ref[rows, cv]
            s = lax.dot_general(q, k, NT_DIMS, preferred_element_type=F32) * dmat
            st = s_ref[h]
            o_ref[rows, cv] = _dot(s.astype(BF16), v) + _dot((q.astype(F32) * q_scale).astype(BF16),
                                                             st.astype(BF16))
            s_ref[h] = s_decay * st + lax.dot_general((k.astype(F32) * k_scale).astype(BF16), v, TN_DIMS,
                                                      preferred_element_type=F32)
        return carry

    lax.fori_loop(0, T // C, chunk, 0)


def _ret_bwd_kernel(lg_ref, q_ref, k_ref, v_ref, g_ref, o1_ref, o_ref, s_ref, *, T, C, H, DK, DV):
    @pl.when(pl.program_id(1) == 0)
    def _():
        s_ref[...] = jnp.zeros_like(s_ref)

    ridx = lax.broadcasted_iota(jnp.int32, (C, 1), 0).astype(F32)
    nc = T // C
    consts = []
    for h in range(H):
        lgb = lg_ref[1, h]
        consts.append((jnp.exp(lgb * (C - ridx)), jnp.exp(lgb * ridx), jnp.exp(jnp.full((1, 1), C, F32) * lgb)))

    def chunk(cc, carry):
        rows = pl.ds(pl.multiple_of((nc - 1 - cc) * C, C), C)
        for h in range(H):
            q_scale, k_scale, s_decay = consts[h]
            ck = slice(h * DK, (h + 1) * DK)
            cv = slice(h * DV, (h + 1) * DV)
            q = q_ref[rows, ck].astype(F32)
            k = k_ref[rows, ck].astype(F32)
            v = v_ref[rows, cv]
            st = s_ref[h]
            o = o1_ref[rows, cv] + _dot((q * q_scale).astype(BF16), st.astype(BF16))
            y = o * lax.rsqrt(jnp.mean(o * o, axis=-1, keepdims=True) + NORM_EPS)
            o_ref[rows, cv] = (y * _silu(g_ref[rows, cv].astype(F32))).astype(o_ref.dtype)
            s_ref[h] = s_decay * st + lax.dot_general((k * k_scale).astype(BF16), v, TN_DIMS,
                                                      preferred_element_type=F32)
        return carry

    lax.fori_loop(0, nc, chunk, 0)


def _retention(z, log_gamma):
    B, L, W = z.shape
    H, C = RET_HEADS, RET_CHUNK
    DK = W // (6 * H)
    DV = 2 * DK
    T = min(L, 512)
    nT = L // T
    kw = dict(T=T, C=C, H=H, DK=DK, DV=DV)
    smem = pl.BlockSpec(memory_space=pltpu.SMEM)
    blocks = [((T, H * DK), BF16)] * 2 + [((T, H * DV), BF16)] * 2 + [((T, H * DV), F32)] * 2
    state = [((H, DK, DV), F32), ((DK, DV), F32), ((DK, DV), F32)]
    o1 = pl.pallas_call(
        functools.partial(_ret_fwd_kernel, **kw),
        grid=(B, nT),
        in_specs=[smem,
                  pl.BlockSpec((None, T, H * DK), lambda b, i: (b, i, 0)),
                  pl.BlockSpec((None, T, H * DK), lambda b, i: (b, i, 1)),
                  pl.BlockSpec((None, T, H * DV), lambda b, i: (b, i, 1))],
        out_specs=pl.BlockSpec((None, T, H * DV), lambda b, i: (b, i, 0)),
        out_shape=jax.ShapeDtypeStruct((B, L, H * DV), F32),
        scratch_shapes=[pltpu.VMEM((H, DK, DV), F32)],
        compiler_params=_cparams(("parallel", "arbitrary"), blocks, state),
        name="retention_fwd",
    )(log_gamma, z, z, z)
    return pl.pallas_call(
        functools.partial(_ret_bwd_kernel, **kw),
        grid=(B, nT),
        in_specs=[smem,
                  pl.BlockSpec((None, T, H * DK), lambda b, i: (b, nT - 1 - i, 0)),
                  pl.BlockSpec((None, T, H * DK), lambda b, i: (b, nT - 1 - i, 1)),
                  pl.BlockSpec((None, T, H * DV), lambda b, i: (b, nT - 1 - i, 1)),
                  pl.BlockSpec((None, T, H * DV), lambda b, i: (b, nT - 1 - i, 2)),
                  pl.BlockSpec((None, T, H * DV), lambda b, i: (b, nT - 1 - i, 0))],
        out_specs=pl.BlockSpec((None, T, H * DV), lambda b, i: (b, nT - 1 - i, 0)),
        out_shape=jax.ShapeDtypeStruct((B, L, H * DV), BF16),
        scratch_shapes=[pltpu.VMEM((H, DK, DV), F32)],
        compiler_params=_cparams(("parallel", "arbitrary"), blocks, state),
        name="retention_bwd",
    )(log_gamma, z, z, z, z, o1)


def _swa_kernel(sink_ref, q_ref, kp_ref, kc_ref, kn_ref, vp_ref, vc_ref, vn_ref, o_ref, *, HKV, G, BLK):
    i = pl.program_id(1)
    last = pl.num_programs(1) - 1
    R = G * BLK
    r = lax.broadcasted_iota(jnp.int32, (R, 3 * BLK), 0) % BLK
    c = lax.broadcasted_iota(jnp.int32, (R, 3 * BLK), 1)
    rel = r - (c - BLK)
    lo = jnp.where(i == 0, BLK, 0)
    hi = jnp.where(i == last, 2 * BLK, 3 * BLK)
    valid = (jnp.abs(rel) <= WINDOW) & (c >= lo) & (c < hi)
    grp = lax.broadcasted_iota(jnp.int32, (R, 1), 0) // BLK
    for j in range(HKV):
        sl = slice(j * LANES, (j + 1) * LANES)
        k = jnp.concatenate([kp_ref[:, sl], kc_ref[:, sl], kn_ref[:, sl]], axis=0)
        v = jnp.concatenate([vp_ref[:, sl], vc_ref[:, sl], vn_ref[:, sl]], axis=0)
        q = jnp.concatenate([q_ref[:, (j * G + g) * LANES:(j * G + g + 1) * LANES] for g in range(G)], axis=0)
        s = lax.dot_general(q, k, NT_DIMS, preferred_element_type=F32)
        s = jnp.where(valid, s, NEG_INF)
        sink = jnp.zeros((R, 1), F32)
        for g in range(G):
            sink = jnp.where(grp == g, sink_ref[0, j * G + g], sink)
        m = jnp.maximum(jnp.max(s, axis=-1, keepdims=True), sink)
        p = jnp.exp(s - m)
        denom = jnp.sum(p, axis=-1, keepdims=True) + jnp.exp(sink - m)
        o = _dot(p.astype(BF16), v) / denom
        for g in range(G):
            o_ref[:, (j * G + g) * LANES:(j * G + g + 1) * LANES] = o[g * BLK:(g + 1) * BLK].astype(o_ref.dtype)


def _swa_attention(z, sink):
    B, L, _ = z.shape
    BLK = ATTN_BLOCK
    nb = L // BLK
    G = SWA_HQ // SWA_HKV
    qw = SWA_HQ * LANES
    kvw = SWA_HKV * LANES
    kcol = qw // kvw
    prev = lambda b, i: (b, jnp.maximum(i - 1, 0), kcol)
    cur = lambda b, i: (b, i, kcol)
    nxt = lambda b, i: (b, jnp.minimum(i + 1, nb - 1), kcol)
    vprev = lambda b, i: (b, jnp.maximum(i - 1, 0), kcol + 1)
    vcur = lambda b, i: (b, i, kcol + 1)
    vnxt = lambda b, i: (b, jnp.minimum(i + 1, nb - 1), kcol + 1)
    kv = lambda f: pl.BlockSpec((None, BLK, kvw), f)
    return pl.pallas_call(
        functools.partial(_swa_kernel, HKV=SWA_HKV, G=G, BLK=BLK),
        grid=(B, nb),
        in_specs=[pl.BlockSpec(memory_space=pltpu.SMEM),
                  pl.BlockSpec((None, BLK, qw), lambda b, i: (b, i, 0)),
                  kv(prev), kv(cur), kv(nxt), kv(vprev), kv(vcur), kv(vnxt)],
        out_specs=pl.BlockSpec((None, BLK, qw), lambda b, i: (b, i, 0)),
        out_shape=jax.ShapeDtypeStruct((B, L, qw), BF16),
        compiler_params=_cparams(("parallel", "parallel"),
                                 [((BLK, qw), BF16)] * 2 + [((BLK, kvw), BF16)] * 6,
                                 [((G * BLK, 3 * BLK), F32)] * 4),
        name="swa_attention",
    )(sink.reshape(1, SWA_HQ), z, z, z, z, z, z, z)


def _swa_layout(w_in, q_gain, k_gain, w_out, L):
    D = w_in.shape[0]
    dh, hq, hkv = SWA_DH, SWA_HQ, SWA_HKV
    hf = dh // 2
    q_end, k_end = hq * dh, (hq + hkv) * dh

    def rot_slots(w, n):
        w = w.reshape(D, n, 2, hf)
        return jnp.pad(w, ((0, 0), (0, 0), (0, 0), (0, LANES // 2 - hf))).reshape(D, n * LANES)

    def val_slots(w, n):
        return jnp.pad(w.reshape(D, n, dh), ((0, 0), (0, 0), (0, LANES - dh))).reshape(D, n * LANES)

    w = jnp.concatenate([rot_slots(w_in[:, :q_end], hq), rot_slots(w_in[:, q_end:k_end], hkv),
                         val_slots(w_in[:, k_end:], hkv)], axis=1)

    def gain_slot(g):
        return jnp.pad(g.reshape(2, hf), ((0, 0), (0, LANES // 2 - hf))).reshape(1, LANES)

    flag = jnp.concatenate([jnp.ones(((hq + hkv) * LANES,), F32), jnp.zeros((hkv * LANES,), F32)]).reshape(1, -1)
    wo = jnp.pad(w_out.reshape(hq, dh, -1), ((0, 0), (0, LANES - dh), (0, 0))).reshape(hq * LANES, -1)

    inv = ROPE_THETA ** (-np.arange(0, dh, 2) / dh)
    ang = np.arange(L)[:, None] * inv[None, :]
    zero = np.zeros((L, LANES // 2 - hf))
    cos = jnp.asarray(np.concatenate([np.cos(ang), zero, np.cos(ang), zero], axis=1), F32)
    sin = jnp.asarray(np.concatenate([-np.sin(ang), zero, np.sin(ang), zero], axis=1), F32)
    gq = gain_slot(q_gain) * (dh ** -0.5)
    gk = gain_slot(k_gain)
    roll = lambda g: jnp.roll(g, LANES // 2, axis=1)
    cos3 = jnp.stack([cos * gq, cos * gk, jnp.ones_like(cos)])
    sin3 = jnp.stack([sin * roll(gq), sin * roll(gk), jnp.zeros_like(sin)])
    return w, flag, wo, cos3, sin3


def _hg_decays(f_ref, q_ref, rows, lb, tri, total_row, mid_row):
    f = lb + (1.0 - lb) * jax.nn.sigmoid(f_ref[rows, :].astype(F32))
    gl = jnp.log(f)
    cum = _cumdot(tri, gl)
    total = cum[total_row:total_row + 1, :]
    mid = cum[mid_row:mid_row + 1, :]
    q_dec = _silu(q_ref[rows, :].astype(F32)) * jnp.exp(cum - mid)
    k_inv = (1.0 - f) * jnp.exp(mid - cum)
    q_full = (q_dec * jnp.exp(mid)).astype(BF16)
    k_end = (k_inv * jnp.exp(total - mid)).astype(BF16)
    return q_dec.astype(BF16), k_inv.astype(BF16), q_full, k_end, jnp.exp(total)


def _hg_fwd_kernel(lb_ref, q_ref, i_ref, f_ref, o_ref, st_ref, *, T, C, H, DK, DV):
    @pl.when(pl.program_id(1) == 0)
    def _():
        st_ref[...] = jnp.zeros_like(st_ref)

    row = lax.broadcasted_iota(jnp.int32, (C, C), 0)
    col = lax.broadcasted_iota(jnp.int32, (C, C), 1)
    causal = row >= col
    tri = jnp.concatenate([causal.astype(BF16)] * 2, axis=1)
    lb = lb_ref[...]

    def chunk(c, carry):
        rows = pl.ds(pl.multiple_of(c * C, C), C)
        q_dec, k_inv, q_full, k_end, s_decay = _hg_decays(f_ref, q_ref, rows, lb, tri, C - 1, C // 2 - 1)
        v = i_ref[rows, :]
        for h in range(H):
            ck = slice(h * DK, (h + 1) * DK)
            cv = slice(h * DV, (h + 1) * DV)
            s = lax.dot_general(q_dec[:, ck], k_inv[:, ck], NT_DIMS, preferred_element_type=F32)
            s = jnp.where(causal, s, 0.0)
            st = st_ref[h]
            o_ref[rows, cv] = _dot(s.astype(BF16), v[:, cv]) + lax.dot_general(
                q_full[:, ck], st.astype(BF16), NT_DIMS, preferred_element_type=F32)
            st_ref[h] = st * s_decay[:, ck] + lax.dot_general(v[:, cv], k_end[:, ck], TN_DIMS,
                                                              preferred_element_type=F32)
        return carry

    lax.fori_loop(0, T // C, chunk, 0)


def _hg_bwd_kernel(lb_ref, gain_ref, q_ref, i_ref, f_ref, gate_ref, o1_ref, o_ref, st_ref, *, T, C, H, DK, DV):
    @pl.when(pl.program_id(1) == 0)
    def _():
        st_ref[...] = jnp.zeros_like(st_ref)

    row = lax.broadcasted_iota(jnp.int32, (C, C), 0)
    col = lax.broadcasted_iota(jnp.int32, (C, C), 1)
    anti = col >= row
    tri = jnp.concatenate([anti.astype(BF16)] * 2, axis=1)
    lb = lb_ref[...]
    gain = gain_ref[...]
    nc = T // C

    def chunk(cc, carry):
        rows = pl.ds(pl.multiple_of((nc - 1 - cc) * C, C), C)
        q_dec, k_inv, q_full, k_end, s_decay = _hg_decays(f_ref, q_ref, rows, lb, tri, 0, C // 2)
        v = i_ref[rows, :]
        for h in range(H):
            ck = slice(h * DK, (h + 1) * DK)
            cv = slice(h * DV, (h + 1) * DV)
            s = lax.dot_general(q_dec[:, ck], k_inv[:, ck], NT_DIMS, preferred_element_type=F32)
            s = jnp.where(anti, s, 0.0)
            st = st_ref[h]
            o = o1_ref[rows, cv] + _dot(s.astype(BF16), v[:, cv]) + lax.dot_general(
                q_full[:, ck], st.astype(BF16), NT_DIMS, preferred_element_type=F32)
            y = o * lax.rsqrt(jnp.mean(o * o, axis=-1, keepdims=True) + NORM_EPS) * gain
            o_ref[rows, cv] = (y * _silu(gate_ref[rows, cv].astype(F32))).astype(o_ref.dtype)
            st_ref[h] = st * s_decay[:, ck] + lax.dot_general(v[:, cv], k_end[:, ck], TN_DIMS,
                                                              preferred_element_type=F32)
        return carry

    lax.fori_loop(0, nc, chunk, 0)


def _hgrn(z, lb, gain):
    B, L, _ = z.shape
    H, C = HG_HEADS, HG_CHUNK
    HD = lb.shape[1]
    DK = HD // H
    DV = gain.shape[0]
    T = min(L, 512)
    nT = L // T
    kw = dict(T=T, C=C, H=H, DK=DK, DV=DV)
    blocks = [((T, HD), BF16)] * 5 + [((T, H * DV), F32)] * 2
    state = [((H, DV, DK), F32)] + [((C, HD), F32)] * 12
    lb3 = lb.reshape(2, 1, HD)
    o1 = pl.pallas_call(
        functools.partial(_hg_fwd_kernel, **kw),
        grid=(B, nT),
        in_specs=[pl.BlockSpec((None, 1, HD), lambda b, i: (0, 0, 0)),
                  pl.BlockSpec((None, T, HD), lambda b, i: (b, i, 0)),
                  pl.BlockSpec((None, T, H * DV), lambda b, i: (b, i, 1)),
                  pl.BlockSpec((None, T, HD), lambda b, i: (b, i, 2))],
        out_specs=pl.BlockSpec((None, T, H * DV), lambda b, i: (b, i, 0)),
        out_shape=jax.ShapeDtypeStruct((B, L, H * DV), F32),
        scratch_shapes=[pltpu.VMEM((H, DV, DK), F32)],
        compiler_params=_cparams(("parallel", "arbitrary"), blocks, state),
        name="hgrn_fwd",
    )(lb3, z, z, z)
    return pl.pallas_call(
        functools.partial(_hg_bwd_kernel, **kw),
        grid=(B, nT),
        in_specs=[pl.BlockSpec((None, 1, HD), lambda b, i: (1, 0, 0)),
                  pl.BlockSpec((1, DV), lambda b, i: (0, 0)),
                  pl.BlockSpec((None, T, HD), lambda b, i: (b, nT - 1 - i, 0)),
                  pl.BlockSpec((None, T, H * DV), lambda b, i: (b, nT - 1 - i, 1)),
                  pl.BlockSpec((None, T, HD), lambda b, i: (b, nT - 1 - i, 3)),
                  pl.BlockSpec((None, T, H * DV), lambda b, i: (b, nT - 1 - i, 4)),
                  pl.BlockSpec((None, T, H * DV), lambda b, i: (b, nT - 1 - i, 0))],
        out_specs=pl.BlockSpec((None, T, H * DV), lambda b, i: (b, nT - 1 - i, 0)),
        out_shape=jax.ShapeDtypeStruct((B, L, H * DV), BF16),
        scratch_shapes=[pltpu.VMEM((H, DV, DK), F32)],
        compiler_params=_cparams(("parallel", "arbitrary"), blocks, state),
        name="hgrn_bwd",
    )(lb3, gain.reshape(1, DV), z, z, z, z, o1)


def _ret_rope_tables(L, dk):
    inv = ROPE_THETA ** (-np.arange(0, dk, 2) / dk)
    ang = np.arange(L)[:, None] * inv[None, :]
    scale = np.array([1.0, dk ** -0.5])[:, None, None]
    return jnp.asarray(np.cos(ang)[None] * scale, F32), jnp.asarray(np.sin(ang)[None] * scale, F32)


def kernel(x_prompt, x_sample, c_prompt, c_sample, ada_w, ada_b, norm_g, hy_w_in, hy_conv_w, hy_conv_b, hy_w1, hy_b1, hy_w2, hy_b2, hy_w3, hy_freq, hy_decay, hy_skip, hy_w_out, ret_w_in, ret_decay, ret_w_out, swa_w_in, swa_q_gain, swa_k_gain, swa_sink, swa_w_out, hg_w_in, hg_lb, hg_gain, hg_w_out, ffn_w_gate, ffn_w_val, ffn_conv_w, ffn_conv_b, ffn_w_down):
    depth, D = norm_g.shape[0], norm_g.shape[2]
    groups = [(x_prompt, c_prompt), (x_sample, c_sample)]
    mods = _ada_mod(jnp.concatenate([c for _, c in groups], axis=0), ada_w, ada_b)

    bf = lambda w: w.astype(BF16)
    hy_w_in_b, hy_w_out_b = bf(hy_w_in), bf(hy_w_out)
    ret_w_in_b, ret_w_out_b = bf(ret_w_in), bf(ret_w_out)
    hg_w_in_b, hg_w_out_b = bf(hg_w_in), bf(hg_w_out)
    wg_b, wv_b, wd_b = bf(ffn_w_gate), bf(ffn_w_val), bf(ffn_w_down)
    hg_sm = jax.nn.softmax(hg_lb.astype(F32), axis=1)
    hg_lower = jnp.cumsum(hg_sm, axis=1) - hg_sm
    ret_log_gamma = -jnp.exp(ret_decay.astype(F32))

    outs = []
    row0 = 0
    for x, c in groups:
        B, L, _ = x.shape
        for layer in range(depth):
            kind, j = layer % N_MIXERS, layer // N_MIXERS
            mod = mods[layer, row0:row0 + B].reshape(B, N_MOD, 1, D)
            sh1, sc1, g1, sh2, sc2, g2 = (mod[:, m] for m in range(N_MOD))
            gn1 = norm_g[layer, 0].reshape(1, D)
            gn2 = norm_g[layer, 1].reshape(1, D)
            if kind == 0:
                tabs = _fft_tables(L)
                z = _proj_in_conv(x, gn1, sc1, sh1, hy_w_in_b[j], hy_conv_w[j], hy_conv_b[j], tn=1024)
                taps, asum = _hy_filter_taps(L, hy_w1[j], hy_b1[j], hy_w2[j], hy_b2[j], hy_w3[j], hy_freq[j],
                                             hy_decay[j])
                spec = _hy_spectrum(taps, asum, tabs)
                a = _hy_conv(z, hy_skip[j], spec, tabs)
                x = _proj_out(a, hy_w_out_b[j], x, g1)
            elif kind == 1:
                dk = D // RET_HEADS
                cos, sin = _ret_rope_tables(L, dk)
                z = _proj_in_halfrope(x, gn1, sc1, sh1, ret_w_in_b[j], cos, sin, tn=RET_HEADS * dk, rope_tiles=2,
                                      dk=dk)
                a = _retention(z, ret_log_gamma[j])
                x = _proj_out(a, ret_w_out_b[j], x, g1)
            elif kind == 2:
                w, flag, wo, cos, sin = _swa_layout(swa_w_in[j], swa_q_gain[j], swa_k_gain[j], swa_w_out[j], L)
                z = _proj_in_rope(x, gn1, sc1, sh1, bf(w), flag, cos, sin, tn=SWA_HKV * LANES, dh=SWA_DH,
                                  q_tiles=SWA_HQ // SWA_HKV)
                a = _swa_attention(z, swa_sink[j])
                x = _proj_out(a, bf(wo), x, g1)
            else:
                z = _proj_in(x, gn1, sc1, sh1, hg_w_in_b[j], tn=1024)
                a = _hgrn(z, hg_lower[:, layer], hg_gain[j])
                x = _proj_out(a, hg_w_out_b[j], x, g1)
            x = _ffn(x, gn2, sc2, sh2, g2, wg_b[layer], wv_b[layer], ffn_conv_w[layer], ffn_conv_b[layer],
                     wd_b[layer])
        outs.append(x)
        row0 += B
    return tuple(outs)
```

```python
import functools
import math

import numpy as np
import jax
import jax.numpy as jnp
from jax import lax
from jax.experimental import pallas as pl
from jax.experimental.pallas import tpu as pltpu

F32 = jnp.float32
BF16 = jnp.bfloat16

NORM_EPS = 1e-6
N_MIXERS = 4
N_MOD = 6
HY_BANDS = 16
RET_HEADS = 4
RET_CHUNK = 128
SWA_HQ = 16
SWA_HKV = 4
SWA_DH = 64
WINDOW = 128
ATTN_BLOCK = 128
ROPE_THETA = 10000.0
NEG_INF = -1e30
HG_HEADS = 8
HG_CHUNK = 128

LANES = 128
SUBLANES = 8
BF16_ROWS = 16
VMEM_LIMIT_CAP = 60 * 1024 * 1024
VMEM_SLACK = 8 * 1024 * 1024

NT_DIMS = (((1,), (1,)), ((), ()))
TN_DIMS = (((0,), (0,)), ((), ()))


def _nbytes(shape, dtype):
    item = jnp.dtype(dtype).itemsize
    sub = SUBLANES * 4 // item
    dims = list(shape)
    dims[-1] = -(-dims[-1] // LANES) * LANES
    if len(dims) > 1:
        dims[-2] = -(-dims[-2] // sub) * sub
    return int(np.prod(dims)) * item


def _cparams(semantics, pipelined, resident):
    need = 2 * sum(_nbytes(s, d) for s, d in pipelined) + sum(_nbytes(s, d) for s, d in resident)
    return pltpu.CompilerParams(dimension_semantics=semantics,
                                vmem_limit_bytes=min(need + VMEM_SLACK, VMEM_LIMIT_CAP))


def _dot(a, b):
    return jnp.dot(a, b, preferred_element_type=F32)


def _split(x):
    hi = x.astype(BF16)
    lo = (x - hi.astype(F32)).astype(BF16)
    return hi, lo


def _dot_tab(t_cat, d):
    d_hi, d_lo = _split(d)
    return _dot(t_cat, jnp.concatenate([d_hi, d_hi, d_lo], axis=0))


def _dot3(a, b):
    a_hi, a_lo = _split(a)
    b_hi, b_lo = _split(b)
    return _dot(a_hi, b_hi) + _dot(a_lo, b_hi) + _dot(a_hi, b_lo)


def _cumdot(tri2, g):
    g_hi, g_lo = _split(g)
    return _dot(tri2, jnp.concatenate([g_hi, g_lo], axis=0))


def _silu(x):
    return x * jax.nn.sigmoid(x)


def _norm_mod(x, g, sc, sh):
    xf = x.astype(F32)
    y = xf * lax.rsqrt(jnp.mean(xf * xf, axis=-1, keepdims=True) + NORM_EPS)
    return (y * g) * (1.0 + sc) + sh


def _row_tile(L):
    return min(L, 1024)


def _ada_kernel(c_ref, w_ref, b_ref, o_ref):
    cs = _silu(c_ref[...]).astype(BF16)
    o_ref[...] = _dot(cs, w_ref[...].astype(BF16)) + b_ref[...]


def _ada_mod(c_all, ada_w, ada_b):
    depth, D, N = ada_w.shape
    R = c_all.shape[0]
    tn = 1024
    return pl.pallas_call(
        _ada_kernel,
        grid=(depth, N // tn),
        in_specs=[pl.BlockSpec((R, D), lambda l, j: (0, 0)),
                  pl.BlockSpec((None, D, tn), lambda l, j: (l, 0, j)),
                  pl.BlockSpec((None, 1, tn), lambda l, j: (l, 0, j))],
        out_specs=pl.BlockSpec((None, R, tn), lambda l, j: (l, 0, j)),
        out_shape=jax.ShapeDtypeStruct((depth, R, N), F32),
        compiler_params=_cparams(("parallel", "parallel"),
                                 [((D, tn), F32), ((R, D), F32), ((R, tn), F32)], []),
        name="ada_mod",
    )(c_all, ada_w, ada_b.reshape(depth, 1, N))


def _fill_h(h_ref, x_ref, g_ref, sc_ref, sh_ref):
    h_ref[...] = _norm_mod(x_ref[...], g_ref[...], sc_ref[...], sh_ref[...]).astype(BF16)


def _fill_h_halo(h_ref, x_ref, xp_ref, xn_ref, g_ref, sc_ref, sh_ref, tm):
    i = pl.program_id(1)
    last = pl.num_programs(1) - 1
    g, sc, sh = g_ref[...], sc_ref[...], sh_ref[...]
    h_ref[BF16_ROWS:BF16_ROWS + tm, :] = _norm_mod(x_ref[...], g, sc, sh).astype(BF16)
    hp = _norm_mod(xp_ref[...], g, sc, sh)
    hn = _norm_mod(xn_ref[...], g, sc, sh)
    h_ref[0:BF16_ROWS, :] = jnp.where(i > 0, hp, 0.0).astype(BF16)
    h_ref[BF16_ROWS + tm:, :] = jnp.where(i < last, hn, 0.0).astype(BF16)


def _conv3_rows(z_ref, cw, cb, tm):
    o = BF16_ROWS
    return (z_ref[o - 1:o - 1 + tm, :] * cw[0:1, :] + z_ref[o:o + tm, :] * cw[1:2, :]
            + z_ref[o + 1:o + 1 + tm, :] * cw[2:3, :] + cb)


def _proj_plain_kernel(x_ref, g_ref, sc_ref, sh_ref, w_ref, o_ref, h_ref):
    @pl.when(pl.program_id(2) == 0)
    def _():
        _fill_h(h_ref, x_ref, g_ref, sc_ref, sh_ref)

    o_ref[...] = _dot(h_ref[...], w_ref[...]).astype(o_ref.dtype)


def _proj_halfrope_kernel(x_ref, g_ref, sc_ref, sh_ref, w_ref, cos_ref, sin_ref, o_ref, h_ref, *, dk):
    @pl.when(pl.program_id(2) == 0)
    def _():
        _fill_h(h_ref, x_ref, g_ref, sc_ref, sh_ref)

    z = _dot(h_ref[...], w_ref[...])
    cos, sin = cos_ref[...], sin_ref[...]
    half = dk // 2
    for c0 in range(0, z.shape[1], dk):
        x1, x2 = z[:, c0:c0 + half], z[:, c0 + half:c0 + dk]
        o_ref[:, c0:c0 + half] = (x1 * cos - x2 * sin).astype(o_ref.dtype)
        o_ref[:, c0 + half:c0 + dk] = (x1 * sin + x2 * cos).astype(o_ref.dtype)


def _proj_conv_kernel(x_ref, xp_ref, xn_ref, g_ref, sc_ref, sh_ref, w_ref, cw_ref, cb_ref,
                      o_ref, h_ref, z_ref, *, tm):
    @pl.when(pl.program_id(2) == 0)
    def _():
        _fill_h_halo(h_ref, x_ref, xp_ref, xn_ref, g_ref, sc_ref, sh_ref, tm)

    z_ref[...] = _dot(h_ref[...], w_ref[...])
    r = _conv3_rows(z_ref, cw_ref[...], cb_ref[...], tm).astype(o_ref.dtype)
    for c in range(o_ref.shape[0]):
        o_ref[c] = r[:, c * LANES:(c + 1) * LANES]


def _proj_rope_kernel(x_ref, g_ref, sc_ref, sh_ref, w_ref, flag_ref, ones_ref, cos_ref, sin_ref,
                      o_ref, h_ref, *, dh):
    @pl.when(pl.program_id(2) == 0)
    def _():
        _fill_h(h_ref, x_ref, g_ref, sc_ref, sh_ref)

    z = _dot(h_ref[...], w_ref[...])
    ms = _dot((z * z).astype(BF16), ones_ref[...]) * (1.0 / dh)
    zn = z * jnp.where(flag_ref[...] > 0.0, lax.rsqrt(ms + NORM_EPS), 1.0)
    cos, sin = cos_ref[...], sin_ref[...]
    for s in range(z.shape[1] // LANES):
        sl = slice(s * LANES, (s + 1) * LANES)
        zs = zn[:, sl]
        o_ref[:, sl] = (zs * cos + pltpu.roll(zs, LANES // 2, axis=1) * sin).astype(o_ref.dtype)


def _mod_specs(D):
    return [pl.BlockSpec((1, D), lambda b, i, j: (0, 0)),
            pl.BlockSpec((None, 1, D), lambda b, i, j: (b, 0, 0)),
            pl.BlockSpec((None, 1, D), lambda b, i, j: (b, 0, 0))]


def _halo_specs(tm, D, L):
    hb = tm // BF16_ROWS
    nhb = L // BF16_ROWS
    return [pl.BlockSpec((None, tm, D), lambda b, i, j: (b, i, 0)),
            pl.BlockSpec((None, BF16_ROWS, D), lambda b, i, j: (b, jnp.maximum(i * hb - 1, 0), 0)),
            pl.BlockSpec((None, BF16_ROWS, D), lambda b, i, j: (b, jnp.minimum((i + 1) * hb, nhb - 1), 0))]


def _proj_in(x, g, sc, sh, w, *, tn):
    B, L, D = x.shape
    N = w.shape[1]
    tm = _row_tile(L)
    return pl.pallas_call(
        _proj_plain_kernel,
        grid=(B, L // tm, N // tn),
        in_specs=[pl.BlockSpec((None, tm, D), lambda b, i, j: (b, i, 0))] + _mod_specs(D)
        + [pl.BlockSpec((D, tn), lambda b, i, j: (0, j))],
        out_specs=pl.BlockSpec((None, tm, tn), lambda b, i, j: (b, i, j)),
        out_shape=jax.ShapeDtypeStruct((B, L, N), BF16),
        scratch_shapes=[pltpu.VMEM((tm, D), BF16)],
        compiler_params=_cparams(("parallel", "parallel", "arbitrary"),
                                 [((tm, D), F32), ((D, tn), BF16), ((tm, tn), BF16)],
                                 [((tm, D), BF16), ((tm, tn), F32)]),
        name="proj_in",
    )(x, g, sc, sh, w)


def _proj_in_halfrope(x, g, sc, sh, w, cos, sin, *, tn, rope_tiles, dk):
    B, L, D = x.shape
    N = w.shape[1]
    tm = _row_tile(L)
    half = dk // 2
    table = lambda b, i, j: (jnp.minimum(j, rope_tiles), i, 0)
    return pl.pallas_call(
        functools.partial(_proj_halfrope_kernel, dk=dk),
        grid=(B, L // tm, N // tn),
        in_specs=[pl.BlockSpec((None, tm, D), lambda b, i, j: (b, i, 0))] + _mod_specs(D)
        + [pl.BlockSpec((D, tn), lambda b, i, j: (0, j)),
           pl.BlockSpec((None, tm, half), table),
           pl.BlockSpec((None, tm, half), table)],
        out_specs=pl.BlockSpec((None, tm, tn), lambda b, i, j: (b, i, j)),
        out_shape=jax.ShapeDtypeStruct((B, L, N), BF16),
        scratch_shapes=[pltpu.VMEM((tm, D), BF16)],
        compiler_params=_cparams(("parallel", "parallel", "arbitrary"),
                                 [((tm, D), F32), ((D, tn), BF16), ((tm, tn), BF16), ((tm, half), F32),
                                  ((tm, half), F32)],
                                 [((tm, D), BF16), ((tm, tn), F32), ((tm, tn), F32)]),
        name="proj_in_halfrope",
    )(x, g, sc, sh, w, cos, sin)


def _proj_in_conv(x, g, sc, sh, w, cw, cb, *, tn):
    B, L, D = x.shape
    N = w.shape[1]
    tm = _row_tile(L)
    te = tm + 2 * BF16_ROWS
    return pl.pallas_call(
        functools.partial(_proj_conv_kernel, tm=tm),
        grid=(B, L // tm, N // tn),
        in_specs=_halo_specs(tm, D, L) + _mod_specs(D)
        + [pl.BlockSpec((D, tn), lambda b, i, j: (0, j)),
           pl.BlockSpec((3, tn), lambda b, i, j: (0, j)),
           pl.BlockSpec((1, tn), lambda b, i, j: (0, j))],
        out_specs=pl.BlockSpec((None, tn // LANES, tm, LANES), lambda b, i, j: (b, j, i, 0)),
        out_shape=jax.ShapeDtypeStruct((B, N // LANES, L, LANES), BF16),
        scratch_shapes=[pltpu.VMEM((te, D), BF16), pltpu.VMEM((te, tn), F32)],
        compiler_params=_cparams(("parallel", "parallel", "arbitrary"),
                                 [((tm, D), F32), ((D, tn), BF16), ((tm, tn), BF16)],
                                 [((te, D), BF16), ((te, tn), F32), ((te, tn), F32)]),
        name="proj_in_conv",
    )(x, x, x, g, sc, sh, w, cw, cb.reshape(1, N))


def _proj_in_rope(x, g, sc, sh, w, flag, cos, sin, *, tn, dh, q_tiles):
    B, L, D = x.shape
    N = w.shape[1]
    tm = _row_tile(L)
    slot = lax.broadcasted_iota(jnp.int32, (tn, tn), 0) // LANES == lax.broadcasted_iota(jnp.int32, (tn, tn), 1) // LANES
    table = lambda b, i, j: (jnp.maximum(j - (q_tiles - 1), 0), i, 0)
    return pl.pallas_call(
        functools.partial(_proj_rope_kernel, dh=dh),
        grid=(B, L // tm, N // tn),
        in_specs=[pl.BlockSpec((None, tm, D), lambda b, i, j: (b, i, 0))] + _mod_specs(D)
        + [pl.BlockSpec((D, tn), lambda b, i, j: (0, j)),
           pl.BlockSpec((1, tn), lambda b, i, j: (0, j)),
           pl.BlockSpec((tn, tn), lambda b, i, j: (0, 0)),
           pl.BlockSpec((None, tm, LANES), table),
           pl.BlockSpec((None, tm, LANES), table)],
        out_specs=pl.BlockSpec((None, tm, tn), lambda b, i, j: (b, i, j)),
        out_shape=jax.ShapeDtypeStruct((B, L, N), BF16),
        scratch_shapes=[pltpu.VMEM((tm, D), BF16)],
        compiler_params=_cparams(("parallel", "parallel", "arbitrary"),
                                 [((tm, D), F32), ((D, tn), BF16), ((tm, tn), BF16), ((tm, LANES), F32),
                                  ((tm, LANES), F32), ((tn, tn), BF16)],
                                 [((tm, D), BF16), ((tm, tn), F32), ((tm, tn), F32), ((tm, tn), F32)]),
        name="proj_in_rope",
    )(x, g, sc, sh, w, flag, slot.astype(BF16), cos, sin)


def _proj_out_kernel(a_ref, w_ref, x_ref, gate_ref, o_ref):
    o_ref[...] = x_ref[...] + gate_ref[...] * _dot(a_ref[...], w_ref[...])


def _proj_out_blocked_kernel(a_ref, w_ref, x_ref, gate_ref, o_ref):
    a = jnp.concatenate([a_ref[c] for c in range(a_ref.shape[0])], axis=1)
    o_ref[...] = x_ref[...] + gate_ref[...] * _dot(a, w_ref[...])


def _proj_out(a, w, x, gate):
    blocked = a.ndim == 4
    B, L = x.shape[:2]
    K, D = w.shape
    tm = _row_tile(L)
    if blocked:
        a_spec = pl.BlockSpec((None, K // LANES, tm, LANES), lambda b, i: (b, 0, i, 0))
    else:
        a_spec = pl.BlockSpec((None, tm, K), lambda b, i: (b, i, 0))
    return pl.pallas_call(
        _proj_out_blocked_kernel if blocked else _proj_out_kernel,
        grid=(B, L // tm),
        in_specs=[a_spec,
                  pl.BlockSpec((K, D), lambda b, i: (0, 0)),
                  pl.BlockSpec((None, tm, D), lambda b, i: (b, i, 0)),
                  pl.BlockSpec((None, 1, D), lambda b, i: (b, 0, 0))],
        out_specs=pl.BlockSpec((None, tm, D), lambda b, i: (b, i, 0)),
        out_shape=jax.ShapeDtypeStruct((B, L, D), F32),
        compiler_params=_cparams(("parallel", "parallel"),
                                 [((tm, K), BF16), ((K, D), BF16), ((tm, D), F32), ((tm, D), F32)],
                                 [((tm, D), F32)]),
        name="proj_out",
    )(a, w, x, gate)


def _ffn_kernel(x_ref, xp_ref, xn_ref, g_ref, sc_ref, sh_ref, gate_ref, wg_ref, wv_ref, cw_ref, cb_ref,
                wd_ref, o_ref, h_ref, z_ref, *, tm, tf):
    _fill_h_halo(h_ref, x_ref, xp_ref, xn_ref, g_ref, sc_ref, sh_ref, tm)
    pieces = [slice(c * tf, (c + 1) * tf) for c in range(wg_ref.shape[1] // tf)]
    vals = []
    for c, cols in enumerate(pieces):
        z_ref[c] = _dot(h_ref[...], wg_ref[:, cols])
        vals.append(_dot(h_ref[BF16_ROWS:BF16_ROWS + tm, :], wv_ref[:, cols]))
    acc = None
    for c, cols in enumerate(pieces):
        a = _conv3_rows(z_ref.at[c], cw_ref[:, cols], cb_ref[:, cols], tm)
        part = _dot((_silu(a) * vals[c]).astype(BF16), wd_ref[cols, :])
        acc = part if acc is None else acc + part
    o_ref[...] = x_ref[...] + gate_ref[...] * acc


def _ffn(x, g, sc, sh, gate, wg, wv, cw, cb, wd):
    B, L, D = x.shape
    F = wg.shape[1]
    tm = min(L, 512)
    te = tm + 2 * BF16_ROWS
    tf = F // 2
    const = lambda b, i, j: (0, 0)
    resident = lambda shape: pl.BlockSpec(shape, const, pipeline_mode=pl.Buffered(1))
    return pl.pallas_call(
        functools.partial(_ffn_kernel, tm=tm, tf=tf),
        grid=(B, L // tm, 1),
        in_specs=_halo_specs(tm, D, L) + _mod_specs(D)
        + [pl.BlockSpec((None, 1, D), lambda b, i, j: (b, 0, 0)),
           resident((D, F)), resident((D, F)), resident((3, F)), resident((1, F)), resident((F, D))],
        out_specs=pl.BlockSpec((None, tm, D), lambda b, i, j: (b, i, 0)),
        out_shape=jax.ShapeDtypeStruct((B, L, D), F32),
        scratch_shapes=[pltpu.VMEM((te, D), BF16), pltpu.VMEM((F // tf, te, tf), F32)],
        compiler_params=_cparams(("parallel", "parallel", "arbitrary"),
                                 [((tm, D), F32), ((tm, D), F32)],
                                 [((D, F), BF16), ((D, F), BF16), ((F, D), BF16), ((te, D), BF16),
                                  ((te, F), F32), ((tm, F), F32), ((tm, tf), F32), ((tm, tf), BF16),
                                  ((tm, D), F32), ((tm, D), F32)]),
        name="ffn",
    )(x, x, x, g, sc, sh, gate, wg, wv, cw, cb.reshape(1, F), wd)


def _fft_dims(L):
    N = 2 * L
    p = N.bit_length() - 1
    n1 = 1 << ((p + 1) // 2)
    return N, n1, N // n1


def _k1_pad(n1):
    return n1 // 2 + 8


def _fft_tables(L):
    N, n1, n2 = _fft_dims(L)
    k1p = _k1_pad(n1)
    k1 = np.arange(k1p)
    valid = (k1 <= n1 // 2).astype(np.float64)
    weight = np.where((k1 == 0) | (k1 == n1 // 2), 1.0, 2.0) * valid
    m1 = np.arange(n1 // 2)
    n2v = np.arange(n2)
    n_idx = n2 * m1[None, :] + n2v[:, None]
    phi = 2.0 * np.pi * k1[None, :, None] * n_idx[:, None, :] / N
    s1 = np.concatenate([np.cos(phi) * valid[None, :, None], -np.sin(phi) * valid[None, :, None]], axis=1)
    phit = np.transpose(phi, (0, 2, 1))
    gl = np.concatenate([np.cos(phit) * weight[None, None, :], -np.sin(phit) * weight[None, None, :]],
                        axis=2) / N
    th = 2.0 * np.pi * np.outer(n2v, n2v) / n2
    c, s = np.cos(th), np.sin(th)
    f2 = np.block([[c, s], [-s, c]])
    f2c = np.block([[c, -s], [s, c]])

    def cat(a):
        a32 = jnp.asarray(a, F32)
        hi = a32.astype(BF16)
        lo = (a32 - hi.astype(F32)).astype(BF16)
        return jnp.concatenate([hi, lo, hi], axis=-1)

    return dict(N=N, n1=n1, n2=n2, k1p=k1p, s1=cat(s1), gl=cat(gl), f2=cat(f2), f2c=cat(f2c))


def _hy_positions(L):
    idx = np.concatenate([np.arange(L), np.array([0]), np.arange(L - 1, 0, -1)])
    t = idx / (L - 1)
    ang = (2.0 * np.pi / L) * idx
    bands = np.linspace(1e-4, HY_BANDS - 1, HY_BANDS)
    z = np.concatenate([t[:, None], np.cos(bands[None, :] * ang[:, None]), -np.sin(bands[None, :] * ang[:, None])],
                       axis=1)
    zp = np.zeros((2 * L, LANES))
    zp[:, : z.shape[1]] = z
    return jnp.asarray(zp, F32)


def _hy_mlp_kernel(z_ref, w1_ref, b1_ref, w2_ref, b2_ref, w3_ref, fr_ref, dec_ref, taps_ref, asum_ref, *, L, tr):
    i = pl.program_id(0)
    z = z_ref[...]
    fr = fr_ref[...]
    h = jnp.sin(fr * (_dot3(z, w1_ref[...]) + b1_ref[...]))
    h = jnp.sin(fr * (_dot3(h, w2_ref[...]) + b2_ref[...]))
    h = _dot3(h, w3_ref[...])
    t = z[:, 0:1]
    taps = h * jnp.exp(-t * jnp.abs(dec_ref[...]))
    row = i * tr + lax.broadcasted_iota(jnp.int32, (tr, 1), 0)
    taps = jnp.where(row == L, 0.0, taps)
    taps_ref[...] = taps

    @pl.when(i == 0)
    def _():
        asum_ref[...] = jnp.zeros_like(asum_ref)

    asum_ref[...] += jnp.sum(jnp.abs(taps), axis=0, keepdims=True)


def _hy_filter_taps(L, w1, b1, w2, b2, w3, freq, decay):
    W = decay.shape[1]
    E, O = w1.shape
    tr = 512
    half = L // tr
    z = _hy_positions(L)
    w1p = jnp.zeros((LANES, O), F32).at[:E].set(w1)
    return pl.pallas_call(
        functools.partial(_hy_mlp_kernel, L=L, tr=tr),
        grid=(2 * L // tr,),
        in_specs=[pl.BlockSpec((tr, LANES), lambda i: (i, 0)),
                  pl.BlockSpec((LANES, O), lambda i: (0, 0)),
                  pl.BlockSpec((1, O), lambda i: (0, 0)),
                  pl.BlockSpec((O, O), lambda i: (0, 0)),
                  pl.BlockSpec((1, O), lambda i: (0, 0)),
                  pl.BlockSpec((O, W), lambda i: (0, i // half)),
                  pl.BlockSpec((1, O), lambda i: (0, 0)),
                  pl.BlockSpec((None, 1, W), lambda i: (i // half, 0, 0))],
        out_specs=[pl.BlockSpec((tr, W), lambda i: (i, 0)),
                   pl.BlockSpec((1, W), lambda i: (0, 0))],
        out_shape=[jax.ShapeDtypeStruct((2 * L, W), F32), jax.ShapeDtypeStruct((1, W), F32)],
        compiler_params=_cparams(("arbitrary",), [((tr, W), F32), ((O, W), F32), ((tr, LANES), F32)],
                                 [((tr, W), F32), ((tr, W), F32)]),
        name="hyena_filter_mlp",
    )(z, w1p, b1.reshape(1, O), w2, b2.reshape(1, O), w3, freq.reshape(1, O), decay.reshape(2, 1, W))


FFT_UNROLL = 16


def _fft_stage1(src_ref, src_off, a_ref, s1_ref, n1, n2, k1p, sign=None, src_off2=None):
    def body(j, carry):
        d = src_ref[pl.ds(src_off + j, n1 // 2, stride=n2), :]
        r = _dot_tab(s1_ref[j], d)
        if src_off2 is not None:
            d2 = src_ref[pl.ds(src_off2 + j, n1 // 2, stride=n2), :]
            r = r + sign * _dot_tab(s1_ref[j], d2)
        a_ref[pl.ds(pl.multiple_of(j * 2 * k1p, 8), 2 * k1p), :] = r
        return carry

    lax.fori_loop(0, n2, body, 0, unroll=FFT_UNROLL)


def _load_k1(a_ref, k, n2, k1p):
    br = a_ref[pl.ds(k, n2, stride=2 * k1p), :]
    bi = a_ref[pl.ds(k1p + k, n2, stride=2 * k1p), :]
    return jnp.concatenate([br, bi], axis=0)


def _hy_spec_kernel(taps_ref, asum_ref, s1_ref, f2_ref, h_ref, a_ref, *, L, n1, n2, k1p):
    r = lax.broadcasted_iota(jnp.int32, (2 * k1p, 1), 0)
    k1 = jnp.where(r < k1p, r, r - k1p)
    sign = (1 - 2 * (k1 & 1)).astype(F32)
    _fft_stage1(taps_ref, 0, a_ref, s1_ref, n1, n2, k1p, sign=sign, src_off2=L)
    inv = 1.0 / asum_ref[...]

    def body(k, carry):
        h_ref[k] = _dot_tab(f2_ref[...], _load_k1(a_ref, k, n2, k1p)) * inv
        return carry

    lax.fori_loop(0, n1 // 2, body, 0, unroll=2)
    body(n1 // 2, 0)


def _hy_spectrum(taps, asum, tabs):
    n1, n2, k1p = tabs["n1"], tabs["n2"], tabs["k1p"]
    N, W = taps.shape
    L = N // 2
    cb = LANES
    k1v = n1 // 2 + 1
    s1, f2 = tabs["s1"], tabs["f2"]
    return pl.pallas_call(
        functools.partial(_hy_spec_kernel, L=L, n1=n1, n2=n2, k1p=k1p),
        grid=(W // cb,),
        in_specs=[pl.BlockSpec((N, cb), lambda c: (0, c)),
                  pl.BlockSpec((1, cb), lambda c: (0, c)),
                  pl.BlockSpec(s1.shape, lambda c: (0, 0, 0), pipeline_mode=pl.Buffered(1)),
                  pl.BlockSpec(f2.shape, lambda c: (0, 0), pipeline_mode=pl.Buffered(1))],
        out_specs=pl.BlockSpec((k1v, 2 * n2, cb), lambda c: (0, 0, c)),
        out_shape=jax.ShapeDtypeStruct((k1v, 2 * n2, W), F32),
        scratch_shapes=[pltpu.VMEM((n2 * 2 * k1p, cb), F32)],
        compiler_params=_cparams(("parallel",),
                                 [((N, cb), F32), ((k1v, 2 * n2, cb), F32)],
                                 [((n2 * 2 * k1p, cb), F32), (s1.shape, BF16), (f2.shape, BF16)]),
        name="hyena_filter_spectrum",
    )(taps, asum, s1, f2)


def _hy_conv_kernel(x0_ref, x1_ref, v_ref, skip_ref, h_ref, s1_ref, f2_ref, f2c_ref, gl_ref,
                    o_ref, u_ref, a_ref, *, n1, n2, k1p):
    u_ref[...] = x1_ref[...].astype(F32) * v_ref[...].astype(F32)
    _fft_stage1(u_ref, 0, a_ref, s1_ref, n1, n2, k1p)

    cb = u_ref.shape[1]

    def mid(groups):
        loaded = [jnp.concatenate([_load_k1(a_ref, k, n2, k1p) for k in ks], axis=1) for ks in groups]
        results = []
        for ks, rhs in zip(groups, loaded):
            x = _dot_tab(f2_ref[...], rhs)
            xr, xi = x[:n2], x[n2:]
            hr = jnp.concatenate([h_ref[k, :n2, :] for k in ks], axis=1)
            hi = jnp.concatenate([h_ref[k, n2:, :] for k in ks], axis=1)
            y = jnp.concatenate([xr * hr - xi * hi, xr * hi + xi * hr], axis=0)
            results.append(_dot_tab(f2c_ref[...], y))
        for ks, c in zip(groups, results):
            for p, k in enumerate(ks):
                a_ref[pl.ds(k, n2, stride=2 * k1p), :] = c[:n2, p * cb:(p + 1) * cb]
                a_ref[pl.ds(k1p + k, n2, stride=2 * k1p), :] = c[n2:, p * cb:(p + 1) * cb]

    def mid_oct(ko, carry):
        k = 8 * ko
        mid(((k, k + 1, k + 2, k + 3), (k + 4, k + 5, k + 6, k + 7)))
        return carry

    lax.fori_loop(0, n1 // 16, mid_oct, 0)
    mid(((n1 // 2,),))
    skip = skip_ref[...]

    def last(j, carry):
        rhs = a_ref[pl.ds(pl.multiple_of(j * 2 * k1p, 8), 2 * k1p), :]
        y = _dot_tab(gl_ref[j], rhs)
        rows = pl.ds(j, n1 // 2, stride=n2)
        u_ref[rows, :] = y + u_ref[rows, :] * skip
        return carry

    lax.fori_loop(0, n2, last, 0, unroll=FFT_UNROLL)
    o_ref[...] = (u_ref[...].astype(F32) * x0_ref[...].astype(F32)).astype(o_ref.dtype)


def _hy_conv(z, skip, spec, tabs):
    n1, n2, k1p = tabs["n1"], tabs["n2"], tabs["k1p"]
    B, nc3, L, cb = z.shape
    nc = nc3 // 3
    W = nc * cb
    k1v = n1 // 2 + 1
    tables = [tabs["s1"], tabs["f2"], tabs["f2c"], tabs["gl"]]

    def const_spec(t):
        return pl.BlockSpec(t.shape, (lambda c, b: (0, 0, 0)) if t.ndim == 3 else (lambda c, b: (0, 0)),
                            pipeline_mode=pl.Buffered(1))

    return pl.pallas_call(
        functools.partial(_hy_conv_kernel, n1=n1, n2=n2, k1p=k1p),
        grid=(nc, B),
        in_specs=[pl.BlockSpec((None, None, L, cb), lambda c, b: (b, c, 0, 0), pipeline_mode=pl.Buffered(1)),
                  pl.BlockSpec((None, None, L, cb), lambda c, b: (b, nc + c, 0, 0), pipeline_mode=pl.Buffered(1)),
                  pl.BlockSpec((None, None, L, cb), lambda c, b: (b, 2 * nc + c, 0, 0),
                               pipeline_mode=pl.Buffered(1)),
                  pl.BlockSpec((1, cb), lambda c, b: (0, c)),
                  pl.BlockSpec((k1v, 2 * n2, cb), lambda c, b: (0, 0, c), pipeline_mode=pl.Buffered(1))]
        + [const_spec(t) for t in tables],
        out_specs=pl.BlockSpec((None, None, L, cb), lambda c, b: (b, c, 0, 0)),
        out_shape=jax.ShapeDtypeStruct((B, nc, L, cb), BF16),
        scratch_shapes=[pltpu.VMEM((L, cb), F32), pltpu.VMEM((n2 * 2 * k1p, cb), F32)],
        compiler_params=_cparams(("parallel", "arbitrary"),
                                 [((L, cb), BF16)],
                                 [((L, cb), BF16)] * 3
                                 + [((L, cb), F32), ((n2 * 2 * k1p, cb), F32), ((k1v, 2 * n2, cb), F32)]
                                 + [(t.shape, BF16) for t in tables]),
        name="hyena_long_conv",
    )(z, z, z, skip.reshape(1, W), spec, *tables)


def _ret_fwd_kernel(lg_ref, q_ref, k_ref, v_ref, o_ref, s_ref, *, T, C, H, DK, DV):
    @pl.when(pl.program_id(1) == 0)
    def _():
        s_ref[...] = jnp.zeros_like(s_ref)

    row = lax.broadcasted_iota(jnp.int32, (C, C), 0)
    col = lax.broadcasted_iota(jnp.int32, (C, C), 1)
    rel = (row - col).astype(F32)
    ridx = lax.broadcasted_iota(jnp.int32, (C, 1), 0).astype(F32)
    consts = []
    for h in range(H):
        lgf = lg_ref[0, h]
        lgb = lg_ref[1, h]
        dmat = jnp.where(rel > 0.0, jnp.exp(lgf * jnp.maximum(rel, 0.0)),
                         jnp.where(rel < 0.0, jnp.exp(lgb * jnp.maximum(-rel, 0.0)), 2.0))
        consts.append((dmat, jnp.exp(lgf * (ridx + 1.0)), jnp.exp(lgf * (C - 1.0 - ridx)),
                       jnp.exp(jnp.full((1, 1), C, F32) * lgf)))

    def chunk(c, carry):
        rows = pl.ds(pl.multiple_of(c * C, C), C)
        for h in range(H):
            dmat, q_scale, k_scale, s_decay = consts[h]
            ck = slice(h * DK, (h + 1) * DK)
            cv = slice(h * DV, (h + 1) * DV)
            q = q_ref[rows, ck]
            k = k_ref[rows, ck]
            v = v_ref[rows, cv]
            s = lax.dot_general(q, k, NT_DIMS, preferred_element_type=F32) * dmat
            st = s_ref[h]
            o_ref[rows, cv] = (_dot(s.astype(BF16), v) + _dot((q.astype(F32) * q_scale).astype(BF16),
                                                              st.astype(BF16))).astype(o_ref.dtype)
            s_ref[h] = s_decay * st + lax.dot_general((k.astype(F32) * k_scale).astype(BF16), v, TN_DIMS,
                                                      preferred_element_type=F32)
        return carry

    lax.fori_loop(0, T // C, chunk, 0)


def _ret_bwd_kernel(lg_ref, q_ref, k_ref, v_ref, g_ref, o1_ref, o_ref, s_ref, *, T, C, H, DK, DV):
    @pl.when(pl.program_id(1) == 0)
    def _():
        s_ref[...] = jnp.zeros_like(s_ref)

    ridx = lax.broadcasted_iota(jnp.int32, (C, 1), 0).astype(F32)
    nc = T // C
    consts = []
    for h in range(H):
        lgb = lg_ref[1, h]
        consts.append((jnp.exp(lgb * (C - ridx)), jnp.exp(lgb * ridx), jnp.exp(jnp.full((1, 1), C, F32) * lgb)))

    def chunk(cc, carry):
        rows = pl.ds(pl.multiple_of((nc - 1 - cc) * C, C), C)
        for h in range(H):
            q_scale, k_scale, s_decay = consts[h]
            ck = slice(h * DK, (h + 1) * DK)
            cv = slice(h * DV, (h + 1) * DV)
            q = q_ref[rows, ck].astype(F32)
            k = k_ref[rows, ck].astype(F32)
            v = v_ref[rows, cv]
            st = s_ref[h]
            o = o1_ref[rows, cv].astype(F32) + _dot((q * q_scale).astype(BF16), st.astype(BF16))
            y = o * lax.rsqrt(jnp.mean(o * o, axis=-1, keepdims=True) + NORM_EPS)
            o_ref[rows, cv] = (y * _silu(g_ref[rows, cv].astype(F32))).astype(o_ref.dtype)
            s_ref[h] = s_decay * st + lax.dot_general((k * k_scale).astype(BF16), v, TN_DIMS,
                                                      preferred_element_type=F32)
        return carry

    lax.fori_loop(0, nc, chunk, 0)


def _retention(z, log_gamma):
    B, L, W = z.shape
    H, C = RET_HEADS, RET_CHUNK
    DK = W // (6 * H)
    DV = 2 * DK
    T = min(L, 512)
    nT = L // T
    kw = dict(T=T, C=C, H=H, DK=DK, DV=DV)
    smem = pl.BlockSpec(memory_space=pltpu.SMEM)
    blocks = [((T, H * DK), BF16)] * 2 + [((T, H * DV), BF16)] * 4
    state = [((H, DK, DV), F32), ((DK, DV), F32), ((DK, DV), F32)]
    o1 = pl.pallas_call(
        functools.partial(_ret_fwd_kernel, **kw),
        grid=(B, nT),
        in_specs=[smem,
                  pl.BlockSpec((None, T, H * DK), lambda b, i: (b, i, 0)),
                  pl.BlockSpec((None, T, H * DK), lambda b, i: (b, i, 1)),
                  pl.BlockSpec((None, T, H * DV), lambda b, i: (b, i, 1))],
        out_specs=pl.BlockSpec((None, T, H * DV), lambda b, i: (b, i, 0)),
        out_shape=jax.ShapeDtypeStruct((B, L, H * DV), BF16),
        scratch_shapes=[pltpu.VMEM((H, DK, DV), F32)],
        compiler_params=_cparams(("parallel", "arbitrary"), blocks, state),
        name="retention_fwd",
    )(log_gamma, z, z, z)
    return pl.pallas_call(
        functools.partial(_ret_bwd_kernel, **kw),
        grid=(B, nT),
        in_specs=[smem,
                  pl.BlockSpec((None, T, H * DK), lambda b, i: (b, nT - 1 - i, 0)),
                  pl.BlockSpec((None, T, H * DK), lambda b, i: (b, nT - 1 - i, 1)),
                  pl.BlockSpec((None, T, H * DV), lambda b, i: (b, nT - 1 - i, 1)),
                  pl.BlockSpec((None, T, H * DV), lambda b, i: (b, nT - 1 - i, 2)),
                  pl.BlockSpec((None, T, H * DV), lambda b, i: (b, nT - 1 - i, 0))],
        out_specs=pl.BlockSpec((None, T, H * DV), lambda b, i: (b, nT - 1 - i, 0)),
        out_shape=jax.ShapeDtypeStruct((B, L, H * DV), BF16),
        scratch_shapes=[pltpu.VMEM((H, DK, DV), F32)],
        compiler_params=_cparams(("parallel", "arbitrary"), blocks, state),
        name="retention_bwd",
    )(log_gamma, z, z, z, z, o1)


def _swa_kernel(sink_ref, q_ref, kp_ref, kc_ref, kn_ref, vp_ref, vc_ref, vn_ref, o_ref, *, HKV, G, BLK):
    i = pl.program_id(1)
    last = pl.num_programs(1) - 1
    R = G * BLK
    r = lax.broadcasted_iota(jnp.int32, (R, 3 * BLK), 0) % BLK
    c = lax.broadcasted_iota(jnp.int32, (R, 3 * BLK), 1)
    rel = r - (c - BLK)
    lo = jnp.where(i == 0, BLK, 0)
    hi = jnp.where(i == last, 2 * BLK, 3 * BLK)
    valid = (jnp.abs(rel) <= WINDOW) & (c >= lo) & (c < hi)
    grp = lax.broadcasted_iota(jnp.int32, (R, 1), 0) // BLK
    for j in range(HKV):
        sl = slice(j * LANES, (j + 1) * LANES)
        k = jnp.concatenate([kp_ref[:, sl], kc_ref[:, sl], kn_ref[:, sl]], axis=0)
        v = jnp.concatenate([vp_ref[:, sl], vc_ref[:, sl], vn_ref[:, sl]], axis=0)
        q = jnp.concatenate([q_ref[:, (j * G + g) * LANES:(j * G + g + 1) * LANES] for g in range(G)], axis=0)
        s = lax.dot_general(q, k, NT_DIMS, preferred_element_type=F32)
        s = jnp.where(valid, s, NEG_INF)
        sink = jnp.zeros((R, 1), F32)
        for g in range(G):
            sink = jnp.where(grp == g, sink_ref[0, j * G + g], sink)
        m = jnp.maximum(jnp.max(s, axis=-1, keepdims=True), sink)
        p = jnp.exp(s - m)
        denom = jnp.sum(p, axis=-1, keepdims=True) + jnp.exp(sink - m)
        o = _dot(p.astype(BF16), v) / denom
        for g in range(G):
            o_ref[:, (j * G + g) * LANES:(j * G + g + 1) * LANES] = o[g * BLK:(g + 1) * BLK].astype(o_ref.dtype)


def _swa_attention(z, sink):
    B, L, _ = z.shape
    BLK = ATTN_BLOCK
    nb = L // BLK
    G = SWA_HQ // SWA_HKV
    qw = SWA_HQ * LANES
    kvw = SWA_HKV * LANES
    kcol = qw // kvw
    prev = lambda b, i: (b, jnp.maximum(i - 1, 0), kcol)
    cur = lambda b, i: (b, i, kcol)
    nxt = lambda b, i: (b, jnp.minimum(i + 1, nb - 1), kcol)
    vprev = lambda b, i: (b, jnp.maximum(i - 1, 0), kcol + 1)
    vcur = lambda b, i: (b, i, kcol + 1)
    vnxt = lambda b, i: (b, jnp.minimum(i + 1, nb - 1), kcol + 1)
    kv = lambda f: pl.BlockSpec((None, BLK, kvw), f)
    return pl.pallas_call(
        functools.partial(_swa_kernel, HKV=SWA_HKV, G=G, BLK=BLK),
        grid=(B, nb),
        in_specs=[pl.BlockSpec(memory_space=pltpu.SMEM),
                  pl.BlockSpec((None, BLK, qw), lambda b, i: (b, i, 0)),
                  kv(prev), kv(cur), kv(nxt), kv(vprev), kv(vcur), kv(vnxt)],
        out_specs=pl.BlockSpec((None, BLK, qw), lambda b, i: (b, i, 0)),
        out_shape=jax.ShapeDtypeStruct((B, L, qw), BF16),
        compiler_params=_cparams(("parallel", "parallel"),
                                 [((BLK, qw), BF16)] * 2 + [((BLK, kvw), BF16)] * 6,
                                 [((G * BLK, 3 * BLK), F32)] * 4),
        name="swa_attention",
    )(sink.reshape(1, SWA_HQ), z, z, z, z, z, z, z)


def _swa_layout(w_in, q_gain, k_gain, w_out, L):
    D = w_in.shape[0]
    dh, hq, hkv = SWA_DH, SWA_HQ, SWA_HKV
    hf = dh // 2
    q_end, k_end = hq * dh, (hq + hkv) * dh

    def rot_slots(w, n):
        w = w.reshape(D, n, 2, hf)
        return jnp.pad(w, ((0, 0), (0, 0), (0, 0), (0, LANES // 2 - hf))).reshape(D, n * LANES)

    def val_slots(w, n):
        return jnp.pad(w.reshape(D, n, dh), ((0, 0), (0, 0), (0, LANES - dh))).reshape(D, n * LANES)

    w = jnp.concatenate([rot_slots(w_in[:, :q_end], hq), rot_slots(w_in[:, q_end:k_end], hkv),
                         val_slots(w_in[:, k_end:], hkv)], axis=1)

    def gain_slot(g):
        return jnp.pad(g.reshape(2, hf), ((0, 0), (0, LANES // 2 - hf))).reshape(1, LANES)

    flag = jnp.concatenate([jnp.ones(((hq + hkv) * LANES,), F32), jnp.zeros((hkv * LANES,), F32)]).reshape(1, -1)
    wo = jnp.pad(w_out.reshape(hq, dh, -1), ((0, 0), (0, LANES - dh), (0, 0))).reshape(hq * LANES, -1)

    inv = ROPE_THETA ** (-np.arange(0, dh, 2) / dh)
    ang = np.arange(L)[:, None] * inv[None, :]
    zero = np.zeros((L, LANES // 2 - hf))
    cos = jnp.asarray(np.concatenate([np.cos(ang), zero, np.cos(ang), zero], axis=1), F32)
    sin = jnp.asarray(np.concatenate([-np.sin(ang), zero, np.sin(ang), zero], axis=1), F32)
    gq = gain_slot(q_gain) * (dh ** -0.5)
    gk = gain_slot(k_gain)
    roll = lambda g: jnp.roll(g, LANES // 2, axis=1)
    cos3 = jnp.stack([cos * gq, cos * gk, jnp.ones_like(cos)])
    sin3 = jnp.stack([sin * roll(gq), sin * roll(gk), jnp.zeros_like(sin)])
    return w, flag, wo, cos3, sin3


def _hg_decays(f_ref, q_ref, rows, lb, tri, total_row, mid_row):
    f = lb + (1.0 - lb) * jax.nn.sigmoid(f_ref[rows, :].astype(F32))
    gl = jnp.log(f)
    cum = _cumdot(tri, gl)
    total = cum[total_row:total_row + 1, :]
    mid = cum[mid_row:mid_row + 1, :]
    q_dec = _silu(q_ref[rows, :].astype(F32)) * jnp.exp(cum - mid)
    k_inv = (1.0 - f) * jnp.exp(mid - cum)
    q_full = (q_dec * jnp.exp(mid)).astype(BF16)
    k_end = (k_inv * jnp.exp(total - mid)).astype(BF16)
    return q_dec.astype(BF16), k_inv.astype(BF16), q_full, k_end, jnp.exp(total)


def _hg_fwd_kernel(lb_ref, q_ref, i_ref, f_ref, o_ref, st_ref, *, T, C, H, DK, DV):
    @pl.when(pl.program_id(1) == 0)
    def _():
        st_ref[...] = jnp.zeros_like(st_ref)

    row = lax.broadcasted_iota(jnp.int32, (C, C), 0)
    col = lax.broadcasted_iota(jnp.int32, (C, C), 1)
    causal = row >= col
    tri = jnp.concatenate([causal.astype(BF16)] * 2, axis=1)
    lb = lb_ref[...]

    def chunk(c, carry):
        rows = pl.ds(pl.multiple_of(c * C, C), C)
        q_dec, k_inv, q_full, k_end, s_decay = _hg_decays(f_ref, q_ref, rows, lb, tri, C - 1, C // 2 - 1)
        v = i_ref[rows, :]
        for h in range(H):
            ck = slice(h * DK, (h + 1) * DK)
            cv = slice(h * DV, (h + 1) * DV)
            s = lax.dot_general(q_dec[:, ck], k_inv[:, ck], NT_DIMS, preferred_element_type=F32)
            s = jnp.where(causal, s, 0.0)
            st = st_ref[h]
            o_ref[rows, cv] = _dot(s.astype(BF16), v[:, cv]) + lax.dot_general(
                q_full[:, ck], st.astype(BF16), NT_DIMS, preferred_element_type=F32)
            st_ref[h] = st * s_decay[:, ck] + lax.dot_general(v[:, cv], k_end[:, ck], TN_DIMS,
                                                              preferred_element_type=F32)
        return carry

    lax.fori_loop(0, T // C, chunk, 0)


def _hg_bwd_kernel(lb_ref, gain_ref, q_ref, i_ref, f_ref, gate_ref, o1_ref, o_ref, st_ref, *, T, C, H, DK, DV):
    @pl.when(pl.program_id(1) == 0)
    def _():
        st_ref[...] = jnp.zeros_like(st_ref)

    row = lax.broadcasted_iota(jnp.int32, (C, C), 0)
    col = lax.broadcasted_iota(jnp.int32, (C, C), 1)
    anti = col >= row
    tri = jnp.concatenate([anti.astype(BF16)] * 2, axis=1)
    lb = lb_ref[...]
    gain = gain_ref[...]
    nc = T // C

    def chunk(cc, carry):
        rows = pl.ds(pl.multiple_of((nc - 1 - cc) * C, C), C)
        q_dec, k_inv, q_full, k_end, s_decay = _hg_decays(f_ref, q_ref, rows, lb, tri, 0, C // 2)
        v = i_ref[rows, :]
        for h in range(H):
            ck = slice(h * DK, (h + 1) * DK)
            cv = slice(h * DV, (h + 1) * DV)
            s = lax.dot_general(q_dec[:, ck], k_inv[:, ck], NT_DIMS, preferred_element_type=F32)
            s = jnp.where(anti, s, 0.0)
            st = st_ref[h]
            o = o1_ref[rows, cv] + _dot(s.astype(BF16), v[:, cv]) + lax.dot_general(
                q_full[:, ck], st.astype(BF16), NT_DIMS, preferred_element_type=F32)
            y = o * lax.rsqrt(jnp.mean(o * o, axis=-1, keepdims=True) + NORM_EPS) * gain
            o_ref[rows, cv] = (y * _silu(gate_ref[rows, cv].astype(F32))).astype(o_ref.dtype)
            st_ref[h] = st * s_decay[:, ck] + lax.dot_general(v[:, cv], k_end[:, ck], TN_DIMS,
                                                              preferred_element_type=F32)
        return carry

    lax.fori_loop(0, nc, chunk, 0)


def _hgrn(z, lb, gain):
    B, L, _ = z.shape
    H, C = HG_HEADS, HG_CHUNK
    HD = lb.shape[1]
    DK = HD // H
    DV = gain.shape[0]
    T = min(L, 512)
    nT = L // T
    kw = dict(T=T, C=C, H=H, DK=DK, DV=DV)
    blocks = [((T, HD), BF16)] * 5 + [((T, H * DV), F32)] * 2
    state = [((H, DV, DK), F32)] + [((C, HD), F32)] * 12
    lb3 = lb.reshape(2, 1, HD)
    o1 = pl.pallas_call(
        functools.partial(_hg_fwd_kernel, **kw),
        grid=(B, nT),
        in_specs=[pl.BlockSpec((None, 1, HD), lambda b, i: (0, 0, 0)),
                  pl.BlockSpec((None, T, HD), lambda b, i: (b, i, 0)),
                  pl.BlockSpec((None, T, H * DV), lambda b, i: (b, i, 1)),
                  pl.BlockSpec((None, T, HD), lambda b, i: (b, i, 2))],
        out_specs=pl.BlockSpec((None, T, H * DV), lambda b, i: (b, i, 0)),
        out_shape=jax.ShapeDtypeStruct((B, L, H * DV), F32),
        scratch_shapes=[pltpu.VMEM((H, DV, DK), F32)],
        compiler_params=_cparams(("parallel", "arbitrary"), blocks, state),
        name="hgrn_fwd",
    )(lb3, z, z, z)
    return pl.pallas_call(
        functools.partial(_hg_bwd_kernel, **kw),
        grid=(B, nT),
        in_specs=[pl.BlockSpec((None, 1, HD), lambda b, i: (1, 0, 0)),
                  pl.BlockSpec((1, DV), lambda b, i: (0, 0)),
                  pl.BlockSpec((None, T, HD), lambda b, i: (b, nT - 1 - i, 0)),
                  pl.BlockSpec((None, T, H * DV), lambda b, i: (b, nT - 1 - i, 1)),
                  pl.BlockSpec((None, T, HD), lambda b, i: (b, nT - 1 - i, 3)),
                  pl.BlockSpec((None, T, H * DV), lambda b, i: (b, nT - 1 - i, 4)),
                  pl.BlockSpec((None, T, H * DV), lambda b, i: (b, nT - 1 - i, 0))],
        out_specs=pl.BlockSpec((None, T, H * DV), lambda b, i: (b, nT - 1 - i, 0)),
        out_shape=jax.ShapeDtypeStruct((B, L, H * DV), BF16),
        scratch_shapes=[pltpu.VMEM((H, DV, DK), F32)],
        compiler_params=_cparams(("parallel", "arbitrary"), blocks, state),
        name="hgrn_bwd",
    )(lb3, gain.reshape(1, DV), z, z, z, z, o1)


def _ret_rope_tables(L, dk):
    inv = ROPE_THETA ** (-np.arange(0, dk, 2) / dk)
    ang = np.arange(L)[:, None] * inv[None, :]
    cos = np.stack([np.cos(ang), np.cos(ang) * dk ** -0.5, np.ones_like(ang)])
    sin = np.stack([np.sin(ang), np.sin(ang) * dk ** -0.5, np.zeros_like(ang)])
    return jnp.asarray(cos, F32), jnp.asarray(sin, F32)


def kernel(x_prompt, x_sample, c_prompt, c_sample, ada_w, ada_b, norm_g, hy_w_in, hy_conv_w, hy_conv_b, hy_w1, hy_b1, hy_w2, hy_b2, hy_w3, hy_freq, hy_decay, hy_skip, hy_w_out, ret_w_in, ret_decay, ret_w_out, swa_w_in, swa_q_gain, swa_k_gain, swa_sink, swa_w_out, hg_w_in, hg_lb, hg_gain, hg_w_out, ffn_w_gate, ffn_w_val, ffn_conv_w, ffn_conv_b, ffn_w_down):
    depth, D = norm_g.shape[0], norm_g.shape[2]
    groups = [(x_prompt, c_prompt), (x_sample, c_sample)]
    mods = _ada_mod(jnp.concatenate([c for _, c in groups], axis=0), ada_w, ada_b)

    bf = lambda w: w.astype(BF16)
    hy_w_in_b, hy_w_out_b = bf(hy_w_in), bf(hy_w_out)
    ret_w_in_b, ret_w_out_b = bf(ret_w_in), bf(ret_w_out)
    hg_w_in_b, hg_w_out_b = bf(hg_w_in), bf(hg_w_out)
    wg_b, wv_b, wd_b = bf(ffn_w_gate), bf(ffn_w_val), bf(ffn_w_down)
    hg_sm = jax.nn.softmax(hg_lb.astype(F32), axis=1)
    hg_lower = jnp.cumsum(hg_sm, axis=1) - hg_sm
    ret_log_gamma = -jnp.exp(ret_decay.astype(F32))

    outs = []
    row0 = 0
    for x, c in groups:
        B, L, _ = x.shape
        for layer in range(depth):
            kind, j = layer % N_MIXERS, layer // N_MIXERS
            mod = mods[layer, row0:row0 + B].reshape(B, N_MOD, 1, D)
            sh1, sc1, g1, sh2, sc2, g2 = (mod[:, m] for m in range(N_MOD))
            gn1 = norm_g[layer, 0].reshape(1, D)
            gn2 = norm_g[layer, 1].reshape(1, D)
            if kind == 0:
                tabs = _fft_tables(L)
                z = _proj_in_conv(x, gn1, sc1, sh1, hy_w_in_b[j], hy_conv_w[j], hy_conv_b[j], tn=1024)
                taps, asum = _hy_filter_taps(L, hy_w1[j], hy_b1[j], hy_w2[j], hy_b2[j], hy_w3[j], hy_freq[j],
                                             hy_decay[j])
                spec = _hy_spectrum(taps, asum, tabs)
                a = _hy_conv(z, hy_skip[j], spec, tabs)
                x = _proj_out(a, hy_w_out_b[j], x, g1)
            elif kind == 1:
                dk = D // RET_HEADS
                cos, sin = _ret_rope_tables(L, dk)
                z = _proj_in_halfrope(x, gn1, sc1, sh1, ret_w_in_b[j], cos, sin, tn=RET_HEADS * dk, rope_tiles=2,
                                      dk=dk)
                a = _retention(z, ret_log_gamma[j])
                x = _proj_out(a, ret_w_out_b[j], x, g1)
            elif kind == 2:
                w, flag, wo, cos, sin = _swa_layout(swa_w_in[j], swa_q_gain[j], swa_k_gain[j], swa_w_out[j], L)
                z = _proj_in_rope(x, gn1, sc1, sh1, bf(w), flag, cos, sin, tn=SWA_HKV * LANES, dh=SWA_DH,
                                  q_tiles=SWA_HQ // SWA_HKV)
                a = _swa_attention(z, swa_sink[j])
                x = _proj_out(a, bf(wo), x, g1)
            else:
                z = _proj_in(x, gn1, sc1, sh1, hg_w_in_b[j], tn=1024)
                a = _hgrn(z, hg_lower[:, layer], hg_gain[j])
                x = _proj_out(a, hg_w_out_b[j], x, g1)
            x = _ffn(x, gn2, sc2, sh2, g2, wg_b[layer], wv_b[layer], ffn_conv_w[layer], ffn_conv_b[layer],
                     wd_b[layer])
        outs.append(x)
        row0 += B
    return tuple(outs)
```

```python
import functools
import math

import numpy as np
import jax
import jax.numpy as jnp
from jax import lax
from jax.experimental import pallas as pl
from jax.experimental.pallas import tpu as pltpu

F32 = jnp.float32
BF16 = jnp.bfloat16

NORM_EPS = 1e-6
N_MIXERS = 4
N_MOD = 6
HY_BANDS = 16
RET_HEADS = 4
RET_CHUNK = 128
SWA_HQ = 16
SWA_HKV = 4
SWA_DH = 64
WINDOW = 128
ATTN_BLOCK = 128
ROPE_THETA = 10000.0
NEG_INF = -1e30
HG_HEADS = 8
HG_CHUNK = 128

LANES = 128
SUBLANES = 8
BF16_ROWS = 16
VMEM_LIMIT_CAP = 60 * 1024 * 1024
VMEM_SLACK = 8 * 1024 * 1024

NT_DIMS = (((1,), (1,)), ((), ()))
TN_DIMS = (((0,), (0,)), ((), ()))


def _nbytes(shape, dtype):
    item = jnp.dtype(dtype).itemsize
    sub = SUBLANES * 4 // item
    dims = list(shape)
    dims[-1] = -(-dims[-1] // LANES) * LANES
    if len(dims) > 1:
        dims[-2] = -(-dims[-2] // sub) * sub
    return int(np.prod(dims)) * item


def _cparams(semantics, pipelined, resident):
    need = 2 * sum(_nbytes(s, d) for s, d in pipelined) + sum(_nbytes(s, d) for s, d in resident)
    return pltpu.CompilerParams(dimension_semantics=semantics,
                                vmem_limit_bytes=min(need + VMEM_SLACK, VMEM_LIMIT_CAP))


def _dot(a, b):
    return jnp.dot(a, b, preferred_element_type=F32)


def _split(x):
    hi = x.astype(BF16)
    lo = (x - hi.astype(F32)).astype(BF16)
    return hi, lo


def _dot_tab(t_cat, d):
    d_hi, d_lo = _split(d)
    return _dot(t_cat, jnp.concatenate([d_hi, d_hi, d_lo], axis=0))


def _dot3(a, b):
    a_hi, a_lo = _split(a)
    b_hi, b_lo = _split(b)
    return _dot(a_hi, b_hi) + _dot(a_lo, b_hi) + _dot(a_hi, b_lo)


def _cumdot(tri2, g):
    g_hi, g_lo = _split(g)
    return _dot(tri2, jnp.concatenate([g_hi, g_lo], axis=0))


def _silu(x):
    return x * jax.nn.sigmoid(x)


def _norm_mod(x, g, sc, sh):
    xf = x.astype(F32)
    y = xf * lax.rsqrt(jnp.mean(xf * xf, axis=-1, keepdims=True) + NORM_EPS)
    return (y * g) * (1.0 + sc) + sh


def _row_tile(L):
    return min(L, 1024)


def _ada_kernel(c_ref, w_ref, b_ref, o_ref):
    cs = _silu(c_ref[...]).astype(BF16)
    o_ref[...] = _dot(cs, w_ref[...].astype(BF16)) + b_ref[...]


def _ada_mod(c_all, ada_w, ada_b):
    depth, D, N = ada_w.shape
    R = c_all.shape[0]
    tn = 1024
    return pl.pallas_call(
        _ada_kernel,
        grid=(depth, N // tn),
        in_specs=[pl.BlockSpec((R, D), lambda l, j: (0, 0)),
                  pl.BlockSpec((None, D, tn), lambda l, j: (l, 0, j)),
                  pl.BlockSpec((None, 1, tn), lambda l, j: (l, 0, j))],
        out_specs=pl.BlockSpec((None, R, tn), lambda l, j: (l, 0, j)),
        out_shape=jax.ShapeDtypeStruct((depth, R, N), F32),
        compiler_params=_cparams(("parallel", "parallel"),
                                 [((D, tn), F32), ((R, D), F32), ((R, tn), F32)], []),
        name="ada_mod",
    )(c_all, ada_w, ada_b.reshape(depth, 1, N))


def _fill_h(h_ref, x_ref, g_ref, sc_ref, sh_ref):
    h_ref[...] = _norm_mod(x_ref[...], g_ref[...], sc_ref[...], sh_ref[...]).astype(BF16)


def _fill_h_halo(h_ref, x_ref, xp_ref, xn_ref, g_ref, sc_ref, sh_ref, tm):
    i = pl.program_id(1)
    last = pl.num_programs(1) - 1
    g, sc, sh = g_ref[...], sc_ref[...], sh_ref[...]
    h_ref[BF16_ROWS:BF16_ROWS + tm, :] = _norm_mod(x_ref[...], g, sc, sh).astype(BF16)
    hp = _norm_mod(xp_ref[...], g, sc, sh)
    hn = _norm_mod(xn_ref[...], g, sc, sh)
    h_ref[0:BF16_ROWS, :] = jnp.where(i > 0, hp, 0.0).astype(BF16)
    h_ref[BF16_ROWS + tm:, :] = jnp.where(i < last, hn, 0.0).astype(BF16)


def _conv3_rows(z_ref, cw, cb, tm):
    o = BF16_ROWS
    return (z_ref[o - 1:o - 1 + tm, :] * cw[0:1, :] + z_ref[o:o + tm, :] * cw[1:2, :]
            + z_ref[o + 1:o + 1 + tm, :] * cw[2:3, :] + cb)


def _proj_plain_kernel(x_ref, g_ref, sc_ref, sh_ref, w_ref, o_ref, h_ref):
    @pl.when(pl.program_id(2) == 0)
    def _():
        _fill_h(h_ref, x_ref, g_ref, sc_ref, sh_ref)

    o_ref[...] = _dot(h_ref[...], w_ref[...]).astype(o_ref.dtype)


def _proj_halfrope_kernel(x_ref, g_ref, sc_ref, sh_ref, w_ref, cos_ref, sin_ref, o_ref, h_ref, *, dk):
    @pl.when(pl.program_id(2) == 0)
    def _():
        _fill_h(h_ref, x_ref, g_ref, sc_ref, sh_ref)

    z = _dot(h_ref[...], w_ref[...])
    cos, sin = cos_ref[...], sin_ref[...]
    half = dk // 2
    for c0 in range(0, z.shape[1], dk):
        x1, x2 = z[:, c0:c0 + half], z[:, c0 + half:c0 + dk]
        o_ref[:, c0:c0 + half] = (x1 * cos - x2 * sin).astype(o_ref.dtype)
        o_ref[:, c0 + half:c0 + dk] = (x1 * sin + x2 * cos).astype(o_ref.dtype)


def _proj_conv_kernel(x_ref, xp_ref, xn_ref, g_ref, sc_ref, sh_ref, w_ref, cw_ref, cb_ref,
                      o_ref, h_ref, z_ref, *, tm):
    @pl.when(pl.program_id(2) == 0)
    def _():
        _fill_h_halo(h_ref, x_ref, xp_ref, xn_ref, g_ref, sc_ref, sh_ref, tm)

    z_ref[...] = _dot(h_ref[...], w_ref[...])
    r = _conv3_rows(z_ref, cw_ref[...], cb_ref[...], tm).astype(o_ref.dtype)
    for c in range(o_ref.shape[0]):
        o_ref[c] = r[:, c * LANES:(c + 1) * LANES]


def _proj_rope_kernel(x_ref, g_ref, sc_ref, sh_ref, w_ref, flag_ref, ones_ref, cos_ref, sin_ref,
                      o_ref, h_ref, *, dh):
    @pl.when(pl.program_id(2) == 0)
    def _():
        _fill_h(h_ref, x_ref, g_ref, sc_ref, sh_ref)

    z = _dot(h_ref[...], w_ref[...])
    ms = _dot((z * z).astype(BF16), ones_ref[...]) * (1.0 / dh)
    zn = z * jnp.where(flag_ref[...] > 0.0, lax.rsqrt(ms + NORM_EPS), 1.0)
    cos, sin = cos_ref[...], sin_ref[...]
    for s in range(z.shape[1] // LANES):
        sl = slice(s * LANES, (s + 1) * LANES)
        zs = zn[:, sl]
        o_ref[:, sl] = (zs * cos + pltpu.roll(zs, LANES // 2, axis=1) * sin).astype(o_ref.dtype)


def _mod_specs(D):
    return [pl.BlockSpec((1, D), lambda b, i, j: (0, 0)),
            pl.BlockSpec((None, 1, D), lambda b, i, j: (b, 0, 0)),
            pl.BlockSpec((None, 1, D), lambda b, i, j: (b, 0, 0))]


def _halo_specs(tm, D, L):
    hb = tm // BF16_ROWS
    nhb = L // BF16_ROWS
    return [pl.BlockSpec((None, tm, D), lambda b, i, j: (b, i, 0)),
            pl.BlockSpec((None, BF16_ROWS, D), lambda b, i, j: (b, jnp.maximum(i * hb - 1, 0), 0)),
            pl.BlockSpec((None, BF16_ROWS, D), lambda b, i, j: (b, jnp.minimum((i + 1) * hb, nhb - 1), 0))]


def _proj_in(x, g, sc, sh, w, *, tn):
    B, L, D = x.shape
    N = w.shape[1]
    tm = _row_tile(L)
    return pl.pallas_call(
        _proj_plain_kernel,
        grid=(B, L // tm, N // tn),
        in_specs=[pl.BlockSpec((None, tm, D), lambda b, i, j: (b, i, 0))] + _mod_specs(D)
        + [pl.BlockSpec((D, tn), lambda b, i, j: (0, j))],
        out_specs=pl.BlockSpec((None, tm, tn), lambda b, i, j: (b, i, j)),
        out_shape=jax.ShapeDtypeStruct((B, L, N), BF16),
        scratch_shapes=[pltpu.VMEM((tm, D), BF16)],
        compiler_params=_cparams(("parallel", "parallel", "arbitrary"),
                                 [((tm, D), F32), ((D, tn), BF16), ((tm, tn), BF16)],
                                 [((tm, D), BF16), ((tm, tn), F32)]),
        name="proj_in",
    )(x, g, sc, sh, w)


def _proj_in_halfrope(x, g, sc, sh, w, cos, sin, *, tn, rope_tiles, dk):
    B, L, D = x.shape
    N = w.shape[1]
    tm = _row_tile(L)
    half = dk // 2
    table = lambda b, i, j: (jnp.minimum(j, rope_tiles), i, 0)
    return pl.pallas_call(
        functools.partial(_proj_halfrope_kernel, dk=dk),
        grid=(B, L // tm, N // tn),
        in_specs=[pl.BlockSpec((None, tm, D), lambda b, i, j: (b, i, 0))] + _mod_specs(D)
        + [pl.BlockSpec((D, tn), lambda b, i, j: (0, j)),
           pl.BlockSpec((None, tm, half), table),
           pl.BlockSpec((None, tm, half), table)],
        out_specs=pl.BlockSpec((None, tm, tn), lambda b, i, j: (b, i, j)),
        out_shape=jax.ShapeDtypeStruct((B, L, N), BF16),
        scratch_shapes=[pltpu.VMEM((tm, D), BF16)],
        compiler_params=_cparams(("parallel", "parallel", "arbitrary"),
                                 [((tm, D), F32), ((D, tn), BF16), ((tm, tn), BF16), ((tm, half), F32),
                                  ((tm, half), F32)],
                                 [((tm, D), BF16), ((tm, tn), F32), ((tm, tn), F32)]),
        name="proj_in_halfrope",
    )(x, g, sc, sh, w, cos, sin)


def _proj_in_conv(x, g, sc, sh, w, cw, cb, *, tn):
    B, L, D = x.shape
    N = w.shape[1]
    tm = _row_tile(L)
    te = tm + 2 * BF16_ROWS
    return pl.pallas_call(
        functools.partial(_proj_conv_kernel, tm=tm),
        grid=(B, L // tm, N // tn),
        in_specs=_halo_specs(tm, D, L) + _mod_specs(D)
        + [pl.BlockSpec((D, tn), lambda b, i, j: (0, j)),
           pl.BlockSpec((3, tn), lambda b, i, j: (0, j)),
           pl.BlockSpec((1, tn), lambda b, i, j: (0, j))],
        out_specs=pl.BlockSpec((None, tn // LANES, tm, LANES), lambda b, i, j: (b, j, i, 0)),
        out_shape=jax.ShapeDtypeStruct((B, N // LANES, L, LANES), BF16),
        scratch_shapes=[pltpu.VMEM((te, D), BF16), pltpu.VMEM((te, tn), F32)],
        compiler_params=_cparams(("parallel", "parallel", "arbitrary"),
                                 [((tm, D), F32), ((D, tn), BF16), ((tm, tn), BF16)],
                                 [((te, D), BF16), ((te, tn), F32), ((te, tn), F32)]),
        name="proj_in_conv",
    )(x, x, x, g, sc, sh, w, cw, cb.reshape(1, N))


def _proj_in_rope(x, g, sc, sh, w, flag, cos, sin, *, tn, dh, q_tiles):
    B, L, D = x.shape
    N = w.shape[1]
    tm = _row_tile(L)
    slot = lax.broadcasted_iota(jnp.int32, (tn, tn), 0) // LANES == lax.broadcasted_iota(jnp.int32, (tn, tn), 1) // LANES
    table = lambda b, i, j: (jnp.maximum(j - (q_tiles - 1), 0), i, 0)
    return pl.pallas_call(
        functools.partial(_proj_rope_kernel, dh=dh),
        grid=(B, L // tm, N // tn),
        in_specs=[pl.BlockSpec((None, tm, D), lambda b, i, j: (b, i, 0))] + _mod_specs(D)
        + [pl.BlockSpec((D, tn), lambda b, i, j: (0, j)),
           pl.BlockSpec((1, tn), lambda b, i, j: (0, j)),
           pl.BlockSpec((tn, tn), lambda b, i, j: (0, 0)),
           pl.BlockSpec((None, tm, LANES), table),
           pl.BlockSpec((None, tm, LANES), table)],
        out_specs=pl.BlockSpec((None, tm, tn), lambda b, i, j: (b, i, j)),
        out_shape=jax.ShapeDtypeStruct((B, L, N), BF16),
        scratch_shapes=[pltpu.VMEM((tm, D), BF16)],
        compiler_params=_cparams(("parallel", "parallel", "arbitrary"),
                                 [((tm, D), F32), ((D, tn), BF16), ((tm, tn), BF16), ((tm, LANES), F32),
                                  ((tm, LANES), F32), ((tn, tn), BF16)],
                                 [((tm, D), BF16), ((tm, tn), F32), ((tm, tn), F32), ((tm, tn), F32)]),
        name="proj_in_rope",
    )(x, g, sc, sh, w, flag, slot.astype(BF16), cos, sin)


def _proj_out_kernel(a_ref, w_ref, x_ref, gate_ref, o_ref):
    o_ref[...] = x_ref[...] + gate_ref[...] * _dot(a_ref[...], w_ref[...])


def _proj_out_gated_kernel(a_ref, m_ref, w_ref, x_ref, gate_ref, o_ref):
    a = jnp.concatenate([(a_ref[c].astype(F32) * m_ref[c].astype(F32)).astype(BF16)
                         for c in range(a_ref.shape[0])], axis=1)
    o_ref[...] = x_ref[...] + gate_ref[...] * _dot(a, w_ref[...])


def _proj_out(a, w, x, gate, mult=None):
    blocked = a.ndim == 4
    B, L = x.shape[:2]
    K, D = w.shape
    tm = _row_tile(L)
    if blocked:
        a_specs = [pl.BlockSpec((None, K // LANES, tm, LANES), lambda b, i: (b, 0, i, 0))] * 2
        operands = (a, mult)
    else:
        a_specs = [pl.BlockSpec((None, tm, K), lambda b, i: (b, i, 0))]
        operands = (a,)
    return pl.pallas_call(
        _proj_out_gated_kernel if blocked else _proj_out_kernel,
        grid=(B, L // tm),
        in_specs=a_specs + [
                  pl.BlockSpec((K, D), lambda b, i: (0, 0)),
                  pl.BlockSpec((None, tm, D), lambda b, i: (b, i, 0)),
                  pl.BlockSpec((None, 1, D), lambda b, i: (b, 0, 0))],
        out_specs=pl.BlockSpec((None, tm, D), lambda b, i: (b, i, 0)),
        out_shape=jax.ShapeDtypeStruct((B, L, D), F32),
        compiler_params=_cparams(("parallel", "parallel"),
                                 [((tm, K), BF16)] * len(operands)
                                 + [((K, D), BF16), ((tm, D), F32), ((tm, D), F32)],
                                 [((tm, D), F32), ((tm, K), F32)]),
        name="proj_out",
    )(*operands, w, x, gate)


def _ffn_kernel(x_ref, xp_ref, xn_ref, g_ref, sc_ref, sh_ref, gate_ref, wg_ref, wv_ref, cw_ref, cb_ref,
                wd_ref, o_ref, h_ref, z_ref, *, tm, tf):
    _fill_h_halo(h_ref, x_ref, xp_ref, xn_ref, g_ref, sc_ref, sh_ref, tm)
    pieces = [slice(c * tf, (c + 1) * tf) for c in range(wg_ref.shape[1] // tf)]
    vals = []
    for c, cols in enumerate(pieces):
        z_ref[c] = _dot(h_ref[...], wg_ref[:, cols])
        vals.append(_dot(h_ref[BF16_ROWS:BF16_ROWS + tm, :], wv_ref[:, cols]))
    acc = None
    for c, cols in enumerate(pieces):
        a = _conv3_rows(z_ref.at[c], cw_ref[:, cols], cb_ref[:, cols], tm)
        part = _dot((_silu(a) * vals[c]).astype(BF16), wd_ref[cols, :])
        acc = part if acc is None else acc + part
    o_ref[...] = x_ref[...] + gate_ref[...] * acc


def _ffn(x, g, sc, sh, gate, wg, wv, cw, cb, wd):
    B, L, D = x.shape
    F = wg.shape[1]
    tm = min(L, 512)
    te = tm + 2 * BF16_ROWS
    tf = F // 2
    const = lambda b, i, j: (0, 0)
    resident = lambda shape: pl.BlockSpec(shape, const, pipeline_mode=pl.Buffered(1))
    return pl.pallas_call(
        functools.partial(_ffn_kernel, tm=tm, tf=tf),
        grid=(B, L // tm, 1),
        in_specs=_halo_specs(tm, D, L) + _mod_specs(D)
        + [pl.BlockSpec((None, 1, D), lambda b, i, j: (b, 0, 0)),
           resident((D, F)), resident((D, F)), resident((3, F)), resident((1, F)), resident((F, D))],
        out_specs=pl.BlockSpec((None, tm, D), lambda b, i, j: (b, i, 0)),
        out_shape=jax.ShapeDtypeStruct((B, L, D), F32),
        scratch_shapes=[pltpu.VMEM((te, D), BF16), pltpu.VMEM((F // tf, te, tf), F32)],
        compiler_params=_cparams(("parallel", "parallel", "arbitrary"),
                                 [((tm, D), F32), ((tm, D), F32)],
                                 [((D, F), BF16), ((D, F), BF16), ((F, D), BF16), ((te, D), BF16),
                                  ((te, F), F32), ((tm, F), F32), ((tm, tf), F32), ((tm, tf), BF16),
                                  ((tm, D), F32), ((tm, D), F32)]),
        name="ffn",
    )(x, x, x, g, sc, sh, gate, wg, wv, cw, cb.reshape(1, F), wd)


def _fft_dims(L):
    N = 2 * L
    p = N.bit_length() - 1
    n1 = 1 << ((p + 1) // 2)
    return N, n1, N // n1


def _k1_pad(n1):
    return n1 // 2 + 8


def _fft_tables(L):
    N, n1, n2 = _fft_dims(L)
    k1p = _k1_pad(n1)
    k1 = np.arange(k1p)
    valid = (k1 <= n1 // 2).astype(np.float64)
    weight = np.where((k1 == 0) | (k1 == n1 // 2), 1.0, 2.0) * valid
    m1 = np.arange(n1 // 2)
    n2v = np.arange(n2)
    n_idx = n2 * m1[None, :] + n2v[:, None]
    phi = 2.0 * np.pi * k1[None, :, None] * n_idx[:, None, :] / N
    s1 = np.concatenate([np.cos(phi) * valid[None, :, None], -np.sin(phi) * valid[None, :, None]], axis=1)
    phit = np.transpose(phi, (0, 2, 1))
    gl = np.concatenate([np.cos(phit) * weight[None, None, :], -np.sin(phit) * weight[None, None, :]],
                        axis=2) / N
    th = 2.0 * np.pi * np.outer(n2v, n2v) / n2
    c, s = np.cos(th), np.sin(th)
    f2 = np.block([[c, s], [-s, c]])
    f2c = np.block([[c, -s], [s, c]])

    def cat(a):
        a32 = jnp.asarray(a, F32)
        hi = a32.astype(BF16)
        lo = (a32 - hi.astype(F32)).astype(BF16)
        return jnp.concatenate([hi, lo, hi], axis=-1)

    return dict(N=N, n1=n1, n2=n2, k1p=k1p, s1=cat(s1), gl=cat(gl), f2=cat(f2), f2c=cat(f2c))


def _hy_positions(L):
    idx = np.concatenate([np.arange(L), np.array([0]), np.arange(L - 1, 0, -1)])
    t = idx / (L - 1)
    ang = (2.0 * np.pi / L) * idx
    bands = np.linspace(1e-4, HY_BANDS - 1, HY_BANDS)
    z = np.concatenate([t[:, None], np.cos(bands[None, :] * ang[:, None]), -np.sin(bands[None, :] * ang[:, None])],
                       axis=1)
    zp = np.zeros((2 * L, LANES))
    zp[:, : z.shape[1]] = z
    return jnp.asarray(zp, F32)


def _hy_mlp_kernel(z_ref, w1_ref, b1_ref, w2_ref, b2_ref, w3_ref, fr_ref, dec_ref, taps_ref, asum_ref, *, L, tr):
    i = pl.program_id(0)
    z = z_ref[...]
    fr = fr_ref[...]
    h = jnp.sin(fr * (_dot3(z, w1_ref[...]) + b1_ref[...]))
    h = jnp.sin(fr * (_dot3(h, w2_ref[...]) + b2_ref[...]))
    h = _dot3(h, w3_ref[...])
    t = z[:, 0:1]
    taps = h * jnp.exp(-t * jnp.abs(dec_ref[...]))
    row = i * tr + lax.broadcasted_iota(jnp.int32, (tr, 1), 0)
    taps = jnp.where(row == L, 0.0, taps)
    taps_ref[...] = taps

    @pl.when(i == 0)
    def _():
        asum_ref[...] = jnp.zeros_like(asum_ref)

    asum_ref[...] += jnp.sum(jnp.abs(taps), axis=0, keepdims=True)


def _hy_filter_taps(L, w1, b1, w2, b2, w3, freq, decay):
    W = decay.shape[1]
    E, O = w1.shape
    tr = 512
    half = L // tr
    z = _hy_positions(L)
    w1p = jnp.zeros((LANES, O), F32).at[:E].set(w1)
    return pl.pallas_call(
        functools.partial(_hy_mlp_kernel, L=L, tr=tr),
        grid=(2 * L // tr,),
        in_specs=[pl.BlockSpec((tr, LANES), lambda i: (i, 0)),
                  pl.BlockSpec((LANES, O), lambda i: (0, 0)),
                  pl.BlockSpec((1, O), lambda i: (0, 0)),
                  pl.BlockSpec((O, O), lambda i: (0, 0)),
                  pl.BlockSpec((1, O), lambda i: (0, 0)),
                  pl.BlockSpec((O, W), lambda i: (0, i // half)),
                  pl.BlockSpec((1, O), lambda i: (0, 0)),
                  pl.BlockSpec((None, 1, W), lambda i: (i // half, 0, 0))],
        out_specs=[pl.BlockSpec((tr, W), lambda i: (i, 0)),
                   pl.BlockSpec((1, W), lambda i: (0, 0))],
        out_shape=[jax.ShapeDtypeStruct((2 * L, W), F32), jax.ShapeDtypeStruct((1, W), F32)],
        compiler_params=_cparams(("arbitrary",), [((tr, W), F32), ((O, W), F32), ((tr, LANES), F32)],
                                 [((tr, W), F32), ((tr, W), F32)]),
        name="hyena_filter_mlp",
    )(z, w1p, b1.reshape(1, O), w2, b2.reshape(1, O), w3, freq.reshape(1, O), decay.reshape(2, 1, W))


FFT_UNROLL = 16


def _fft_stage1(src_ref, src_off, a_ref, s1_ref, n1, n2, k1p, sign=None, src_off2=None):
    def body(j, carry):
        d = src_ref[pl.ds(src_off + j, n1 // 2, stride=n2), :]
        r = _dot_tab(s1_ref[j], d)
        if src_off2 is not None:
            d2 = src_ref[pl.ds(src_off2 + j, n1 // 2, stride=n2), :]
            r = r + sign * _dot_tab(s1_ref[j], d2)
        a_ref[pl.ds(pl.multiple_of(j * 2 * k1p, 8), 2 * k1p), :] = r
        return carry

    lax.fori_loop(0, n2, body, 0, unroll=FFT_UNROLL)


def _load_k1(a_ref, k, n2, k1p):
    br = a_ref[pl.ds(k, n2, stride=2 * k1p), :]
    bi = a_ref[pl.ds(k1p + k, n2, stride=2 * k1p), :]
    return jnp.concatenate([br, bi], axis=0)


def _hy_spec_kernel(taps_ref, asum_ref, s1_ref, f2_ref, h_ref, a_ref, *, L, n1, n2, k1p):
    r = lax.broadcasted_iota(jnp.int32, (2 * k1p, 1), 0)
    k1 = jnp.where(r < k1p, r, r - k1p)
    sign = (1 - 2 * (k1 & 1)).astype(F32)
    _fft_stage1(taps_ref, 0, a_ref, s1_ref, n1, n2, k1p, sign=sign, src_off2=L)
    inv = 1.0 / asum_ref[...]

    def body(k, carry):
        h_ref[k] = _dot_tab(f2_ref[...], _load_k1(a_ref, k, n2, k1p)) * inv
        return carry

    lax.fori_loop(0, n1 // 2, body, 0, unroll=2)
    body(n1 // 2, 0)


def _hy_spectrum(taps, asum, tabs):
    n1, n2, k1p = tabs["n1"], tabs["n2"], tabs["k1p"]
    N, W = taps.shape
    L = N // 2
    cb = LANES
    k1v = n1 // 2 + 1
    s1, f2 = tabs["s1"], tabs["f2"]
    return pl.pallas_call(
        functools.partial(_hy_spec_kernel, L=L, n1=n1, n2=n2, k1p=k1p),
        grid=(W // cb,),
        in_specs=[pl.BlockSpec((N, cb), lambda c: (0, c)),
                  pl.BlockSpec((1, cb), lambda c: (0, c)),
                  pl.BlockSpec(s1.shape, lambda c: (0, 0, 0), pipeline_mode=pl.Buffered(1)),
                  pl.BlockSpec(f2.shape, lambda c: (0, 0), pipeline_mode=pl.Buffered(1))],
        out_specs=pl.BlockSpec((k1v, 2 * n2, cb), lambda c: (0, 0, c)),
        out_shape=jax.ShapeDtypeStruct((k1v, 2 * n2, W), F32),
        scratch_shapes=[pltpu.VMEM((n2 * 2 * k1p, cb), F32)],
        compiler_params=_cparams(("parallel",),
                                 [((N, cb), F32), ((k1v, 2 * n2, cb), F32)],
                                 [((n2 * 2 * k1p, cb), F32), (s1.shape, BF16), (f2.shape, BF16)]),
        name="hyena_filter_spectrum",
    )(taps, asum, s1, f2)


def _hy_conv_kernel(x1_ref, v_ref, skip_ref, h_ref, s1_ref, f2_ref, f2c_ref, gl_ref,
                    o_ref, u_ref, a_ref, *, n1, n2, k1p):
    u_ref[...] = x1_ref[...].astype(F32) * v_ref[...].astype(F32)
    _fft_stage1(u_ref, 0, a_ref, s1_ref, n1, n2, k1p)

    cb = u_ref.shape[1]

    def mid(groups):
        loaded = [jnp.concatenate([_load_k1(a_ref, k, n2, k1p) for k in ks], axis=1) for ks in groups]
        results = []
        for ks, rhs in zip(groups, loaded):
            x = _dot_tab(f2_ref[...], rhs)
            xr, xi = x[:n2], x[n2:]
            hr = jnp.concatenate([h_ref[k, :n2, :] for k in ks], axis=1)
            hi = jnp.concatenate([h_ref[k, n2:, :] for k in ks], axis=1)
            y = jnp.concatenate([xr * hr - xi * hi, xr * hi + xi * hr], axis=0)
            results.append(_dot_tab(f2c_ref[...], y))
        for ks, c in zip(groups, results):
            for p, k in enumerate(ks):
                a_ref[pl.ds(k, n2, stride=2 * k1p), :] = c[:n2, p * cb:(p + 1) * cb]
                a_ref[pl.ds(k1p + k, n2, stride=2 * k1p), :] = c[n2:, p * cb:(p + 1) * cb]

    def mid_oct(ko, carry):
        k = 8 * ko
        mid(((k, k + 1, k + 2, k + 3), (k + 4, k + 5, k + 6, k + 7)))
        return carry

    lax.fori_loop(0, n1 // 16, mid_oct, 0)
    mid(((n1 // 2,),))
    skip = skip_ref[...]

    def last(j, carry):
        rhs = a_ref[pl.ds(pl.multiple_of(j * 2 * k1p, 8), 2 * k1p), :]
        y = _dot_tab(gl_ref[j], rhs)
        rows = pl.ds(j, n1 // 2, stride=n2)
        u_ref[rows, :] = y + u_ref[rows, :] * skip
        return carry

    lax.fori_loop(0, n2, last, 0, unroll=FFT_UNROLL)
    o_ref[...] = u_ref[...].astype(o_ref.dtype)


def _hy_conv(z, skip, spec, tabs):
    n1, n2, k1p = tabs["n1"], tabs["n2"], tabs["k1p"]
    B, nc3, L, cb = z.shape
    nc = nc3 // 3
    W = nc * cb
    k1v = n1 // 2 + 1
    tables = [tabs["s1"], tabs["f2"], tabs["f2c"], tabs["gl"]]

    def const_spec(t):
        return pl.BlockSpec(t.shape, (lambda c, b: (0, 0, 0)) if t.ndim == 3 else (lambda c, b: (0, 0)),
                            pipeline_mode=pl.Buffered(1))

    return pl.pallas_call(
        functools.partial(_hy_conv_kernel, n1=n1, n2=n2, k1p=k1p),
        grid=(nc, B),
        in_specs=[pl.BlockSpec((None, None, L, cb), lambda c, b: (b, nc + c, 0, 0)),
                  pl.BlockSpec((None, None, L, cb), lambda c, b: (b, 2 * nc + c, 0, 0)),
                  pl.BlockSpec((1, cb), lambda c, b: (0, c)),
                  pl.BlockSpec((k1v, 2 * n2, cb), lambda c, b: (0, 0, c), pipeline_mode=pl.Buffered(1))]
        + [const_spec(t) for t in tables],
        out_specs=pl.BlockSpec((None, None, L, cb), lambda c, b: (b, c, 0, 0)),
        out_shape=jax.ShapeDtypeStruct((B, nc, L, cb), BF16),
        scratch_shapes=[pltpu.VMEM((L, cb), F32), pltpu.VMEM((n2 * 2 * k1p, cb), F32)],
        compiler_params=_cparams(("parallel", "arbitrary"),
                                 [((L, cb), BF16)] * 3,
                                 [((L, cb), F32), ((n2 * 2 * k1p, cb), F32), ((k1v, 2 * n2, cb), F32)]
                                 + [(t.shape, BF16) for t in tables]),
        name="hyena_long_conv",
    )(z, z, skip.reshape(1, W), spec, *tables)


def _ret_fwd_kernel(lg_ref, q_ref, k_ref, v_ref, o_ref, s_ref, *, T, C, H, DK, DV):
    @pl.when(pl.program_id(1) == 0)
    def _():
        s_ref[...] = jnp.zeros_like(s_ref)

    row = lax.broadcasted_iota(jnp.int32, (C, C), 0)
    col = lax.broadcasted_iota(jnp.int32, (C, C), 1)
    rel = (row - col).astype(F32)
    ridx = lax.broadcasted_iota(jnp.int32, (C, 1), 0).astype(F32)
    consts = []
    for h in range(H):
        lgf = lg_ref[0, h]
        lgb = lg_ref[1, h]
        dmat = jnp.where(rel > 0.0, jnp.exp(lgf * jnp.maximum(rel, 0.0)),
                         jnp.where(rel < 0.0, jnp.exp(lgb * jnp.maximum(-rel, 0.0)), 2.0))
        consts.append((dmat, jnp.exp(lgf * (ridx + 1.0)), jnp.exp(lgf * (C - 1.0 - ridx)),
                       jnp.exp(jnp.full((1, 1), C, F32) * lgf)))

    def chunk(c, carry):
        rows = pl.ds(pl.multiple_of(c * C, C), C)
        for h in range(H):
            dmat, q_scale, k_scale, s_decay = consts[h]
            ck = slice(h * DK, (h + 1) * DK)
            cv = slice(h * DV, (h + 1) * DV)
            q = q_ref[rows, ck]
            k = k_ref[rows, ck]
            v = v_ref[rows, cv]
            s = lax.dot_general(q, k, NT_DIMS, preferred_element_type=F32) * dmat
            st = s_ref[h]
            o_ref[rows, cv] = (_dot(s.astype(BF16), v) + _dot((q.astype(F32) * q_scale).astype(BF16),
                                                              st.astype(BF16))).astype(o_ref.dtype)
            s_ref[h] = s_decay * st + lax.dot_general((k.astype(F32) * k_scale).astype(BF16), v, TN_DIMS,
                                                      preferred_element_type=F32)
        return carry

    lax.fori_loop(0, T // C, chunk, 0)


def _ret_bwd_kernel(lg_ref, q_ref, k_ref, v_ref, g_ref, o1_ref, o_ref, s_ref, *, T, C, H, DK, DV):
    @pl.when(pl.program_id(1) == 0)
    def _():
        s_ref[...] = jnp.zeros_like(s_ref)

    ridx = lax.broadcasted_iota(jnp.int32, (C, 1), 0).astype(F32)
    nc = T // C
    consts = []
    for h in range(H):
        lgb = lg_ref[1, h]
        consts.append((jnp.exp(lgb * (C - ridx)), jnp.exp(lgb * ridx), jnp.exp(jnp.full((1, 1), C, F32) * lgb)))

    def chunk(cc, carry):
        rows = pl.ds(pl.multiple_of((nc - 1 - cc) * C, C), C)
        for h in range(H):
            q_scale, k_scale, s_decay = consts[h]
            ck = slice(h * DK, (h + 1) * DK)
            cv = slice(h * DV, (h + 1) * DV)
            q = q_ref[rows, ck].astype(F32)
            k = k_ref[rows, ck].astype(F32)
            v = v_ref[rows, cv]
            st = s_ref[h]
            o = o1_ref[rows, cv].astype(F32) + _dot((q * q_scale).astype(BF16), st.astype(BF16))
            y = o * lax.rsqrt(jnp.mean(o * o, axis=-1, keepdims=True) + NORM_EPS)
            o_ref[rows, cv] = (y * _silu(g_ref[rows, cv].astype(F32))).astype(o_ref.dtype)
            s_ref[h] = s_decay * st + lax.dot_general((k * k_scale).astype(BF16), v, TN_DIMS,
                                                      preferred_element_type=F32)
        return carry

    lax.fori_loop(0, nc, chunk, 0)


def _retention(z, log_gamma):
    B, L, W = z.shape
    H, C = RET_HEADS, RET_CHUNK
    DK = W // (6 * H)
    DV = 2 * DK
    T = min(L, 512)
    nT = L // T
    kw = dict(T=T, C=C, H=H, DK=DK, DV=DV)
    smem = pl.BlockSpec(memory_space=pltpu.SMEM)
    blocks = [((T, H * DK), BF16)] * 2 + [((T, H * DV), BF16)] * 4
    state = [((H, DK, DV), F32), ((DK, DV), F32), ((DK, DV), F32)]
    o1 = pl.pallas_call(
        functools.partial(_ret_fwd_kernel, **kw),
        grid=(B, nT),
        in_specs=[smem,
                  pl.BlockSpec((None, T, H * DK), lambda b, i: (b, i, 0)),
                  pl.BlockSpec((None, T, H * DK), lambda b, i: (b, i, 1)),
                  pl.BlockSpec((None, T, H * DV), lambda b, i: (b, i, 1))],
        out_specs=pl.BlockSpec((None, T, H * DV), lambda b, i: (b, i, 0)),
        out_shape=jax.ShapeDtypeStruct((B, L, H * DV), BF16),
        scratch_shapes=[pltpu.VMEM((H, DK, DV), F32)],
        compiler_params=_cparams(("parallel", "arbitrary"), blocks, state),
        name="retention_fwd",
    )(log_gamma, z, z, z)
    return pl.pallas_call(
        functools.partial(_ret_bwd_kernel, **kw),
        grid=(B, nT),
        in_specs=[smem,
                  pl.BlockSpec((None, T, H * DK), lambda b, i: (b, nT - 1 - i, 0)),
                  pl.BlockSpec((None, T, H * DK), lambda b, i: (b, nT - 1 - i, 1)),
                  pl.BlockSpec((None, T, H * DV), lambda b, i: (b, nT - 1 - i, 1)),
                  pl.BlockSpec((None, T, H * DV), lambda b, i: (b, nT - 1 - i, 2)),
                  pl.BlockSpec((None, T, H * DV), lambda b, i: (b, nT - 1 - i, 0))],
        out_specs=pl.BlockSpec((None, T, H * DV), lambda b, i: (b, nT - 1 - i, 0)),
        out_shape=jax.ShapeDtypeStruct((B, L, H * DV), BF16),
        scratch_shapes=[pltpu.VMEM((H, DK, DV), F32)],
        compiler_params=_cparams(("parallel", "arbitrary"), blocks, state),
        name="retention_bwd",
    )(log_gamma, z, z, z, z, o1)


def _swa_kernel(sink_ref, q_ref, kp_ref, kc_ref, kn_ref, vp_ref, vc_ref, vn_ref, o_ref, *, HKV, G, BLK):
    i = pl.program_id(1)
    last = pl.num_programs(1) - 1
    R = G * BLK
    r = lax.broadcasted_iota(jnp.int32, (R, 3 * BLK), 0) % BLK
    c = lax.broadcasted_iota(jnp.int32, (R, 3 * BLK), 1)
    rel = r - (c - BLK)
    lo = jnp.where(i == 0, BLK, 0)
    hi = jnp.where(i == last, 2 * BLK, 3 * BLK)
    valid = (jnp.abs(rel) <= WINDOW) & (c >= lo) & (c < hi)
    grp = lax.broadcasted_iota(jnp.int32, (R, 1), 0) // BLK
    for j in range(HKV):
        sl = slice(j * LANES, (j + 1) * LANES)
        k = jnp.concatenate([kp_ref[:, sl], kc_ref[:, sl], kn_ref[:, sl]], axis=0)
        v = jnp.concatenate([vp_ref[:, sl], vc_ref[:, sl], vn_ref[:, sl]], axis=0)
        q = jnp.concatenate([q_ref[:, (j * G + g) * LANES:(j * G + g + 1) * LANES] for g in range(G)], axis=0)
        s = lax.dot_general(q, k, NT_DIMS, preferred_element_type=F32)
        s = jnp.where(valid, s, NEG_INF)
        sink = jnp.zeros((R, 1), F32)
        for g in range(G):
            sink = jnp.where(grp == g, sink_ref[0, j * G + g], sink)
        m = jnp.maximum(jnp.max(s, axis=-1, keepdims=True), sink)
        p = jnp.exp(s - m)
        denom = jnp.sum(p, axis=-1, keepdims=True) + jnp.exp(sink - m)
        o = _dot(p.astype(BF16), v) / denom
        for g in range(G):
            o_ref[:, (j * G + g) * LANES:(j * G + g + 1) * LANES] = o[g * BLK:(g + 1) * BLK].astype(o_ref.dtype)


def _swa_attention(z, sink):
    B, L, _ = z.shape
    BLK = ATTN_BLOCK
    nb = L // BLK
    G = SWA_HQ // SWA_HKV
    qw = SWA_HQ * LANES
    kvw = SWA_HKV * LANES
    kcol = qw // kvw
    prev = lambda b, i: (b, jnp.maximum(i - 1, 0), kcol)
    cur = lambda b, i: (b, i, kcol)
    nxt = lambda b, i: (b, jnp.minimum(i + 1, nb - 1), kcol)
    vprev = lambda b, i: (b, jnp.maximum(i - 1, 0), kcol + 1)
    vcur = lambda b, i: (b, i, kcol + 1)
    vnxt = lambda b, i: (b, jnp.minimum(i + 1, nb - 1), kcol + 1)
    kv = lambda f: pl.BlockSpec((None, BLK, kvw), f)
    return pl.pallas_call(
        functools.partial(_swa_kernel, HKV=SWA_HKV, G=G, BLK=BLK),
        grid=(B, nb),
        in_specs=[pl.BlockSpec(memory_space=pltpu.SMEM),
                  pl.BlockSpec((None, BLK, qw), lambda b, i: (b, i, 0)),
                  kv(prev), kv(cur), kv(nxt), kv(vprev), kv(vcur), kv(vnxt)],
        out_specs=pl.BlockSpec((None, BLK, qw), lambda b, i: (b, i, 0)),
        out_shape=jax.ShapeDtypeStruct((B, L, qw), BF16),
        compiler_params=_cparams(("parallel", "parallel"),
                                 [((BLK, qw), BF16)] * 2 + [((BLK, kvw), BF16)] * 6,
                                 [((G * BLK, 3 * BLK), F32)] * 4),
        name="swa_attention",
    )(sink.reshape(1, SWA_HQ), z, z, z, z, z, z, z)


def _swa_layout(w_in, q_gain, k_gain, w_out, L):
    D = w_in.shape[0]
    dh, hq, hkv = SWA_DH, SWA_HQ, SWA_HKV
    hf = dh // 2
    q_end, k_end = hq * dh, (hq + hkv) * dh

    def rot_slots(w, n):
        w = w.reshape(D, n, 2, hf)
        return jnp.pad(w, ((0, 0), (0, 0), (0, 0), (0, LANES // 2 - hf))).reshape(D, n * LANES)

    def val_slots(w, n):
        return jnp.pad(w.reshape(D, n, dh), ((0, 0), (0, 0), (0, LANES - dh))).reshape(D, n * LANES)

    w = jnp.concatenate([rot_slots(w_in[:, :q_end], hq), rot_slots(w_in[:, q_end:k_end], hkv),
                         val_slots(w_in[:, k_end:], hkv)], axis=1)

    def gain_slot(g):
        return jnp.pad(g.reshape(2, hf), ((0, 0), (0, LANES // 2 - hf))).reshape(1, LANES)

    flag = jnp.concatenate([jnp.ones(((hq + hkv) * LANES,), F32), jnp.zeros((hkv * LANES,), F32)]).reshape(1, -1)
    wo = jnp.pad(w_out.reshape(hq, dh, -1), ((0, 0), (0, LANES - dh), (0, 0))).reshape(hq * LANES, -1)

    inv = ROPE_THETA ** (-np.arange(0, dh, 2) / dh)
    ang = np.arange(L)[:, None] * inv[None, :]
    zero = np.zeros((L, LANES // 2 - hf))
    cos = jnp.asarray(np.concatenate([np.cos(ang), zero, np.cos(ang), zero], axis=1), F32)
    sin = jnp.asarray(np.concatenate([-np.sin(ang), zero, np.sin(ang), zero], axis=1), F32)
    gq = gain_slot(q_gain) * (dh ** -0.5)
    gk = gain_slot(k_gain)
    roll = lambda g: jnp.roll(g, LANES // 2, axis=1)
    cos3 = jnp.stack([cos * gq, cos * gk, jnp.ones_like(cos)])
    sin3 = jnp.stack([sin * roll(gq), sin * roll(gk), jnp.zeros_like(sin)])
    return w, flag, wo, cos3, sin3


def _hg_decays(f_ref, q_ref, rows, lb, tri, total_row, mid_row):
    f = lb + (1.0 - lb) * jax.nn.sigmoid(f_ref[rows, :].astype(F32))
    gl = jnp.log(f)
    cum = _cumdot(tri, gl)
    total = cum[total_row:total_row + 1, :]
    mid = cum[mid_row:mid_row + 1, :]
    q_dec = _silu(q_ref[rows, :].astype(F32)) * jnp.exp(cum - mid)
    k_inv = (1.0 - f) * jnp.exp(mid - cum)
    q_full = (q_dec * jnp.exp(mid)).astype(BF16)
    k_end = (k_inv * jnp.exp(total - mid)).astype(BF16)
    return q_dec.astype(BF16), k_inv.astype(BF16), q_full, k_end, jnp.exp(total)


def _hg_fwd_kernel(lb_ref, q_ref, i_ref, f_ref, o_ref, st_ref, *, T, C, H, DK, DV):
    @pl.when(pl.program_id(1) == 0)
    def _():
        st_ref[...] = jnp.zeros_like(st_ref)

    row = lax.broadcasted_iota(jnp.int32, (C, C), 0)
    col = lax.broadcasted_iota(jnp.int32, (C, C), 1)
    causal = row >= col
    tri = jnp.concatenate([causal.astype(BF16)] * 2, axis=1)
    lb = lb_ref[...]

    def chunk(c, carry):
        rows = pl.ds(pl.multiple_of(c * C, C), C)
        q_dec, k_inv, q_full, k_end, s_decay = _hg_decays(f_ref, q_ref, rows, lb, tri, C - 1, C // 2 - 1)
        v = i_ref[rows, :]
        for h in range(H):
            ck = slice(h * DK, (h + 1) * DK)
            cv = slice(h * DV, (h + 1) * DV)
            s = lax.dot_general(q_dec[:, ck], k_inv[:, ck], NT_DIMS, preferred_element_type=F32)
            s = jnp.where(causal, s, 0.0)
            st = st_ref[h]
            o_ref[rows, cv] = _dot(s.astype(BF16), v[:, cv]) + lax.dot_general(
                q_full[:, ck], st.astype(BF16), NT_DIMS, preferred_element_type=F32)
            st_ref[h] = st * s_decay[:, ck] + lax.dot_general(v[:, cv], k_end[:, ck], TN_DIMS,
                                                              preferred_element_type=F32)
        return carry

    lax.fori_loop(0, T // C, chunk, 0)


def _hg_bwd_kernel(lb_ref, gain_ref, q_ref, i_ref, f_ref, gate_ref, o1_ref, o_ref, st_ref, *, T, C, H, DK, DV):
    @pl.when(pl.program_id(1) == 0)
    def _():
        st_ref[...] = jnp.zeros_like(st_ref)

    row = lax.broadcasted_iota(jnp.int32, (C, C), 0)
    col = lax.broadcasted_iota(jnp.int32, (C, C), 1)
    anti = col >= row
    tri = jnp.concatenate([anti.astype(BF16)] * 2, axis=1)
    lb = lb_ref[...]
    gain = gain_ref[...]
    nc = T // C

    def chunk(cc, carry):
        rows = pl.ds(pl.multiple_of((nc - 1 - cc) * C, C), C)
        q_dec, k_inv, q_full, k_end, s_decay = _hg_decays(f_ref, q_ref, rows, lb, tri, 0, C // 2)
        v = i_ref[rows, :]
        for h in range(H):
            ck = slice(h * DK, (h + 1) * DK)
            cv = slice(h * DV, (h + 1) * DV)
            s = lax.dot_general(q_dec[:, ck], k_inv[:, ck], NT_DIMS, preferred_element_type=F32)
            s = jnp.where(anti, s, 0.0)
            st = st_ref[h]
            o = o1_ref[rows, cv] + _dot(s.astype(BF16), v[:, cv]) + lax.dot_general(
                q_full[:, ck], st.astype(BF16), NT_DIMS, preferred_element_type=F32)
            y = o * lax.rsqrt(jnp.mean(o * o, axis=-1, keepdims=True) + NORM_EPS) * gain
            o_ref[rows, cv] = (y * _silu(gate_ref[rows, cv].astype(F32))).astype(o_ref.dtype)
            st_ref[h] = st * s_decay[:, ck] + lax.dot_general(v[:, cv], k_end[:, ck], TN_DIMS,
                                                              preferred_element_type=F32)
        return carry

    lax.fori_loop(0, nc, chunk, 0)


def _hgrn(z, lb, gain):
    B, L, _ = z.shape
    H, C = HG_HEADS, HG_CHUNK
    HD = lb.shape[1]
    DK = HD // H
    DV = gain.shape[0]
    T = min(L, 512)
    nT = L // T
    kw = dict(T=T, C=C, H=H, DK=DK, DV=DV)
    blocks = [((T, HD), BF16)] * 5 + [((T, H * DV), F32)] * 2
    state = [((H, DV, DK), F32)] + [((C, HD), F32)] * 12
    lb3 = lb.reshape(2, 1, HD)
    o1 = pl.pallas_call(
        functools.partial(_hg_fwd_kernel, **kw),
        grid=(B, nT),
        in_specs=[pl.BlockSpec((None, 1, HD), lambda b, i: (0, 0, 0)),
                  pl.BlockSpec((None, T, HD), lambda b, i: (b, i, 0)),
                  pl.BlockSpec((None, T, H * DV), lambda b, i: (b, i, 1)),
                  pl.BlockSpec((None, T, HD), lambda b, i: (b, i, 2))],
        out_specs=pl.BlockSpec((None, T, H * DV), lambda b, i: (b, i, 0)),
        out_shape=jax.ShapeDtypeStruct((B, L, H * DV), F32),
        scratch_shapes=[pltpu.VMEM((H, DV, DK), F32)],
        compiler_params=_cparams(("parallel", "arbitrary"), blocks, state),
        name="hgrn_fwd",
    )(lb3, z, z, z)
    return pl.pallas_call(
        functools.partial(_hg_bwd_kernel, **kw),
        grid=(B, nT),
        in_specs=[pl.BlockSpec((None, 1, HD), lambda b, i: (1, 0, 0)),
                  pl.BlockSpec((1, DV), lambda b, i: (0, 0)),
                  pl.BlockSpec((None, T, HD), lambda b, i: (b, nT - 1 - i, 0)),
                  pl.BlockSpec((None, T, H * DV), lambda b, i: (b, nT - 1 - i, 1)),
                  pl.BlockSpec((None, T, HD), lambda b, i: (b, nT - 1 - i, 3)),
                  pl.BlockSpec((None, T, H * DV), lambda b, i: (b, nT - 1 - i, 4)),
                  pl.BlockSpec((None, T, H * DV), lambda b, i: (b, nT - 1 - i, 0))],
        out_specs=pl.BlockSpec((None, T, H * DV), lambda b, i: (b, nT - 1 - i, 0)),
        out_shape=jax.ShapeDtypeStruct((B, L, H * DV), BF16),
        scratch_shapes=[pltpu.VMEM((H, DV, DK), F32)],
        compiler_params=_cparams(("parallel", "arbitrary"), blocks, state),
        name="hgrn_bwd",
    )(lb3, gain.reshape(1, DV), z, z, z, z, o1)


def _ret_rope_tables(L, dk):
    inv = ROPE_THETA ** (-np.arange(0, dk, 2) / dk)
    ang = np.arange(L)[:, None] * inv[None, :]
    cos = np.stack([np.cos(ang), np.cos(ang) * dk ** -0.5, np.ones_like(ang)])
    sin = np.stack([np.sin(ang), np.sin(ang) * dk ** -0.5, np.zeros_like(ang)])
    return jnp.asarray(cos, F32), jnp.asarray(sin, F32)


def kernel(x_prompt, x_sample, c_prompt, c_sample, ada_w, ada_b, norm_g, hy_w_in, hy_conv_w, hy_conv_b, hy_w1, hy_b1, hy_w2, hy_b2, hy_w3, hy_freq, hy_decay, hy_skip, hy_w_out, ret_w_in, ret_decay, ret_w_out, swa_w_in, swa_q_gain, swa_k_gain, swa_sink, swa_w_out, hg_w_in, hg_lb, hg_gain, hg_w_out, ffn_w_gate, ffn_w_val, ffn_conv_w, ffn_conv_b, ffn_w_down):
    depth, D = norm_g.shape[0], norm_g.shape[2]
    groups = [(x_prompt, c_prompt), (x_sample, c_sample)]
    mods = _ada_mod(jnp.concatenate([c for _, c in groups], axis=0), ada_w, ada_b)

    bf = lambda w: w.astype(BF16)
    hy_w_in_b, hy_w_out_b = bf(hy_w_in), bf(hy_w_out)
    ret_w_in_b, ret_w_out_b = bf(ret_w_in), bf(ret_w_out)
    hg_w_in_b, hg_w_out_b = bf(hg_w_in), bf(hg_w_out)
    wg_b, wv_b, wd_b = bf(ffn_w_gate), bf(ffn_w_val), bf(ffn_w_down)
    hg_sm = jax.nn.softmax(hg_lb.astype(F32), axis=1)
    hg_lower = jnp.cumsum(hg_sm, axis=1) - hg_sm
    ret_log_gamma = -jnp.exp(ret_decay.astype(F32))

    outs = []
    row0 = 0
    for x, c in groups:
        B, L, _ = x.shape
        for layer in range(depth):
            kind, j = layer % N_MIXERS, layer // N_MIXERS
            mod = mods[layer, row0:row0 + B].reshape(B, N_MOD, 1, D)
            sh1, sc1, g1, sh2, sc2, g2 = (mod[:, m] for m in range(N_MOD))
            gn1 = norm_g[layer, 0].reshape(1, D)
            gn2 = norm_g[layer, 1].reshape(1, D)
            if kind == 0:
                tabs = _fft_tables(L)
                z = _proj_in_conv(x, gn1, sc1, sh1, hy_w_in_b[j], hy_conv_w[j], hy_conv_b[j], tn=1024)
                taps, asum = _hy_filter_taps(L, hy_w1[j], hy_b1[j], hy_w2[j], hy_b2[j], hy_w3[j], hy_freq[j],
                                             hy_decay[j])
                spec = _hy_spectrum(taps, asum, tabs)
                a = _hy_conv(z, hy_skip[j], spec, tabs)
                x = _proj_out(a, hy_w_out_b[j], x, g1, mult=z)
            elif kind == 1:
                dk = D // RET_HEADS
                cos, sin = _ret_rope_tables(L, dk)
                z = _proj_in_halfrope(x, gn1, sc1, sh1, ret_w_in_b[j], cos, sin, tn=RET_HEADS * dk, rope_tiles=2,
                                      dk=dk)
                a = _retention(z, ret_log_gamma[j])
                x = _proj_out(a, ret_w_out_b[j], x, g1)
            elif kind == 2:
                w, flag, wo, cos, sin = _swa_layout(swa_w_in[j], swa_q_gain[j], swa_k_gain[j], swa_w_out[j], L)
                z = _proj_in_rope(x, gn1, sc1, sh1, bf(w), flag, cos, sin, tn=SWA_HKV * LANES, dh=SWA_DH,
                                  q_tiles=SWA_HQ // SWA_HKV)
                a = _swa_attention(z, swa_sink[j])
                x = _proj_out(a, bf(wo), x, g1)
            else:
                z = _proj_in(x, gn1, sc1, sh1, hg_w_in_b[j], tn=1024)
                a = _hgrn(z, hg_lower[:, layer], hg_gain[j])
                x = _proj_out(a, hg_w_out_b[j], x, g1)
            x = _ffn(x, gn2, sc2, sh2, g2, wg_b[layer], wv_b[layer], ffn_conv_w[layer], ffn_conv_b[layer],
                     wd_b[layer])
        outs.append(x)
        row0 += B
    return tuple(outs)
```

```python
import functools
import math

import numpy as np
import jax
import jax.numpy as jnp
from jax import lax
from jax.experimental import pallas as pl
from jax.experimental.pallas import tpu as pltpu

F32 = jnp.float32
BF16 = jnp.bfloat16

NORM_EPS = 1e-6
N_MIXERS = 4
N_MOD = 6
HY_BANDS = 16
RET_HEADS = 4
RET_CHUNK = 128
SWA_HQ = 16
SWA_HKV = 4
SWA_DH = 64
WINDOW = 128
ATTN_BLOCK = 128
ROPE_THETA = 10000.0
NEG_INF = -1e30
HG_HEADS = 8
HG_CHUNK = 128

LANES = 128
SUBLANES = 8
BF16_ROWS = 16
VMEM_LIMIT_CAP = 60 * 1024 * 1024
VMEM_SLACK = 8 * 1024 * 1024

NT_DIMS = (((1,), (1,)), ((), ()))
TN_DIMS = (((0,), (0,)), ((), ()))


def _nbytes(shape, dtype):
    item = jnp.dtype(dtype).itemsize
    sub = SUBLANES * 4 // item
    dims = list(shape)
    dims[-1] = -(-dims[-1] // LANES) * LANES
    if len(dims) > 1:
        dims[-2] = -(-dims[-2] // sub) * sub
    return int(np.prod(dims)) * item


def _cparams(semantics, pipelined, resident):
    need = 2 * sum(_nbytes(s, d) for s, d in pipelined) + sum(_nbytes(s, d) for s, d in resident)
    return pltpu.CompilerParams(dimension_semantics=semantics,
                                vmem_limit_bytes=min(need + VMEM_SLACK, VMEM_LIMIT_CAP))


def _dot(a, b):
    return jnp.dot(a, b, preferred_element_type=F32)


def _split(x):
    hi = x.astype(BF16)
    lo = (x - hi.astype(F32)).astype(BF16)
    return hi, lo


def _dot_tab(t_cat, d):
    d_hi, d_lo = _split(d)
    return _dot(t_cat, jnp.concatenate([d_hi, d_hi, d_lo], axis=0))


def _dot3(a, b):
    a_hi, a_lo = _split(a)
    b_hi, b_lo = _split(b)
    return _dot(a_hi, b_hi) + _dot(a_lo, b_hi) + _dot(a_hi, b_lo)


def _cumdot(tri2, g):
    g_hi, g_lo = _split(g)
    return _dot(tri2, jnp.concatenate([g_hi, g_lo], axis=0))


def _silu(x):
    return x * jax.nn.sigmoid(x)


def _norm_mod(x, g, sc, sh):
    xf = x.astype(F32)
    y = xf * lax.rsqrt(jnp.mean(xf * xf, axis=-1, keepdims=True) + NORM_EPS)
    return (y * g) * (1.0 + sc) + sh


def _row_tile(L):
    return min(L, 1024)


def _ada_kernel(c_ref, w_ref, b_ref, o_ref):
    cs = _silu(c_ref[...]).astype(BF16)
    o_ref[...] = _dot(cs, w_ref[...].astype(BF16)) + b_ref[...]


def _ada_mod(c_all, ada_w, ada_b):
    depth, D, N = ada_w.shape
    R = c_all.shape[0]
    tn = 1024
    return pl.pallas_call(
        _ada_kernel,
        grid=(depth, N // tn),
        in_specs=[pl.BlockSpec((R, D), lambda l, j: (0, 0)),
                  pl.BlockSpec((None, D, tn), lambda l, j: (l, 0, j)),
                  pl.BlockSpec((None, 1, tn), lambda l, j: (l, 0, j))],
        out_specs=pl.BlockSpec((None, R, tn), lambda l, j: (l, 0, j)),
        out_shape=jax.ShapeDtypeStruct((depth, R, N), F32),
        compiler_params=_cparams(("parallel", "parallel"),
                                 [((D, tn), F32), ((R, D), F32), ((R, tn), F32)], []),
        name="ada_mod",
    )(c_all, ada_w, ada_b.reshape(depth, 1, N))


def _fill_h(h_ref, x_ref, g_ref, sc_ref, sh_ref):
    h_ref[...] = _norm_mod(x_ref[...], g_ref[...], sc_ref[...], sh_ref[...]).astype(BF16)


def _fill_h_halo(h_ref, x_ref, xp_ref, xn_ref, g_ref, sc_ref, sh_ref, tm):
    i = pl.program_id(1)
    last = pl.num_programs(1) - 1
    g, sc, sh = g_ref[...], sc_ref[...], sh_ref[...]
    h_ref[BF16_ROWS:BF16_ROWS + tm, :] = _norm_mod(x_ref[...], g, sc, sh).astype(BF16)
    hp = _norm_mod(xp_ref[...], g, sc, sh)
    hn = _norm_mod(xn_ref[...], g, sc, sh)
    h_ref[0:BF16_ROWS, :] = jnp.where(i > 0, hp, 0.0).astype(BF16)
    h_ref[BF16_ROWS + tm:, :] = jnp.where(i < last, hn, 0.0).astype(BF16)


def _conv3_rows(z_ref, cw, cb, tm):
    o = BF16_ROWS
    return (z_ref[o - 1:o - 1 + tm, :] * cw[0:1, :] + z_ref[o:o + tm, :] * cw[1:2, :]
            + z_ref[o + 1:o + 1 + tm, :] * cw[2:3, :] + cb)


def _proj_plain_kernel(x_ref, g_ref, sc_ref, sh_ref, w_ref, o_ref, h_ref):
    @pl.when(pl.program_id(2) == 0)
    def _():
        _fill_h(h_ref, x_ref, g_ref, sc_ref, sh_ref)

    o_ref[...] = _dot(h_ref[...], w_ref[...]).astype(o_ref.dtype)


def _proj_halfrope_kernel(x_ref, g_ref, sc_ref, sh_ref, w_ref, cos_ref, sin_ref, o_ref, h_ref, *, dk):
    @pl.when(pl.program_id(2) == 0)
    def _():
        _fill_h(h_ref, x_ref, g_ref, sc_ref, sh_ref)

    z = _dot(h_ref[...], w_ref[...])
    cos, sin = cos_ref[...], sin_ref[...]
    half = dk // 2
    for c0 in range(0, z.shape[1], dk):
        x1, x2 = z[:, c0:c0 + half], z[:, c0 + half:c0 + dk]
        o_ref[:, c0:c0 + half] = (x1 * cos - x2 * sin).astype(o_ref.dtype)
        o_ref[:, c0 + half:c0 + dk] = (x1 * sin + x2 * cos).astype(o_ref.dtype)


def _proj_conv_kernel(x_ref, xp_ref, xn_ref, g_ref, sc_ref, sh_ref, w_ref, cw_ref, cb_ref,
                      o_ref, h_ref, z_ref, *, tm):
    @pl.when(pl.program_id(2) == 0)
    def _():
        _fill_h_halo(h_ref, x_ref, xp_ref, xn_ref, g_ref, sc_ref, sh_ref, tm)

    z = _dot(h_ref[...], w_ref[...])
    for c in range(o_ref.shape[0]):
        cols = slice(c * LANES, (c + 1) * LANES)
        z_ref[c] = z[:, cols]
        o_ref[c] = _conv3_rows(z_ref.at[c], cw_ref[:, cols], cb_ref[:, cols], tm).astype(o_ref.dtype)


def _proj_rope_kernel(x_ref, g_ref, sc_ref, sh_ref, w_ref, flag_ref, ones_ref, cos_ref, sin_ref,
                      o_ref, h_ref, *, dh):
    @pl.when(pl.program_id(2) == 0)
    def _():
        _fill_h(h_ref, x_ref, g_ref, sc_ref, sh_ref)

    z = _dot(h_ref[...], w_ref[...])
    ms = _dot((z * z).astype(BF16), ones_ref[...]) * (1.0 / dh)
    zn = z * jnp.where(flag_ref[...] > 0.0, lax.rsqrt(ms + NORM_EPS), 1.0)
    cos, sin = cos_ref[...], sin_ref[...]
    for s in range(z.shape[1] // LANES):
        sl = slice(s * LANES, (s + 1) * LANES)
        zs = zn[:, sl]
        o_ref[:, sl] = (zs * cos + pltpu.roll(zs, LANES // 2, axis=1) * sin).astype(o_ref.dtype)


def _mod_specs(D):
    return [pl.BlockSpec((1, D), lambda b, i, j: (0, 0)),
            pl.BlockSpec((None, 1, D), lambda b, i, j: (b, 0, 0)),
            pl.BlockSpec((None, 1, D), lambda b, i, j: (b, 0, 0))]


def _halo_specs(tm, D, L):
    hb = tm // BF16_ROWS
    nhb = L // BF16_ROWS
    return [pl.BlockSpec((None, tm, D), lambda b, i, j: (b, i, 0)),
            pl.BlockSpec((None, BF16_ROWS, D), lambda b, i, j: (b, jnp.maximum(i * hb - 1, 0), 0)),
            pl.BlockSpec((None, BF16_ROWS, D), lambda b, i, j: (b, jnp.minimum((i + 1) * hb, nhb - 1), 0))]


def _proj_in(x, g, sc, sh, w, *, tn):
    B, L, D = x.shape
    N = w.shape[1]
    tm = _row_tile(L)
    return pl.pallas_call(
        _proj_plain_kernel,
        grid=(B, L // tm, N // tn),
        in_specs=[pl.BlockSpec((None, tm, D), lambda b, i, j: (b, i, 0))] + _mod_specs(D)
        + [pl.BlockSpec((D, tn), lambda b, i, j: (0, j))],
        out_specs=pl.BlockSpec((None, tm, tn), lambda b, i, j: (b, i, j)),
        out_shape=jax.ShapeDtypeStruct((B, L, N), BF16),
        scratch_shapes=[pltpu.VMEM((tm, D), BF16)],
        compiler_params=_cparams(("parallel", "parallel", "arbitrary"),
                                 [((tm, D), F32), ((D, tn), BF16), ((tm, tn), BF16)],
                                 [((tm, D), BF16), ((tm, tn), F32)]),
        name="proj_in",
    )(x, g, sc, sh, w)


def _proj_in_halfrope(x, g, sc, sh, w, cos, sin, *, tn, rope_tiles, dk):
    B, L, D = x.shape
    N = w.shape[1]
    tm = _row_tile(L)
    half = dk // 2
    table = lambda b, i, j: (jnp.minimum(j, rope_tiles), i, 0)
    return pl.pallas_call(
        functools.partial(_proj_halfrope_kernel, dk=dk),
        grid=(B, L // tm, N // tn),
        in_specs=[pl.BlockSpec((None, tm, D), lambda b, i, j: (b, i, 0))] + _mod_specs(D)
        + [pl.BlockSpec((D, tn), lambda b, i, j: (0, j)),
           pl.BlockSpec((None, tm, half), table),
           pl.BlockSpec((None, tm, half), table)],
        out_specs=pl.BlockSpec((None, tm, tn), lambda b, i, j: (b, i, j)),
        out_shape=jax.ShapeDtypeStruct((B, L, N), BF16),
        scratch_shapes=[pltpu.VMEM((tm, D), BF16)],
        compiler_params=_cparams(("parallel", "parallel", "arbitrary"),
                                 [((tm, D), F32), ((D, tn), BF16), ((tm, tn), BF16), ((tm, half), F32),
                                  ((tm, half), F32)],
                                 [((tm, D), BF16), ((tm, tn), F32), ((tm, tn), F32)]),
        name="proj_in_halfrope",
    )(x, g, sc, sh, w, cos, sin)


def _proj_in_conv(x, g, sc, sh, w, cw, cb, *, tn):
    B, L, D = x.shape
    N = w.shape[1]
    tm = _row_tile(L)
    te = tm + 2 * BF16_ROWS
    return pl.pallas_call(
        functools.partial(_proj_conv_kernel, tm=tm),
        grid=(B, L // tm, N // tn),
        in_specs=_halo_specs(tm, D, L) + _mod_specs(D)
        + [pl.BlockSpec((D, tn), lambda b, i, j: (0, j)),
           pl.BlockSpec((3, tn), lambda b, i, j: (0, j)),
           pl.BlockSpec((1, tn), lambda b, i, j: (0, j))],
        out_specs=pl.BlockSpec((None, tn // LANES, tm, LANES), lambda b, i, j: (b, j, i, 0)),
        out_shape=jax.ShapeDtypeStruct((B, N // LANES, L, LANES), BF16),
        scratch_shapes=[pltpu.VMEM((te, D), BF16), pltpu.VMEM((tn // LANES, te, LANES), F32)],
        compiler_params=_cparams(("parallel", "parallel", "arbitrary"),
                                 [((tm, D), F32), ((D, tn), BF16), ((tm, tn), BF16)],
                                 [((te, D), BF16), ((te, tn), F32), ((te, tn), F32)]),
        name="proj_in_conv",
    )(x, x, x, g, sc, sh, w, cw, cb.reshape(1, N))


def _proj_in_rope(x, g, sc, sh, w, flag, cos, sin, *, tn, dh, q_tiles):
    B, L, D = x.shape
    N = w.shape[1]
    tm = _row_tile(L)
    slot = lax.broadcasted_iota(jnp.int32, (tn, tn), 0) // LANES == lax.broadcasted_iota(jnp.int32, (tn, tn), 1) // LANES
    table = lambda b, i, j: (jnp.maximum(j - (q_tiles - 1), 0), i, 0)
    return pl.pallas_call(
        functools.partial(_proj_rope_kernel, dh=dh),
        grid=(B, L // tm, N // tn),
        in_specs=[pl.BlockSpec((None, tm, D), lambda b, i, j: (b, i, 0))] + _mod_specs(D)
        + [pl.BlockSpec((D, tn), lambda b, i, j: (0, j)),
           pl.BlockSpec((1, tn), lambda b, i, j: (0, j)),
           pl.BlockSpec((tn, tn), lambda b, i, j: (0, 0)),
           pl.BlockSpec((None, tm, LANES), table),
           pl.BlockSpec((None, tm, LANES), table)],
        out_specs=pl.BlockSpec((None, tm, tn), lambda b, i, j: (b, i, j)),
        out_shape=jax.ShapeDtypeStruct((B, L, N), BF16),
        scratch_shapes=[pltpu.VMEM((tm, D), BF16)],
        compiler_params=_cparams(("parallel", "parallel", "arbitrary"),
                                 [((tm, D), F32), ((D, tn), BF16), ((tm, tn), BF16), ((tm, LANES), F32),
                                  ((tm, LANES), F32), ((tn, tn), BF16)],
                                 [((tm, D), BF16), ((tm, tn), F32), ((tm, tn), F32), ((tm, tn), F32)]),
        name="proj_in_rope",
    )(x, g, sc, sh, w, flag, slot.astype(BF16), cos, sin)


def _proj_out_kernel(a_ref, w_ref, x_ref, gate_ref, o_ref):
    o_ref[...] = x_ref[...] + gate_ref[...] * _dot(a_ref[...], w_ref[...])


def _proj_out_gated_kernel(a_ref, m_ref, w_ref, x_ref, gate_ref, o_ref):
    a = jnp.concatenate([(a_ref[c].astype(F32) * m_ref[c].astype(F32)).astype(BF16)
                         for c in range(a_ref.shape[0])], axis=1)
    o_ref[...] = x_ref[...] + gate_ref[...] * _dot(a, w_ref[...])


def _proj_out(a, w, x, gate, mult=None):
    blocked = a.ndim == 4
    B, L = x.shape[:2]
    K, D = w.shape
    tm = _row_tile(L)
    if blocked:
        a_specs = [pl.BlockSpec((None, K // LANES, tm, LANES), lambda b, i: (b, 0, i, 0))] * 2
        operands = (a, mult)
    else:
        a_specs = [pl.BlockSpec((None, tm, K), lambda b, i: (b, i, 0))]
        operands = (a,)
    return pl.pallas_call(
        _proj_out_gated_kernel if blocked else _proj_out_kernel,
        grid=(B, L // tm),
        in_specs=a_specs + [
                  pl.BlockSpec((K, D), lambda b, i: (0, 0)),
                  pl.BlockSpec((None, tm, D), lambda b, i: (b, i, 0)),
                  pl.BlockSpec((None, 1, D), lambda b, i: (b, 0, 0))],
        out_specs=pl.BlockSpec((None, tm, D), lambda b, i: (b, i, 0)),
        out_shape=jax.ShapeDtypeStruct((B, L, D), F32),
        compiler_params=_cparams(("parallel", "parallel"),
                                 [((tm, K), BF16)] * len(operands)
                                 + [((K, D), BF16), ((tm, D), F32), ((tm, D), F32)],
                                 [((tm, D), F32), ((tm, K), F32)]),
        name="proj_out",
    )(*operands, w, x, gate)


def _ffn_kernel(x_ref, xp_ref, xn_ref, g_ref, sc_ref, sh_ref, gate_ref, wg_ref, wv_ref, cw_ref, cb_ref,
                wd_ref, o_ref, h_ref, z_ref, *, tm, tf):
    _fill_h_halo(h_ref, x_ref, xp_ref, xn_ref, g_ref, sc_ref, sh_ref, tm)
    pieces = [slice(c * tf, (c + 1) * tf) for c in range(wg_ref.shape[1] // tf)]
    vals = []
    for c, cols in enumerate(pieces):
        z_ref[c] = _dot(h_ref[...], wg_ref[:, cols])
        vals.append(_dot(h_ref[BF16_ROWS:BF16_ROWS + tm, :], wv_ref[:, cols]))
    acc = None
    for c, cols in enumerate(pieces):
        a = _conv3_rows(z_ref.at[c], cw_ref[:, cols], cb_ref[:, cols], tm)
        part = _dot((_silu(a) * vals[c]).astype(BF16), wd_ref[cols, :])
        acc = part if acc is None else acc + part
    o_ref[...] = x_ref[...] + gate_ref[...] * acc


def _ffn(x, g, sc, sh, gate, wg, wv, cw, cb, wd):
    B, L, D = x.shape
    F = wg.shape[1]
    tm = min(L, 512)
    te = tm + 2 * BF16_ROWS
    tf = F // 2
    const = lambda b, i, j: (0, 0)
    resident = lambda shape: pl.BlockSpec(shape, const, pipeline_mode=pl.Buffered(1))
    return pl.pallas_call(
        functools.partial(_ffn_kernel, tm=tm, tf=tf),
        grid=(B, L // tm, 1),
        in_specs=_halo_specs(tm, D, L) + _mod_specs(D)
        + [pl.BlockSpec((None, 1, D), lambda b, i, j: (b, 0, 0)),
           resident((D, F)), resident((D, F)), resident((3, F)), resident((1, F)), resident((F, D))],
        out_specs=pl.BlockSpec((None, tm, D), lambda b, i, j: (b, i, 0)),
        out_shape=jax.ShapeDtypeStruct((B, L, D), F32),
        scratch_shapes=[pltpu.VMEM((te, D), BF16), pltpu.VMEM((F // tf, te, tf), F32)],
        compiler_params=_cparams(("parallel", "parallel", "arbitrary"),
                                 [((tm, D), F32), ((tm, D), F32)],
                                 [((D, F), BF16), ((D, F), BF16), ((F, D), BF16), ((te, D), BF16),
                                  ((te, F), F32), ((tm, F), F32), ((tm, tf), F32), ((tm, tf), BF16),
                                  ((tm, D), F32), ((tm, D), F32)]),
        name="ffn",
    )(x, x, x, g, sc, sh, gate, wg, wv, cw, cb.reshape(1, F), wd)


def _fft_dims(L):
    N = 2 * L
    p = N.bit_length() - 1
    n1 = 1 << ((p + 1) // 2)
    return N, n1, N // n1


def _k1_pad(n1):
    return n1 // 2 + 8


def _fft_tables(L):
    N, n1, n2 = _fft_dims(L)
    k1p = _k1_pad(n1)
    k1 = np.arange(k1p)
    valid = (k1 <= n1 // 2).astype(np.float64)
    weight = np.where((k1 == 0) | (k1 == n1 // 2), 1.0, 2.0) * valid
    m1 = np.arange(n1 // 2)
    n2v = np.arange(n2)
    n_idx = n2 * m1[None, :] + n2v[:, None]
    phi = 2.0 * np.pi * k1[None, :, None] * n_idx[:, None, :] / N
    s1 = np.concatenate([np.cos(phi) * valid[None, :, None], -np.sin(phi) * valid[None, :, None]], axis=1)
    phit = np.transpose(phi, (0, 2, 1))
    gl = np.concatenate([np.cos(phit) * weight[None, None, :], -np.sin(phit) * weight[None, None, :]],
                        axis=2) / N
    th = 2.0 * np.pi * np.outer(n2v, n2v) / n2
    c, s = np.cos(th), np.sin(th)
    f2 = np.block([[c, s], [-s, c]])
    f2c = np.block([[c, -s], [s, c]])

    def cat(a):
        a32 = jnp.asarray(a, F32)
        hi = a32.astype(BF16)
        lo = (a32 - hi.astype(F32)).astype(BF16)
        return jnp.concatenate([hi, lo, hi], axis=-1)

    return dict(N=N, n1=n1, n2=n2, k1p=k1p, s1=cat(s1), gl=cat(gl), f2=cat(f2), f2c=cat(f2c))


def _hy_positions(L):
    idx = np.concatenate([np.arange(L), np.array([0]), np.arange(L - 1, 0, -1)])
    t = idx / (L - 1)
    ang = (2.0 * np.pi / L) * idx
    bands = np.linspace(1e-4, HY_BANDS - 1, HY_BANDS)
    z = np.concatenate([t[:, None], np.cos(bands[None, :] * ang[:, None]), -np.sin(bands[None, :] * ang[:, None])],
                       axis=1)
    zp = np.zeros((2 * L, LANES))
    zp[:, : z.shape[1]] = z
    return jnp.asarray(zp, F32)


def _hy_mlp_kernel(z_ref, w1_ref, b1_ref, w2_ref, b2_ref, w3_ref, fr_ref, dec_ref, taps_ref, asum_ref, *, L, tr):
    i = pl.program_id(0)
    z = z_ref[...]
    fr = fr_ref[...]
    h = jnp.sin(fr * (_dot3(z, w1_ref[...]) + b1_ref[...]))
    h = jnp.sin(fr * (_dot3(h, w2_ref[...]) + b2_ref[...]))
    h = _dot3(h, w3_ref[...])
    t = z[:, 0:1]
    taps = h * jnp.exp(-t * jnp.abs(dec_ref[...]))
    row = i * tr + lax.broadcasted_iota(jnp.int32, (tr, 1), 0)
    taps = jnp.where(row == L, 0.0, taps)
    taps_ref[...] = taps

    @pl.when(i == 0)
    def _():
        asum_ref[...] = jnp.zeros_like(asum_ref)

    asum_ref[...] += jnp.sum(jnp.abs(taps), axis=0, keepdims=True)


def _hy_filter_taps(L, w1, b1, w2, b2, w3, freq, decay):
    W = decay.shape[1]
    E, O = w1.shape
    tr = 512
    half = L // tr
    z = _hy_positions(L)
    w1p = jnp.zeros((LANES, O), F32).at[:E].set(w1)
    return pl.pallas_call(
        functools.partial(_hy_mlp_kernel, L=L, tr=tr),
        grid=(2 * L // tr,),
        in_specs=[pl.BlockSpec((tr, LANES), lambda i: (i, 0)),
                  pl.BlockSpec((LANES, O), lambda i: (0, 0)),
                  pl.BlockSpec((1, O), lambda i: (0, 0)),
                  pl.BlockSpec((O, O), lambda i: (0, 0)),
                  pl.BlockSpec((1, O), lambda i: (0, 0)),
                  pl.BlockSpec((O, W), lambda i: (0, i // half)),
                  pl.BlockSpec((1, O), lambda i: (0, 0)),
                  pl.BlockSpec((None, 1, W), lambda i: (i // half, 0, 0))],
        out_specs=[pl.BlockSpec((tr, W), lambda i: (i, 0)),
                   pl.BlockSpec((1, W), lambda i: (0, 0))],
        out_shape=[jax.ShapeDtypeStruct((2 * L, W), F32), jax.ShapeDtypeStruct((1, W), F32)],
        compiler_params=_cparams(("arbitrary",), [((tr, W), F32), ((O, W), F32), ((tr, LANES), F32)],
                                 [((tr, W), F32), ((tr, W), F32)]),
        name="hyena_filter_mlp",
    )(z, w1p, b1.reshape(1, O), w2, b2.reshape(1, O), w3, freq.reshape(1, O), decay.reshape(2, 1, W))


FFT_UNROLL = 16


def _fft_stage1(src_ref, src_off, a_ref, s1_ref, n1, n2, k1p, sign=None, src_off2=None):
    def body(j, carry):
        d = src_ref[pl.ds(src_off + j, n1 // 2, stride=n2), :]
        r = _dot_tab(s1_ref[j], d)
        if src_off2 is not None:
            d2 = src_ref[pl.ds(src_off2 + j, n1 // 2, stride=n2), :]
            r = r + sign * _dot_tab(s1_ref[j], d2)
        a_ref[pl.ds(pl.multiple_of(j * 2 * k1p, 8), 2 * k1p), :] = r
        return carry

    lax.fori_loop(0, n2, body, 0, unroll=FFT_UNROLL)


def _load_k1(a_ref, k, n2, k1p):
    br = a_ref[pl.ds(k, n2, stride=2 * k1p), :]
    bi = a_ref[pl.ds(k1p + k, n2, stride=2 * k1p), :]
    return jnp.concatenate([br, bi], axis=0)


def _hy_spec_kernel(taps_ref, asum_ref, s1_ref, f2_ref, h_ref, a_ref, *, L, n1, n2, k1p):
    r = lax.broadcasted_iota(jnp.int32, (2 * k1p, 1), 0)
    k1 = jnp.where(r < k1p, r, r - k1p)
    sign = (1 - 2 * (k1 & 1)).astype(F32)
    _fft_stage1(taps_ref, 0, a_ref, s1_ref, n1, n2, k1p, sign=sign, src_off2=L)
    inv = 1.0 / asum_ref[...]

    def body(k, carry):
        h_ref[k] = _dot_tab(f2_ref[...], _load_k1(a_ref, k, n2, k1p)) * inv
        return carry

    lax.fori_loop(0, n1 // 2, body, 0, unroll=2)
    body(n1 // 2, 0)


def _hy_spectrum(taps, asum, tabs):
    n1, n2, k1p = tabs["n1"], tabs["n2"], tabs["k1p"]
    N, W = taps.shape
    L = N // 2
    cb = LANES
    k1v = n1 // 2 + 1
    s1, f2 = tabs["s1"], tabs["f2"]
    return pl.pallas_call(
        functools.partial(_hy_spec_kernel, L=L, n1=n1, n2=n2, k1p=k1p),
        grid=(W // cb,),
        in_specs=[pl.BlockSpec((N, cb), lambda c: (0, c)),
                  pl.BlockSpec((1, cb), lambda c: (0, c)),
                  pl.BlockSpec(s1.shape, lambda c: (0, 0, 0), pipeline_mode=pl.Buffered(1)),
                  pl.BlockSpec(f2.shape, lambda c: (0, 0), pipeline_mode=pl.Buffered(1))],
        out_specs=pl.BlockSpec((k1v, 2 * n2, cb), lambda c: (0, 0, c)),
        out_shape=jax.ShapeDtypeStruct((k1v, 2 * n2, W), F32),
        scratch_shapes=[pltpu.VMEM((n2 * 2 * k1p, cb), F32)],
        compiler_params=_cparams(("parallel",),
                                 [((N, cb), F32), ((k1v, 2 * n2, cb), F32)],
                                 [((n2 * 2 * k1p, cb), F32), (s1.shape, BF16), (f2.shape, BF16)]),
        name="hyena_filter_spectrum",
    )(taps, asum, s1, f2)


def _hy_conv_kernel(x1_ref, v_ref, skip_ref, h_ref, s1_ref, f2_ref, f2c_ref, gl_ref,
                    o_ref, u_ref, a_ref, *, n1, n2, k1p):
    u_ref[...] = x1_ref[...].astype(F32) * v_ref[...].astype(F32)
    _fft_stage1(u_ref, 0, a_ref, s1_ref, n1, n2, k1p)

    cb = u_ref.shape[1]

    def mid(groups):
        loaded = [jnp.concatenate([_load_k1(a_ref, k, n2, k1p) for k in ks], axis=1) for ks in groups]
        results = []
        for ks, rhs in zip(groups, loaded):
            x = _dot_tab(f2_ref[...], rhs)
            xr, xi = x[:n2], x[n2:]
            hr = jnp.concatenate([h_ref[k, :n2, :] for k in ks], axis=1)
            hi = jnp.concatenate([h_ref[k, n2:, :] for k in ks], axis=1)
            y = jnp.concatenate([xr * hr - xi * hi, xr * hi + xi * hr], axis=0)
            results.append(_dot_tab(f2c_ref[...], y))
        for ks, c in zip(groups, results):
            for p, k in enumerate(ks):
                a_ref[pl.ds(k, n2, stride=2 * k1p), :] = c[:n2, p * cb:(p + 1) * cb]
                a_ref[pl.ds(k1p + k, n2, stride=2 * k1p), :] = c[n2:, p * cb:(p + 1) * cb]

    def mid_oct(ko, carry):
        k = 8 * ko
        mid(((k, k + 1, k + 2, k + 3), (k + 4, k + 5, k + 6, k + 7)))
        return carry

    lax.fori_loop(0, n1 // 16, mid_oct, 0)
    mid(((n1 // 2,),))
    skip = skip_ref[...]

    def last(j, carry):
        rhs = a_ref[pl.ds(pl.multiple_of(j * 2 * k1p, 8), 2 * k1p), :]
        y = _dot_tab(gl_ref[j], rhs)
        rows = pl.ds(j, n1 // 2, stride=n2)
        u_ref[rows, :] = y + u_ref[rows, :] * skip
        return carry

    lax.fori_loop(0, n2, last, 0, unroll=FFT_UNROLL)
    o_ref[...] = u_ref[...].astype(o_ref.dtype)


def _hy_conv(z, skip, spec, tabs):
    n1, n2, k1p = tabs["n1"], tabs["n2"], tabs["k1p"]
    B, nc3, L, cb = z.shape
    nc = nc3 // 3
    W = nc * cb
    k1v = n1 // 2 + 1
    tables = [tabs["s1"], tabs["f2"], tabs["f2c"], tabs["gl"]]

    def const_spec(t):
        return pl.BlockSpec(t.shape, (lambda c, b: (0, 0, 0)) if t.ndim == 3 else (lambda c, b: (0, 0)),
                            pipeline_mode=pl.Buffered(1))

    return pl.pallas_call(
        functools.partial(_hy_conv_kernel, n1=n1, n2=n2, k1p=k1p),
        grid=(nc, B),
        in_specs=[pl.BlockSpec((None, None, L, cb), lambda c, b: (b, nc + c, 0, 0)),
                  pl.BlockSpec((None, None, L, cb), lambda c, b: (b, 2 * nc + c, 0, 0)),
                  pl.BlockSpec((1, cb), lambda c, b: (0, c)),
                  pl.BlockSpec((k1v, 2 * n2, cb), lambda c, b: (0, 0, c), pipeline_mode=pl.Buffered(1))]
        + [const_spec(t) for t in tables],
        out_specs=pl.BlockSpec((None, None, L, cb), lambda c, b: (b, c, 0, 0)),
        out_shape=jax.ShapeDtypeStruct((B, nc, L, cb), BF16),
        scratch_shapes=[pltpu.VMEM((L, cb), F32), pltpu.VMEM((n2 * 2 * k1p, cb), F32)],
        compiler_params=_cparams(("parallel", "arbitrary"),
                                 [((L, cb), BF16)] * 3,
                                 [((L, cb), F32), ((n2 * 2 * k1p, cb), F32), ((k1v, 2 * n2, cb), F32)]
                                 + [(t.shape, BF16) for t in tables]),
        name="hyena_long_conv",
    )(z, z, skip.reshape(1, W), spec, *tables)


def _ret_fwd_kernel(lg_ref, q_ref, k_ref, v_ref, o_ref, s_ref, *, T, C, H, DK, DV):
    @pl.when(pl.program_id(1) == 0)
    def _():
        s_ref[...] = jnp.zeros_like(s_ref)

    row = lax.broadcasted_iota(jnp.int32, (C, C), 0)
    col = lax.broadcasted_iota(jnp.int32, (C, C), 1)
    rel = (row - col).astype(F32)
    ridx = lax.broadcasted_iota(jnp.int32, (C, 1), 0).astype(F32)
    consts = []
    for h in range(H):
        lgf = lg_ref[0, h]
        lgb = lg_ref[1, h]
        dmat = jnp.where(rel > 0.0, jnp.exp(lgf * jnp.maximum(rel, 0.0)),
                         jnp.where(rel < 0.0, jnp.exp(lgb * jnp.maximum(-rel, 0.0)), 2.0))
        consts.append((dmat, jnp.exp(lgf * (ridx + 1.0)), jnp.exp(lgf * (C - 1.0 - ridx)),
                       jnp.exp(jnp.full((1, 1), C, F32) * lgf)))

    def chunk(c, carry):
        rows = pl.ds(pl.multiple_of(c * C, C), C)
        for h in range(H):
            dmat, q_scale, k_scale, s_decay = consts[h]
            ck = slice(h * DK, (h + 1) * DK)
            cv = slice(h * DV, (h + 1) * DV)
            q = q_ref[rows, ck]
            k = k_ref[rows, ck]
            v = v_ref[rows, cv]
            s = lax.dot_general(q, k, NT_DIMS, preferred_element_type=F32) * dmat
            st = s_ref[h]
            o_ref[rows, cv] = (_dot(s.astype(BF16), v) + _dot((q.astype(F32) * q_scale).astype(BF16),
                                                              st.astype(BF16))).astype(o_ref.dtype)
            s_ref[h] = s_decay * st + lax.dot_general((k.astype(F32) * k_scale).astype(BF16), v, TN_DIMS,
                                                      preferred_element_type=F32)
        return carry

    lax.fori_loop(0, T // C, chunk, 0)


def _ret_bwd_kernel(lg_ref, q_ref, k_ref, v_ref, g_ref, o1_ref, w_ref, x_ref, gate_ref, o_ref, s_ref, a_ref,
                    *, T, C, H, DK, DV):
    @pl.when(pl.program_id(1) == 0)
    def _():
        s_ref[...] = jnp.zeros_like(s_ref)

    ridx = lax.broadcasted_iota(jnp.int32, (C, 1), 0).astype(F32)
    nc = T // C
    consts = []
    for h in range(H):
        lgb = lg_ref[1, h]
        consts.append((jnp.exp(lgb * (C - ridx)), jnp.exp(lgb * ridx), jnp.exp(jnp.full((1, 1), C, F32) * lgb)))

    def chunk(cc, carry):
        rows = pl.ds(pl.multiple_of((nc - 1 - cc) * C, C), C)
        for h in range(H):
            q_scale, k_scale, s_decay = consts[h]
            ck = slice(h * DK, (h + 1) * DK)
            cv = slice(h * DV, (h + 1) * DV)
            q = q_ref[rows, ck].astype(F32)
            k = k_ref[rows, ck].astype(F32)
            v = v_ref[rows, cv]
            st = s_ref[h]
            o = o1_ref[rows, cv].astype(F32) + _dot((q * q_scale).astype(BF16), st.astype(BF16))
            y = o * lax.rsqrt(jnp.mean(o * o, axis=-1, keepdims=True) + NORM_EPS)
            a_ref[rows, cv] = (y * _silu(g_ref[rows, cv].astype(F32))).astype(a_ref.dtype)
            s_ref[h] = s_decay * st + lax.dot_general((k * k_scale).astype(BF16), v, TN_DIMS,
                                                      preferred_element_type=F32)
        return carry

    lax.fori_loop(0, nc, chunk, 0)
    o_ref[...] = x_ref[...] + gate_ref[...] * _dot(a_ref[...], w_ref[...])


def _retention(z, log_gamma, w_out, x, gate):
    B, L, W = z.shape
    D = w_out.shape[1]
    H, C = RET_HEADS, RET_CHUNK
    DK = W // (6 * H)
    DV = 2 * DK
    T = min(L, 512)
    nT = L // T
    kw = dict(T=T, C=C, H=H, DK=DK, DV=DV)
    smem = pl.BlockSpec(memory_space=pltpu.SMEM)
    blocks = [((T, H * DK), BF16)] * 2 + [((T, H * DV), BF16)] * 4
    state = [((H, DK, DV), F32), ((DK, DV), F32), ((DK, DV), F32)]
    o1 = pl.pallas_call(
        functools.partial(_ret_fwd_kernel, **kw),
        grid=(B, nT),
        in_specs=[smem,
                  pl.BlockSpec((None, T, H * DK), lambda b, i: (b, i, 0)),
                  pl.BlockSpec((None, T, H * DK), lambda b, i: (b, i, 1)),
                  pl.BlockSpec((None, T, H * DV), lambda b, i: (b, i, 1))],
        out_specs=pl.BlockSpec((None, T, H * DV), lambda b, i: (b, i, 0)),
        out_shape=jax.ShapeDtypeStruct((B, L, H * DV), BF16),
        scratch_shapes=[pltpu.VMEM((H, DK, DV), F32)],
        compiler_params=_cparams(("parallel", "arbitrary"), blocks, state),
        name="retention_fwd",
    )(log_gamma, z, z, z)
    return pl.pallas_call(
        functools.partial(_ret_bwd_kernel, **kw),
        grid=(B, nT),
        in_specs=[smem,
                  pl.BlockSpec((None, T, H * DK), lambda b, i: (b, nT - 1 - i, 0)),
                  pl.BlockSpec((None, T, H * DK), lambda b, i: (b, nT - 1 - i, 1)),
                  pl.BlockSpec((None, T, H * DV), lambda b, i: (b, nT - 1 - i, 1)),
                  pl.BlockSpec((None, T, H * DV), lambda b, i: (b, nT - 1 - i, 2)),
                  pl.BlockSpec((None, T, H * DV), lambda b, i: (b, nT - 1 - i, 0)),
                  pl.BlockSpec((H * DV, D), lambda b, i: (0, 0), pipeline_mode=pl.Buffered(1)),
                  pl.BlockSpec((None, T, D), lambda b, i: (b, nT - 1 - i, 0)),
                  pl.BlockSpec((None, 1, D), lambda b, i: (b, 0, 0))],
        out_specs=pl.BlockSpec((None, T, D), lambda b, i: (b, nT - 1 - i, 0)),
        out_shape=jax.ShapeDtypeStruct((B, L, D), F32),
        scratch_shapes=[pltpu.VMEM((H, DK, DV), F32), pltpu.VMEM((T, H * DV), BF16)],
        compiler_params=_cparams(("parallel", "arbitrary"), blocks[:-1] + [((T, D), F32)] * 2,
                                 state + [((H * DV, D), BF16), ((T, H * DV), BF16), ((T, D), F32)]),
        name="retention_bwd",
    )(log_gamma, z, z, z, z, o1, w_out, x, gate)


def _swa_kernel(sink_ref, q_ref, kp_ref, kc_ref, kn_ref, vp_ref, vc_ref, vn_ref, o_ref, *, HKV, G, BLK):
    i = pl.program_id(1)
    last = pl.num_programs(1) - 1
    R = G * BLK
    r = lax.broadcasted_iota(jnp.int32, (R, 3 * BLK), 0) % BLK
    c = lax.broadcasted_iota(jnp.int32, (R, 3 * BLK), 1)
    rel = r - (c - BLK)
    lo = jnp.where(i == 0, BLK, 0)
    hi = jnp.where(i == last, 2 * BLK, 3 * BLK)
    valid = (jnp.abs(rel) <= WINDOW) & (c >= lo) & (c < hi)
    grp = lax.broadcasted_iota(jnp.int32, (R, 1), 0) // BLK
    for j in range(HKV):
        sl = slice(j * LANES, (j + 1) * LANES)
        k = jnp.concatenate([kp_ref[:, sl], kc_ref[:, sl], kn_ref[:, sl]], axis=0)
        v = jnp.concatenate([vp_ref[:, sl], vc_ref[:, sl], vn_ref[:, sl]], axis=0)
        q = jnp.concatenate([q_ref[:, (j * G + g) * LANES:(j * G + g + 1) * LANES] for g in range(G)], axis=0)
        s = lax.dot_general(q, k, NT_DIMS, preferred_element_type=F32)
        s = jnp.where(valid, s, NEG_INF)
        sink = jnp.zeros((R, 1), F32)
        for g in range(G):
            sink = jnp.where(grp == g, sink_ref[0, j * G + g], sink)
        m = jnp.maximum(jnp.max(s, axis=-1, keepdims=True), sink)
        p = jnp.exp(s - m)
        denom = jnp.sum(p, axis=-1, keepdims=True) + jnp.exp(sink - m)
        o = _dot(p.astype(BF16), v) / denom
        for g in range(G):
            o_ref[:, (j * G + g) * LANES:(j * G + g + 1) * LANES] = o[g * BLK:(g + 1) * BLK].astype(o_ref.dtype)


def _swa_attention(z, sink):
    B, L, _ = z.shape
    BLK = ATTN_BLOCK
    nb = L // BLK
    G = SWA_HQ // SWA_HKV
    qw = SWA_HQ * LANES
    kvw = SWA_HKV * LANES
    kcol = qw // kvw
    prev = lambda b, i: (b, jnp.maximum(i - 1, 0), kcol)
    cur = lambda b, i: (b, i, kcol)
    nxt = lambda b, i: (b, jnp.minimum(i + 1, nb - 1), kcol)
    vprev = lambda b, i: (b, jnp.maximum(i - 1, 0), kcol + 1)
    vcur = lambda b, i: (b, i, kcol + 1)
    vnxt = lambda b, i: (b, jnp.minimum(i + 1, nb - 1), kcol + 1)
    kv = lambda f: pl.BlockSpec((None, BLK, kvw), f)
    return pl.pallas_call(
        functools.partial(_swa_kernel, HKV=SWA_HKV, G=G, BLK=BLK),
        grid=(B, nb),
        in_specs=[pl.BlockSpec(memory_space=pltpu.SMEM),
                  pl.BlockSpec((None, BLK, qw), lambda b, i: (b, i, 0)),
                  kv(prev), kv(cur), kv(nxt), kv(vprev), kv(vcur), kv(vnxt)],
        out_specs=pl.BlockSpec((None, BLK, qw), lambda b, i: (b, i, 0)),
        out_shape=jax.ShapeDtypeStruct((B, L, qw), BF16),
        compiler_params=_cparams(("parallel", "parallel"),
                                 [((BLK, qw), BF16)] * 2 + [((BLK, kvw), BF16)] * 6,
                                 [((G * BLK, 3 * BLK), F32)] * 4),
        name="swa_attention",
    )(sink.reshape(1, SWA_HQ), z, z, z, z, z, z, z)


def _swa_layout(w_in, q_gain, k_gain, w_out, L):
    D = w_in.shape[0]
    dh, hq, hkv = SWA_DH, SWA_HQ, SWA_HKV
    hf = dh // 2
    q_end, k_end = hq * dh, (hq + hkv) * dh

    def rot_slots(w, n):
        w = w.reshape(D, n, 2, hf)
        return jnp.pad(w, ((0, 0), (0, 0), (0, 0), (0, LANES // 2 - hf))).reshape(D, n * LANES)

    def val_slots(w, n):
        return jnp.pad(w.reshape(D, n, dh), ((0, 0), (0, 0), (0, LANES - dh))).reshape(D, n * LANES)

    w = jnp.concatenate([rot_slots(w_in[:, :q_end], hq), rot_slots(w_in[:, q_end:k_end], hkv),
                         val_slots(w_in[:, k_end:], hkv)], axis=1)

    def gain_slot(g):
        return jnp.pad(g.reshape(2, hf), ((0, 0), (0, LANES // 2 - hf))).reshape(1, LANES)

    flag = jnp.concatenate([jnp.ones(((hq + hkv) * LANES,), F32), jnp.zeros((hkv * LANES,), F32)]).reshape(1, -1)
    wo = jnp.pad(w_out.reshape(hq, dh, -1), ((0, 0), (0, LANES - dh), (0, 0))).reshape(hq * LANES, -1)

    inv = ROPE_THETA ** (-np.arange(0, dh, 2) / dh)
    ang = np.arange(L)[:, None] * inv[None, :]
    zero = np.zeros((L, LANES // 2 - hf))
    cos = jnp.asarray(np.concatenate([np.cos(ang), zero, np.cos(ang), zero], axis=1), F32)
    sin = jnp.asarray(np.concatenate([-np.sin(ang), zero, np.sin(ang), zero], axis=1), F32)
    gq = gain_slot(q_gain) * (dh ** -0.5)
    gk = gain_slot(k_gain)
    roll = lambda g: jnp.roll(g, LANES // 2, axis=1)
    cos3 = jnp.stack([cos * gq, cos * gk, jnp.ones_like(cos)])
    sin3 = jnp.stack([sin * roll(gq), sin * roll(gk), jnp.zeros_like(sin)])
    return w, flag, wo, cos3, sin3


def _hg_decays(f_ref, q_ref, rows, lb, tri, total_row, mid_row):
    f = lb + (1.0 - lb) * jax.nn.sigmoid(f_ref[rows, :].astype(F32))
    gl = jnp.log(f)
    cum = _cumdot(tri, gl)
    total = cum[total_row:total_row + 1, :]
    mid = cum[mid_row:mid_row + 1, :]
    q_dec = _silu(q_ref[rows, :].astype(F32)) * jnp.exp(cum - mid)
    k_inv = (1.0 - f) * jnp.exp(mid - cum)
    q_full = (q_dec * jnp.exp(mid)).astype(BF16)
    k_end = (k_inv * jnp.exp(total - mid)).astype(BF16)
    return q_dec.astype(BF16), k_inv.astype(BF16), q_full, k_end, jnp.exp(total)


def _hg_fwd_kernel(lb_ref, q_ref, i_ref, f_ref, o_ref, st_ref, *, T, C, H, DK, DV):
    @pl.when(pl.program_id(1) == 0)
    def _():
        st_ref[...] = jnp.zeros_like(st_ref)

    row = lax.broadcasted_iota(jnp.int32, (C, C), 0)
    col = lax.broadcasted_iota(jnp.int32, (C, C), 1)
    causal = row >= col
    tri = jnp.concatenate([causal.astype(BF16)] * 2, axis=1)
    lb = lb_ref[...]

    def chunk(c, carry):
        rows = pl.ds(pl.multiple_of(c * C, C), C)
        q_dec, k_inv, q_full, k_end, s_decay = _hg_decays(f_ref, q_ref, rows, lb, tri, C - 1, C // 2 - 1)
        v = i_ref[rows, :]
        for h in range(H):
            ck = slice(h * DK, (h + 1) * DK)
            cv = slice(h * DV, (h + 1) * DV)
            s = lax.dot_general(q_dec[:, ck], k_inv[:, ck], NT_DIMS, preferred_element_type=F32)
            s = jnp.where(causal, s, 0.0)
            st = st_ref[h]
            o_ref[rows, cv] = _dot(s.astype(BF16), v[:, cv]) + lax.dot_general(
                q_full[:, ck], st.astype(BF16), NT_DIMS, preferred_element_type=F32)
            st_ref[h] = st * s_decay[:, ck] + lax.dot_general(v[:, cv], k_end[:, ck], TN_DIMS,
                                                              preferred_element_type=F32)
        return carry

    lax.fori_loop(0, T // C, chunk, 0)


def _hg_bwd_kernel(lb_ref, gain_ref, q_ref, i_ref, f_ref, gate_ref, o1_ref, w_ref, x_ref, xgate_ref,
                   o_ref, st_ref, a_ref, *, T, C, H, DK, DV):
    @pl.when(pl.program_id(1) == 0)
    def _():
        st_ref[...] = jnp.zeros_like(st_ref)

    row = lax.broadcasted_iota(jnp.int32, (C, C), 0)
    col = lax.broadcasted_iota(jnp.int32, (C, C), 1)
    anti = col >= row
    tri = jnp.concatenate([anti.astype(BF16)] * 2, axis=1)
    lb = lb_ref[...]
    gain = gain_ref[...]
    nc = T // C

    def chunk(cc, carry):
        rows = pl.ds(pl.multiple_of((nc - 1 - cc) * C, C), C)
        q_dec, k_inv, q_full, k_end, s_decay = _hg_decays(f_ref, q_ref, rows, lb, tri, 0, C // 2)
        v = i_ref[rows, :]
        for h in range(H):
            ck = slice(h * DK, (h + 1) * DK)
            cv = slice(h * DV, (h + 1) * DV)
            s = lax.dot_general(q_dec[:, ck], k_inv[:, ck], NT_DIMS, preferred_element_type=F32)
            s = jnp.where(anti, s, 0.0)
            st = st_ref[h]
            o = o1_ref[rows, cv] + _dot(s.astype(BF16), v[:, cv]) + lax.dot_general(
                q_full[:, ck], st.astype(BF16), NT_DIMS, preferred_element_type=F32)
            y = o * lax.rsqrt(jnp.mean(o * o, axis=-1, keepdims=True) + NORM_EPS) * gain
            a_ref[rows, cv] = (y * _silu(gate_ref[rows, cv].astype(F32))).astype(a_ref.dtype)
            st_ref[h] = st * s_decay[:, ck] + lax.dot_general(v[:, cv], k_end[:, ck], TN_DIMS,
                                                              preferred_element_type=F32)
        return carry

    lax.fori_loop(0, nc, chunk, 0)
    o_ref[...] = x_ref[...] + xgate_ref[...] * _dot(a_ref[...], w_ref[...])


def _hgrn(z, lb, gain, w_out, x, xgate):
    B, L, _ = z.shape
    D = w_out.shape[1]
    H, C = HG_HEADS, HG_CHUNK
    HD = lb.shape[1]
    DK = HD // H
    DV = gain.shape[0]
    T = min(L, 512)
    nT = L // T
    kw = dict(T=T, C=C, H=H, DK=DK, DV=DV)
    blocks = [((T, HD), BF16)] * 5 + [((T, H * DV), F32)] * 2
    state = [((H, DV, DK), F32)] + [((C, HD), F32)] * 12
    lb3 = lb.reshape(2, 1, HD)
    o1 = pl.pallas_call(
        functools.partial(_hg_fwd_kernel, **kw),
        grid=(B, nT),
        in_specs=[pl.BlockSpec((None, 1, HD), lambda b, i: (0, 0, 0)),
                  pl.BlockSpec((None, T, HD), lambda b, i: (b, i, 0)),
                  pl.BlockSpec((None, T, H * DV), lambda b, i: (b, i, 1)),
                  pl.BlockSpec((None, T, HD), lambda b, i: (b, i, 2))],
        out_specs=pl.BlockSpec((None, T, H * DV), lambda b, i: (b, i, 0)),
        out_shape=jax.ShapeDtypeStruct((B, L, H * DV), F32),
        scratch_shapes=[pltpu.VMEM((H, DV, DK), F32)],
        compiler_params=_cparams(("parallel", "arbitrary"), blocks, state),
        name="hgrn_fwd",
    )(lb3, z, z, z)
    return pl.pallas_call(
        functools.partial(_hg_bwd_kernel, **kw),
        grid=(B, nT),
        in_specs=[pl.BlockSpec((None, 1, HD), lambda b, i: (1, 0, 0)),
                  pl.BlockSpec((1, DV), lambda b, i: (0, 0)),
                  pl.BlockSpec((None, T, HD), lambda b, i: (b, nT - 1 - i, 0)),
                  pl.BlockSpec((None, T, H * DV), lambda b, i: (b, nT - 1 - i, 1)),
                  pl.BlockSpec((None, T, HD), lambda b, i: (b, nT - 1 - i, 3)),
                  pl.BlockSpec((None, T, H * DV), lambda b, i: (b, nT - 1 - i, 4)),
                  pl.BlockSpec((None, T, H * DV), lambda b, i: (b, nT - 1 - i, 0)),
                  pl.BlockSpec((H * DV, D), lambda b, i: (0, 0), pipeline_mode=pl.Buffered(1)),
                  pl.BlockSpec((None, T, D), lambda b, i: (b, nT - 1 - i, 0)),
                  pl.BlockSpec((None, 1, D), lambda b, i: (b, 0, 0))],
        out_specs=pl.BlockSpec((None, T, D), lambda b, i: (b, nT - 1 - i, 0)),
        out_shape=jax.ShapeDtypeStruct((B, L, D), F32),
        scratch_shapes=[pltpu.VMEM((H, DV, DK), F32), pltpu.VMEM((T, H * DV), BF16)],
        compiler_params=_cparams(("parallel", "arbitrary"), blocks + [((T, D), F32)],
                                 state + [((H * DV, D), BF16), ((T, H * DV), BF16), ((T, D), F32)]),
        name="hgrn_bwd",
    )(lb3, gain.reshape(1, DV), z, z, z, z, o1, w_out, x, xgate)


def _ret_rope_tables(L, dk):
    inv = ROPE_THETA ** (-np.arange(0, dk, 2) / dk)
    ang = np.arange(L)[:, None] * inv[None, :]
    cos = np.stack([np.cos(ang), np.cos(ang) * dk ** -0.5, np.ones_like(ang)])
    sin = np.stack([np.sin(ang), np.sin(ang) * dk ** -0.5, np.zeros_like(ang)])
    return jnp.asarray(cos, F32), jnp.asarray(sin, F32)


def kernel(x_prompt, x_sample, c_prompt, c_sample, ada_w, ada_b, norm_g, hy_w_in, hy_conv_w, hy_conv_b, hy_w1, hy_b1, hy_w2, hy_b2, hy_w3, hy_freq, hy_decay, hy_skip, hy_w_out, ret_w_in, ret_decay, ret_w_out, swa_w_in, swa_q_gain, swa_k_gain, swa_sink, swa_w_out, hg_w_in, hg_lb, hg_gain, hg_w_out, ffn_w_gate, ffn_w_val, ffn_conv_w, ffn_conv_b, ffn_w_down):
    depth, D = norm_g.shape[0], norm_g.shape[2]
    groups = [(x_prompt, c_prompt), (x_sample, c_sample)]
    mods = _ada_mod(jnp.concatenate([c for _, c in groups], axis=0), ada_w, ada_b)

    bf = lambda w: w.astype(BF16)
    hy_w_in_b, hy_w_out_b = bf(hy_w_in), bf(hy_w_out)
    ret_w_in_b, ret_w_out_b = bf(ret_w_in), bf(ret_w_out)
    hg_w_in_b, hg_w_out_b = bf(hg_w_in), bf(hg_w_out)
    wg_b, wv_b, wd_b = bf(ffn_w_gate), bf(ffn_w_val), bf(ffn_w_down)
    hg_sm = jax.nn.softmax(hg_lb.astype(F32), axis=1)
    hg_lower = jnp.cumsum(hg_sm, axis=1) - hg_sm
    ret_log_gamma = -jnp.exp(ret_decay.astype(F32))

    outs = []
    row0 = 0
    for x, c in groups:
        B, L, _ = x.shape
        for layer in range(depth):
            kind, j = layer % N_MIXERS, layer // N_MIXERS
            mod = mods[layer, row0:row0 + B].reshape(B, N_MOD, 1, D)
            sh1, sc1, g1, sh2, sc2, g2 = (mod[:, m] for m in range(N_MOD))
            gn1 = norm_g[layer, 0].reshape(1, D)
            gn2 = norm_g[layer, 1].reshape(1, D)
            if kind == 0:
                tabs = _fft_tables(L)
                z = _proj_in_conv(x, gn1, sc1, sh1, hy_w_in_b[j], hy_conv_w[j], hy_conv_b[j], tn=1024)
                taps, asum = _hy_filter_taps(L, hy_w1[j], hy_b1[j], hy_w2[j], hy_b2[j], hy_w3[j], hy_freq[j],
                                             hy_decay[j])
                spec = _hy_spectrum(taps, asum, tabs)
                a = _hy_conv(z, hy_skip[j], spec, tabs)
                x = _proj_out(a, hy_w_out_b[j], x, g1, mult=z)
            elif kind == 1:
                dk = D // RET_HEADS
                cos, sin = _ret_rope_tables(L, dk)
                z = _proj_in_halfrope(x, gn1, sc1, sh1, ret_w_in_b[j], cos, sin, tn=RET_HEADS * dk, rope_tiles=2,
                                      dk=dk)
                x = _retention(z, ret_log_gamma[j], ret_w_out_b[j], x, g1)
            elif kind == 2:
                w, flag, wo, cos, sin = _swa_layout(swa_w_in[j], swa_q_gain[j], swa_k_gain[j], swa_w_out[j], L)
                z = _proj_in_rope(x, gn1, sc1, sh1, bf(w), flag, cos, sin, tn=SWA_HKV * LANES, dh=SWA_DH,
                                  q_tiles=SWA_HQ // SWA_HKV)
                a = _swa_attention(z, swa_sink[j])
                x = _proj_out(a, bf(wo), x, g1)
            else:
                z = _proj_in(x, gn1, sc1, sh1, hg_w_in_b[j], tn=1024)
                x = _hgrn(z, hg_lower[:, layer], hg_gain[j], hg_w_out_b[j], x, g1)
            x = _ffn(x, gn2, sc2, sh2, g2, wg_b[layer], wv_b[layer], ffn_conv_w[layer], ffn_conv_b[layer],
                     wd_b[layer])
        outs.append(x)
        row0 += B
    return tuple(outs)
```

```python
import functools
import math

import numpy as np
import jax
import jax.numpy as jnp
from jax import lax
from jax.experimental import pallas as pl
from jax.experimental.pallas import tpu as pltpu

F32 = jnp.float32
BF16 = jnp.bfloat16

NORM_EPS = 1e-6
N_MIXERS = 4
N_MOD = 6
HY_BANDS = 16
RET_HEADS = 4
RET_CHUNK = 128
SWA_HQ = 16
SWA_HKV = 4
SWA_DH = 64
WINDOW = 128
ATTN_BLOCK = 128
ROPE_THETA = 10000.0
NEG_INF = -1e30
HG_HEADS = 8
HG_CHUNK = 128

LANES = 128
SUBLANES = 8
BF16_ROWS = 16
VMEM_LIMIT_CAP = 60 * 1024 * 1024
VMEM_SLACK = 8 * 1024 * 1024

NT_DIMS = (((1,), (1,)), ((), ()))
TN_DIMS = (((0,), (0,)), ((), ()))


def _nbytes(shape, dtype):
    item = jnp.dtype(dtype).itemsize
    sub = SUBLANES * 4 // item
    dims = list(shape)
    dims[-1] = -(-dims[-1] // LANES) * LANES
    if len(dims) > 1:
        dims[-2] = -(-dims[-2] // sub) * sub
    return int(np.prod(dims)) * item


def _cparams(semantics, pipelined, resident):
    need = 2 * sum(_nbytes(s, d) for s, d in pipelined) + sum(_nbytes(s, d) for s, d in resident)
    return pltpu.CompilerParams(dimension_semantics=semantics,
                                vmem_limit_bytes=min(need + VMEM_SLACK, VMEM_LIMIT_CAP))


def _dot(a, b):
    return jnp.dot(a, b, preferred_element_type=F32)


def _split(x):
    hi = x.astype(BF16)
    lo = (x - hi.astype(F32)).astype(BF16)
    return hi, lo


def _dot_tab(t_cat, d):
    d_hi, d_lo = _split(d)
    return _dot(t_cat, jnp.concatenate([d_hi, d_lo], axis=0))


def _dot3(a, b):
    a_hi, a_lo = _split(a)
    b_hi, b_lo = _split(b)
    return _dot(a_hi, b_hi) + _dot(a_lo, b_hi) + _dot(a_hi, b_lo)


def _cumdot(tri2, g):
    g_hi, g_lo = _split(g)
    return _dot(tri2, jnp.concatenate([g_hi, g_lo], axis=0))


def _silu(x):
    return x * jax.nn.sigmoid(x)


def _norm_mod(x, g, sc, sh):
    xf = x.astype(F32)
    y = xf * lax.rsqrt(jnp.mean(xf * xf, axis=-1, keepdims=True) + NORM_EPS)
    return (y * g) * (1.0 + sc) + sh


def _row_tile(L):
    return min(L, 1024)


def _ada_kernel(c_ref, w_ref, b_ref, o_ref):
    cs = _silu(c_ref[...]).astype(BF16)
    o_ref[...] = _dot(cs, w_ref[...].astype(BF16)) + b_ref[...]


def _ada_mod(c_all, ada_w, ada_b):
    depth, D, N = ada_w.shape
    R = c_all.shape[0]
    tn = 1024
    return pl.pallas_call(
        _ada_kernel,
        grid=(depth, N // tn),
        in_specs=[pl.BlockSpec((R, D), lambda l, j: (0, 0)),
                  pl.BlockSpec((None, D, tn), lambda l, j: (l, 0, j)),
                  pl.BlockSpec((None, 1, tn), lambda l, j: (l, 0, j))],
        out_specs=pl.BlockSpec((None, R, tn), lambda l, j: (l, 0, j)),
        out_shape=jax.ShapeDtypeStruct((depth, R, N), F32),
        compiler_params=_cparams(("parallel", "parallel"),
                                 [((D, tn), F32), ((R, D), F32), ((R, tn), F32)], []),
        name="ada_mod",
    )(c_all, ada_w, ada_b.reshape(depth, 1, N))


def _fill_h(h_ref, x_ref, g_ref, sc_ref, sh_ref):
    h_ref[...] = _norm_mod(x_ref[...], g_ref[...], sc_ref[...], sh_ref[...]).astype(BF16)


def _fill_h_halo(h_ref, x_ref, xp_ref, xn_ref, g_ref, sc_ref, sh_ref, tm):
    i = pl.program_id(1)
    last = pl.num_programs(1) - 1
    g, sc, sh = g_ref[...], sc_ref[...], sh_ref[...]
    h_ref[BF16_ROWS:BF16_ROWS + tm, :] = _norm_mod(x_ref[...], g, sc, sh).astype(BF16)
    hp = _norm_mod(xp_ref[...], g, sc, sh)
    hn = _norm_mod(xn_ref[...], g, sc, sh)
    h_ref[0:BF16_ROWS, :] = jnp.where(i > 0, hp, 0.0).astype(BF16)
    h_ref[BF16_ROWS + tm:, :] = jnp.where(i < last, hn, 0.0).astype(BF16)


def _conv3_rows(z_ref, cw, cb, tm):
    o = BF16_ROWS
    return (z_ref[o - 1:o - 1 + tm, :] * cw[0:1, :] + z_ref[o:o + tm, :] * cw[1:2, :]
            + z_ref[o + 1:o + 1 + tm, :] * cw[2:3, :] + cb)


def _proj_plain_kernel(x_ref, g_ref, sc_ref, sh_ref, w_ref, o_ref, h_ref):
    @pl.when(pl.program_id(2) == 0)
    def _():
        _fill_h(h_ref, x_ref, g_ref, sc_ref, sh_ref)

    o_ref[...] = _dot(h_ref[...], w_ref[...]).astype(o_ref.dtype)


def _proj_halfrope_kernel(x_ref, g_ref, sc_ref, sh_ref, w_ref, cos_ref, sin_ref, o_ref, h_ref, *, dk):
    @pl.when(pl.program_id(2) == 0)
    def _():
        _fill_h(h_ref, x_ref, g_ref, sc_ref, sh_ref)

    z = _dot(h_ref[...], w_ref[...])
    cos, sin = cos_ref[...], sin_ref[...]
    half = dk // 2
    for c0 in range(0, z.shape[1], dk):
        x1, x2 = z[:, c0:c0 + half], z[:, c0 + half:c0 + dk]
        o_ref[:, c0:c0 + half] = (x1 * cos - x2 * sin).astype(o_ref.dtype)
        o_ref[:, c0 + half:c0 + dk] = (x1 * sin + x2 * cos).astype(o_ref.dtype)


def _proj_conv_kernel(x_ref, xp_ref, xn_ref, g_ref, sc_ref, sh_ref, w_ref, cw_ref, cb_ref,
                      o_ref, h_ref, z_ref, *, tm):
    @pl.when(pl.program_id(2) == 0)
    def _():
        _fill_h_halo(h_ref, x_ref, xp_ref, xn_ref, g_ref, sc_ref, sh_ref, tm)

    z = _dot(h_ref[...], w_ref[...])
    for c in range(o_ref.shape[0]):
        cols = slice(c * LANES, (c + 1) * LANES)
        z_ref[c] = z[:, cols]
        o_ref[c] = _conv3_rows(z_ref.at[c], cw_ref[:, cols], cb_ref[:, cols], tm).astype(o_ref.dtype)


def _proj_rope_kernel(x_ref, g_ref, sc_ref, sh_ref, w_ref, flag_ref, ones_ref, cos_ref, sin_ref,
                      o_ref, h_ref, *, dh):
    @pl.when(pl.program_id(2) == 0)
    def _():
        _fill_h(h_ref, x_ref, g_ref, sc_ref, sh_ref)

    z = _dot(h_ref[...], w_ref[...])
    cos, sin = cos_ref[...], sin_ref[...]
    ones = ones_ref[...]
    for s in range(z.shape[1] // LANES):
        sl = slice(s * LANES, (s + 1) * LANES)
        zs = z[:, sl]
        ms = _dot((zs * zs).astype(BF16), ones) * (1.0 / dh)
        zs = zs * jnp.where(flag_ref[:, sl] > 0.0, lax.rsqrt(ms + NORM_EPS), 1.0)
        o_ref[:, sl] = (zs * cos + pltpu.roll(zs, LANES // 2, axis=1) * sin).astype(o_ref.dtype)


def _mod_specs(D):
    return [pl.BlockSpec((1, D), lambda b, i, j: (0, 0)),
            pl.BlockSpec((None, 1, D), lambda b, i, j: (b, 0, 0)),
            pl.BlockSpec((None, 1, D), lambda b, i, j: (b, 0, 0))]


def _halo_specs(tm, D, L):
    hb = tm // BF16_ROWS
    nhb = L // BF16_ROWS
    return [pl.BlockSpec((None, tm, D), lambda b, i, j: (b, i, 0)),
            pl.BlockSpec((None, BF16_ROWS, D), lambda b, i, j: (b, jnp.maximum(i * hb - 1, 0), 0)),
            pl.BlockSpec((None, BF16_ROWS, D), lambda b, i, j: (b, jnp.minimum((i + 1) * hb, nhb - 1), 0))]


def _proj_in(x, g, sc, sh, w, *, tn):
    B, L, D = x.shape
    N = w.shape[1]
    tm = _row_tile(L)
    return pl.pallas_call(
        _proj_plain_kernel,
        grid=(B, L // tm, N // tn),
        in_specs=[pl.BlockSpec((None, tm, D), lambda b, i, j: (b, i, 0))] + _mod_specs(D)
        + [pl.BlockSpec((D, tn), lambda b, i, j: (0, j))],
        out_specs=pl.BlockSpec((None, tm, tn), lambda b, i, j: (b, i, j)),
        out_shape=jax.ShapeDtypeStruct((B, L, N), BF16),
        scratch_shapes=[pltpu.VMEM((tm, D), BF16)],
        compiler_params=_cparams(("parallel", "parallel", "arbitrary"),
                                 [((tm, D), F32), ((D, tn), BF16), ((tm, tn), BF16)],
                                 [((tm, D), BF16), ((tm, tn), F32)]),
        name="proj_in",
    )(x, g, sc, sh, w)


def _proj_in_halfrope(x, g, sc, sh, w, cos, sin, *, tn, rope_tiles, dk):
    B, L, D = x.shape
    N = w.shape[1]
    tm = _row_tile(L)
    half = dk // 2
    table = lambda b, i, j: (jnp.minimum(j, rope_tiles), i, 0)
    return pl.pallas_call(
        functools.partial(_proj_halfrope_kernel, dk=dk),
        grid=(B, L // tm, N // tn),
        in_specs=[pl.BlockSpec((None, tm, D), lambda b, i, j: (b, i, 0))] + _mod_specs(D)
        + [pl.BlockSpec((D, tn), lambda b, i, j: (0, j)),
           pl.BlockSpec((None, tm, half), table),
           pl.BlockSpec((None, tm, half), table)],
        out_specs=pl.BlockSpec((None, tm, tn), lambda b, i, j: (b, i, j)),
        out_shape=jax.ShapeDtypeStruct((B, L, N), BF16),
        scratch_shapes=[pltpu.VMEM((tm, D), BF16)],
        compiler_params=_cparams(("parallel", "parallel", "arbitrary"),
                                 [((tm, D), F32), ((D, tn), BF16), ((tm, tn), BF16), ((tm, half), F32),
                                  ((tm, half), F32)],
                                 [((tm, D), BF16), ((tm, tn), F32), ((tm, tn), F32)]),
        name="proj_in_halfrope",
    )(x, g, sc, sh, w, cos, sin)


def _proj_in_conv(x, g, sc, sh, w, cw, cb, *, tn):
    B, L, D = x.shape
    N = w.shape[1]
    tm = _row_tile(L)
    te = tm + 2 * BF16_ROWS
    return pl.pallas_call(
        functools.partial(_proj_conv_kernel, tm=tm),
        grid=(B, L // tm, N // tn),
        in_specs=_halo_specs(tm, D, L) + _mod_specs(D)
        + [pl.BlockSpec((D, tn), lambda b, i, j: (0, j)),
           pl.BlockSpec((3, tn), lambda b, i, j: (0, j)),
           pl.BlockSpec((1, tn), lambda b, i, j: (0, j))],
        out_specs=pl.BlockSpec((None, tn // LANES, tm, LANES), lambda b, i, j: (b, j, i, 0)),
        out_shape=jax.ShapeDtypeStruct((B, N // LANES, L, LANES), BF16),
        scratch_shapes=[pltpu.VMEM((te, D), BF16), pltpu.VMEM((tn // LANES, te, LANES), F32)],
        compiler_params=_cparams(("parallel", "parallel", "arbitrary"),
                                 [((tm, D), F32), ((D, tn), BF16), ((tm, tn), BF16)],
                                 [((te, D), BF16), ((te, tn), F32), ((te, tn), F32)]),
        name="proj_in_conv",
    )(x, x, x, g, sc, sh, w, cw, cb.reshape(1, N))


def _proj_in_rope(x, g, sc, sh, w, flag, cos, sin, *, tn, dh, q_tiles):
    B, L, D = x.shape
    N = w.shape[1]
    tm = _row_tile(L)
    table = lambda b, i, j: (jnp.maximum(j - (q_tiles - 1), 0), i, 0)
    return pl.pallas_call(
        functools.partial(_proj_rope_kernel, dh=dh),
        grid=(B, L // tm, N // tn),
        in_specs=[pl.BlockSpec((None, tm, D), lambda b, i, j: (b, i, 0))] + _mod_specs(D)
        + [pl.BlockSpec((D, tn), lambda b, i, j: (0, j)),
           pl.BlockSpec((1, tn), lambda b, i, j: (0, j)),
           pl.BlockSpec((LANES, LANES), lambda b, i, j: (0, 0)),
           pl.BlockSpec((None, tm, LANES), table),
           pl.BlockSpec((None, tm, LANES), table)],
        out_specs=pl.BlockSpec((None, tm, tn), lambda b, i, j: (b, i, j)),
        out_shape=jax.ShapeDtypeStruct((B, L, N), BF16),
        scratch_shapes=[pltpu.VMEM((tm, D), BF16)],
        compiler_params=_cparams(("parallel", "parallel", "arbitrary"),
                                 [((tm, D), F32), ((D, tn), BF16), ((tm, tn), BF16), ((tm, LANES), F32),
                                  ((tm, LANES), F32), ((LANES, LANES), BF16)],
                                 [((tm, D), BF16), ((tm, tn), F32), ((tm, tn), F32), ((tm, tn), F32)]),
        name="proj_in_rope",
    )(x, g, sc, sh, w, flag, jnp.ones((LANES, LANES), BF16), cos, sin)


def _proj_out_kernel(a_ref, w_ref, x_ref, gate_ref, o_ref):
    o_ref[...] = x_ref[...] + gate_ref[...] * _dot(a_ref[...], w_ref[...])


def _proj_out_gated_kernel(a_ref, m_ref, w_ref, x_ref, gate_ref, o_ref):
    a = jnp.concatenate([(a_ref[c].astype(F32) * m_ref[c].astype(F32)).astype(BF16)
                         for c in range(a_ref.shape[0])], axis=1)
    o_ref[...] = x_ref[...] + gate_ref[...] * _dot(a, w_ref[...])


def _proj_out(a, w, x, gate, mult=None):
    blocked = a.ndim == 4
    B, L = x.shape[:2]
    K, D = w.shape
    tm = _row_tile(L)
    if blocked:
        a_specs = [pl.BlockSpec((None, K // LANES, tm, LANES), lambda b, i: (b, 0, i, 0))] * 2
        operands = (a, mult)
    else:
        a_specs = [pl.BlockSpec((None, tm, K), lambda b, i: (b, i, 0))]
        operands = (a,)
    return pl.pallas_call(
        _proj_out_gated_kernel if blocked else _proj_out_kernel,
        grid=(B, L // tm),
        in_specs=a_specs + [
                  pl.BlockSpec((K, D), lambda b, i: (0, 0)),
                  pl.BlockSpec((None, tm, D), lambda b, i: (b, i, 0)),
                  pl.BlockSpec((None, 1, D), lambda b, i: (b, 0, 0))],
        out_specs=pl.BlockSpec((None, tm, D), lambda b, i: (b, i, 0)),
        out_shape=jax.ShapeDtypeStruct((B, L, D), F32),
        compiler_params=_cparams(("parallel", "parallel"),
                                 [((tm, K), BF16)] * len(operands)
                                 + [((K, D), BF16), ((tm, D), F32), ((tm, D), F32)],
                                 [((tm, D), F32), ((tm, K), F32)]),
        name="proj_out",
    )(*operands, w, x, gate)


def _ffn_kernel(x_ref, xp_ref, xn_ref, g_ref, sc_ref, sh_ref, gate_ref, wg_ref, wv_ref, cw_ref, cb_ref,
                wd_ref, o_ref, h_ref, z_ref, *, tm, tf):
    _fill_h_halo(h_ref, x_ref, xp_ref, xn_ref, g_ref, sc_ref, sh_ref, tm)
    pieces = [slice(c * tf, (c + 1) * tf) for c in range(wg_ref.shape[1] // tf)]
    vals = []
    for c, cols in enumerate(pieces):
        z_ref[c] = _dot(h_ref[...], wg_ref[:, cols])
        vals.append(_dot(h_ref[BF16_ROWS:BF16_ROWS + tm, :], wv_ref[:, cols]))
    acc = None
    for c, cols in enumerate(pieces):
        a = _conv3_rows(z_ref.at[c], cw_ref[:, cols], cb_ref[:, cols], tm)
        part = _dot((_silu(a) * vals[c]).astype(BF16), wd_ref[cols, :])
        acc = part if acc is None else acc + part
    o_ref[...] = x_ref[...] + gate_ref[...] * acc


def _ffn(x, g, sc, sh, gate, wg, wv, cw, cb, wd):
    B, L, D = x.shape
    F = wg.shape[1]
    tm = min(L, 512)
    te = tm + 2 * BF16_ROWS
    tf = F // 2
    const = lambda b, i, j: (0, 0)
    resident = lambda shape: pl.BlockSpec(shape, const, pipeline_mode=pl.Buffered(1))
    return pl.pallas_call(
        functools.partial(_ffn_kernel, tm=tm, tf=tf),
        grid=(B, L // tm, 1),
        in_specs=_halo_specs(tm, D, L) + _mod_specs(D)
        + [pl.BlockSpec((None, 1, D), lambda b, i, j: (b, 0, 0)),
           resident((D, F)), resident((D, F)), resident((3, F)), resident((1, F)), resident((F, D))],
        out_specs=pl.BlockSpec((None, tm, D), lambda b, i, j: (b, i, 0)),
        out_shape=jax.ShapeDtypeStruct((B, L, D), F32),
        scratch_shapes=[pltpu.VMEM((te, D), BF16), pltpu.VMEM((F // tf, te, tf), F32)],
        compiler_params=_cparams(("parallel", "parallel", "arbitrary"),
                                 [((tm, D), F32), ((tm, D), F32)],
                                 [((D, F), BF16), ((D, F), BF16), ((F, D), BF16), ((te, D), BF16),
                                  ((te, F), F32), ((tm, F), F32), ((tm, tf), F32), ((tm, tf), BF16),
                                  ((tm, D), F32), ((tm, D), F32)]),
        name="ffn",
    )(x, x, x, g, sc, sh, gate, wg, wv, cw, cb.reshape(1, F), wd)


def _fft_dims(L):
    N = 2 * L
    p = N.bit_length() - 1
    n1 = 1 << ((p + 1) // 2)
    return N, n1, N // n1


def _k1_pad(n1):
    return n1 // 2 + 8


def _slab_rows(k1p):
    return 2 * k1p + SUBLANES


def _fft_tables(L):
    N, n1, n2 = _fft_dims(L)
    k1p = _k1_pad(n1)
    k1 = np.arange(k1p)
    valid = (k1 <= n1 // 2).astype(np.float64)
    weight = np.where((k1 == 0) | (k1 == n1 // 2), 1.0, 2.0) * valid
    m1 = np.arange(n1 // 2)
    n2v = np.arange(n2)
    n_idx = n2 * m1[None, :] + n2v[:, None]
    phi = 2.0 * np.pi * k1[None, :, None] * n_idx[:, None, :] / N
    s1 = np.concatenate([np.cos(phi) * valid[None, :, None], -np.sin(phi) * valid[None, :, None]], axis=1)
    phit = np.transpose(phi, (0, 2, 1))
    gl = np.concatenate([np.cos(phit) * weight[None, None, :], -np.sin(phit) * weight[None, None, :]],
                        axis=2) / N
    th = 2.0 * np.pi * np.outer(n2v, n2v) / n2
    c, s = np.cos(th), np.sin(th)
    f2 = np.block([[c, s], [-s, c]])
    f2c = np.block([[c, -s], [s, c]])

    def cat(a):
        t = jnp.asarray(a, F32).astype(BF16)
        return jnp.concatenate([t, t], axis=-1)

    return dict(N=N, n1=n1, n2=n2, k1p=k1p, s1=cat(s1), gl=cat(gl), f2=cat(f2), f2c=cat(f2c))


def _hy_positions(L):
    idx = np.concatenate([np.arange(L), np.array([0]), np.arange(L - 1, 0, -1)])
    t = idx / (L - 1)
    ang = (2.0 * np.pi / L) * idx
    bands = np.linspace(1e-4, HY_BANDS - 1, HY_BANDS)
    z = np.concatenate([t[:, None], np.cos(bands[None, :] * ang[:, None]), -np.sin(bands[None, :] * ang[:, None])],
                       axis=1)
    zp = np.zeros((2 * L, LANES))
    zp[:, : z.shape[1]] = z
    return jnp.asarray(zp, F32)


def _hy_mlp_kernel(z_ref, w1_ref, b1_ref, w2_ref, b2_ref, w3_ref, fr_ref, dec_ref, taps_ref, asum_ref, *, L, tr):
    i = pl.program_id(0)
    z = z_ref[...]
    fr = fr_ref[...]
    h = jnp.sin(fr * (_dot3(z, w1_ref[...]) + b1_ref[...]))
    h = jnp.sin(fr * (_dot3(h, w2_ref[...]) + b2_ref[...]))
    h = _dot3(h, w3_ref[...])
    t = z[:, 0:1]
    taps = h * jnp.exp(-t * jnp.abs(dec_ref[...]))
    row = i * tr + lax.broadcasted_iota(jnp.int32, (tr, 1), 0)
    taps = jnp.where(row == L, 0.0, taps)
    taps_ref[...] = taps

    @pl.when(i == 0)
    def _():
        asum_ref[...] = jnp.zeros_like(asum_ref)

    asum_ref[...] += jnp.sum(jnp.abs(taps), axis=0, keepdims=True)


def _hy_filter_taps(L, w1, b1, w2, b2, w3, freq, decay):
    W = decay.shape[1]
    E, O = w1.shape
    tr = 512
    half = L // tr
    z = _hy_positions(L)
    w1p = jnp.zeros((LANES, O), F32).at[:E].set(w1)
    return pl.pallas_call(
        functools.partial(_hy_mlp_kernel, L=L, tr=tr),
        grid=(2 * L // tr,),
        in_specs=[pl.BlockSpec((tr, LANES), lambda i: (i, 0)),
                  pl.BlockSpec((LANES, O), lambda i: (0, 0)),
                  pl.BlockSpec((1, O), lambda i: (0, 0)),
                  pl.BlockSpec((O, O), lambda i: (0, 0)),
                  pl.BlockSpec((1, O), lambda i: (0, 0)),
                  pl.BlockSpec((O, W), lambda i: (0, i // half)),
                  pl.BlockSpec((1, O), lambda i: (0, 0)),
                  pl.BlockSpec((None, 1, W), lambda i: (i // half, 0, 0))],
        out_specs=[pl.BlockSpec((tr, W), lambda i: (i, 0)),
                   pl.BlockSpec((1, W), lambda i: (0, 0))],
        out_shape=[jax.ShapeDtypeStruct((2 * L, W), F32), jax.ShapeDtypeStruct((1, W), F32)],
        compiler_params=_cparams(("arbitrary",), [((tr, W), F32), ((O, W), F32), ((tr, LANES), F32)],
                                 [((tr, W), F32), ((tr, W), F32)]),
        name="hyena_filter_mlp",
    )(z, w1p, b1.reshape(1, O), w2, b2.reshape(1, O), w3, freq.reshape(1, O), decay.reshape(2, 1, W))


FFT_UNROLL = 16


def _fft_stage1(src_ref, src_off, a_ref, s1_ref, n1, n2, k1p, sign=None, src_off2=None):
    slab = _slab_rows(k1p)

    def body(j, carry):
        d = src_ref[pl.ds(src_off + j, n1 // 2, stride=n2), :]
        r = _dot_tab(s1_ref[j], d)
        if src_off2 is not None:
            d2 = src_ref[pl.ds(src_off2 + j, n1 // 2, stride=n2), :]
            r = r + sign * _dot_tab(s1_ref[j], d2)
        a_ref[pl.ds(pl.multiple_of(j * slab, 8), 2 * k1p), :] = r
        return carry

    lax.fori_loop(0, n2, body, 0, unroll=FFT_UNROLL)


def _load_k1(a_ref, k, n2, k1p):
    br = a_ref[pl.ds(k, n2, stride=_slab_rows(k1p)), :]
    bi = a_ref[pl.ds(k1p + k, n2, stride=_slab_rows(k1p)), :]
    return jnp.concatenate([br, bi], axis=0)


def _hy_spec_kernel(taps_ref, asum_ref, s1_ref, f2_ref, h_ref, a_ref, *, L, n1, n2, k1p):
    r = lax.broadcasted_iota(jnp.int32, (2 * k1p, 1), 0)
    k1 = jnp.where(r < k1p, r, r - k1p)
    sign = (1 - 2 * (k1 & 1)).astype(F32)
    _fft_stage1(taps_ref, 0, a_ref, s1_ref, n1, n2, k1p, sign=sign, src_off2=L)
    inv = 1.0 / asum_ref[...]

    def body(k, carry):
        h_ref[k] = _dot_tab(f2_ref[...], _load_k1(a_ref, k, n2, k1p)) * inv
        return carry

    lax.fori_loop(0, n1 // 2, body, 0, unroll=2)
    body(n1 // 2, 0)


def _hy_spectrum(taps, asum, tabs):
    n1, n2, k1p = tabs["n1"], tabs["n2"], tabs["k1p"]
    N, W = taps.shape
    L = N // 2
    cb = LANES
    k1v = n1 // 2 + 1
    s1, f2 = tabs["s1"], tabs["f2"]
    return pl.pallas_call(
        functools.partial(_hy_spec_kernel, L=L, n1=n1, n2=n2, k1p=k1p),
        grid=(W // cb,),
        in_specs=[pl.BlockSpec((N, cb), lambda c: (0, c)),
                  pl.BlockSpec((1, cb), lambda c: (0, c)),
                  pl.BlockSpec(s1.shape, lambda c: (0, 0, 0), pipeline_mode=pl.Buffered(1)),
                  pl.BlockSpec(f2.shape, lambda c: (0, 0), pipeline_mode=pl.Buffered(1))],
        out_specs=pl.BlockSpec((k1v, 2 * n2, cb), lambda c: (0, 0, c)),
        out_shape=jax.ShapeDtypeStruct((k1v, 2 * n2, W), F32),
        scratch_shapes=[pltpu.VMEM((n2 * _slab_rows(k1p), cb), F32)],
        compiler_params=_cparams(("parallel",),
                                 [((N, cb), F32), ((k1v, 2 * n2, cb), F32)],
                                 [((n2 * _slab_rows(k1p), cb), F32), (s1.shape, BF16), (f2.shape, BF16)]),
        name="hyena_filter_spectrum",
    )(taps, asum, s1, f2)


def _hy_conv_kernel(x1_ref, v_ref, skip_ref, h_ref, s1_ref, f2_ref, f2c_ref, gl_ref,
                    o_ref, u_ref, a_ref, *, n1, n2, k1p):
    u_ref[...] = x1_ref[...].astype(F32) * v_ref[...].astype(F32)
    _fft_stage1(u_ref, 0, a_ref, s1_ref, n1, n2, k1p)

    cb = u_ref.shape[1]

    def mid(groups):
        loaded = [jnp.concatenate([_load_k1(a_ref, k, n2, k1p) for k in ks], axis=1) for ks in groups]
        results = []
        for ks, rhs in zip(groups, loaded):
            x = _dot_tab(f2_ref[...], rhs)
            xr, xi = x[:n2], x[n2:]
            hr = jnp.concatenate([h_ref[k, :n2, :] for k in ks], axis=1)
            hi = jnp.concatenate([h_ref[k, n2:, :] for k in ks], axis=1)
            y = jnp.concatenate([xr * hr - xi * hi, xr * hi + xi * hr], axis=0)
            results.append(_dot_tab(f2c_ref[...], y))
        for ks, c in zip(groups, results):
            for p, k in enumerate(ks):
                a_ref[pl.ds(k, n2, stride=_slab_rows(k1p)), :] = c[:n2, p * cb:(p + 1) * cb]
                a_ref[pl.ds(k1p + k, n2, stride=_slab_rows(k1p)), :] = c[n2:, p * cb:(p + 1) * cb]

    def mid_oct(ko, carry):
        k = 8 * ko
        mid(((k, k + 1, k + 2, k + 3), (k + 4, k + 5, k + 6, k + 7)))
        return carry

    lax.fori_loop(0, n1 // 16, mid_oct, 0)
    mid(((n1 // 2,),))
    skip = skip_ref[...]

    def last(j, carry):
        rhs = a_ref[pl.ds(pl.multiple_of(j * _slab_rows(k1p), 8), 2 * k1p), :]
        y = _dot_tab(gl_ref[j], rhs)
        rows = pl.ds(j, n1 // 2, stride=n2)
        u_ref[rows, :] = y + u_ref[rows, :] * skip
        return carry

    lax.fori_loop(0, n2, last, 0, unroll=FFT_UNROLL)
    o_ref[...] = u_ref[...].astype(o_ref.dtype)


def _hy_conv(z, skip, spec, tabs):
    n1, n2, k1p = tabs["n1"], tabs["n2"], tabs["k1p"]
    B, nc3, L, cb = z.shape
    nc = nc3 // 3
    W = nc * cb
    k1v = n1 // 2 + 1
    tables = [tabs["s1"], tabs["f2"], tabs["f2c"], tabs["gl"]]

    def const_spec(t):
        return pl.BlockSpec(t.shape, (lambda c, b: (0, 0, 0)) if t.ndim == 3 else (lambda c, b: (0, 0)),
                            pipeline_mode=pl.Buffered(1))

    return pl.pallas_call(
        functools.partial(_hy_conv_kernel, n1=n1, n2=n2, k1p=k1p),
        grid=(nc, B),
        in_specs=[pl.BlockSpec((None, None, L, cb), lambda c, b: (b, nc + c, 0, 0)),
                  pl.BlockSpec((None, None, L, cb), lambda c, b: (b, 2 * nc + c, 0, 0)),
                  pl.BlockSpec((1, cb), lambda c, b: (0, c)),
                  pl.BlockSpec((k1v, 2 * n2, cb), lambda c, b: (0, 0, c), pipeline_mode=pl.Buffered(1))]
        + [const_spec(t) for t in tables],
        out_specs=pl.BlockSpec((None, None, L, cb), lambda c, b: (b, c, 0, 0)),
        out_shape=jax.ShapeDtypeStruct((B, nc, L, cb), BF16),
        scratch_shapes=[pltpu.VMEM((L, cb), F32), pltpu.VMEM((n2 * _slab_rows(k1p), cb), F32)],
        compiler_params=_cparams(("parallel", "arbitrary"),
                                 [((L, cb), BF16)] * 3,
                                 [((L, cb), F32), ((n2 * _slab_rows(k1p), cb), F32), ((k1v, 2 * n2, cb), F32)]
                                 + [(t.shape, BF16) for t in tables]),
        name="hyena_long_conv",
    )(z, z, skip.reshape(1, W), spec, *tables)


def _ret_fwd_kernel(lg_ref, q_ref, k_ref, v_ref, o_ref, s_ref, *, T, C, H, DK, DV):
    @pl.when(pl.program_id(1) == 0)
    def _():
        s_ref[...] = jnp.zeros_like(s_ref)

    row = lax.broadcasted_iota(jnp.int32, (C, C), 0)
    col = lax.broadcasted_iota(jnp.int32, (C, C), 1)
    rel = (row - col).astype(F32)
    ridx = lax.broadcasted_iota(jnp.int32, (C, 1), 0).astype(F32)
    consts = []
    for h in range(H):
        lgf = lg_ref[0, h]
        lgb = lg_ref[1, h]
        dmat = jnp.where(rel > 0.0, jnp.exp(lgf * jnp.maximum(rel, 0.0)),
                         jnp.where(rel < 0.0, jnp.exp(lgb * jnp.maximum(-rel, 0.0)), 2.0))
        consts.append((dmat, jnp.exp(lgf * (ridx + 1.0)), jnp.exp(lgf * (C - 1.0 - ridx)),
                       jnp.exp(jnp.full((1, 1), C, F32) * lgf)))

    def chunk(c, carry):
        rows = pl.ds(pl.multiple_of(c * C, C), C)
        for h in range(H):
            dmat, q_scale, k_scale, s_decay = consts[h]
            ck = slice(h * DK, (h + 1) * DK)
            cv = slice(h * DV, (h + 1) * DV)
            q = q_ref[rows, ck]
            k = k_ref[rows, ck]
            v = v_ref[rows, cv]
            s = lax.dot_general(q, k, NT_DIMS, preferred_element_type=F32) * dmat
            st = s_ref[h]
            o_ref[rows, cv] = (_dot(s.astype(BF16), v) + _dot((q.astype(F32) * q_scale).astype(BF16),
                                                              st.astype(BF16))).astype(o_ref.dtype)
            s_ref[h] = s_decay * st + lax.dot_general((k.astype(F32) * k_scale).astype(BF16), v, TN_DIMS,
                                                      preferred_element_type=F32)
        return carry

    lax.fori_loop(0, T // C, chunk, 0)


def _ret_bwd_kernel(lg_ref, q_ref, k_ref, v_ref, g_ref, o1_ref, w_ref, x_ref, gate_ref, o_ref, s_ref, a_ref,
                    *, T, C, H, DK, DV):
    @pl.when(pl.program_id(1) == 0)
    def _():
        s_ref[...] = jnp.zeros_like(s_ref)

    ridx = lax.broadcasted_iota(jnp.int32, (C, 1), 0).astype(F32)
    nc = T // C
    consts = []
    for h in range(H):
        lgb = lg_ref[1, h]
        consts.append((jnp.exp(lgb * (C - ridx)), jnp.exp(lgb * ridx), jnp.exp(jnp.full((1, 1), C, F32) * lgb)))

    def chunk(cc, carry):
        rows = pl.ds(pl.multiple_of((nc - 1 - cc) * C, C), C)
        for h in range(H):
            q_scale, k_scale, s_decay = consts[h]
            ck = slice(h * DK, (h + 1) * DK)
            cv = slice(h * DV, (h + 1) * DV)
            q = q_ref[rows, ck].astype(F32)
            k = k_ref[rows, ck].astype(F32)
            v = v_ref[rows, cv]
            st = s_ref[h]
            o = o1_ref[rows, cv].astype(F32) + _dot((q * q_scale).astype(BF16), st.astype(BF16))
            y = o * lax.rsqrt(jnp.mean(o * o, axis=-1, keepdims=True) + NORM_EPS)
            a_ref[rows, cv] = (y * _silu(g_ref[rows, cv].astype(F32))).astype(a_ref.dtype)
            s_ref[h] = s_decay * st + lax.dot_general((k * k_scale).astype(BF16), v, TN_DIMS,
                                                      preferred_element_type=F32)
        return carry

    lax.fori_loop(0, nc, chunk, 0)
    o_ref[...] = x_ref[...] + gate_ref[...] * _dot(a_ref[...], w_ref[...])


def _retention(z, log_gamma, w_out, x, gate):
    B, L, W = z.shape
    D = w_out.shape[1]
    H, C = RET_HEADS, RET_CHUNK
    DK = W // (6 * H)
    DV = 2 * DK
    T = min(L, 512)
    nT = L // T
    kw = dict(T=T, C=C, H=H, DK=DK, DV=DV)
    smem = pl.BlockSpec(memory_space=pltpu.SMEM)
    blocks = [((T, H * DK), BF16)] * 2 + [((T, H * DV), BF16)] * 4
    state = [((H, DK, DV), F32), ((DK, DV), F32), ((DK, DV), F32)]
    o1 = pl.pallas_call(
        functools.partial(_ret_fwd_kernel, **kw),
        grid=(B, nT),
        in_specs=[smem,
                  pl.BlockSpec((None, T, H * DK), lambda b, i: (b, i, 0)),
                  pl.BlockSpec((None, T, H * DK), lambda b, i: (b, i, 1)),
                  pl.BlockSpec((None, T, H * DV), lambda b, i: (b, i, 1))],
        out_specs=pl.BlockSpec((None, T, H * DV), lambda b, i: (b, i, 0)),
        out_shape=jax.ShapeDtypeStruct((B, L, H * DV), BF16),
        scratch_shapes=[pltpu.VMEM((H, DK, DV), F32)],
        compiler_params=_cparams(("parallel", "arbitrary"), blocks, state),
        name="retention_fwd",
    )(log_gamma, z, z, z)
    return pl.pallas_call(
        functools.partial(_ret_bwd_kernel, **kw),
        grid=(B, nT),
        in_specs=[smem,
                  pl.BlockSpec((None, T, H * DK), lambda b, i: (b, nT - 1 - i, 0)),
                  pl.BlockSpec((None, T, H * DK), lambda b, i: (b, nT - 1 - i, 1)),
                  pl.BlockSpec((None, T, H * DV), lambda b, i: (b, nT - 1 - i, 1)),
                  pl.BlockSpec((None, T, H * DV), lambda b, i: (b, nT - 1 - i, 2)),
                  pl.BlockSpec((None, T, H * DV), lambda b, i: (b, nT - 1 - i, 0)),
                  pl.BlockSpec((H * DV, D), lambda b, i: (0, 0), pipeline_mode=pl.Buffered(1)),
                  pl.BlockSpec((None, T, D), lambda b, i: (b, nT - 1 - i, 0)),
                  pl.BlockSpec((None, 1, D), lambda b, i: (b, 0, 0))],
        out_specs=pl.BlockSpec((None, T, D), lambda b, i: (b, nT - 1 - i, 0)),
        out_shape=jax.ShapeDtypeStruct((B, L, D), F32),
        scratch_shapes=[pltpu.VMEM((H, DK, DV), F32), pltpu.VMEM((T, H * DV), BF16)],
        compiler_params=_cparams(("parallel", "arbitrary"), blocks[:-1] + [((T, D), F32)] * 2,
                                 state + [((H * DV, D), BF16), ((T, H * DV), BF16), ((T, D), F32)]),
        name="retention_bwd",
    )(log_gamma, z, z, z, z, o1, w_out, x, gate)


def _swa_kernel(sink_ref, q_ref, kp_ref, kc_ref, kn_ref, vp_ref, vc_ref, vn_ref, o_ref, *, HKV, G, BLK):
    i = pl.program_id(1)
    last = pl.num_programs(1) - 1
    R = G * BLK
    r = lax.broadcasted_iota(jnp.int32, (R, 3 * BLK), 0) % BLK
    c = lax.broadcasted_iota(jnp.int32, (R, 3 * BLK), 1)
    rel = r - (c - BLK)
    lo = jnp.where(i == 0, BLK, 0)
    hi = jnp.where(i == last, 2 * BLK, 3 * BLK)
    valid = (jnp.abs(rel) <= WINDOW) & (c >= lo) & (c < hi)
    grp = lax.broadcasted_iota(jnp.int32, (R, 1), 0) // BLK
    for j in range(HKV):
        sl = slice(j * LANES, (j + 1) * LANES)
        k = jnp.concatenate([kp_ref[:, sl], kc_ref[:, sl], kn_ref[:, sl]], axis=0)
        v = jnp.concatenate([vp_ref[:, sl], vc_ref[:, sl], vn_ref[:, sl]], axis=0)
        q = jnp.concatenate([q_ref[:, (j * G + g) * LANES:(j * G + g + 1) * LANES] for g in range(G)], axis=0)
        s = lax.dot_general(q, k, NT_DIMS, preferred_element_type=F32)
        s = jnp.where(valid, s, NEG_INF)
        sink = jnp.zeros((R, 1), F32)
        for g in range(G):
            sink = jnp.where(grp == g, sink_ref[0, j * G + g], sink)
        m = jnp.maximum(jnp.max(s, axis=-1, keepdims=True), sink)
        p = jnp.exp(s - m)
        denom = jnp.sum(p, axis=-1, keepdims=True) + jnp.exp(sink - m)
        o = _dot(p.astype(BF16), v) / denom
        for g in range(G):
            o_ref[:, (j * G + g) * LANES:(j * G + g + 1) * LANES] = o[g * BLK:(g + 1) * BLK].astype(o_ref.dtype)


def _swa_attention(z, sink):
    B, L, _ = z.shape
    BLK = ATTN_BLOCK
    nb = L // BLK
    G = SWA_HQ // SWA_HKV
    qw = SWA_HQ * LANES
    kvw = SWA_HKV * LANES
    kcol = qw // kvw
    prev = lambda b, i: (b, jnp.maximum(i - 1, 0), kcol)
    cur = lambda b, i: (b, i, kcol)
    nxt = lambda b, i: (b, jnp.minimum(i + 1, nb - 1), kcol)
    vprev = lambda b, i: (b, jnp.maximum(i - 1, 0), kcol + 1)
    vcur = lambda b, i: (b, i, kcol + 1)
    vnxt = lambda b, i: (b, jnp.minimum(i + 1, nb - 1), kcol + 1)
    kv = lambda f: pl.BlockSpec((None, BLK, kvw), f)
    return pl.pallas_call(
        functools.partial(_swa_kernel, HKV=SWA_HKV, G=G, BLK=BLK),
        grid=(B, nb),
        in_specs=[pl.BlockSpec(memory_space=pltpu.SMEM),
                  pl.BlockSpec((None, BLK, qw), lambda b, i: (b, i, 0)),
                  kv(prev), kv(cur), kv(nxt), kv(vprev), kv(vcur), kv(vnxt)],
        out_specs=pl.BlockSpec((None, BLK, qw), lambda b, i: (b, i, 0)),
        out_shape=jax.ShapeDtypeStruct((B, L, qw), BF16),
        compiler_params=_cparams(("parallel", "parallel"),
                                 [((BLK, qw), BF16)] * 2 + [((BLK, kvw), BF16)] * 6,
                                 [((G * BLK, 3 * BLK), F32)] * 4),
        name="swa_attention",
    )(sink.reshape(1, SWA_HQ), z, z, z, z, z, z, z)


def _swa_layout(w_in, q_gain, k_gain, w_out, L):
    D = w_in.shape[0]
    dh, hq, hkv = SWA_DH, SWA_HQ, SWA_HKV
    hf = dh // 2
    q_end, k_end = hq * dh, (hq + hkv) * dh

    def rot_slots(w, n):
        w = w.reshape(D, n, 2, hf)
        return jnp.pad(w, ((0, 0), (0, 0), (0, 0), (0, LANES // 2 - hf))).reshape(D, n * LANES)

    def val_slots(w, n):
        return jnp.pad(w.reshape(D, n, dh), ((0, 0), (0, 0), (0, LANES - dh))).reshape(D, n * LANES)

    w = jnp.concatenate([rot_slots(w_in[:, :q_end], hq), rot_slots(w_in[:, q_end:k_end], hkv),
                         val_slots(w_in[:, k_end:], hkv)], axis=1)

    def gain_slot(g):
        return jnp.pad(g.reshape(2, hf), ((0, 0), (0, LANES // 2 - hf))).reshape(1, LANES)

    flag = jnp.concatenate([jnp.ones(((hq + hkv) * LANES,), F32), jnp.zeros((hkv * LANES,), F32)]).reshape(1, -1)
    wo = jnp.pad(w_out.reshape(hq, dh, -1), ((0, 0), (0, LANES - dh), (0, 0))).reshape(hq * LANES, -1)

    inv = ROPE_THETA ** (-np.arange(0, dh, 2) / dh)
    ang = np.arange(L)[:, None] * inv[None, :]
    zero = np.zeros((L, LANES // 2 - hf))
    cos = jnp.asarray(np.concatenate([np.cos(ang), zero, np.cos(ang), zero], axis=1), F32)
    sin = jnp.asarray(np.concatenate([-np.sin(ang), zero, np.sin(ang), zero], axis=1), F32)
    gq = gain_slot(q_gain) * (dh ** -0.5)
    gk = gain_slot(k_gain)
    roll = lambda g: jnp.roll(g, LANES // 2, axis=1)
    cos3 = jnp.stack([cos * gq, cos * gk, jnp.ones_like(cos)])
    sin3 = jnp.stack([sin * roll(gq), sin * roll(gk), jnp.zeros_like(sin)])
    return w, flag, wo, cos3, sin3


def _hg_decays(f_ref, q_ref, rows, lb, tri, total_row, mid_row):
    f = lb + (1.0 - lb) * jax.nn.sigmoid(f_ref[rows, :].astype(F32))
    gl = jnp.log(f)
    cum = _cumdot(tri, gl)
    total = cum[total_row:total_row + 1, :]
    mid = cum[mid_row:mid_row + 1, :]
    q_dec = _silu(q_ref[rows, :].astype(F32)) * jnp.exp(cum - mid)
    k_inv = (1.0 - f) * jnp.exp(mid - cum)
    q_full = (q_dec * jnp.exp(mid)).astype(BF16)
    k_end = (k_inv * jnp.exp(total - mid)).astype(BF16)
    return q_dec.astype(BF16), k_inv.astype(BF16), q_full, k_end, jnp.exp(total)


def _hg_fwd_kernel(lb_ref, q_ref, i_ref, f_ref, o_ref, st_ref, *, T, C, H, DK, DV):
    @pl.when(pl.program_id(1) == 0)
    def _():
        st_ref[...] = jnp.zeros_like(st_ref)

    row = lax.broadcasted_iota(jnp.int32, (C, C), 0)
    col = lax.broadcasted_iota(jnp.int32, (C, C), 1)
    causal = row >= col
    tri = jnp.concatenate([causal.astype(BF16)] * 2, axis=1)
    lb = lb_ref[...]

    def chunk(c, carry):
        rows = pl.ds(pl.multiple_of(c * C, C), C)
        q_dec, k_inv, q_full, k_end, s_decay = _hg_decays(f_ref, q_ref, rows, lb, tri, C - 1, C // 2 - 1)
        v = i_ref[rows, :]
        for h in range(H):
            ck = slice(h * DK, (h + 1) * DK)
            cv = slice(h * DV, (h + 1) * DV)
            s = lax.dot_general(q_dec[:, ck], k_inv[:, ck], NT_DIMS, preferred_element_type=F32)
            s = jnp.where(causal, s, 0.0)
            st = st_ref[h]
            o_ref[rows, cv] = _dot(s.astype(BF16), v[:, cv]) + lax.dot_general(
                q_full[:, ck], st.astype(BF16), NT_DIMS, preferred_element_type=F32)
            st_ref[h] = st * s_decay[:, ck] + lax.dot_general(v[:, cv], k_end[:, ck], TN_DIMS,
                                                              preferred_element_type=F32)
        return carry

    lax.fori_loop(0, T // C, chunk, 0)


def _hg_bwd_kernel(lb_ref, gain_ref, q_ref, i_ref, f_ref, gate_ref, o1_ref, w_ref, x_ref, xgate_ref,
                   o_ref, st_ref, a_ref, *, T, C, H, DK, DV):
    @pl.when(pl.program_id(1) == 0)
    def _():
        st_ref[...] = jnp.zeros_like(st_ref)

    row = lax.broadcasted_iota(jnp.int32, (C, C), 0)
    col = lax.broadcasted_iota(jnp.int32, (C, C), 1)
    anti = col >= row
    tri = jnp.concatenate([anti.astype(BF16)] * 2, axis=1)
    lb = lb_ref[...]
    gain = gain_ref[...]
    nc = T // C

    def chunk(cc, carry):
        rows = pl.ds(pl.multiple_of((nc - 1 - cc) * C, C), C)
        q_dec, k_inv, q_full, k_end, s_decay = _hg_decays(f_ref, q_ref, rows, lb, tri, 0, C // 2)
        v = i_ref[rows, :]
        for h in range(H):
            ck = slice(h * DK, (h + 1) * DK)
            cv = slice(h * DV, (h + 1) * DV)
            s = lax.dot_general(q_dec[:, ck], k_inv[:, ck], NT_DIMS, preferred_element_type=F32)
            s = jnp.where(anti, s, 0.0)
            st = st_ref[h]
            o = o1_ref[rows, cv] + _dot(s.astype(BF16), v[:, cv]) + lax.dot_general(
                q_full[:, ck], st.astype(BF16), NT_DIMS, preferred_element_type=F32)
            y = o * lax.rsqrt(jnp.mean(o * o, axis=-1, keepdims=True) + NORM_EPS) * gain
            a_ref[rows, cv] = (y * _silu(gate_ref[rows, cv].astype(F32))).astype(a_ref.dtype)
            st_ref[h] = st * s_decay[:, ck] + lax.dot_general(v[:, cv], k_end[:, ck], TN_DIMS,
                                                              preferred_element_type=F32)
        return carry

    lax.fori_loop(0, nc, chunk, 0)
    o_ref[...] = x_ref[...] + xgate_ref[...] * _dot(a_ref[...], w_ref[...])


def _hgrn(z, lb, gain, w_out, x, xgate):
    B, L, _ = z.shape
    D = w_out.shape[1]
    H, C = HG_HEADS, HG_CHUNK
    HD = lb.shape[1]
    DK = HD // H
    DV = gain.shape[0]
    T = min(L, 512)
    nT = L // T
    kw = dict(T=T, C=C, H=H, DK=DK, DV=DV)
    blocks = [((T, HD), BF16)] * 5 + [((T, H * DV), F32)] * 2
    state = [((H, DV, DK), F32)] + [((C, HD), F32)] * 12
    lb3 = lb.reshape(2, 1, HD)
    o1 = pl.pallas_call(
        functools.partial(_hg_fwd_kernel, **kw),
        grid=(B, nT),
        in_specs=[pl.BlockSpec((None, 1, HD), lambda b, i: (0, 0, 0)),
                  pl.BlockSpec((None, T, HD), lambda b, i: (b, i, 0)),
                  pl.BlockSpec((None, T, H * DV), lambda b, i: (b, i, 1)),
                  pl.BlockSpec((None, T, HD), lambda b, i: (b, i, 2))],
        out_specs=pl.BlockSpec((None, T, H * DV), lambda b, i: (b, i, 0)),
        out_shape=jax.ShapeDtypeStruct((B, L, H * DV), F32),
        scratch_shapes=[pltpu.VMEM((H, DV, DK), F32)],
        compiler_params=_cparams(("parallel", "arbitrary"), blocks, state),
        name="hgrn_fwd",
    )(lb3, z, z, z)
    return pl.pallas_call(
        functools.partial(_hg_bwd_kernel, **kw),
        grid=(B, nT),
        in_specs=[pl.BlockSpec((None, 1, HD), lambda b, i: (1, 0, 0)),
                  pl.BlockSpec((1, DV), lambda b, i: (0, 0)),
                  pl.BlockSpec((None, T, HD), lambda b, i: (b, nT - 1 - i, 0)),
                  pl.BlockSpec((None, T, H * DV), lambda b, i: (b, nT - 1 - i, 1)),
                  pl.BlockSpec((None, T, HD), lambda b, i: (b, nT - 1 - i, 3)),
                  pl.BlockSpec((None, T, H * DV), lambda b, i: (b, nT - 1 - i, 4)),
                  pl.BlockSpec((None, T, H * DV), lambda b, i: (b, nT - 1 - i, 0)),
                  pl.BlockSpec((H * DV, D), lambda b, i: (0, 0), pipeline_mode=pl.Buffered(1)),
                  pl.BlockSpec((None, T, D), lambda b, i: (b, nT - 1 - i, 0)),
                  pl.BlockSpec((None, 1, D), lambda b, i: (b, 0, 0))],
        out_specs=pl.BlockSpec((None, T, D), lambda b, i: (b, nT - 1 - i, 0)),
        out_shape=jax.ShapeDtypeStruct((B, L, D), F32),
        scratch_shapes=[pltpu.VMEM((H, DV, DK), F32), pltpu.VMEM((T, H * DV), BF16)],
        compiler_params=_cparams(("parallel", "arbitrary"), blocks + [((T, D), F32)],
                                 state + [((H * DV, D), BF16), ((T, H * DV), BF16), ((T, D), F32)]),
        name="hgrn_bwd",
    )(lb3, gain.reshape(1, DV), z, z, z, z, o1, w_out, x, xgate)


def _ret_rope_tables(L, dk):
    inv = ROPE_THETA ** (-np.arange(0, dk, 2) / dk)
    ang = np.arange(L)[:, None] * inv[None, :]
    cos = np.stack([np.cos(ang), np.cos(ang) * dk ** -0.5, np.ones_like(ang)])
    sin = np.stack([np.sin(ang), np.sin(ang) * dk ** -0.5, np.zeros_like(ang)])
    return jnp.asarray(cos, F32), jnp.asarray(sin, F32)


def kernel(x_prompt, x_sample, c_prompt, c_sample, ada_w, ada_b, norm_g, hy_w_in, hy_conv_w, hy_conv_b, hy_w1, hy_b1, hy_w2, hy_b2, hy_w3, hy_freq, hy_decay, hy_skip, hy_w_out, ret_w_in, ret_decay, ret_w_out, swa_w_in, swa_q_gain, swa_k_gain, swa_sink, swa_w_out, hg_w_in, hg_lb, hg_gain, hg_w_out, ffn_w_gate, ffn_w_val, ffn_conv_w, ffn_conv_b, ffn_w_down):
    depth, D = norm_g.shape[0], norm_g.shape[2]
    groups = [(x_prompt, c_prompt), (x_sample, c_sample)]
    mods = _ada_mod(jnp.concatenate([c for _, c in groups], axis=0), ada_w, ada_b)

    bf = lambda w: w.astype(BF16)
    hy_w_in_b, hy_w_out_b = bf(hy_w_in), bf(hy_w_out)
    ret_w_in_b, ret_w_out_b = bf(ret_w_in), bf(ret_w_out)
    hg_w_in_b, hg_w_out_b = bf(hg_w_in), bf(hg_w_out)
    wg_b, wv_b, wd_b = bf(ffn_w_gate), bf(ffn_w_val), bf(ffn_w_down)
    hg_sm = jax.nn.softmax(hg_lb.astype(F32), axis=1)
    hg_lower = jnp.cumsum(hg_sm, axis=1) - hg_sm
    ret_log_gamma = -jnp.exp(ret_decay.astype(F32))

    outs = []
    row0 = 0
    for x, c in groups:
        B, L, _ = x.shape
        for layer in range(depth):
            kind, j = layer % N_MIXERS, layer // N_MIXERS
            mod = mods[layer, row0:row0 + B].reshape(B, N_MOD, 1, D)
            sh1, sc1, g1, sh2, sc2, g2 = (mod[:, m] for m in range(N_MOD))
            gn1 = norm_g[layer, 0].reshape(1, D)
            gn2 = norm_g[layer, 1].reshape(1, D)
            if kind == 0:
                tabs = _fft_tables(L)
                z = _proj_in_conv(x, gn1, sc1, sh1, hy_w_in_b[j], hy_conv_w[j], hy_conv_b[j], tn=1024)
                taps, asum = _hy_filter_taps(L, hy_w1[j], hy_b1[j], hy_w2[j], hy_b2[j], hy_w3[j], hy_freq[j],
                                             hy_decay[j])
                spec = _hy_spectrum(taps, asum, tabs)
                a = _hy_conv(z, hy_skip[j], spec, tabs)
                x = _proj_out(a, hy_w_out_b[j], x, g1, mult=z)
            elif kind == 1:
                dk = D // RET_HEADS
                cos, sin = _ret_rope_tables(L, dk)
                z = _proj_in_halfrope(x, gn1, sc1, sh1, ret_w_in_b[j], cos, sin, tn=RET_HEADS * dk, rope_tiles=2,
                                      dk=dk)
                x = _retention(z, ret_log_gamma[j], ret_w_out_b[j], x, g1)
            elif kind == 2:
                w, flag, wo, cos, sin = _swa_layout(swa_w_in[j], swa_q_gain[j], swa_k_gain[j], swa_w_out[j], L)
                z = _proj_in_rope(x, gn1, sc1, sh1, bf(w), flag, cos, sin, tn=SWA_HKV * LANES, dh=SWA_DH,
                                  q_tiles=SWA_HQ // SWA_HKV)
                a = _swa_attention(z, swa_sink[j])
                x = _proj_out(a, bf(wo), x, g1)
            else:
                z = _proj_in(x, gn1, sc1, sh1, hg_w_in_b[j], tn=1024)
                x = _hgrn(z, hg_lower[:, layer], hg_gain[j], hg_w_out_b[j], x, g1)
            x = _ffn(x, gn2, sc2, sh2, g2, wg_b[layer], wv_b[layer], ffn_conv_w[layer], ffn_conv_b[layer],
                     wd_b[layer])
        outs.append(x)
        row0 += B
    return tuple(outs)
```

```python
import functools
import math

import numpy as np
import jax
import jax.numpy as jnp
from jax import lax
from jax.experimental import pallas as pl
from jax.experimental.pallas import tpu as pltpu

F32 = jnp.float32
BF16 = jnp.bfloat16

NORM_EPS = 1e-6
N_MIXERS = 4
N_MOD = 6
HY_BANDS = 16
RET_HEADS = 4
RET_CHUNK = 256
SWA_HQ = 16
SWA_HKV = 4
SWA_DH = 64
WINDOW = 128
ATTN_BLOCK = 128
ROPE_THETA = 10000.0
NEG_INF = -1e30
HG_HEADS = 8
HG_CHUNK = 128

LANES = 128
SUBLANES = 8
BF16_ROWS = 16
VMEM_LIMIT_CAP = 60 * 1024 * 1024
VMEM_SLACK = 8 * 1024 * 1024

NT_DIMS = (((1,), (1,)), ((), ()))
TN_DIMS = (((0,), (0,)), ((), ()))


def _nbytes(shape, dtype):
    item = jnp.dtype(dtype).itemsize
    sub = SUBLANES * 4 // item
    dims = list(shape)
    dims[-1] = -(-dims[-1] // LANES) * LANES
    if len(dims) > 1:
        dims[-2] = -(-dims[-2] // sub) * sub
    return int(np.prod(dims)) * item


def _cparams(semantics, pipelined, resident):
    need = 2 * sum(_nbytes(s, d) for s, d in pipelined) + sum(_nbytes(s, d) for s, d in resident)
    return pltpu.CompilerParams(dimension_semantics=semantics,
                                vmem_limit_bytes=min(need + VMEM_SLACK, VMEM_LIMIT_CAP))


def _dot(a, b):
    return jnp.dot(a, b, preferred_element_type=F32)


def _split(x):
    hi = x.astype(BF16)
    lo = (x - hi.astype(F32)).astype(BF16)
    return hi, lo


def _dot_tab(t_cat, d):
    d_hi, d_lo = _split(d)
    return _dot(t_cat, jnp.concatenate([d_hi, d_lo], axis=0))


def _dot3(a, b):
    a_hi, a_lo = _split(a)
    b_hi, b_lo = _split(b)
    return _dot(a_hi, b_hi) + _dot(a_lo, b_hi) + _dot(a_hi, b_lo)


def _cumdot(tri2, g):
    g_hi, g_lo = _split(g)
    return _dot(tri2, jnp.concatenate([g_hi, g_lo], axis=0))


def _silu(x):
    return x * jax.nn.sigmoid(x)


def _norm_mod(x, g, sc, sh):
    xf = x.astype(F32)
    y = xf * lax.rsqrt(jnp.mean(xf * xf, axis=-1, keepdims=True) + NORM_EPS)
    return (y * g) * (1.0 + sc) + sh


def _row_tile(L):
    return min(L, 1024)


def _ada_kernel(c_ref, w_ref, b_ref, o_ref):
    cs = _silu(c_ref[...]).astype(BF16)
    o_ref[...] = _dot(cs, w_ref[...].astype(BF16)) + b_ref[...]


def _ada_mod(c_all, ada_w, ada_b):
    depth, D, N = ada_w.shape
    R = c_all.shape[0]
    tn = 1024
    return pl.pallas_call(
        _ada_kernel,
        grid=(depth, N // tn),
        in_specs=[pl.BlockSpec((R, D), lambda l, j: (0, 0)),
                  pl.BlockSpec((None, D, tn), lambda l, j: (l, 0, j)),
                  pl.BlockSpec((None, 1, tn), lambda l, j: (l, 0, j))],
        out_specs=pl.BlockSpec((None, R, tn), lambda l, j: (l, 0, j)),
        out_shape=jax.ShapeDtypeStruct((depth, R, N), F32),
        compiler_params=_cparams(("parallel", "parallel"),
                                 [((D, tn), F32), ((R, D), F32), ((R, tn), F32)], []),
        name="ada_mod",
    )(c_all, ada_w, ada_b.reshape(depth, 1, N))


def _fill_h(h_ref, x_ref, g_ref, sc_ref, sh_ref):
    h_ref[...] = _norm_mod(x_ref[...], g_ref[...], sc_ref[...], sh_ref[...]).astype(BF16)


def _fill_h_halo(h_ref, x_ref, xp_ref, xn_ref, g_ref, sc_ref, sh_ref, tm):
    i = pl.program_id(1)
    last = pl.num_programs(1) - 1
    g, sc, sh = g_ref[...], sc_ref[...], sh_ref[...]
    h_ref[BF16_ROWS:BF16_ROWS + tm, :] = _norm_mod(x_ref[...], g, sc, sh).astype(BF16)
    hp = _norm_mod(xp_ref[...], g, sc, sh)
    hn = _norm_mod(xn_ref[...], g, sc, sh)
    h_ref[0:BF16_ROWS, :] = jnp.where(i > 0, hp, 0.0).astype(BF16)
    h_ref[BF16_ROWS + tm:, :] = jnp.where(i < last, hn, 0.0).astype(BF16)


def _conv3_rows(z_ref, cw, cb, tm):
    o = BF16_ROWS
    return (z_ref[o - 1:o - 1 + tm, :] * cw[0:1, :] + z_ref[o:o + tm, :] * cw[1:2, :]
            + z_ref[o + 1:o + 1 + tm, :] * cw[2:3, :] + cb)


def _proj_plain_kernel(x_ref, g_ref, sc_ref, sh_ref, w_ref, o_ref, h_ref):
    @pl.when(pl.program_id(2) == 0)
    def _():
        _fill_h(h_ref, x_ref, g_ref, sc_ref, sh_ref)

    o_ref[...] = _dot(h_ref[...], w_ref[...]).astype(o_ref.dtype)


def _proj_halfrope_kernel(x_ref, g_ref, sc_ref, sh_ref, w_ref, cos_ref, sin_ref, o_ref, h_ref, *, dk):
    @pl.when(pl.program_id(2) == 0)
    def _():
        _fill_h(h_ref, x_ref, g_ref, sc_ref, sh_ref)

    z = _dot(h_ref[...], w_ref[...])
    cos, sin = cos_ref[...], sin_ref[...]
    half = dk // 2
    for c0 in range(0, z.shape[1], dk):
        x1, x2 = z[:, c0:c0 + half], z[:, c0 + half:c0 + dk]
        o_ref[:, c0:c0 + half] = (x1 * cos - x2 * sin).astype(o_ref.dtype)
        o_ref[:, c0 + half:c0 + dk] = (x1 * sin + x2 * cos).astype(o_ref.dtype)


def _proj_conv_kernel(x_ref, xp_ref, xn_ref, g_ref, sc_ref, sh_ref, w_ref, cw_ref, cb_ref,
                      o_ref, h_ref, z_ref, *, tm):
    @pl.when(pl.program_id(2) == 0)
    def _():
        _fill_h_halo(h_ref, x_ref, xp_ref, xn_ref, g_ref, sc_ref, sh_ref, tm)

    z = _dot(h_ref[...], w_ref[...])
    for c in range(o_ref.shape[0]):
        cols = slice(c * LANES, (c + 1) * LANES)
        z_ref[c] = z[:, cols]
        o_ref[c] = _conv3_rows(z_ref.at[c], cw_ref[:, cols], cb_ref[:, cols], tm).astype(o_ref.dtype)


def _proj_rope_kernel(x_ref, g_ref, sc_ref, sh_ref, w_ref, flag_ref, ones_ref, cos_ref, sin_ref,
                      o_ref, h_ref, *, dh):
    @pl.when(pl.program_id(2) == 0)
    def _():
        _fill_h(h_ref, x_ref, g_ref, sc_ref, sh_ref)

    z = _dot(h_ref[...], w_ref[...])
    cos, sin = cos_ref[...], sin_ref[...]
    ones = ones_ref[...]
    for s in range(z.shape[1] // LANES):
        sl = slice(s * LANES, (s + 1) * LANES)
        zs = z[:, sl]
        ms = _dot((zs * zs).astype(BF16), ones) * (1.0 / dh)
        zs = zs * jnp.where(flag_ref[:, sl] > 0.0, lax.rsqrt(ms + NORM_EPS), 1.0)
        o_ref[:, sl] = (zs * cos + pltpu.roll(zs, LANES // 2, axis=1) * sin).astype(o_ref.dtype)


def _mod_specs(D):
    return [pl.BlockSpec((1, D), lambda b, i, j: (0, 0)),
            pl.BlockSpec((None, 1, D), lambda b, i, j: (b, 0, 0)),
            pl.BlockSpec((None, 1, D), lambda b, i, j: (b, 0, 0))]


def _halo_specs(tm, D, L):
    hb = tm // BF16_ROWS
    nhb = L // BF16_ROWS
    return [pl.BlockSpec((None, tm, D), lambda b, i, j: (b, i, 0)),
            pl.BlockSpec((None, BF16_ROWS, D), lambda b, i, j: (b, jnp.maximum(i * hb - 1, 0), 0)),
            pl.BlockSpec((None, BF16_ROWS, D), lambda b, i, j: (b, jnp.minimum((i + 1) * hb, nhb - 1), 0))]


def _proj_in(x, g, sc, sh, w, *, tn):
    B, L, D = x.shape
    N = w.shape[1]
    tm = _row_tile(L)
    return pl.pallas_call(
        _proj_plain_kernel,
        grid=(B, L // tm, N // tn),
        in_specs=[pl.BlockSpec((None, tm, D), lambda b, i, j: (b, i, 0))] + _mod_specs(D)
        + [pl.BlockSpec((D, tn), lambda b, i, j: (0, j))],
        out_specs=pl.BlockSpec((None, tm, tn), lambda b, i, j: (b, i, j)),
        out_shape=jax.ShapeDtypeStruct((B, L, N), BF16),
        scratch_shapes=[pltpu.VMEM((tm, D), BF16)],
        compiler_params=_cparams(("parallel", "parallel", "arbitrary"),
                                 [((tm, D), F32), ((D, tn), BF16), ((tm, tn), BF16)],
                                 [((tm, D), BF16), ((tm, tn), F32)]),
        name="proj_in",
    )(x, g, sc, sh, w)


def _proj_in_halfrope(x, g, sc, sh, w, cos, sin, *, tn, rope_tiles, dk):
    B, L, D = x.shape
    N = w.shape[1]
    tm = _row_tile(L)
    half = dk // 2
    table = lambda b, i, j: (jnp.minimum(j, rope_tiles), i, 0)
    return pl.pallas_call(
        functools.partial(_proj_halfrope_kernel, dk=dk),
        grid=(B, L // tm, N // tn),
        in_specs=[pl.BlockSpec((None, tm, D), lambda b, i, j: (b, i, 0))] + _mod_specs(D)
        + [pl.BlockSpec((D, tn), lambda b, i, j: (0, j)),
           pl.BlockSpec((None, tm, half), table),
           pl.BlockSpec((None, tm, half), table)],
        out_specs=pl.BlockSpec((None, tm, tn), lambda b, i, j: (b, i, j)),
        out_shape=jax.ShapeDtypeStruct((B, L, N), BF16),
        scratch_shapes=[pltpu.VMEM((tm, D), BF16)],
        compiler_params=_cparams(("parallel", "parallel", "arbitrary"),
                                 [((tm, D), F32), ((D, tn), BF16), ((tm, tn), BF16), ((tm, half), F32),
                                  ((tm, half), F32)],
                                 [((tm, D), BF16), ((tm, tn), F32), ((tm, tn), F32)]),
        name="proj_in_halfrope",
    )(x, g, sc, sh, w, cos, sin)


def _proj_in_conv(x, g, sc, sh, w, cw, cb, *, tn):
    B, L, D = x.shape
    N = w.shape[1]
    tm = _row_tile(L)
    te = tm + 2 * BF16_ROWS
    return pl.pallas_call(
        functools.partial(_proj_conv_kernel, tm=tm),
        grid=(B, L // tm, N // tn),
        in_specs=_halo_specs(tm, D, L) + _mod_specs(D)
        + [pl.BlockSpec((D, tn), lambda b, i, j: (0, j)),
           pl.BlockSpec((3, tn), lambda b, i, j: (0, j)),
           pl.BlockSpec((1, tn), lambda b, i, j: (0, j))],
        out_specs=pl.BlockSpec((None, tn // LANES, tm, LANES), lambda b, i, j: (b, j, i, 0)),
        out_shape=jax.ShapeDtypeStruct((B, N // LANES, L, LANES), BF16),
        scratch_shapes=[pltpu.VMEM((te, D), BF16), pltpu.VMEM((tn // LANES, te, LANES), F32)],
        compiler_params=_cparams(("parallel", "parallel", "arbitrary"),
                                 [((tm, D), F32), ((D, tn), BF16), ((tm, tn), BF16)],
                                 [((te, D), BF16), ((te, tn), F32), ((te, tn), F32)]),
        name="proj_in_conv",
    )(x, x, x, g, sc, sh, w, cw, cb.reshape(1, N))


def _proj_in_rope(x, g, sc, sh, w, flag, cos, sin, *, tn, dh, q_tiles):
    B, L, D = x.shape
    N = w.shape[1]
    tm = _row_tile(L)
    table = lambda b, i, j: (jnp.maximum(j - (q_tiles - 1), 0), i, 0)
    return pl.pallas_call(
        functools.partial(_proj_rope_kernel, dh=dh),
        grid=(B, L // tm, N // tn),
        in_specs=[pl.BlockSpec((None, tm, D), lambda b, i, j: (b, i, 0))] + _mod_specs(D)
        + [pl.BlockSpec((D, tn), lambda b, i, j: (0, j)),
           pl.BlockSpec((1, tn), lambda b, i, j: (0, j)),
           pl.BlockSpec((LANES, LANES), lambda b, i, j: (0, 0)),
           pl.BlockSpec((None, tm, LANES), table),
           pl.BlockSpec((None, tm, LANES), table)],
        out_specs=pl.BlockSpec((None, tm, tn), lambda b, i, j: (b, i, j)),
        out_shape=jax.ShapeDtypeStruct((B, L, N), BF16),
        scratch_shapes=[pltpu.VMEM((tm, D), BF16)],
        compiler_params=_cparams(("parallel", "parallel", "arbitrary"),
                                 [((tm, D), F32), ((D, tn), BF16), ((tm, tn), BF16), ((tm, LANES), F32),
                                  ((tm, LANES), F32), ((LANES, LANES), BF16)],
                                 [((tm, D), BF16), ((tm, tn), F32), ((tm, tn), F32), ((tm, tn), F32)]),
        name="proj_in_rope",
    )(x, g, sc, sh, w, flag, jnp.ones((LANES, LANES), BF16), cos, sin)


def _proj_out_kernel(a_ref, w_ref, x_ref, gate_ref, o_ref):
    o_ref[...] = x_ref[...] + gate_ref[...] * _dot(a_ref[...], w_ref[...])


def _proj_out_gated_kernel(a_ref, m_ref, w_ref, x_ref, gate_ref, o_ref):
    a = jnp.concatenate([(a_ref[c].astype(F32) * m_ref[c].astype(F32)).astype(BF16)
                         for c in range(a_ref.shape[0])], axis=1)
    o_ref[...] = x_ref[...] + gate_ref[...] * _dot(a, w_ref[...])


def _proj_out(a, w, x, gate, mult=None):
    blocked = a.ndim == 4
    B, L = x.shape[:2]
    K, D = w.shape
    tm = _row_tile(L)
    if blocked:
        a_specs = [pl.BlockSpec((None, K // LANES, tm, LANES), lambda b, i: (b, 0, i, 0))] * 2
        operands = (a, mult)
    else:
        a_specs = [pl.BlockSpec((None, tm, K), lambda b, i: (b, i, 0))]
        operands = (a,)
    return pl.pallas_call(
        _proj_out_gated_kernel if blocked else _proj_out_kernel,
        grid=(B, L // tm),
        in_specs=a_specs + [
                  pl.BlockSpec((K, D), lambda b, i: (0, 0)),
                  pl.BlockSpec((None, tm, D), lambda b, i: (b, i, 0)),
                  pl.BlockSpec((None, 1, D), lambda b, i: (b, 0, 0))],
        out_specs=pl.BlockSpec((None, tm, D), lambda b, i: (b, i, 0)),
        out_shape=jax.ShapeDtypeStruct((B, L, D), F32),
        compiler_params=_cparams(("parallel", "parallel"),
                                 [((tm, K), BF16)] * len(operands)
                                 + [((K, D), BF16), ((tm, D), F32), ((tm, D), F32)],
                                 [((tm, D), F32), ((tm, K), F32)]),
        name="proj_out",
    )(*operands, w, x, gate)


def _ffn_kernel(x_ref, xp_ref, xn_ref, g_ref, sc_ref, sh_ref, gate_ref, wg_ref, wv_ref, cw_ref, cb_ref,
                wd_ref, o_ref, h_ref, z_ref, *, tm, tf):
    _fill_h_halo(h_ref, x_ref, xp_ref, xn_ref, g_ref, sc_ref, sh_ref, tm)
    pieces = [slice(c * tf, (c + 1) * tf) for c in range(wg_ref.shape[1] // tf)]
    vals = []
    for c, cols in enumerate(pieces):
        z_ref[c] = _dot(h_ref[...], wg_ref[:, cols])
        vals.append(_dot(h_ref[BF16_ROWS:BF16_ROWS + tm, :], wv_ref[:, cols]))
    acc = None
    for c, cols in enumerate(pieces):
        a = _conv3_rows(z_ref.at[c], cw_ref[:, cols], cb_ref[:, cols], tm)
        part = _dot((_silu(a) * vals[c]).astype(BF16), wd_ref[cols, :])
        acc = part if acc is None else acc + part
    o_ref[...] = x_ref[...] + gate_ref[...] * acc


def _ffn(x, g, sc, sh, gate, wg, wv, cw, cb, wd):
    B, L, D = x.shape
    F = wg.shape[1]
    tm = min(L, 512)
    te = tm + 2 * BF16_ROWS
    tf = F // 2
    const = lambda b, i, j: (0, 0)
    resident = lambda shape: pl.BlockSpec(shape, const, pipeline_mode=pl.Buffered(1))
    return pl.pallas_call(
        functools.partial(_ffn_kernel, tm=tm, tf=tf),
        grid=(B, L // tm, 1),
        in_specs=_halo_specs(tm, D, L) + _mod_specs(D)
        + [pl.BlockSpec((None, 1, D), lambda b, i, j: (b, 0, 0)),
           resident((D, F)), resident((D, F)), resident((3, F)), resident((1, F)), resident((F, D))],
        out_specs=pl.BlockSpec((None, tm, D), lambda b, i, j: (b, i, 0)),
        out_shape=jax.ShapeDtypeStruct((B, L, D), F32),
        scratch_shapes=[pltpu.VMEM((te, D), BF16), pltpu.VMEM((F // tf, te, tf), F32)],
        compiler_params=_cparams(("parallel", "parallel", "arbitrary"),
                                 [((tm, D), F32), ((tm, D), F32)],
                                 [((D, F), BF16), ((D, F), BF16), ((F, D), BF16), ((te, D), BF16),
                                  ((te, F), F32), ((tm, F), F32), ((tm, tf), F32), ((tm, tf), BF16),
                                  ((tm, D), F32), ((tm, D), F32)]),
        name="ffn",
    )(x, x, x, g, sc, sh, gate, wg, wv, cw, cb.reshape(1, F), wd)


def _fft_dims(L):
    N = 2 * L
    p = N.bit_length() - 1
    n1 = 1 << ((p + 1) // 2)
    return N, n1, N // n1


def _k1_pad(n1):
    return n1 // 2 + 8


def _slab_rows(k1p):
    return 2 * k1p + SUBLANES


def _fft_tables(L):
    N, n1, n2 = _fft_dims(L)
    k1p = _k1_pad(n1)
    k1 = np.arange(k1p)
    valid = (k1 <= n1 // 2).astype(np.float64)
    weight = np.where((k1 == 0) | (k1 == n1 // 2), 1.0, 2.0) * valid
    m1 = np.arange(n1 // 2)
    n2v = np.arange(n2)
    n_idx = n2 * m1[None, :] + n2v[:, None]
    phi = 2.0 * np.pi * k1[None, :, None] * n_idx[:, None, :] / N
    s1 = np.concatenate([np.cos(phi) * valid[None, :, None], -np.sin(phi) * valid[None, :, None]], axis=1)
    phit = np.transpose(phi, (0, 2, 1))
    gl = np.concatenate([np.cos(phit) * weight[None, None, :], -np.sin(phit) * weight[None, None, :]],
                        axis=2) / N
    th = 2.0 * np.pi * np.outer(n2v, n2v) / n2
    c, s = np.cos(th), np.sin(th)
    f2 = np.block([[c, s], [-s, c]])
    f2c = np.block([[c, -s], [s, c]])

    def cat(a):
        t = jnp.asarray(a, F32).astype(BF16)
        return jnp.concatenate([t, t], axis=-1)

    return dict(N=N, n1=n1, n2=n2, k1p=k1p, s1=cat(s1), gl=cat(gl), f2=cat(f2), f2c=cat(f2c))


def _hy_positions(L):
    idx = np.concatenate([np.arange(L), np.array([0]), np.arange(L - 1, 0, -1)])
    t = idx / (L - 1)
    ang = (2.0 * np.pi / L) * idx
    bands = np.linspace(1e-4, HY_BANDS - 1, HY_BANDS)
    z = np.concatenate([t[:, None], np.cos(bands[None, :] * ang[:, None]), -np.sin(bands[None, :] * ang[:, None])],
                       axis=1)
    zp = np.zeros((2 * L, LANES))
    zp[:, : z.shape[1]] = z
    return jnp.asarray(zp, F32)


def _hy_mlp_kernel(z_ref, w1_ref, b1_ref, w2_ref, b2_ref, w3_ref, fr_ref, dec_ref, taps_ref, asum_ref, *, L, tr):
    i = pl.program_id(0)
    z = z_ref[...]
    fr = fr_ref[...]
    h = jnp.sin(fr * (_dot3(z, w1_ref[...]) + b1_ref[...]))
    h = jnp.sin(fr * (_dot3(h, w2_ref[...]) + b2_ref[...]))
    h = _dot3(h, w3_ref[...])
    t = z[:, 0:1]
    taps = h * jnp.exp(-t * jnp.abs(dec_ref[...]))
    row = i * tr + lax.broadcasted_iota(jnp.int32, (tr, 1), 0)
    taps = jnp.where(row == L, 0.0, taps)
    taps_ref[...] = taps

    @pl.when(i == 0)
    def _():
        asum_ref[...] = jnp.zeros_like(asum_ref)

    asum_ref[...] += jnp.sum(jnp.abs(taps), axis=0, keepdims=True)


def _hy_filter_taps(L, w1, b1, w2, b2, w3, freq, decay):
    W = decay.shape[1]
    E, O = w1.shape
    tr = 512
    half = L // tr
    z = _hy_positions(L)
    w1p = jnp.zeros((LANES, O), F32).at[:E].set(w1)
    return pl.pallas_call(
        functools.partial(_hy_mlp_kernel, L=L, tr=tr),
        grid=(2 * L // tr,),
        in_specs=[pl.BlockSpec((tr, LANES), lambda i: (i, 0)),
                  pl.BlockSpec((LANES, O), lambda i: (0, 0)),
                  pl.BlockSpec((1, O), lambda i: (0, 0)),
                  pl.BlockSpec((O, O), lambda i: (0, 0)),
                  pl.BlockSpec((1, O), lambda i: (0, 0)),
                  pl.BlockSpec((O, W), lambda i: (0, i // half)),
                  pl.BlockSpec((1, O), lambda i: (0, 0)),
                  pl.BlockSpec((None, 1, W), lambda i: (i // half, 0, 0))],
        out_specs=[pl.BlockSpec((tr, W), lambda i: (i, 0)),
                   pl.BlockSpec((1, W), lambda i: (0, 0))],
        out_shape=[jax.ShapeDtypeStruct((2 * L, W), F32), jax.ShapeDtypeStruct((1, W), F32)],
        compiler_params=_cparams(("arbitrary",), [((tr, W), F32), ((O, W), F32), ((tr, LANES), F32)],
                                 [((tr, W), F32), ((tr, W), F32)]),
        name="hyena_filter_mlp",
    )(z, w1p, b1.reshape(1, O), w2, b2.reshape(1, O), w3, freq.reshape(1, O), decay.reshape(2, 1, W))


FFT_UNROLL = 16


def _fft_stage1(src_ref, src_off, a_ref, s1_ref, n1, n2, k1p, sign=None, src_off2=None):
    slab = _slab_rows(k1p)

    def body(j, carry):
        d = src_ref[pl.ds(src_off + j, n1 // 2, stride=n2), :]
        r = _dot_tab(s1_ref[j], d)
        if src_off2 is not None:
            d2 = src_ref[pl.ds(src_off2 + j, n1 // 2, stride=n2), :]
            r = r + sign * _dot_tab(s1_ref[j], d2)
        a_ref[pl.ds(pl.multiple_of(j * slab, 8), 2 * k1p), :] = r
        return carry

    lax.fori_loop(0, n2, body, 0, unroll=FFT_UNROLL)


def _load_k1(a_ref, k, n2, k1p):
    br = a_ref[pl.ds(k, n2, stride=_slab_rows(k1p)), :]
    bi = a_ref[pl.ds(k1p + k, n2, stride=_slab_rows(k1p)), :]
    return jnp.concatenate([br, bi], axis=0)


def _hy_spec_kernel(taps_ref, asum_ref, s1_ref, f2_ref, h_ref, a_ref, *, L, n1, n2, k1p):
    r = lax.broadcasted_iota(jnp.int32, (2 * k1p, 1), 0)
    k1 = jnp.where(r < k1p, r, r - k1p)
    sign = (1 - 2 * (k1 & 1)).astype(F32)
    _fft_stage1(taps_ref, 0, a_ref, s1_ref, n1, n2, k1p, sign=sign, src_off2=L)
    inv = 1.0 / asum_ref[...]

    def body(k, carry):
        h_ref[k] = _dot_tab(f2_ref[...], _load_k1(a_ref, k, n2, k1p)) * inv
        return carry

    lax.fori_loop(0, n1 // 2, body, 0, unroll=2)
    body(n1 // 2, 0)


def _hy_spectrum(taps, asum, tabs):
    n1, n2, k1p = tabs["n1"], tabs["n2"], tabs["k1p"]
    N, W = taps.shape
    L = N // 2
    cb = LANES
    k1v = n1 // 2 + 1
    s1, f2 = tabs["s1"], tabs["f2"]
    return pl.pallas_call(
        functools.partial(_hy_spec_kernel, L=L, n1=n1, n2=n2, k1p=k1p),
        grid=(W // cb,),
        in_specs=[pl.BlockSpec((N, cb), lambda c: (0, c)),
                  pl.BlockSpec((1, cb), lambda c: (0, c)),
                  pl.BlockSpec(s1.shape, lambda c: (0, 0, 0), pipeline_mode=pl.Buffered(1)),
                  pl.BlockSpec(f2.shape, lambda c: (0, 0), pipeline_mode=pl.Buffered(1))],
        out_specs=pl.BlockSpec((k1v, 2 * n2, cb), lambda c: (0, 0, c)),
        out_shape=jax.ShapeDtypeStruct((k1v, 2 * n2, W), F32),
        scratch_shapes=[pltpu.VMEM((n2 * _slab_rows(k1p), cb), F32)],
        compiler_params=_cparams(("parallel",),
                                 [((N, cb), F32), ((k1v, 2 * n2, cb), F32)],
                                 [((n2 * _slab_rows(k1p), cb), F32), (s1.shape, BF16), (f2.shape, BF16)]),
        name="hyena_filter_spectrum",
    )(taps, asum, s1, f2)


def _hy_conv_kernel(x1_ref, v_ref, skip_ref, h_ref, s1_ref, f2_ref, f2c_ref, gl_ref,
                    o_ref, u_ref, a_ref, *, n1, n2, k1p):
    u_ref[...] = x1_ref[...].astype(F32) * v_ref[...].astype(F32)
    _fft_stage1(u_ref, 0, a_ref, s1_ref, n1, n2, k1p)

    cb = u_ref.shape[1]

    def mid(groups):
        loaded = [jnp.concatenate([_load_k1(a_ref, k, n2, k1p) for k in ks], axis=1) for ks in groups]
        results = []
        for ks, rhs in zip(groups, loaded):
            x = _dot_tab(f2_ref[...], rhs)
            xr, xi = x[:n2], x[n2:]
            hr = jnp.concatenate([h_ref[k, :n2, :] for k in ks], axis=1)
            hi = jnp.concatenate([h_ref[k, n2:, :] for k in ks], axis=1)
            y = jnp.concatenate([xr * hr - xi * hi, xr * hi + xi * hr], axis=0)
            results.append(_dot_tab(f2c_ref[...], y))
        for ks, c in zip(groups, results):
            for p, k in enumerate(ks):
                a_ref[pl.ds(k, n2, stride=_slab_rows(k1p)), :] = c[:n2, p * cb:(p + 1) * cb]
                a_ref[pl.ds(k1p + k, n2, stride=_slab_rows(k1p)), :] = c[n2:, p * cb:(p + 1) * cb]

    def mid_oct(ko, carry):
        k = 8 * ko
        mid(((k, k + 1, k + 2, k + 3), (k + 4, k + 5, k + 6, k + 7)))
        return carry

    lax.fori_loop(0, n1 // 16, mid_oct, 0)
    mid(((n1 // 2,),))
    skip = skip_ref[...]

    def last(j, carry):
        rhs = a_ref[pl.ds(pl.multiple_of(j * _slab_rows(k1p), 8), 2 * k1p), :]
        y = _dot_tab(gl_ref[j], rhs)
        rows = pl.ds(j, n1 // 2, stride=n2)
        u_ref[rows, :] = y + u_ref[rows, :] * skip
        return carry

    lax.fori_loop(0, n2, last, 0, unroll=FFT_UNROLL)
    o_ref[...] = u_ref[...].astype(o_ref.dtype)


def _hy_conv(z, skip, spec, tabs):
    n1, n2, k1p = tabs["n1"], tabs["n2"], tabs["k1p"]
    B, nc3, L, cb = z.shape
    nc = nc3 // 3
    W = nc * cb
    k1v = n1 // 2 + 1
    tables = [tabs["s1"], tabs["f2"], tabs["f2c"], tabs["gl"]]

    def const_spec(t):
        return pl.BlockSpec(t.shape, (lambda c, b: (0, 0, 0)) if t.ndim == 3 else (lambda c, b: (0, 0)),
                            pipeline_mode=pl.Buffered(1))

    return pl.pallas_call(
        functools.partial(_hy_conv_kernel, n1=n1, n2=n2, k1p=k1p),
        grid=(nc, B),
        in_specs=[pl.BlockSpec((None, None, L, cb), lambda c, b: (b, nc + c, 0, 0)),
                  pl.BlockSpec((None, None, L, cb), lambda c, b: (b, 2 * nc + c, 0, 0)),
                  pl.BlockSpec((1, cb), lambda c, b: (0, c)),
                  pl.BlockSpec((k1v, 2 * n2, cb), lambda c, b: (0, 0, c), pipeline_mode=pl.Buffered(1))]
        + [const_spec(t) for t in tables],
        out_specs=pl.BlockSpec((None, None, L, cb), lambda c, b: (b, c, 0, 0)),
        out_shape=jax.ShapeDtypeStruct((B, nc, L, cb), BF16),
        scratch_shapes=[pltpu.VMEM((L, cb), F32), pltpu.VMEM((n2 * _slab_rows(k1p), cb), F32)],
        compiler_params=_cparams(("parallel", "arbitrary"),
                                 [((L, cb), BF16)] * 3,
                                 [((L, cb), F32), ((n2 * _slab_rows(k1p), cb), F32), ((k1v, 2 * n2, cb), F32)]
                                 + [(t.shape, BF16) for t in tables]),
        name="hyena_long_conv",
    )(z, z, skip.reshape(1, W), spec, *tables)


def _ret_fwd_kernel(lg_ref, q_ref, k_ref, v_ref, o_ref, s_ref, *, T, C, H, DK, DV):
    @pl.when(pl.program_id(1) == 0)
    def _():
        s_ref[...] = jnp.zeros_like(s_ref)

    row = lax.broadcasted_iota(jnp.int32, (C, C), 0)
    col = lax.broadcasted_iota(jnp.int32, (C, C), 1)
    rel = (row - col).astype(F32)
    ridx = lax.broadcasted_iota(jnp.int32, (C, 1), 0).astype(F32)
    consts = []
    for h in range(H):
        lgf = lg_ref[0, h]
        lgb = lg_ref[1, h]
        dmat = jnp.where(rel > 0.0, jnp.exp(lgf * jnp.maximum(rel, 0.0)),
                         jnp.where(rel < 0.0, jnp.exp(lgb * jnp.maximum(-rel, 0.0)), 2.0))
        consts.append((dmat, jnp.exp(lgf * (ridx + 1.0)), jnp.exp(lgf * (C - 1.0 - ridx)),
                       jnp.exp(jnp.full((1, 1), C, F32) * lgf)))

    def chunk(c, carry):
        rows = pl.ds(pl.multiple_of(c * C, C), C)
        for h in range(H):
            dmat, q_scale, k_scale, s_decay = consts[h]
            ck = slice(h * DK, (h + 1) * DK)
            cv = slice(h * DV, (h + 1) * DV)
            q = q_ref[rows, ck]
            k = k_ref[rows, ck]
            v = v_ref[rows, cv]
            s = lax.dot_general(q, k, NT_DIMS, preferred_element_type=F32) * dmat
            st = s_ref[h]
            o_ref[rows, cv] = (_dot(s.astype(BF16), v) + _dot((q.astype(F32) * q_scale).astype(BF16),
                                                              st.astype(BF16))).astype(o_ref.dtype)
            s_ref[h] = s_decay * st + lax.dot_general((k.astype(F32) * k_scale).astype(BF16), v, TN_DIMS,
                                                      preferred_element_type=F32)
        return carry

    lax.fori_loop(0, T // C, chunk, 0)


def _ret_bwd_kernel(lg_ref, q_ref, k_ref, v_ref, g_ref, o1_ref, w_ref, x_ref, gate_ref, o_ref, s_ref, a_ref,
                    *, T, C, H, DK, DV):
    @pl.when(pl.program_id(1) == 0)
    def _():
        s_ref[...] = jnp.zeros_like(s_ref)

    ridx = lax.broadcasted_iota(jnp.int32, (C, 1), 0).astype(F32)
    nc = T // C
    consts = []
    for h in range(H):
        lgb = lg_ref[1, h]
        consts.append((jnp.exp(lgb * (C - ridx)), jnp.exp(lgb * ridx), jnp.exp(jnp.full((1, 1), C, F32) * lgb)))

    def chunk(cc, carry):
        rows = pl.ds(pl.multiple_of((nc - 1 - cc) * C, C), C)
        for h in range(H):
            q_scale, k_scale, s_decay = consts[h]
            ck = slice(h * DK, (h + 1) * DK)
            cv = slice(h * DV, (h + 1) * DV)
            q = q_ref[rows, ck].astype(F32)
            k = k_ref[rows, ck].astype(F32)
            v = v_ref[rows, cv]
            st = s_ref[h]
            o = o1_ref[rows, cv].astype(F32) + _dot((q * q_scale).astype(BF16), st.astype(BF16))
            y = o * lax.rsqrt(jnp.mean(o * o, axis=-1, keepdims=True) + NORM_EPS)
            a_ref[rows, cv] = (y * _silu(g_ref[rows, cv].astype(F32))).astype(a_ref.dtype)
            s_ref[h] = s_decay * st + lax.dot_general((k * k_scale).astype(BF16), v, TN_DIMS,
                                                      preferred_element_type=F32)
        return carry

    lax.fori_loop(0, nc, chunk, 0)
    o_ref[...] = x_ref[...] + gate_ref[...] * _dot(a_ref[...], w_ref[...])


def _retention(z, log_gamma, w_out, x, gate):
    B, L, W = z.shape
    D = w_out.shape[1]
    H, C = RET_HEADS, RET_CHUNK
    DK = W // (6 * H)
    DV = 2 * DK
    T = min(L, 512)
    nT = L // T
    kw = dict(T=T, C=C, H=H, DK=DK, DV=DV)
    smem = pl.BlockSpec(memory_space=pltpu.SMEM)
    blocks = [((T, H * DK), BF16)] * 2 + [((T, H * DV), BF16)] * 4
    state = [((H, DK, DV), F32), ((DK, DV), F32), ((DK, DV), F32)]
    o1 = pl.pallas_call(
        functools.partial(_ret_fwd_kernel, **kw),
        grid=(B, nT),
        in_specs=[smem,
                  pl.BlockSpec((None, T, H * DK), lambda b, i: (b, i, 0)),
                  pl.BlockSpec((None, T, H * DK), lambda b, i: (b, i, 1)),
                  pl.BlockSpec((None, T, H * DV), lambda b, i: (b, i, 1))],
        out_specs=pl.BlockSpec((None, T, H * DV), lambda b, i: (b, i, 0)),
        out_shape=jax.ShapeDtypeStruct((B, L, H * DV), BF16),
        scratch_shapes=[pltpu.VMEM((H, DK, DV), F32)],
        compiler_params=_cparams(("parallel", "arbitrary"), blocks, state),
        name="retention_fwd",
    )(log_gamma, z, z, z)
    return pl.pallas_call(
        functools.partial(_ret_bwd_kernel, **kw),
        grid=(B, nT),
        in_specs=[smem,
                  pl.BlockSpec((None, T, H * DK), lambda b, i: (b, nT - 1 - i, 0)),
                  pl.BlockSpec((None, T, H * DK), lambda b, i: (b, nT - 1 - i, 1)),
                  pl.BlockSpec((None, T, H * DV), lambda b, i: (b, nT - 1 - i, 1)),
                  pl.BlockSpec((None, T, H * DV), lambda b, i: (b, nT - 1 - i, 2)),
                  pl.BlockSpec((None, T, H * DV), lambda b, i: (b, nT - 1 - i, 0)),
                  pl.BlockSpec((H * DV, D), lambda b, i: (0, 0), pipeline_mode=pl.Buffered(1)),
                  pl.BlockSpec((None, T, D), lambda b, i: (b, nT - 1 - i, 0)),
                  pl.BlockSpec((None, 1, D), lambda b, i: (b, 0, 0))],
        out_specs=pl.BlockSpec((None, T, D), lambda b, i: (b, nT - 1 - i, 0)),
        out_shape=jax.ShapeDtypeStruct((B, L, D), F32),
        scratch_shapes=[pltpu.VMEM((H, DK, DV), F32), pltpu.VMEM((T, H * DV), BF16)],
        compiler_params=_cparams(("parallel", "arbitrary"), blocks[:-1] + [((T, D), F32)] * 2,
                                 state + [((H * DV, D), BF16), ((T, H * DV), BF16), ((T, D), F32)]),
        name="retention_bwd",
    )(log_gamma, z, z, z, z, o1, w_out, x, gate)


def _swa_kernel(sink_ref, q_ref, kp_ref, kc_ref, kn_ref, vp_ref, vc_ref, vn_ref, o_ref, *, HKV, G, BLK):
    i = pl.program_id(1)
    last = pl.num_programs(1) - 1
    R = G * BLK
    r = lax.broadcasted_iota(jnp.int32, (R, 3 * BLK), 0) % BLK
    c = lax.broadcasted_iota(jnp.int32, (R, 3 * BLK), 1)
    rel = r - (c - BLK)
    lo = jnp.where(i == 0, BLK, 0)
    hi = jnp.where(i == last, 2 * BLK, 3 * BLK)
    valid = (jnp.abs(rel) <= WINDOW) & (c >= lo) & (c < hi)
    grp = lax.broadcasted_iota(jnp.int32, (R, 1), 0) // BLK
    for j in range(HKV):
        sl = slice(j * LANES, (j + 1) * LANES)
        k = jnp.concatenate([kp_ref[:, sl], kc_ref[:, sl], kn_ref[:, sl]], axis=0)
        v = jnp.concatenate([vp_ref[:, sl], vc_ref[:, sl], vn_ref[:, sl]], axis=0)
        q = jnp.concatenate([q_ref[:, (j * G + g) * LANES:(j * G + g + 1) * LANES] for g in range(G)], axis=0)
        s = lax.dot_general(q, k, NT_DIMS, preferred_element_type=F32)
        s = jnp.where(valid, s, NEG_INF)
        sink = jnp.zeros((R, 1), F32)
        for g in range(G):
            sink = jnp.where(grp == g, sink_ref[0, j * G + g], sink)
        m = jnp.maximum(jnp.max(s, axis=-1, keepdims=True), sink)
        p = jnp.exp(s - m)
        denom = jnp.sum(p, axis=-1, keepdims=True) + jnp.exp(sink - m)
        o = _dot(p.astype(BF16), v) / denom
        for g in range(G):
            o_ref[:, (j * G + g) * LANES:(j * G + g + 1) * LANES] = o[g * BLK:(g + 1) * BLK].astype(o_ref.dtype)


def _swa_attention(z, sink):
    B, L, _ = z.shape
    BLK = ATTN_BLOCK
    nb = L // BLK
    G = SWA_HQ // SWA_HKV
    qw = SWA_HQ * LANES
    kvw = SWA_HKV * LANES
    kcol = qw // kvw
    prev = lambda b, i: (b, jnp.maximum(i - 1, 0), kcol)
    cur = lambda b, i: (b, i, kcol)
    nxt = lambda b, i: (b, jnp.minimum(i + 1, nb - 1), kcol)
    vprev = lambda b, i: (b, jnp.maximum(i - 1, 0), kcol + 1)
    vcur = lambda b, i: (b, i, kcol + 1)
    vnxt = lambda b, i: (b, jnp.minimum(i + 1, nb - 1), kcol + 1)
    kv = lambda f: pl.BlockSpec((None, BLK, kvw), f)
    return pl.pallas_call(
        functools.partial(_swa_kernel, HKV=SWA_HKV, G=G, BLK=BLK),
        grid=(B, nb),
        in_specs=[pl.BlockSpec(memory_space=pltpu.SMEM),
                  pl.BlockSpec((None, BLK, qw), lambda b, i: (b, i, 0)),
                  kv(prev), kv(cur), kv(nxt), kv(vprev), kv(vcur), kv(vnxt)],
        out_specs=pl.BlockSpec((None, BLK, qw), lambda b, i: (b, i, 0)),
        out_shape=jax.ShapeDtypeStruct((B, L, qw), BF16),
        compiler_params=_cparams(("parallel", "parallel"),
                                 [((BLK, qw), BF16)] * 2 + [((BLK, kvw), BF16)] * 6,
                                 [((G * BLK, 3 * BLK), F32)] * 4),
        name="swa_attention",
    )(sink.reshape(1, SWA_HQ), z, z, z, z, z, z, z)


def _swa_layout(w_in, q_gain, k_gain, w_out, L):
    D = w_in.shape[0]
    dh, hq, hkv = SWA_DH, SWA_HQ, SWA_HKV
    hf = dh // 2
    q_end, k_end = hq * dh, (hq + hkv) * dh

    def rot_slots(w, n):
        w = w.reshape(D, n, 2, hf)
        return jnp.pad(w, ((0, 0), (0, 0), (0, 0), (0, LANES // 2 - hf))).reshape(D, n * LANES)

    def val_slots(w, n):
        return jnp.pad(w.reshape(D, n, dh), ((0, 0), (0, 0), (0, LANES - dh))).reshape(D, n * LANES)

    w = jnp.concatenate([rot_slots(w_in[:, :q_end], hq), rot_slots(w_in[:, q_end:k_end], hkv),
                         val_slots(w_in[:, k_end:], hkv)], axis=1)

    def gain_slot(g):
        return jnp.pad(g.reshape(2, hf), ((0, 0), (0, LANES // 2 - hf))).reshape(1, LANES)

    flag = jnp.concatenate([jnp.ones(((hq + hkv) * LANES,), F32), jnp.zeros((hkv * LANES,), F32)]).reshape(1, -1)
    wo = jnp.pad(w_out.reshape(hq, dh, -1), ((0, 0), (0, LANES - dh), (0, 0))).reshape(hq * LANES, -1)

    inv = ROPE_THETA ** (-np.arange(0, dh, 2) / dh)
    ang = np.arange(L)[:, None] * inv[None, :]
    zero = np.zeros((L, LANES // 2 - hf))
    cos = jnp.asarray(np.concatenate([np.cos(ang), zero, np.cos(ang), zero], axis=1), F32)
    sin = jnp.asarray(np.concatenate([-np.sin(ang), zero, np.sin(ang), zero], axis=1), F32)
    gq = gain_slot(q_gain) * (dh ** -0.5)
    gk = gain_slot(k_gain)
    roll = lambda g: jnp.roll(g, LANES // 2, axis=1)
    cos3 = jnp.stack([cos * gq, cos * gk, jnp.ones_like(cos)])
    sin3 = jnp.stack([sin * roll(gq), sin * roll(gk), jnp.zeros_like(sin)])
    return w, flag, wo, cos3, sin3


def _hg_decays(f_ref, q_ref, rows, lb, tri, total_row, mid_row):
    f = lb + (1.0 - lb) * jax.nn.sigmoid(f_ref[rows, :].astype(F32))
    gl = jnp.log(f)
    cum = _cumdot(tri, gl)
    total = cum[total_row:total_row + 1, :]
    mid = cum[mid_row:mid_row + 1, :]
    q_dec = _silu(q_ref[rows, :].astype(F32)) * jnp.exp(cum - mid)
    k_inv = (1.0 - f) * jnp.exp(mid - cum)
    q_full = (q_dec * jnp.exp(mid)).astype(BF16)
    k_end = (k_inv * jnp.exp(total - mid)).astype(BF16)
    return q_dec.astype(BF16), k_inv.astype(BF16), q_full, k_end, jnp.exp(total)


def _hg_fwd_kernel(lb_ref, q_ref, i_ref, f_ref, o_ref, st_ref, *, T, C, H, DK, DV):
    @pl.when(pl.program_id(1) == 0)
    def _():
        st_ref[...] = jnp.zeros_like(st_ref)

    row = lax.broadcasted_iota(jnp.int32, (C, C), 0)
    col = lax.broadcasted_iota(jnp.int32, (C, C), 1)
    causal = row >= col
    tri = jnp.concatenate([causal.astype(BF16)] * 2, axis=1)
    lb = lb_ref[...]

    def chunk(c, carry):
        rows = pl.ds(pl.multiple_of(c * C, C), C)
        q_dec, k_inv, q_full, k_end, s_decay = _hg_decays(f_ref, q_ref, rows, lb, tri, C - 1, C // 2 - 1)
        v = i_ref[rows, :]
        for h in range(H):
            ck = slice(h * DK, (h + 1) * DK)
            cv = slice(h * DV, (h + 1) * DV)
            s = lax.dot_general(q_dec[:, ck], k_inv[:, ck], NT_DIMS, preferred_element_type=F32)
            s = jnp.where(causal, s, 0.0)
            st = st_ref[h]
            o_ref[rows, cv] = _dot(s.astype(BF16), v[:, cv]) + lax.dot_general(
                q_full[:, ck], st.astype(BF16), NT_DIMS, preferred_element_type=F32)
            st_ref[h] = st * s_decay[:, ck] + lax.dot_general(v[:, cv], k_end[:, ck], TN_DIMS,
                                                              preferred_element_type=F32)
        return carry

    lax.fori_loop(0, T // C, chunk, 0)


def _hg_bwd_kernel(lb_ref, gain_ref, q_ref, i_ref, f_ref, gate_ref, o1_ref, w_ref, x_ref, xgate_ref,
                   o_ref, st_ref, a_ref, *, T, C, H, DK, DV):
    @pl.when(pl.program_id(1) == 0)
    def _():
        st_ref[...] = jnp.zeros_like(st_ref)

    row = lax.broadcasted_iota(jnp.int32, (C, C), 0)
    col = lax.broadcasted_iota(jnp.int32, (C, C), 1)
    anti = col >= row
    tri = jnp.concatenate([anti.astype(BF16)] * 2, axis=1)
    lb = lb_ref[...]
    gain = gain_ref[...]
    nc = T // C

    def chunk(cc, carry):
        rows = pl.ds(pl.multiple_of((nc - 1 - cc) * C, C), C)
        q_dec, k_inv, q_full, k_end, s_decay = _hg_decays(f_ref, q_ref, rows, lb, tri, 0, C // 2)
        v = i_ref[rows, :]
        for h in range(H):
            ck = slice(h * DK, (h + 1) * DK)
            cv = slice(h * DV, (h + 1) * DV)
            s = lax.dot_general(q_dec[:, ck], k_inv[:, ck], NT_DIMS, preferred_element_type=F32)
            s = jnp.where(anti, s, 0.0)
            st = st_ref[h]
            o = o1_ref[rows, cv] + _dot(s.astype(BF16), v[:, cv]) + lax.dot_general(
                q_full[:, ck], st.astype(BF16), NT_DIMS, preferred_element_type=F32)
            y = o * lax.rsqrt(jnp.mean(o * o, axis=-1, keepdims=True) + NORM_EPS) * gain
            a_ref[rows, cv] = (y * _silu(gate_ref[rows, cv].astype(F32))).astype(a_ref.dtype)
            st_ref[h] = st * s_decay[:, ck] + lax.dot_general(v[:, cv], k_end[:, ck], TN_DIMS,
                                                              preferred_element_type=F32)
        return carry

    lax.fori_loop(0, nc, chunk, 0)
    o_ref[...] = x_ref[...] + xgate_ref[...] * _dot(a_ref[...], w_ref[...])


def _hgrn(z, lb, gain, w_out, x, xgate):
    B, L, _ = z.shape
    D = w_out.shape[1]
    H, C = HG_HEADS, HG_CHUNK
    HD = lb.shape[1]
    DK = HD // H
    DV = gain.shape[0]
    T = min(L, 512)
    nT = L // T
    kw = dict(T=T, C=C, H=H, DK=DK, DV=DV)
    blocks = [((T, HD), BF16)] * 5 + [((T, H * DV), F32)] * 2
    state = [((H, DV, DK), F32)] + [((C, HD), F32)] * 12
    lb3 = lb.reshape(2, 1, HD)
    o1 = pl.pallas_call(
        functools.partial(_hg_fwd_kernel, **kw),
        grid=(B, nT),
        in_specs=[pl.BlockSpec((None, 1, HD), lambda b, i: (0, 0, 0)),
                  pl.BlockSpec((None, T, HD), lambda b, i: (b, i, 0)),
                  pl.BlockSpec((None, T, H * DV), lambda b, i: (b, i, 1)),
                  pl.BlockSpec((None, T, HD), lambda b, i: (b, i, 2))],
        out_specs=pl.BlockSpec((None, T, H * DV), lambda b, i: (b, i, 0)),
        out_shape=jax.ShapeDtypeStruct((B, L, H * DV), F32),
        scratch_shapes=[pltpu.VMEM((H, DV, DK), F32)],
        compiler_params=_cparams(("parallel", "arbitrary"), blocks, state),
        name="hgrn_fwd",
    )(lb3, z, z, z)
    return pl.pallas_call(
        functools.partial(_hg_bwd_kernel, **kw),
        grid=(B, nT),
        in_specs=[pl.BlockSpec((None, 1, HD), lambda b, i: (1, 0, 0)),
                  pl.BlockSpec((1, DV), lambda b, i: (0, 0)),
                  pl.BlockSpec((None, T, HD), lambda b, i: (b, nT - 1 - i, 0)),
                  pl.BlockSpec((None, T, H * DV), lambda b, i: (b, nT - 1 - i, 1)),
                  pl.BlockSpec((None, T, HD), lambda b, i: (b, nT - 1 - i, 3)),
                  pl.BlockSpec((None, T, H * DV), lambda b, i: (b, nT - 1 - i, 4)),
                  pl.BlockSpec((None, T, H * DV), lambda b, i: (b, nT - 1 - i, 0)),
                  pl.BlockSpec((H * DV, D), lambda b, i: (0, 0), pipeline_mode=pl.Buffered(1)),
                  pl.BlockSpec((None, T, D), lambda b, i: (b, nT - 1 - i, 0)),
                  pl.BlockSpec((None, 1, D), lambda b, i: (b, 0, 0))],
        out_specs=pl.BlockSpec((None, T, D), lambda b, i: (b, nT - 1 - i, 0)),
        out_shape=jax.ShapeDtypeStruct((B, L, D), F32),
        scratch_shapes=[pltpu.VMEM((H, DV, DK), F32), pltpu.VMEM((T, H * DV), BF16)],
        compiler_params=_cparams(("parallel", "arbitrary"), blocks + [((T, D), F32)],
                                 state + [((H * DV, D), BF16), ((T, H * DV), BF16), ((T, D), F32)]),
        name="hgrn_bwd",
    )(lb3, gain.reshape(1, DV), z, z, z, z, o1, w_out, x, xgate)


def _ret_rope_tables(L, dk):
    inv = ROPE_THETA ** (-np.arange(0, dk, 2) / dk)
    ang = np.arange(L)[:, None] * inv[None, :]
    cos = np.stack([np.cos(ang), np.cos(ang) * dk ** -0.5, np.ones_like(ang)])
    sin = np.stack([np.sin(ang), np.sin(ang) * dk ** -0.5, np.zeros_like(ang)])
    return jnp.asarray(cos, F32), jnp.asarray(sin, F32)


def kernel(x_prompt, x_sample, c_prompt, c_sample, ada_w, ada_b, norm_g, hy_w_in, hy_conv_w, hy_conv_b, hy_w1, hy_b1, hy_w2, hy_b2, hy_w3, hy_freq, hy_decay, hy_skip, hy_w_out, ret_w_in, ret_decay, ret_w_out, swa_w_in, swa_q_gain, swa_k_gain, swa_sink, swa_w_out, hg_w_in, hg_lb, hg_gain, hg_w_out, ffn_w_gate, ffn_w_val, ffn_conv_w, ffn_conv_b, ffn_w_down):
    depth, D = norm_g.shape[0], norm_g.shape[2]
    groups = [(x_prompt, c_prompt), (x_sample, c_sample)]
    mods = _ada_mod(jnp.concatenate([c for _, c in groups], axis=0), ada_w, ada_b)

    bf = lambda w: w.astype(BF16)
    hy_w_in_b, hy_w_out_b = bf(hy_w_in), bf(hy_w_out)
    ret_w_in_b, ret_w_out_b = bf(ret_w_in), bf(ret_w_out)
    hg_w_in_b, hg_w_out_b = bf(hg_w_in), bf(hg_w_out)
    wg_b, wv_b, wd_b = bf(ffn_w_gate), bf(ffn_w_val), bf(ffn_w_down)
    hg_sm = jax.nn.softmax(hg_lb.astype(F32), axis=1)
    hg_lower = jnp.cumsum(hg_sm, axis=1) - hg_sm
    ret_log_gamma = -jnp.exp(ret_decay.astype(F32))

    outs = []
    row0 = 0
    for x, c in groups:
        B, L, _ = x.shape
        for layer in range(depth):
            kind, j = layer % N_MIXERS, layer // N_MIXERS
            mod = mods[layer, row0:row0 + B].reshape(B, N_MOD, 1, D)
            sh1, sc1, g1, sh2, sc2, g2 = (mod[:, m] for m in range(N_MOD))
            gn1 = norm_g[layer, 0].reshape(1, D)
            gn2 = norm_g[layer, 1].reshape(1, D)
            if kind == 0:
                tabs = _fft_tables(L)
                z = _proj_in_conv(x, gn1, sc1, sh1, hy_w_in_b[j], hy_conv_w[j], hy_conv_b[j], tn=1024)
                taps, asum = _hy_filter_taps(L, hy_w1[j], hy_b1[j], hy_w2[j], hy_b2[j], hy_w3[j], hy_freq[j],
                                             hy_decay[j])
                spec = _hy_spectrum(taps, asum, tabs)
                a = _hy_conv(z, hy_skip[j], spec, tabs)
                x = _proj_out(a, hy_w_out_b[j], x, g1, mult=z)
            elif kind == 1:
                dk = D // RET_HEADS
                cos, sin = _ret_rope_tables(L, dk)
                z = _proj_in_halfrope(x, gn1, sc1, sh1, ret_w_in_b[j], cos, sin, tn=RET_HEADS * dk, rope_tiles=2,
                                      dk=dk)
                x = _retention(z, ret_log_gamma[j], ret_w_out_b[j], x, g1)
            elif kind == 2:
                w, flag, wo, cos, sin = _swa_layout(swa_w_in[j], swa_q_gain[j], swa_k_gain[j], swa_w_out[j], L)
                z = _proj_in_rope(x, gn1, sc1, sh1, bf(w), flag, cos, sin, tn=SWA_HKV * LANES, dh=SWA_DH,
                                  q_tiles=SWA_HQ // SWA_HKV)
                a = _swa_attention(z, swa_sink[j])
                x = _proj_out(a, bf(wo), x, g1)
            else:
                z = _proj_in(x, gn1, sc1, sh1, hg_w_in_b[j], tn=1024)
                x = _hgrn(z, hg_lower[:, layer], hg_gain[j], hg_w_out_b[j], x, g1)
            x = _ffn(x, gn2, sc2, sh2, g2, wg_b[layer], wv_b[layer], ffn_conv_w[layer], ffn_conv_b[layer],
                     wd_b[layer])
        outs.append(x)
        row0 += B
    return tuple(outs)
```

```python
import functools
import math

import numpy as np
import jax
import jax.numpy as jnp
from jax import lax
from jax.experimental import pallas as pl
from jax.experimental.pallas import tpu as pltpu

F32 = jnp.float32
BF16 = jnp.bfloat16

NORM_EPS = 1e-6
N_MIXERS = 4
N_MOD = 6
HY_BANDS = 16
RET_HEADS = 4
RET_CHUNK = 256
SWA_HQ = 16
SWA_HKV = 4
SWA_DH = 64
WINDOW = 128
ATTN_BLOCK = 128
ROPE_THETA = 10000.0
NEG_INF = -1e30
HG_HEADS = 8
HG_CHUNK = 128

LANES = 128
SUBLANES = 8
BF16_ROWS = 16
VMEM_LIMIT_CAP = 60 * 1024 * 1024
VMEM_SLACK = 8 * 1024 * 1024

NT_DIMS = (((1,), (1,)), ((), ()))
TN_DIMS = (((0,), (0,)), ((), ()))


def _nbytes(shape, dtype):
    item = jnp.dtype(dtype).itemsize
    sub = SUBLANES * 4 // item
    dims = list(shape)
    dims[-1] = -(-dims[-1] // LANES) * LANES
    if len(dims) > 1:
        dims[-2] = -(-dims[-2] // sub) * sub
    return int(np.prod(dims)) * item


def _cparams(semantics, pipelined, resident):
    need = 2 * sum(_nbytes(s, d) for s, d in pipelined) + sum(_nbytes(s, d) for s, d in resident)
    return pltpu.CompilerParams(dimension_semantics=semantics,
                                vmem_limit_bytes=min(need + VMEM_SLACK, VMEM_LIMIT_CAP))


def _dot(a, b):
    return jnp.dot(a, b, preferred_element_type=F32)


def _split(x):
    hi = x.astype(BF16)
    lo = (x - hi.astype(F32)).astype(BF16)
    return hi, lo


def _dot_tab(t_cat, d):
    d_hi, d_lo = _split(d)
    return _dot(t_cat, jnp.concatenate([d_hi, d_lo], axis=0))


def _dot3(a, b):
    a_hi, a_lo = _split(a)
    b_hi, b_lo = _split(b)
    return _dot(a_hi, b_hi) + _dot(a_lo, b_hi) + _dot(a_hi, b_lo)


def _cumdot(tri2, g):
    g_hi, g_lo = _split(g)
    return _dot(tri2, jnp.concatenate([g_hi, g_lo], axis=0))


def _silu(x):
    return x * jax.nn.sigmoid(x)


def _norm_mod(x, g, sc, sh):
    xf = x.astype(F32)
    y = xf * lax.rsqrt(jnp.mean(xf * xf, axis=-1, keepdims=True) + NORM_EPS)
    return (y * g) * (1.0 + sc) + sh


def _row_tile(L):
    return min(L, 1024)


def _ada_kernel(c_ref, w_ref, b_ref, o_ref):
    cs = _silu(c_ref[...]).astype(BF16)
    o_ref[...] = _dot(cs, w_ref[...].astype(BF16)) + b_ref[...]


def _ada_mod(c_all, ada_w, ada_b):
    depth, D, N = ada_w.shape
    R = c_all.shape[0]
    tn = 1024
    return pl.pallas_call(
        _ada_kernel,
        grid=(depth, N // tn),
        in_specs=[pl.BlockSpec((R, D), lambda l, j: (0, 0)),
                  pl.BlockSpec((None, D, tn), lambda l, j: (l, 0, j)),
                  pl.BlockSpec((None, 1, tn), lambda l, j: (l, 0, j))],
        out_specs=pl.BlockSpec((None, R, tn), lambda l, j: (l, 0, j)),
        out_shape=jax.ShapeDtypeStruct((depth, R, N), F32),
        compiler_params=_cparams(("parallel", "parallel"),
                                 [((D, tn), F32), ((R, D), F32), ((R, tn), F32)], []),
        name="ada_mod",
    )(c_all, ada_w, ada_b.reshape(depth, 1, N))


def _fill_h(h_ref, x_ref, g_ref, sc_ref, sh_ref):
    h_ref[...] = _norm_mod(x_ref[...], g_ref[...], sc_ref[...], sh_ref[...]).astype(BF16)


def _fill_h_halo(h_ref, x_ref, xp_ref, xn_ref, g_ref, sc_ref, sh_ref, tm):
    i = pl.program_id(1)
    last = pl.num_programs(1) - 1
    g, sc, sh = g_ref[...], sc_ref[...], sh_ref[...]
    h_ref[BF16_ROWS:BF16_ROWS + tm, :] = _norm_mod(x_ref[...], g, sc, sh).astype(BF16)
    hp = _norm_mod(xp_ref[...], g, sc, sh)
    hn = _norm_mod(xn_ref[...], g, sc, sh)
    h_ref[0:BF16_ROWS, :] = jnp.where(i > 0, hp, 0.0).astype(BF16)
    h_ref[BF16_ROWS + tm:, :] = jnp.where(i < last, hn, 0.0).astype(BF16)


def _conv3_rows(z_ref, cw, cb, tm):
    o = BF16_ROWS
    return (z_ref[o - 1:o - 1 + tm, :] * cw[0:1, :] + z_ref[o:o + tm, :] * cw[1:2, :]
            + z_ref[o + 1:o + 1 + tm, :] * cw[2:3, :] + cb)


def _proj_plain_kernel(x_ref, g_ref, sc_ref, sh_ref, w_ref, o_ref, h_ref):
    @pl.when(pl.program_id(2) == 0)
    def _():
        _fill_h(h_ref, x_ref, g_ref, sc_ref, sh_ref)

    o_ref[...] = _dot(h_ref[...], w_ref[...]).astype(o_ref.dtype)


def _proj_halfrope_kernel(x_ref, g_ref, sc_ref, sh_ref, w_ref, cos_ref, sin_ref, o_ref, h_ref, *, dk):
    @pl.when(pl.program_id(2) == 0)
    def _():
        _fill_h(h_ref, x_ref, g_ref, sc_ref, sh_ref)

    z = _dot(h_ref[...], w_ref[...])
    cos, sin = cos_ref[...], sin_ref[...]
    half = dk // 2
    for c0 in range(0, z.shape[1], dk):
        x1, x2 = z[:, c0:c0 + half], z[:, c0 + half:c0 + dk]
        o_ref[:, c0:c0 + half] = (x1 * cos - x2 * sin).astype(o_ref.dtype)
        o_ref[:, c0 + half:c0 + dk] = (x1 * sin + x2 * cos).astype(o_ref.dtype)


def _proj_conv_kernel(x_ref, xp_ref, xn_ref, g_ref, sc_ref, sh_ref, w_ref, cw_ref, cb_ref,
                      o_ref, h_ref, z_ref, *, tm):
    @pl.when(pl.program_id(2) == 0)
    def _():
        _fill_h_halo(h_ref, x_ref, xp_ref, xn_ref, g_ref, sc_ref, sh_ref, tm)

    z = _dot(h_ref[...], w_ref[...])
    for c in range(o_ref.shape[0]):
        cols = slice(c * LANES, (c + 1) * LANES)
        z_ref[c] = z[:, cols]
        o_ref[c] = _conv3_rows(z_ref.at[c], cw_ref[:, cols], cb_ref[:, cols], tm).astype(o_ref.dtype)


def _proj_rope_kernel(x_ref, g_ref, sc_ref, sh_ref, w_ref, flag_ref, ones_ref, cos_ref, sin_ref,
                      o_ref, h_ref, *, dh):
    @pl.when(pl.program_id(2) == 0)
    def _():
        _fill_h(h_ref, x_ref, g_ref, sc_ref, sh_ref)

    z = _dot(h_ref[...], w_ref[...])
    cos, sin = cos_ref[...], sin_ref[...]
    ones = ones_ref[...]
    for s in range(z.shape[1] // LANES):
        sl = slice(s * LANES, (s + 1) * LANES)
        zs = z[:, sl]
        ms = _dot((zs * zs).astype(BF16), ones) * (1.0 / dh)
        zs = zs * jnp.where(flag_ref[:, sl] > 0.0, lax.rsqrt(ms + NORM_EPS), 1.0)
        o_ref[:, sl] = (zs * cos + pltpu.roll(zs, LANES // 2, axis=1) * sin).astype(o_ref.dtype)


def _mod_specs(D):
    return [pl.BlockSpec((1, D), lambda b, i, j: (0, 0)),
            pl.BlockSpec((None, 1, D), lambda b, i, j: (b, 0, 0)),
            pl.BlockSpec((None, 1, D), lambda b, i, j: (b, 0, 0))]


def _halo_specs(tm, D, L):
    hb = tm // BF16_ROWS
    nhb = L // BF16_ROWS
    return [pl.BlockSpec((None, tm, D), lambda b, i, j: (b, i, 0)),
            pl.BlockSpec((None, BF16_ROWS, D), lambda b, i, j: (b, jnp.maximum(i * hb - 1, 0), 0)),
            pl.BlockSpec((None, BF16_ROWS, D), lambda b, i, j: (b, jnp.minimum((i + 1) * hb, nhb - 1), 0))]


def _proj_in(x, g, sc, sh, w, *, tn):
    B, L, D = x.shape
    N = w.shape[1]
    tm = _row_tile(L)
    return pl.pallas_call(
        _proj_plain_kernel,
        grid=(B, L // tm, N // tn),
        in_specs=[pl.BlockSpec((None, tm, D), lambda b, i, j: (b, i, 0))] + _mod_specs(D)
        + [pl.BlockSpec((D, tn), lambda b, i, j: (0, j))],
        out_specs=pl.BlockSpec((None, tm, tn), lambda b, i, j: (b, i, j)),
        out_shape=jax.ShapeDtypeStruct((B, L, N), BF16),
        scratch_shapes=[pltpu.VMEM((tm, D), BF16)],
        compiler_params=_cparams(("parallel", "parallel", "arbitrary"),
                                 [((tm, D), F32), ((D, tn), BF16), ((tm, tn), BF16)],
                                 [((tm, D), BF16), ((tm, tn), F32)]),
        name="proj_in",
    )(x, g, sc, sh, w)


def _proj_in_halfrope(x, g, sc, sh, w, cos, sin, *, tn, rope_tiles, dk):
    B, L, D = x.shape
    N = w.shape[1]
    tm = _row_tile(L)
    half = dk // 2
    table = lambda b, i, j: (jnp.minimum(j, rope_tiles), i, 0)
    return pl.pallas_call(
        functools.partial(_proj_halfrope_kernel, dk=dk),
        grid=(B, L // tm, N // tn),
        in_specs=[pl.BlockSpec((None, tm, D), lambda b, i, j: (b, i, 0))] + _mod_specs(D)
        + [pl.BlockSpec((D, tn), lambda b, i, j: (0, j)),
           pl.BlockSpec((None, tm, half), table),
           pl.BlockSpec((None, tm, half), table)],
        out_specs=pl.BlockSpec((None, tm, tn), lambda b, i, j: (b, i, j)),
        out_shape=jax.ShapeDtypeStruct((B, L, N), BF16),
        scratch_shapes=[pltpu.VMEM((tm, D), BF16)],
        compiler_params=_cparams(("parallel", "parallel", "arbitrary"),
                                 [((tm, D), F32), ((D, tn), BF16), ((tm, tn), BF16), ((tm, half), F32),
                                  ((tm, half), F32)],
                                 [((tm, D), BF16), ((tm, tn), F32), ((tm, tn), F32)]),
        name="proj_in_halfrope",
    )(x, g, sc, sh, w, cos, sin)


def _proj_in_conv(x, g, sc, sh, w, cw, cb, *, tn):
    B, L, D = x.shape
    N = w.shape[1]
    tm = _row_tile(L)
    te = tm + 2 * BF16_ROWS
    return pl.pallas_call(
        functools.partial(_proj_conv_kernel, tm=tm),
        grid=(B, L // tm, N // tn),
        in_specs=_halo_specs(tm, D, L) + _mod_specs(D)
        + [pl.BlockSpec((D, tn), lambda b, i, j: (0, j)),
           pl.BlockSpec((3, tn), lambda b, i, j: (0, j)),
           pl.BlockSpec((1, tn), lambda b, i, j: (0, j))],
        out_specs=pl.BlockSpec((None, tn // LANES, tm, LANES), lambda b, i, j: (b, j, i, 0)),
        out_shape=jax.ShapeDtypeStruct((B, N // LANES, L, LANES), BF16),
        scratch_shapes=[pltpu.VMEM((te, D), BF16), pltpu.VMEM((tn // LANES, te, LANES), F32)],
        compiler_params=_cparams(("parallel", "parallel", "arbitrary"),
                                 [((tm, D), F32), ((D, tn), BF16), ((tm, tn), BF16)],
                                 [((te, D), BF16), ((te, tn), F32), ((te, tn), F32)]),
        name="proj_in_conv",
    )(x, x, x, g, sc, sh, w, cw, cb.reshape(1, N))


def _proj_in_rope(x, g, sc, sh, w, flag, cos, sin, *, tn, dh, q_tiles):
    B, L, D = x.shape
    N = w.shape[1]
    tm = _row_tile(L)
    table = lambda b, i, j: (jnp.maximum(j - (q_tiles - 1), 0), i, 0)
    return pl.pallas_call(
        functools.partial(_proj_rope_kernel, dh=dh),
        grid=(B, L // tm, N // tn),
        in_specs=[pl.BlockSpec((None, tm, D), lambda b, i, j: (b, i, 0))] + _mod_specs(D)
        + [pl.BlockSpec((D, tn), lambda b, i, j: (0, j)),
           pl.BlockSpec((1, tn), lambda b, i, j: (0, j)),
           pl.BlockSpec((LANES, LANES), lambda b, i, j: (0, 0)),
           pl.BlockSpec((None, tm, LANES), table),
           pl.BlockSpec((None, tm, LANES), table)],
        out_specs=pl.BlockSpec((None, tm, tn), lambda b, i, j: (b, i, j)),
        out_shape=jax.ShapeDtypeStruct((B, L, N), BF16),
        scratch_shapes=[pltpu.VMEM((tm, D), BF16)],
        compiler_params=_cparams(("parallel", "parallel", "arbitrary"),
                                 [((tm, D), F32), ((D, tn), BF16), ((tm, tn), BF16), ((tm, LANES), F32),
                                  ((tm, LANES), F32), ((LANES, LANES), BF16)],
                                 [((tm, D), BF16), ((tm, tn), F32), ((tm, tn), F32), ((tm, tn), F32)]),
        name="proj_in_rope",
    )(x, g, sc, sh, w, flag, jnp.ones((LANES, LANES), BF16), cos, sin)


def _proj_out_kernel(a_ref, w_ref, x_ref, gate_ref, o_ref):
    o_ref[...] = x_ref[...] + gate_ref[...] * _dot(a_ref[...], w_ref[...])


def _proj_out_gated_kernel(a_ref, m_ref, w_ref, x_ref, gate_ref, o_ref):
    a = jnp.concatenate([(a_ref[c].astype(F32) * m_ref[c].astype(F32)).astype(BF16)
                         for c in range(a_ref.shape[0])], axis=1)
    o_ref[...] = x_ref[...] + gate_ref[...] * _dot(a, w_ref[...])


def _proj_out(a, w, x, gate, mult=None):
    blocked = a.ndim == 4
    B, L = x.shape[:2]
    K, D = w.shape
    tm = _row_tile(L)
    if blocked:
        a_specs = [pl.BlockSpec((None, K // LANES, tm, LANES), lambda b, i: (b, 0, i, 0))] * 2
        operands = (a, mult)
    else:
        a_specs = [pl.BlockSpec((None, tm, K), lambda b, i: (b, i, 0))]
        operands = (a,)
    return pl.pallas_call(
        _proj_out_gated_kernel if blocked else _proj_out_kernel,
        grid=(B, L // tm),
        in_specs=a_specs + [
                  pl.BlockSpec((K, D), lambda b, i: (0, 0)),
                  pl.BlockSpec((None, tm, D), lambda b, i: (b, i, 0)),
                  pl.BlockSpec((None, 1, D), lambda b, i: (b, 0, 0))],
        out_specs=pl.BlockSpec((None, tm, D), lambda b, i: (b, i, 0)),
        out_shape=jax.ShapeDtypeStruct((B, L, D), F32),
        compiler_params=_cparams(("parallel", "parallel"),
                                 [((tm, K), BF16)] * len(operands)
                                 + [((K, D), BF16), ((tm, D), F32), ((tm, D), F32)],
                                 [((tm, D), F32), ((tm, K), F32)]),
        name="proj_out",
    )(*operands, w, x, gate)


def _ffn_kernel(x_ref, xp_ref, xn_ref, g_ref, sc_ref, sh_ref, gate_ref, wg_ref, wv_ref, cw_ref, cb_ref,
                wd_ref, o_ref, h_ref, z_ref, *, tm, tf):
    _fill_h_halo(h_ref, x_ref, xp_ref, xn_ref, g_ref, sc_ref, sh_ref, tm)
    pieces = [slice(c * tf, (c + 1) * tf) for c in range(wg_ref.shape[1] // tf)]
    vals = []
    for c, cols in enumerate(pieces):
        z_ref[c] = _dot(h_ref[...], wg_ref[:, cols])
        vals.append(_dot(h_ref[BF16_ROWS:BF16_ROWS + tm, :], wv_ref[:, cols]))
    acc = None
    for c, cols in enumerate(pieces):
        a = _conv3_rows(z_ref.at[c], cw_ref[:, cols], cb_ref[:, cols], tm)
        part = _dot((_silu(a) * vals[c]).astype(BF16), wd_ref[cols, :])
        acc = part if acc is None else acc + part
    o_ref[...] = x_ref[...] + gate_ref[...] * acc


def _ffn(x, g, sc, sh, gate, wg, wv, cw, cb, wd):
    B, L, D = x.shape
    F = wg.shape[1]
    tm = min(L, 512)
    te = tm + 2 * BF16_ROWS
    tf = F // 2
    const = lambda b, i, j: (0, 0)
    resident = lambda shape: pl.BlockSpec(shape, const, pipeline_mode=pl.Buffered(1))
    return pl.pallas_call(
        functools.partial(_ffn_kernel, tm=tm, tf=tf),
        grid=(B, L // tm, 1),
        in_specs=_halo_specs(tm, D, L) + _mod_specs(D)
        + [pl.BlockSpec((None, 1, D), lambda b, i, j: (b, 0, 0)),
           resident((D, F)), resident((D, F)), resident((3, F)), resident((1, F)), resident((F, D))],
        out_specs=pl.BlockSpec((None, tm, D), lambda b, i, j: (b, i, 0)),
        out_shape=jax.ShapeDtypeStruct((B, L, D), F32),
        scratch_shapes=[pltpu.VMEM((te, D), BF16), pltpu.VMEM((F // tf, te, tf), F32)],
        compiler_params=_cparams(("parallel", "parallel", "arbitrary"),
                                 [((tm, D), F32), ((tm, D), F32)],
                                 [((D, F), BF16), ((D, F), BF16), ((F, D), BF16), ((te, D), BF16),
                                  ((te, F), F32), ((tm, F), F32), ((tm, tf), F32), ((tm, tf), BF16),
                                  ((tm, D), F32), ((tm, D), F32)]),
        name="ffn",
    )(x, x, x, g, sc, sh, gate, wg, wv, cw, cb.reshape(1, F), wd)


def _fft_dims(L):
    N = 2 * L
    p = N.bit_length() - 1
    n1 = 1 << ((p + 1) // 2)
    return N, n1, N // n1


def _k1_pad(n1):
    return n1 // 2 + 8


def _slab_rows(k1p):
    return 2 * k1p + SUBLANES


def _fft_tables(L):
    N, n1, n2 = _fft_dims(L)
    k1p = _k1_pad(n1)
    k1 = np.arange(k1p)
    valid = (k1 <= n1 // 2).astype(np.float64)
    weight = np.where((k1 == 0) | (k1 == n1 // 2), 1.0, 2.0) * valid
    m1 = np.arange(n1 // 2)
    n2v = np.arange(n2)
    n_idx = n2 * m1[None, :] + n2v[:, None]
    phi = 2.0 * np.pi * k1[None, :, None] * n_idx[:, None, :] / N
    s1 = np.concatenate([np.cos(phi) * valid[None, :, None], -np.sin(phi) * valid[None, :, None]], axis=1)
    phit = np.transpose(phi, (0, 2, 1))
    gl = np.concatenate([np.cos(phit) * weight[None, None, :], -np.sin(phit) * weight[None, None, :]],
                        axis=2) / N
    th = 2.0 * np.pi * np.outer(n2v, n2v) / n2
    c, s = np.cos(th), np.sin(th)
    f2 = np.block([[c, s], [-s, c]])
    f2c = np.block([[c, -s], [s, c]])

    def cat(a):
        t = jnp.asarray(a, F32).astype(BF16)
        return jnp.concatenate([t, t], axis=-1)

    return dict(N=N, n1=n1, n2=n2, k1p=k1p, s1=cat(s1), gl=cat(gl), f2=cat(f2), f2c=cat(f2c))


def _hy_positions(L):
    idx = np.concatenate([np.arange(L), np.array([0]), np.arange(L - 1, 0, -1)])
    t = idx / (L - 1)
    ang = (2.0 * np.pi / L) * idx
    bands = np.linspace(1e-4, HY_BANDS - 1, HY_BANDS)
    z = np.concatenate([t[:, None], np.cos(bands[None, :] * ang[:, None]), -np.sin(bands[None, :] * ang[:, None])],
                       axis=1)
    zp = np.zeros((2 * L, LANES))
    zp[:, : z.shape[1]] = z
    return jnp.asarray(zp, F32)


def _hy_mlp_kernel(z_ref, w1_ref, b1_ref, w2_ref, b2_ref, w3_ref, fr_ref, dec_ref, taps_ref, asum_ref, *, L, tr):
    i = pl.program_id(0)
    z = z_ref[...]
    fr = fr_ref[...]
    h = jnp.sin(fr * (_dot3(z, w1_ref[...]) + b1_ref[...]))
    h = jnp.sin(fr * (_dot3(h, w2_ref[...]) + b2_ref[...]))
    h = _dot3(h, w3_ref[...])
    t = z[:, 0:1]
    taps = h * jnp.exp(-t * jnp.abs(dec_ref[...]))
    row = i * tr + lax.broadcasted_iota(jnp.int32, (tr, 1), 0)
    taps = jnp.where(row == L, 0.0, taps)
    taps_ref[...] = taps

    @pl.when(i == 0)
    def _():
        asum_ref[...] = jnp.zeros_like(asum_ref)

    asum_ref[...] += jnp.sum(jnp.abs(taps), axis=0, keepdims=True)


def _hy_filter_taps(L, w1, b1, w2, b2, w3, freq, decay):
    W = decay.shape[1]
    E, O = w1.shape
    tr = 512
    half = L // tr
    z = _hy_positions(L)
    w1p = jnp.zeros((LANES, O), F32).at[:E].set(w1)
    return pl.pallas_call(
        functools.partial(_hy_mlp_kernel, L=L, tr=tr),
        grid=(2 * L // tr,),
        in_specs=[pl.BlockSpec((tr, LANES), lambda i: (i, 0)),
                  pl.BlockSpec((LANES, O), lambda i: (0, 0)),
                  pl.BlockSpec((1, O), lambda i: (0, 0)),
                  pl.BlockSpec((O, O), lambda i: (0, 0)),
                  pl.BlockSpec((1, O), lambda i: (0, 0)),
                  pl.BlockSpec((O, W), lambda i: (0, i // half)),
                  pl.BlockSpec((1, O), lambda i: (0, 0)),
                  pl.BlockSpec((None, 1, W), lambda i: (i // half, 0, 0))],
        out_specs=[pl.BlockSpec((tr, W), lambda i: (i, 0)),
                   pl.BlockSpec((1, W), lambda i: (0, 0))],
        out_shape=[jax.ShapeDtypeStruct((2 * L, W), F32), jax.ShapeDtypeStruct((1, W), F32)],
        compiler_params=_cparams(("arbitrary",), [((tr, W), F32), ((O, W), F32), ((tr, LANES), F32)],
                                 [((tr, W), F32), ((tr, W), F32)]),
        name="hyena_filter_mlp",
    )(z, w1p, b1.reshape(1, O), w2, b2.reshape(1, O), w3, freq.reshape(1, O), decay.reshape(2, 1, W))


FFT_UNROLL = 16


def _fft_stage1(src_ref, src_off, a_ref, s1_ref, n1, n2, k1p, sign=None, src_off2=None):
    slab = _slab_rows(k1p)

    def body(j, carry):
        d = src_ref[pl.ds(src_off + j, n1 // 2, stride=n2), :]
        r = _dot_tab(s1_ref[j], d)
        if src_off2 is not None:
            d2 = src_ref[pl.ds(src_off2 + j, n1 // 2, stride=n2), :]
            r = r + sign * _dot_tab(s1_ref[j], d2)
        a_ref[pl.ds(pl.multiple_of(j * slab, 8), 2 * k1p), :] = r
        return carry

    lax.fori_loop(0, n2, body, 0, unroll=FFT_UNROLL)


def _load_k1(a_ref, k, n2, k1p):
    br = a_ref[pl.ds(k, n2, stride=_slab_rows(k1p)), :]
    bi = a_ref[pl.ds(k1p + k, n2, stride=_slab_rows(k1p)), :]
    return jnp.concatenate([br, bi], axis=0)


def _hy_spec_kernel(taps_ref, asum_ref, s1_ref, f2_ref, h_ref, a_ref, *, L, n1, n2, k1p):
    r = lax.broadcasted_iota(jnp.int32, (2 * k1p, 1), 0)
    k1 = jnp.where(r < k1p, r, r - k1p)
    sign = (1 - 2 * (k1 & 1)).astype(F32)
    _fft_stage1(taps_ref, 0, a_ref, s1_ref, n1, n2, k1p, sign=sign, src_off2=L)
    inv = 1.0 / asum_ref[...]

    def body(k, carry):
        h_ref[k] = _dot_tab(f2_ref[...], _load_k1(a_ref, k, n2, k1p)) * inv
        return carry

    lax.fori_loop(0, n1 // 2, body, 0, unroll=2)
    body(n1 // 2, 0)


def _hy_spectrum(taps, asum, tabs):
    n1, n2, k1p = tabs["n1"], tabs["n2"], tabs["k1p"]
    N, W = taps.shape
    L = N // 2
    cb = LANES
    k1v = n1 // 2 + 1
    s1, f2 = tabs["s1"], tabs["f2"]
    return pl.pallas_call(
        functools.partial(_hy_spec_kernel, L=L, n1=n1, n2=n2, k1p=k1p),
        grid=(W // cb,),
        in_specs=[pl.BlockSpec((N, cb), lambda c: (0, c)),
                  pl.BlockSpec((1, cb), lambda c: (0, c)),
                  pl.BlockSpec(s1.shape, lambda c: (0, 0, 0), pipeline_mode=pl.Buffered(1)),
                  pl.BlockSpec(f2.shape, lambda c: (0, 0), pipeline_mode=pl.Buffered(1))],
        out_specs=pl.BlockSpec((k1v, 2 * n2, cb), lambda c: (0, 0, c)),
        out_shape=jax.ShapeDtypeStruct((k1v, 2 * n2, W), F32),
        scratch_shapes=[pltpu.VMEM((n2 * _slab_rows(k1p), cb), F32)],
        compiler_params=_cparams(("parallel",),
                                 [((N, cb), F32), ((k1v, 2 * n2, cb), F32)],
                                 [((n2 * _slab_rows(k1p), cb), F32), (s1.shape, BF16), (f2.shape, BF16)]),
        name="hyena_filter_spectrum",
    )(taps, asum, s1, f2)


def _hy_conv_kernel(x1_ref, v_ref, skip_ref, h_ref, s1_ref, f2_ref, f2c_ref, gl_ref,
                    o_ref, u_ref, a_ref, *, n1, n2, k1p):
    u_ref[...] = x1_ref[...].astype(F32) * v_ref[...].astype(F32)
    _fft_stage1(u_ref, 0, a_ref, s1_ref, n1, n2, k1p)

    cb = u_ref.shape[1]

    def mid(groups):
        loaded = [jnp.concatenate([_load_k1(a_ref, k, n2, k1p) for k in ks], axis=1) for ks in groups]
        results = []
        for ks, rhs in zip(groups, loaded):
            x = _dot_tab(f2_ref[...], rhs)
            xr, xi = x[:n2], x[n2:]
            hr = jnp.concatenate([h_ref[k, :n2, :] for k in ks], axis=1)
            hi = jnp.concatenate([h_ref[k, n2:, :] for k in ks], axis=1)
            y = jnp.concatenate([xr * hr - xi * hi, xr * hi + xi * hr], axis=0)
            results.append(_dot_tab(f2c_ref[...], y))
        for ks, c in zip(groups, results):
            for p, k in enumerate(ks):
                a_ref[pl.ds(k, n2, stride=_slab_rows(k1p)), :] = c[:n2, p * cb:(p + 1) * cb]
                a_ref[pl.ds(k1p + k, n2, stride=_slab_rows(k1p)), :] = c[n2:, p * cb:(p + 1) * cb]

    def mid_oct(ko, carry):
        k = 8 * ko
        mid(((k, k + 1, k + 2, k + 3), (k + 4, k + 5, k + 6, k + 7)))
        return carry

    lax.fori_loop(0, n1 // 16, mid_oct, 0)
    mid(((n1 // 2,),))
    skip = skip_ref[...]

    def last(j, carry):
        rhs = a_ref[pl.ds(pl.multiple_of(j * _slab_rows(k1p), 8), 2 * k1p), :]
        y = _dot_tab(gl_ref[j], rhs)
        rows = pl.ds(j, n1 // 2, stride=n2)
        u_ref[rows, :] = y + u_ref[rows, :] * skip
        return carry

    lax.fori_loop(0, n2, last, 0, unroll=FFT_UNROLL)
    o_ref[...] = u_ref[...].astype(o_ref.dtype)


def _hy_conv(z, skip, spec, tabs):
    n1, n2, k1p = tabs["n1"], tabs["n2"], tabs["k1p"]
    B, nc3, L, cb = z.shape
    nc = nc3 // 3
    W = nc * cb
    k1v = n1 // 2 + 1
    tables = [tabs["s1"], tabs["f2"], tabs["f2c"], tabs["gl"]]

    def const_spec(t):
        return pl.BlockSpec(t.shape, (lambda c, b: (0, 0, 0)) if t.ndim == 3 else (lambda c, b: (0, 0)),
                            pipeline_mode=pl.Buffered(1))

    return pl.pallas_call(
        functools.partial(_hy_conv_kernel, n1=n1, n2=n2, k1p=k1p),
        grid=(nc, B),
        in_specs=[pl.BlockSpec((None, None, L, cb), lambda c, b: (b, nc + c, 0, 0)),
                  pl.BlockSpec((None, None, L, cb), lambda c, b: (b, 2 * nc + c, 0, 0)),
                  pl.BlockSpec((1, cb), lambda c, b: (0, c)),
                  pl.BlockSpec((k1v, 2 * n2, cb), lambda c, b: (0, 0, c), pipeline_mode=pl.Buffered(1))]
        + [const_spec(t) for t in tables],
        out_specs=pl.BlockSpec((None, None, L, cb), lambda c, b: (b, c, 0, 0)),
        out_shape=jax.ShapeDtypeStruct((B, nc, L, cb), BF16),
        scratch_shapes=[pltpu.VMEM((L, cb), F32), pltpu.VMEM((n2 * _slab_rows(k1p), cb), F32)],
        compiler_params=_cparams(("parallel", "arbitrary"),
                                 [((L, cb), BF16)] * 3,
                                 [((L, cb), F32), ((n2 * _slab_rows(k1p), cb), F32), ((k1v, 2 * n2, cb), F32)]
                                 + [(t.shape, BF16) for t in tables]),
        name="hyena_long_conv",
    )(z, z, skip.reshape(1, W), spec, *tables)


def _ret_fwd_kernel(lg_ref, q_ref, k_ref, v_ref, o_ref, s_ref, *, T, C, H, DK, DV):
    @pl.when(pl.program_id(1) == 0)
    def _():
        s_ref[...] = jnp.zeros_like(s_ref)

    row = lax.broadcasted_iota(jnp.int32, (C, C), 0)
    col = lax.broadcasted_iota(jnp.int32, (C, C), 1)
    rel = (row - col).astype(F32)
    ridx = lax.broadcasted_iota(jnp.int32, (C, 1), 0).astype(F32)
    consts = []
    for h in range(H):
        lgf = lg_ref[0, h]
        lgb = lg_ref[1, h]
        dmat = jnp.where(rel > 0.0, jnp.exp(lgf * jnp.maximum(rel, 0.0)),
                         jnp.where(rel < 0.0, jnp.exp(lgb * jnp.maximum(-rel, 0.0)), 2.0))
        consts.append((dmat, jnp.exp(lgf * (ridx + 1.0)), jnp.exp(lgf * (C - 1.0 - ridx)),
                       jnp.exp(jnp.full((1, 1), C, F32) * lgf)))

    def chunk(c, carry):
        rows = pl.ds(pl.multiple_of(c * C, C), C)
        for h in range(H):
            dmat, q_scale, k_scale, s_decay = consts[h]
            ck = slice(h * DK, (h + 1) * DK)
            cv = slice(h * DV, (h + 1) * DV)
            q = q_ref[rows, ck]
            k = k_ref[rows, ck]
            v = v_ref[rows, cv]
            s = lax.dot_general(q, k, NT_DIMS, preferred_element_type=F32) * dmat
            st = s_ref[h]
            o_ref[rows, cv] = (_dot(s.astype(BF16), v) + _dot((q.astype(F32) * q_scale).astype(BF16),
                                                              st.astype(BF16))).astype(o_ref.dtype)
            s_ref[h] = s_decay * st + lax.dot_general((k.astype(F32) * k_scale).astype(BF16), v, TN_DIMS,
                                                      preferred_element_type=F32)
        return carry

    lax.fori_loop(0, T // C, chunk, 0)


def _ret_bwd_kernel(lg_ref, q_ref, k_ref, v_ref, g_ref, o1_ref, w_ref, x_ref, gate_ref, o_ref, s_ref, a_ref,
                    *, T, C, H, DK, DV):
    @pl.when(pl.program_id(1) == 0)
    def _():
        s_ref[...] = jnp.zeros_like(s_ref)

    ridx = lax.broadcasted_iota(jnp.int32, (C, 1), 0).astype(F32)
    nc = T // C
    consts = []
    for h in range(H):
        lgb = lg_ref[1, h]
        consts.append((jnp.exp(lgb * (C - ridx)), jnp.exp(lgb * ridx), jnp.exp(jnp.full((1, 1), C, F32) * lgb)))

    def chunk(cc, carry):
        rows = pl.ds(pl.multiple_of((nc - 1 - cc) * C, C), C)
        for h in range(H):
            q_scale, k_scale, s_decay = consts[h]
            ck = slice(h * DK, (h + 1) * DK)
            cv = slice(h * DV, (h + 1) * DV)
            q = q_ref[rows, ck].astype(F32)
            k = k_ref[rows, ck].astype(F32)
            v = v_ref[rows, cv]
            st = s_ref[h]
            o = o1_ref[rows, cv].astype(F32) + _dot((q * q_scale).astype(BF16), st.astype(BF16))
            y = o * lax.rsqrt(jnp.mean(o * o, axis=-1, keepdims=True) + NORM_EPS)
            a_ref[rows, cv] = (y * _silu(g_ref[rows, cv].astype(F32))).astype(a_ref.dtype)
            s_ref[h] = s_decay * st + lax.dot_general((k * k_scale).astype(BF16), v, TN_DIMS,
                                                      preferred_element_type=F32)
        return carry

    lax.fori_loop(0, nc, chunk, 0)
    o_ref[...] = x_ref[...] + gate_ref[...] * _dot(a_ref[...], w_ref[...])


def _retention(z, log_gamma, w_out, x, gate):
    B, L, W = z.shape
    D = w_out.shape[1]
    H, C = RET_HEADS, RET_CHUNK
    DK = W // (6 * H)
    DV = 2 * DK
    T = min(L, 512)
    nT = L // T
    kw = dict(T=T, C=C, H=H, DK=DK, DV=DV)
    smem = pl.BlockSpec(memory_space=pltpu.SMEM)
    blocks = [((T, H * DK), BF16)] * 2 + [((T, H * DV), BF16)] * 4
    state = [((H, DK, DV), F32), ((DK, DV), F32), ((DK, DV), F32)]
    o1 = pl.pallas_call(
        functools.partial(_ret_fwd_kernel, **kw),
        grid=(B, nT),
        in_specs=[smem,
                  pl.BlockSpec((None, T, H * DK), lambda b, i: (b, i, 0)),
                  pl.BlockSpec((None, T, H * DK), lambda b, i: (b, i, 1)),
                  pl.BlockSpec((None, T, H * DV), lambda b, i: (b, i, 1))],
        out_specs=pl.BlockSpec((None, T, H * DV), lambda b, i: (b, i, 0)),
        out_shape=jax.ShapeDtypeStruct((B, L, H * DV), BF16),
        scratch_shapes=[pltpu.VMEM((H, DK, DV), F32)],
        compiler_params=_cparams(("parallel", "arbitrary"), blocks, state),
        name="retention_fwd",
    )(log_gamma, z, z, z)
    return pl.pallas_call(
        functools.partial(_ret_bwd_kernel, **kw),
        grid=(B, nT),
        in_specs=[smem,
                  pl.BlockSpec((None, T, H * DK), lambda b, i: (b, nT - 1 - i, 0)),
                  pl.BlockSpec((None, T, H * DK), lambda b, i: (b, nT - 1 - i, 1)),
                  pl.BlockSpec((None, T, H * DV), lambda b, i: (b, nT - 1 - i, 1)),
                  pl.BlockSpec((None, T, H * DV), lambda b, i: (b, nT - 1 - i, 2)),
                  pl.BlockSpec((None, T, H * DV), lambda b, i: (b, nT - 1 - i, 0)),
                  pl.BlockSpec((H * DV, D), lambda b, i: (0, 0), pipeline_mode=pl.Buffered(1)),
                  pl.BlockSpec((None, T, D), lambda b, i: (b, nT - 1 - i, 0)),
                  pl.BlockSpec((None, 1, D), lambda b, i: (b, 0, 0))],
        out_specs=pl.BlockSpec((None, T, D), lambda b, i: (b, nT - 1 - i, 0)),
        out_shape=jax.ShapeDtypeStruct((B, L, D), F32),
        scratch_shapes=[pltpu.VMEM((H, DK, DV), F32), pltpu.VMEM((T, H * DV), BF16)],
        compiler_params=_cparams(("parallel", "arbitrary"), blocks[:-1] + [((T, D), F32)] * 2,
                                 state + [((H * DV, D), BF16), ((T, H * DV), BF16), ((T, D), F32)]),
        name="retention_bwd",
    )(log_gamma, z, z, z, z, o1, w_out, x, gate)


def _swa_kernel(sink_ref, q_ref, kp_ref, kc_ref, kn_ref, vp_ref, vc_ref, vn_ref, o_ref, *, HKV, G, BLK):
    i = pl.program_id(1)
    last = pl.num_programs(1) - 1
    r = lax.broadcasted_iota(jnp.int32, (BLK, 3 * BLK), 0)
    c = lax.broadcasted_iota(jnp.int32, (BLK, 3 * BLK), 1)
    rel = r - (c - BLK)
    lo = jnp.where(i == 0, BLK, 0)
    hi = jnp.where(i == last, 2 * BLK, 3 * BLK)
    valid = (jnp.abs(rel) <= WINDOW) & (c >= lo) & (c < hi)
    bias = jnp.where(valid, 0.0, NEG_INF)
    for j in range(HKV):
        sl = slice(j * LANES, (j + 1) * LANES)
        k = jnp.concatenate([kp_ref[:, sl], kc_ref[:, sl], kn_ref[:, sl]], axis=0)
        v = jnp.concatenate([vp_ref[:, sl], vc_ref[:, sl], vn_ref[:, sl]], axis=0)
        q = jnp.concatenate([q_ref[:, (j * G + g) * LANES:(j * G + g + 1) * LANES] for g in range(G)], axis=0)
        s = lax.dot_general(q, k, NT_DIMS, preferred_element_type=F32)
        ps, inv = [], []
        for g in range(G):
            sg = s[g * BLK:(g + 1) * BLK] + bias
            sink = sink_ref[0, j * G + g]
            m = jnp.maximum(jnp.max(sg, axis=-1, keepdims=True), sink)
            p = jnp.exp(sg - m)
            inv.append(1.0 / (jnp.sum(p, axis=-1, keepdims=True) + jnp.exp(sink - m)))
            ps.append(p.astype(BF16))
        o = _dot(jnp.concatenate(ps, axis=0), v)
        for g in range(G):
            o_ref[:, (j * G + g) * LANES:(j * G + g + 1) * LANES] = (o[g * BLK:(g + 1) * BLK] * inv[g]).astype(
                o_ref.dtype)


def _swa_attention(z, sink):
    B, L, _ = z.shape
    BLK = ATTN_BLOCK
    nb = L // BLK
    G = SWA_HQ // SWA_HKV
    qw = SWA_HQ * LANES
    kvw = SWA_HKV * LANES
    kcol = qw // kvw
    prev = lambda b, i: (b, jnp.maximum(i - 1, 0), kcol)
    cur = lambda b, i: (b, i, kcol)
    nxt = lambda b, i: (b, jnp.minimum(i + 1, nb - 1), kcol)
    vprev = lambda b, i: (b, jnp.maximum(i - 1, 0), kcol + 1)
    vcur = lambda b, i: (b, i, kcol + 1)
    vnxt = lambda b, i: (b, jnp.minimum(i + 1, nb - 1), kcol + 1)
    kv = lambda f: pl.BlockSpec((None, BLK, kvw), f)
    return pl.pallas_call(
        functools.partial(_swa_kernel, HKV=SWA_HKV, G=G, BLK=BLK),
        grid=(B, nb),
        in_specs=[pl.BlockSpec(memory_space=pltpu.SMEM),
                  pl.BlockSpec((None, BLK, qw), lambda b, i: (b, i, 0)),
                  kv(prev), kv(cur), kv(nxt), kv(vprev), kv(vcur), kv(vnxt)],
        out_specs=pl.BlockSpec((None, BLK, qw), lambda b, i: (b, i, 0)),
        out_shape=jax.ShapeDtypeStruct((B, L, qw), BF16),
        compiler_params=_cparams(("parallel", "parallel"),
                                 [((BLK, qw), BF16)] * 2 + [((BLK, kvw), BF16)] * 6,
                                 [((G * BLK, 3 * BLK), F32)] * 4),
        name="swa_attention",
    )(sink.reshape(1, SWA_HQ), z, z, z, z, z, z, z)


def _swa_layout(w_in, q_gain, k_gain, w_out, L):
    D = w_in.shape[0]
    dh, hq, hkv = SWA_DH, SWA_HQ, SWA_HKV
    hf = dh // 2
    q_end, k_end = hq * dh, (hq + hkv) * dh

    def rot_slots(w, n):
        w = w.reshape(D, n, 2, hf)
        return jnp.pad(w, ((0, 0), (0, 0), (0, 0), (0, LANES // 2 - hf))).reshape(D, n * LANES)

    def val_slots(w, n):
        return jnp.pad(w.reshape(D, n, dh), ((0, 0), (0, 0), (0, LANES - dh))).reshape(D, n * LANES)

    w = jnp.concatenate([rot_slots(w_in[:, :q_end], hq), rot_slots(w_in[:, q_end:k_end], hkv),
                         val_slots(w_in[:, k_end:], hkv)], axis=1)

    def gain_slot(g):
        return jnp.pad(g.reshape(2, hf), ((0, 0), (0, LANES // 2 - hf))).reshape(1, LANES)

    flag = jnp.concatenate([jnp.ones(((hq + hkv) * LANES,), F32), jnp.zeros((hkv * LANES,), F32)]).reshape(1, -1)
    wo = jnp.pad(w_out.reshape(hq, dh, -1), ((0, 0), (0, LANES - dh), (0, 0))).reshape(hq * LANES, -1)

    inv = ROPE_THETA ** (-np.arange(0, dh, 2) / dh)
    ang = np.arange(L)[:, None] * inv[None, :]
    zero = np.zeros((L, LANES // 2 - hf))
    cos = jnp.asarray(np.concatenate([np.cos(ang), zero, np.cos(ang), zero], axis=1), F32)
    sin = jnp.asarray(np.concatenate([-np.sin(ang), zero, np.sin(ang), zero], axis=1), F32)
    gq = gain_slot(q_gain) * (dh ** -0.5)
    gk = gain_slot(k_gain)
    roll = lambda g: jnp.roll(g, LANES // 2, axis=1)
    cos3 = jnp.stack([cos * gq, cos * gk, jnp.ones_like(cos)])
    sin3 = jnp.stack([sin * roll(gq), sin * roll(gk), jnp.zeros_like(sin)])
    return w, flag, wo, cos3, sin3


def _hg_decays(f_ref, q_ref, rows, lb, tri, total_row, mid_row):
    f = lb + (1.0 - lb) * jax.nn.sigmoid(f_ref[rows, :].astype(F32))
    gl = jnp.log(f)
    cum = _cumdot(tri, gl)
    total = cum[total_row:total_row + 1, :]
    mid = cum[mid_row:mid_row + 1, :]
    q_dec = _silu(q_ref[rows, :].astype(F32)) * jnp.exp(cum - mid)
    k_inv = (1.0 - f) * jnp.exp(mid - cum)
    q_full = (q_dec * jnp.exp(mid)).astype(BF16)
    k_end = (k_inv * jnp.exp(total - mid)).astype(BF16)
    return q_dec.astype(BF16), k_inv.astype(BF16), q_full, k_end, jnp.exp(total)


def _hg_fwd_kernel(lb_ref, q_ref, i_ref, f_ref, o_ref, st_ref, *, T, C, H, DK, DV):
    @pl.when(pl.program_id(1) == 0)
    def _():
        st_ref[...] = jnp.zeros_like(st_ref)

    row = lax.broadcasted_iota(jnp.int32, (C, C), 0)
    col = lax.broadcasted_iota(jnp.int32, (C, C), 1)
    causal = row >= col
    tri = jnp.concatenate([causal.astype(BF16)] * 2, axis=1)
    lb = lb_ref[...]

    def chunk(c, carry):
        rows = pl.ds(pl.multiple_of(c * C, C), C)
        q_dec, k_inv, q_full, k_end, s_decay = _hg_decays(f_ref, q_ref, rows, lb, tri, C - 1, C // 2 - 1)
        v = i_ref[rows, :]
        for h in range(H):
            ck = slice(h * DK, (h + 1) * DK)
            cv = slice(h * DV, (h + 1) * DV)
            s = lax.dot_general(q_dec[:, ck], k_inv[:, ck], NT_DIMS, preferred_element_type=F32)
            s = jnp.where(causal, s, 0.0)
            st = st_ref[h]
            o_ref[rows, cv] = _dot(s.astype(BF16), v[:, cv]) + lax.dot_general(
                q_full[:, ck], st.astype(BF16), NT_DIMS, preferred_element_type=F32)
            st_ref[h] = st * s_decay[:, ck] + lax.dot_general(v[:, cv], k_end[:, ck], TN_DIMS,
                                                              preferred_element_type=F32)
        return carry

    lax.fori_loop(0, T // C, chunk, 0)


def _hg_bwd_kernel(lb_ref, gain_ref, q_ref, i_ref, f_ref, gate_ref, o1_ref, w_ref, x_ref, xgate_ref,
                   o_ref, st_ref, a_ref, *, T, C, H, DK, DV):
    @pl.when(pl.program_id(1) == 0)
    def _():
        st_ref[...] = jnp.zeros_like(st_ref)

    row = lax.broadcasted_iota(jnp.int32, (C, C), 0)
    col = lax.broadcasted_iota(jnp.int32, (C, C), 1)
    anti = col >= row
    tri = jnp.concatenate([anti.astype(BF16)] * 2, axis=1)
    lb = lb_ref[...]
    gain = gain_ref[...]
    nc = T // C

    def chunk(cc, carry):
        rows = pl.ds(pl.multiple_of((nc - 1 - cc) * C, C), C)
        q_dec, k_inv, q_full, k_end, s_decay = _hg_decays(f_ref, q_ref, rows, lb, tri, 0, C // 2)
        v = i_ref[rows, :]
        for h in range(H):
            ck = slice(h * DK, (h + 1) * DK)
            cv = slice(h * DV, (h + 1) * DV)
            s = lax.dot_general(q_dec[:, ck], k_inv[:, ck], NT_DIMS, preferred_element_type=F32)
            s = jnp.where(anti, s, 0.0)
            st = st_ref[h]
            o = o1_ref[rows, cv] + _dot(s.astype(BF16), v[:, cv]) + lax.dot_general(
                q_full[:, ck], st.astype(BF16), NT_DIMS, preferred_element_type=F32)
            y = o * lax.rsqrt(jnp.mean(o * o, axis=-1, keepdims=True) + NORM_EPS) * gain
            a_ref[rows, cv] = (y * _silu(gate_ref[rows, cv].astype(F32))).astype(a_ref.dtype)
            st_ref[h] = st * s_decay[:, ck] + lax.dot_general(v[:, cv], k_end[:, ck], TN_DIMS,
                                                              preferred_element_type=F32)
        return carry

    lax.fori_loop(0, nc, chunk, 0)
    o_ref[...] = x_ref[...] + xgate_ref[...] * _dot(a_ref[...], w_ref[...])


def _hgrn(z, lb, gain, w_out, x, xgate):
    B, L, _ = z.shape
    D = w_out.shape[1]
    H, C = HG_HEADS, HG_CHUNK
    HD = lb.shape[1]
    DK = HD // H
    DV = gain.shape[0]
    T = min(L, 512)
    nT = L // T
    kw = dict(T=T, C=C, H=H, DK=DK, DV=DV)
    blocks = [((T, HD), BF16)] * 5 + [((T, H * DV), F32)] * 2
    state = [((H, DV, DK), F32)] + [((C, HD), F32)] * 12
    lb3 = lb.reshape(2, 1, HD)
    o1 = pl.pallas_call(
        functools.partial(_hg_fwd_kernel, **kw),
        grid=(B, nT),
        in_specs=[pl.BlockSpec((None, 1, HD), lambda b, i: (0, 0, 0)),
                  pl.BlockSpec((None, T, HD), lambda b, i: (b, i, 0)),
                  pl.BlockSpec((None, T, H * DV), lambda b, i: (b, i, 1)),
                  pl.BlockSpec((None, T, HD), lambda b, i: (b, i, 2))],
        out_specs=pl.BlockSpec((None, T, H * DV), lambda b, i: (b, i, 0)),
        out_shape=jax.ShapeDtypeStruct((B, L, H * DV), F32),
        scratch_shapes=[pltpu.VMEM((H, DV, DK), F32)],
        compiler_params=_cparams(("parallel", "arbitrary"), blocks, state),
        name="hgrn_fwd",
    )(lb3, z, z, z)
    return pl.pallas_call(
        functools.partial(_hg_bwd_kernel, **kw),
        grid=(B, nT),
        in_specs=[pl.BlockSpec((None, 1, HD), lambda b, i: (1, 0, 0)),
                  pl.BlockSpec((1, DV), lambda b, i: (0, 0)),
                  pl.BlockSpec((None, T, HD), lambda b, i: (b, nT - 1 - i, 0)),
                  pl.BlockSpec((None, T, H * DV), lambda b, i: (b, nT - 1 - i, 1)),
                  pl.BlockSpec((None, T, HD), lambda b, i: (b, nT - 1 - i, 3)),
                  pl.BlockSpec((None, T, H * DV), lambda b, i: (b, nT - 1 - i, 4)),
                  pl.BlockSpec((None, T, H * DV), lambda b, i: (b, nT - 1 - i, 0)),
                  pl.BlockSpec((H * DV, D), lambda b, i: (0, 0), pipeline_mode=pl.Buffered(1)),
                  pl.BlockSpec((None, T, D), lambda b, i: (b, nT - 1 - i, 0)),
                  pl.BlockSpec((None, 1, D), lambda b, i: (b, 0, 0))],
        out_specs=pl.BlockSpec((None, T, D), lambda b, i: (b, nT - 1 - i, 0)),
        out_shape=jax.ShapeDtypeStruct((B, L, D), F32),
        scratch_shapes=[pltpu.VMEM((H, DV, DK), F32), pltpu.VMEM((T, H * DV), BF16)],
        compiler_params=_cparams(("parallel", "arbitrary"), blocks + [((T, D), F32)],
                                 state + [((H * DV, D), BF16), ((T, H * DV), BF16), ((T, D), F32)]),
        name="hgrn_bwd",
    )(lb3, gain.reshape(1, DV), z, z, z, z, o1, w_out, x, xgate)


def _ret_rope_tables(L, dk):
    inv = ROPE_THETA ** (-np.arange(0, dk, 2) / dk)
    ang = np.arange(L)[:, None] * inv[None, :]
    cos = np.stack([np.cos(ang), np.cos(ang) * dk ** -0.5, np.ones_like(ang)])
    sin = np.stack([np.sin(ang), np.sin(ang) * dk ** -0.5, np.zeros_like(ang)])
    return jnp.asarray(cos, F32), jnp.asarray(sin, F32)


def kernel(x_prompt, x_sample, c_prompt, c_sample, ada_w, ada_b, norm_g, hy_w_in, hy_conv_w, hy_conv_b, hy_w1, hy_b1, hy_w2, hy_b2, hy_w3, hy_freq, hy_decay, hy_skip, hy_w_out, ret_w_in, ret_decay, ret_w_out, swa_w_in, swa_q_gain, swa_k_gain, swa_sink, swa_w_out, hg_w_in, hg_lb, hg_gain, hg_w_out, ffn_w_gate, ffn_w_val, ffn_conv_w, ffn_conv_b, ffn_w_down):
    depth, D = norm_g.shape[0], norm_g.shape[2]
    groups = [(x_prompt, c_prompt), (x_sample, c_sample)]
    mods = _ada_mod(jnp.concatenate([c for _, c in groups], axis=0), ada_w, ada_b)

    bf = lambda w: w.astype(BF16)
    hy_w_in_b, hy_w_out_b = bf(hy_w_in), bf(hy_w_out)
    ret_w_in_b, ret_w_out_b = bf(ret_w_in), bf(ret_w_out)
    hg_w_in_b, hg_w_out_b = bf(hg_w_in), bf(hg_w_out)
    wg_b, wv_b, wd_b = bf(ffn_w_gate), bf(ffn_w_val), bf(ffn_w_down)
    hg_sm = jax.nn.softmax(hg_lb.astype(F32), axis=1)
    hg_lower = jnp.cumsum(hg_sm, axis=1) - hg_sm
    ret_log_gamma = -jnp.exp(ret_decay.astype(F32))

    outs = []
    row0 = 0
    for x, c in groups:
        B, L, _ = x.shape
        for layer in range(depth):
            kind, j = layer % N_MIXERS, layer // N_MIXERS
            mod = mods[layer, row0:row0 + B].reshape(B, N_MOD, 1, D)
            sh1, sc1, g1, sh2, sc2, g2 = (mod[:, m] for m in range(N_MOD))
            gn1 = norm_g[layer, 0].reshape(1, D)
            gn2 = norm_g[layer, 1].reshape(1, D)
            if kind == 0:
                tabs = _fft_tables(L)
                z = _proj_in_conv(x, gn1, sc1, sh1, hy_w_in_b[j], hy_conv_w[j], hy_conv_b[j], tn=1024)
                taps, asum = _hy_filter_taps(L, hy_w1[j], hy_b1[j], hy_w2[j], hy_b2[j], hy_w3[j], hy_freq[j],
                                             hy_decay[j])
                spec = _hy_spectrum(taps, asum, tabs)
                a = _hy_conv(z, hy_skip[j], spec, tabs)
                x = _proj_out(a, hy_w_out_b[j], x, g1, mult=z)
            elif kind == 1:
                dk = D // RET_HEADS
                cos, sin = _ret_rope_tables(L, dk)
                z = _proj_in_halfrope(x, gn1, sc1, sh1, ret_w_in_b[j], cos, sin, tn=RET_HEADS * dk, rope_tiles=2,
                                      dk=dk)
                x = _retention(z, ret_log_gamma[j], ret_w_out_b[j], x, g1)
            elif kind == 2:
                w, flag, wo, cos, sin = _swa_layout(swa_w_in[j], swa_q_gain[j], swa_k_gain[j], swa_w_out[j], L)
                z = _proj_in_rope(x, gn1, sc1, sh1, bf(w), flag, cos, sin, tn=SWA_HKV * LANES, dh=SWA_DH,
                                  q_tiles=SWA_HQ // SWA_HKV)
                a = _swa_attention(z, swa_sink[j])
                x = _proj_out(a, bf(wo), x, g1)
            else:
                z = _proj_in(x, gn1, sc1, sh1, hg_w_in_b[j], tn=1024)
                x = _hgrn(z, hg_lower[:, layer], hg_gain[j], hg_w_out_b[j], x, g1)
            x = _ffn(x, gn2, sc2, sh2, g2, wg_b[layer], wv_b[layer], ffn_conv_w[layer], ffn_conv_b[layer],
                     wd_b[layer])
        outs.append(x)
        row0 += B
    return tuple(outs)
```

```python
import functools
import math

import numpy as np
import jax
import jax.numpy as jnp
from jax import lax
from jax.experimental import pallas as pl
from jax.experimental.pallas import tpu as pltpu

F32 = jnp.float32
BF16 = jnp.bfloat16

NORM_EPS = 1e-6
N_MIXERS = 4
N_MOD = 6
HY_BANDS = 16
RET_HEADS = 4
RET_CHUNK = 256
SWA_HQ = 16
SWA_HKV = 4
SWA_DH = 64
WINDOW = 128
ATTN_BLOCK = 128
ROPE_THETA = 10000.0
NEG_INF = -1e30
HG_HEADS = 8
HG_CHUNK = 128

LANES = 128
SUBLANES = 8
BF16_ROWS = 16
VMEM_LIMIT_CAP = 60 * 1024 * 1024
VMEM_SLACK = 8 * 1024 * 1024

NT_DIMS = (((1,), (1,)), ((), ()))
TN_DIMS = (((0,), (0,)), ((), ()))


def _nbytes(shape, dtype):
    item = jnp.dtype(dtype).itemsize
    sub = SUBLANES * 4 // item
    dims = list(shape)
    dims[-1] = -(-dims[-1] // LANES) * LANES
    if len(dims) > 1:
        dims[-2] = -(-dims[-2] // sub) * sub
    return int(np.prod(dims)) * item


def _cparams(semantics, pipelined, resident):
    need = 2 * sum(_nbytes(s, d) for s, d in pipelined) + sum(_nbytes(s, d) for s, d in resident)
    return pltpu.CompilerParams(dimension_semantics=semantics,
                                vmem_limit_bytes=min(need + VMEM_SLACK, VMEM_LIMIT_CAP))


def _dot(a, b):
    return jnp.dot(a, b, preferred_element_type=F32)


def _split(x):
    hi = x.astype(BF16)
    lo = (x - hi.astype(F32)).astype(BF16)
    return hi, lo


def _dot_tab(t_cat, d):
    d_hi, d_lo = _split(d)
    return _dot(t_cat, jnp.concatenate([d_hi, d_lo], axis=0))


def _dot3(a, b):
    a_hi, a_lo = _split(a)
    b_hi, b_lo = _split(b)
    return _dot(a_hi, b_hi) + _dot(a_lo, b_hi) + _dot(a_hi, b_lo)


def _cumdot(tri2, g):
    g_hi, g_lo = _split(g)
    return _dot(tri2, jnp.concatenate([g_hi, g_lo], axis=0))


def _silu(x):
    return x * jax.nn.sigmoid(x)


def _norm_mod(x, g, sc, sh):
    xf = x.astype(F32)
    y = xf * lax.rsqrt(jnp.mean(xf * xf, axis=-1, keepdims=True) + NORM_EPS)
    return (y * g) * (1.0 + sc) + sh


def _row_tile(L):
    return min(L, 1024)


def _ada_kernel(c_ref, w_ref, b_ref, o_ref):
    cs = _silu(c_ref[...]).astype(BF16)
    o_ref[...] = _dot(cs, w_ref[...].astype(BF16)) + b_ref[...]


def _ada_mod(c_all, ada_w, ada_b):
    depth, D, N = ada_w.shape
    R = c_all.shape[0]
    tn = 1024
    return pl.pallas_call(
        _ada_kernel,
        grid=(depth, N // tn),
        in_specs=[pl.BlockSpec((R, D), lambda l, j: (0, 0)),
                  pl.BlockSpec((None, D, tn), lambda l, j: (l, 0, j)),
                  pl.BlockSpec((None, 1, tn), lambda l, j: (l, 0, j))],
        out_specs=pl.BlockSpec((None, R, tn), lambda l, j: (l, 0, j)),
        out_shape=jax.ShapeDtypeStruct((depth, R, N), F32),
        compiler_params=_cparams(("parallel", "parallel"),
                                 [((D, tn), F32), ((R, D), F32), ((R, tn), F32)], []),
        name="ada_mod",
    )(c_all, ada_w, ada_b.reshape(depth, 1, N))


def _fill_h(h_ref, x_ref, g_ref, sc_ref, sh_ref):
    h_ref[...] = _norm_mod(x_ref[...], g_ref[...], sc_ref[...], sh_ref[...]).astype(BF16)


def _fill_h_halo(h_ref, x_ref, xp_ref, xn_ref, g_ref, sc_ref, sh_ref, tm):
    i = pl.program_id(1)
    last = pl.num_programs(1) - 1
    g, sc, sh = g_ref[...], sc_ref[...], sh_ref[...]
    h_ref[BF16_ROWS:BF16_ROWS + tm, :] = _norm_mod(x_ref[...], g, sc, sh).astype(BF16)
    hp = _norm_mod(xp_ref[...], g, sc, sh)
    hn = _norm_mod(xn_ref[...], g, sc, sh)
    h_ref[0:BF16_ROWS, :] = jnp.where(i > 0, hp, 0.0).astype(BF16)
    h_ref[BF16_ROWS + tm:, :] = jnp.where(i < last, hn, 0.0).astype(BF16)


def _conv3_rows(z_ref, cw, cb, tm):
    o = BF16_ROWS
    return (z_ref[o - 1:o - 1 + tm, :] * cw[0:1, :] + z_ref[o:o + tm, :] * cw[1:2, :]
            + z_ref[o + 1:o + 1 + tm, :] * cw[2:3, :] + cb)


def _proj_plain_kernel(x_ref, g_ref, sc_ref, sh_ref, w_ref, o_ref, h_ref):
    @pl.when(pl.program_id(2) == 0)
    def _():
        _fill_h(h_ref, x_ref, g_ref, sc_ref, sh_ref)

    o_ref[...] = _dot(h_ref[...], w_ref[...]).astype(o_ref.dtype)


def _proj_halfrope_kernel(x_ref, g_ref, sc_ref, sh_ref, w_ref, cos_ref, sin_ref, o_ref, h_ref, *, dk):
    @pl.when(pl.program_id(2) == 0)
    def _():
        _fill_h(h_ref, x_ref, g_ref, sc_ref, sh_ref)

    z = _dot(h_ref[...], w_ref[...])
    cos, sin = cos_ref[...], sin_ref[...]
    half = dk // 2
    for c0 in range(0, z.shape[1], dk):
        x1, x2 = z[:, c0:c0 + half], z[:, c0 + half:c0 + dk]
        o_ref[:, c0:c0 + half] = (x1 * cos - x2 * sin).astype(o_ref.dtype)
        o_ref[:, c0 + half:c0 + dk] = (x1 * sin + x2 * cos).astype(o_ref.dtype)


def _proj_conv_kernel(x_ref, xp_ref, xn_ref, g_ref, sc_ref, sh_ref, w_ref, cw_ref, cb_ref,
                      o_ref, h_ref, z_ref, *, tm):
    @pl.when(pl.program_id(2) == 0)
    def _():
        _fill_h_halo(h_ref, x_ref, xp_ref, xn_ref, g_ref, sc_ref, sh_ref, tm)

    z = _dot(h_ref[...], w_ref[...])
    for c in range(o_ref.shape[0]):
        cols = slice(c * LANES, (c + 1) * LANES)
        z_ref[c] = z[:, cols]
        o_ref[c] = _conv3_rows(z_ref.at[c], cw_ref[:, cols], cb_ref[:, cols], tm).astype(o_ref.dtype)


def _proj_rope_kernel(x_ref, g_ref, sc_ref, sh_ref, w_ref, flag_ref, ones_ref, cos_ref, sin_ref,
                      o_ref, h_ref, *, dh):
    @pl.when(pl.program_id(2) == 0)
    def _():
        _fill_h(h_ref, x_ref, g_ref, sc_ref, sh_ref)

    z = _dot(h_ref[...], w_ref[...])
    cos, sin = cos_ref[...], sin_ref[...]
    ones = ones_ref[...]
    for s in range(z.shape[1] // LANES):
        sl = slice(s * LANES, (s + 1) * LANES)
        zs = z[:, sl]
        ms = _dot((zs * zs).astype(BF16), ones) * (1.0 / dh)
        zs = zs * jnp.where(flag_ref[:, sl] > 0.0, lax.rsqrt(ms + NORM_EPS), 1.0)
        o_ref[:, sl] = (zs * cos + pltpu.roll(zs, LANES // 2, axis=1) * sin).astype(o_ref.dtype)


def _mod_specs(D):
    return [pl.BlockSpec((1, D), lambda b, i, j: (0, 0)),
            pl.BlockSpec((None, 1, D), lambda b, i, j: (b, 0, 0)),
            pl.BlockSpec((None, 1, D), lambda b, i, j: (b, 0, 0))]


def _halo_specs(tm, D, L):
    hb = tm // BF16_ROWS
    nhb = L // BF16_ROWS
    return [pl.BlockSpec((None, tm, D), lambda b, i, j: (b, i, 0)),
            pl.BlockSpec((None, BF16_ROWS, D), lambda b, i, j: (b, jnp.maximum(i * hb - 1, 0), 0)),
            pl.BlockSpec((None, BF16_ROWS, D), lambda b, i, j: (b, jnp.minimum((i + 1) * hb, nhb - 1), 0))]


def _proj_in(x, g, sc, sh, w, *, tn):
    B, L, D = x.shape
    N = w.shape[1]
    tm = _row_tile(L)
    return pl.pallas_call(
        _proj_plain_kernel,
        grid=(B, L // tm, N // tn),
        in_specs=[pl.BlockSpec((None, tm, D), lambda b, i, j: (b, i, 0))] + _mod_specs(D)
        + [pl.BlockSpec((D, tn), lambda b, i, j: (0, j))],
        out_specs=pl.BlockSpec((None, tm, tn), lambda b, i, j: (b, i, j)),
        out_shape=jax.ShapeDtypeStruct((B, L, N), BF16),
        scratch_shapes=[pltpu.VMEM((tm, D), BF16)],
        compiler_params=_cparams(("parallel", "parallel", "arbitrary"),
                                 [((tm, D), F32), ((D, tn), BF16), ((tm, tn), BF16)],
                                 [((tm, D), BF16), ((tm, tn), F32)]),
        name="proj_in",
    )(x, g, sc, sh, w)


def _proj_in_halfrope(x, g, sc, sh, w, cos, sin, *, tn, rope_tiles, dk):
    B, L, D = x.shape
    N = w.shape[1]
    tm = _row_tile(L)
    half = dk // 2
    table = lambda b, i, j: (jnp.minimum(j, rope_tiles), i, 0)
    return pl.pallas_call(
        functools.partial(_proj_halfrope_kernel, dk=dk),
        grid=(B, L // tm, N // tn),
        in_specs=[pl.BlockSpec((None, tm, D), lambda b, i, j: (b, i, 0))] + _mod_specs(D)
        + [pl.BlockSpec((D, tn), lambda b, i, j: (0, j)),
           pl.BlockSpec((None, tm, half), table),
           pl.BlockSpec((None, tm, half), table)],
        out_specs=pl.BlockSpec((None, tm, tn), lambda b, i, j: (b, i, j)),
        out_shape=jax.ShapeDtypeStruct((B, L, N), BF16),
        scratch_shapes=[pltpu.VMEM((tm, D), BF16)],
        compiler_params=_cparams(("parallel", "parallel", "arbitrary"),
                                 [((tm, D), F32), ((D, tn), BF16), ((tm, tn), BF16), ((tm, half), F32),
                                  ((tm, half), F32)],
                                 [((tm, D), BF16), ((tm, tn), F32), ((tm, tn), F32)]),
        name="proj_in_halfrope",
    )(x, g, sc, sh, w, cos, sin)


def _proj_in_conv(x, g, sc, sh, w, cw, cb, *, tn):
    B, L, D = x.shape
    N = w.shape[1]
    tm = _row_tile(L)
    te = tm + 2 * BF16_ROWS
    return pl.pallas_call(
        functools.partial(_proj_conv_kernel, tm=tm),
        grid=(B, L // tm, N // tn),
        in_specs=_halo_specs(tm, D, L) + _mod_specs(D)
        + [pl.BlockSpec((D, tn), lambda b, i, j: (0, j)),
           pl.BlockSpec((3, tn), lambda b, i, j: (0, j)),
           pl.BlockSpec((1, tn), lambda b, i, j: (0, j))],
        out_specs=pl.BlockSpec((None, tn // LANES, tm, LANES), lambda b, i, j: (b, j, i, 0)),
        out_shape=jax.ShapeDtypeStruct((B, N // LANES, L, LANES), BF16),
        scratch_shapes=[pltpu.VMEM((te, D), BF16), pltpu.VMEM((tn // LANES, te, LANES), F32)],
        compiler_params=_cparams(("parallel", "parallel", "arbitrary"),
                                 [((tm, D), F32), ((D, tn), BF16), ((tm, tn), BF16)],
                                 [((te, D), BF16), ((te, tn), F32), ((te, tn), F32)]),
        name="proj_in_conv",
    )(x, x, x, g, sc, sh, w, cw, cb.reshape(1, N))


def _proj_in_rope(x, g, sc, sh, w, flag, cos, sin, *, tn, dh, q_tiles):
    B, L, D = x.shape
    N = w.shape[1]
    tm = _row_tile(L)
    table = lambda b, i, j: (jnp.maximum(j - (q_tiles - 1), 0), i, 0)
    return pl.pallas_call(
        functools.partial(_proj_rope_kernel, dh=dh),
        grid=(B, L // tm, N // tn),
        in_specs=[pl.BlockSpec((None, tm, D), lambda b, i, j: (b, i, 0))] + _mod_specs(D)
        + [pl.BlockSpec((D, tn), lambda b, i, j: (0, j)),
           pl.BlockSpec((1, tn), lambda b, i, j: (0, j)),
           pl.BlockSpec((LANES, LANES), lambda b, i, j: (0, 0)),
           pl.BlockSpec((None, tm, LANES), table),
           pl.BlockSpec((None, tm, LANES), table)],
        out_specs=pl.BlockSpec((None, tm, tn), lambda b, i, j: (b, i, j)),
        out_shape=jax.ShapeDtypeStruct((B, L, N), BF16),
        scratch_shapes=[pltpu.VMEM((tm, D), BF16)],
        compiler_params=_cparams(("parallel", "parallel", "arbitrary"),
                                 [((tm, D), F32), ((D, tn), BF16), ((tm, tn), BF16), ((tm, LANES), F32),
                                  ((tm, LANES), F32), ((LANES, LANES), BF16)],
                                 [((tm, D), BF16), ((tm, tn), F32), ((tm, tn), F32), ((tm, tn), F32)]),
        name="proj_in_rope",
    )(x, g, sc, sh, w, flag, jnp.ones((LANES, LANES), BF16), cos, sin)


def _proj_out_kernel(a_ref, w_ref, x_ref, gate_ref, o_ref):
    o_ref[...] = x_ref[...] + gate_ref[...] * _dot(a_ref[...], w_ref[...])


def _proj_out_gated_kernel(a_ref, m_ref, w_ref, x_ref, gate_ref, o_ref):
    a = jnp.concatenate([(a_ref[c].astype(F32) * m_ref[c].astype(F32)).astype(BF16)
                         for c in range(a_ref.shape[0])], axis=1)
    o_ref[...] = x_ref[...] + gate_ref[...] * _dot(a, w_ref[...])


def _proj_out(a, w, x, gate, mult=None):
    blocked = a.ndim == 4
    B, L = x.shape[:2]
    K, D = w.shape
    tm = _row_tile(L)
    if blocked:
        a_specs = [pl.BlockSpec((None, K // LANES, tm, LANES), lambda b, i: (b, 0, i, 0))] * 2
        operands = (a, mult)
    else:
        a_specs = [pl.BlockSpec((None, tm, K), lambda b, i: (b, i, 0))]
        operands = (a,)
    return pl.pallas_call(
        _proj_out_gated_kernel if blocked else _proj_out_kernel,
        grid=(B, L // tm),
        in_specs=a_specs + [
                  pl.BlockSpec((K, D), lambda b, i: (0, 0)),
                  pl.BlockSpec((None, tm, D), lambda b, i: (b, i, 0)),
                  pl.BlockSpec((None, 1, D), lambda b, i: (b, 0, 0))],
        out_specs=pl.BlockSpec((None, tm, D), lambda b, i: (b, i, 0)),
        out_shape=jax.ShapeDtypeStruct((B, L, D), F32),
        compiler_params=_cparams(("parallel", "parallel"),
                                 [((tm, K), BF16)] * len(operands)
                                 + [((K, D), BF16), ((tm, D), F32), ((tm, D), F32)],
                                 [((tm, D), F32), ((tm, K), F32)]),
        name="proj_out",
    )(*operands, w, x, gate)


def _ffn_kernel(x_ref, xp_ref, xn_ref, g_ref, sc_ref, sh_ref, gate_ref, wg_ref, wv_ref, cw_ref, cb_ref,
                wd_ref, o_ref, h_ref, z_ref, *, tm, tf):
    _fill_h_halo(h_ref, x_ref, xp_ref, xn_ref, g_ref, sc_ref, sh_ref, tm)
    pieces = [slice(c * tf, (c + 1) * tf) for c in range(wg_ref.shape[1] // tf)]
    vals = []
    for c, cols in enumerate(pieces):
        z_ref[c] = _dot(h_ref[...], wg_ref[:, cols])
        vals.append(_dot(h_ref[BF16_ROWS:BF16_ROWS + tm, :], wv_ref[:, cols]))
    acc = None
    for c, cols in enumerate(pieces):
        a = _conv3_rows(z_ref.at[c], cw_ref[:, cols], cb_ref[:, cols], tm)
        part = _dot((_silu(a) * vals[c]).astype(BF16), wd_ref[cols, :])
        acc = part if acc is None else acc + part
    o_ref[...] = x_ref[...] + gate_ref[...] * acc


def _ffn(x, g, sc, sh, gate, wg, wv, cw, cb, wd):
    B, L, D = x.shape
    F = wg.shape[1]
    tm = min(L, 512)
    te = tm + 2 * BF16_ROWS
    tf = F // 2
    const = lambda b, i, j: (0, 0)
    resident = lambda shape: pl.BlockSpec(shape, const, pipeline_mode=pl.Buffered(1))
    return pl.pallas_call(
        functools.partial(_ffn_kernel, tm=tm, tf=tf),
        grid=(B, L // tm, 1),
        in_specs=_halo_specs(tm, D, L) + _mod_specs(D)
        + [pl.BlockSpec((None, 1, D), lambda b, i, j: (b, 0, 0)),
           resident((D, F)), resident((D, F)), resident((3, F)), resident((1, F)), resident((F, D))],
        out_specs=pl.BlockSpec((None, tm, D), lambda b, i, j: (b, i, 0)),
        out_shape=jax.ShapeDtypeStruct((B, L, D), F32),
        scratch_shapes=[pltpu.VMEM((te, D), BF16), pltpu.VMEM((F // tf, te, tf), F32)],
        compiler_params=_cparams(("parallel", "parallel", "arbitrary"),
                                 [((tm, D), F32), ((tm, D), F32)],
                                 [((D, F), BF16), ((D, F), BF16), ((F, D), BF16), ((te, D), BF16),
                                  ((te, F), F32), ((tm, F), F32), ((tm, tf), F32), ((tm, tf), BF16),
                                  ((tm, D), F32), ((tm, D), F32)]),
        name="ffn",
    )(x, x, x, g, sc, sh, gate, wg, wv, cw, cb.reshape(1, F), wd)


def _fft_dims(L):
    N = 2 * L
    p = N.bit_length() - 1
    n1 = 1 << ((p + 1) // 2)
    return N, n1, N // n1


def _k1_pad(n1):
    return n1 // 2 + 8


def _slab_rows(k1p):
    return 2 * k1p + SUBLANES


def _fft_tables(L):
    N, n1, n2 = _fft_dims(L)
    k1p = _k1_pad(n1)
    k1 = np.arange(k1p)
    valid = (k1 <= n1 // 2).astype(np.float64)
    weight = np.where((k1 == 0) | (k1 == n1 // 2), 1.0, 2.0) * valid
    m1 = np.arange(n1 // 2)
    n2v = np.arange(n2)
    n_idx = n2 * m1[None, :] + n2v[:, None]
    phi = 2.0 * np.pi * k1[None, :, None] * n_idx[:, None, :] / N
    s1 = np.concatenate([np.cos(phi) * valid[None, :, None], -np.sin(phi) * valid[None, :, None]], axis=1)
    phit = np.transpose(phi, (0, 2, 1))
    gl = np.concatenate([np.cos(phit) * weight[None, None, :], -np.sin(phit) * weight[None, None, :]],
                        axis=2) / N
    th = 2.0 * np.pi * np.outer(n2v, n2v) / n2
    c, s = np.cos(th), np.sin(th)
    f2 = np.block([[c, s], [-s, c]])
    f2c = np.block([[c, -s], [s, c]])

    def cat(a):
        t = jnp.asarray(a, F32).astype(BF16)
        return jnp.concatenate([t, t], axis=-1)

    return dict(N=N, n1=n1, n2=n2, k1p=k1p, s1=cat(s1), gl=cat(gl), f2=cat(f2), f2c=cat(f2c))


def _hy_positions(L):
    idx = np.concatenate([np.arange(L), np.array([0]), np.arange(L - 1, 0, -1)])
    t = idx / (L - 1)
    ang = (2.0 * np.pi / L) * idx
    bands = np.linspace(1e-4, HY_BANDS - 1, HY_BANDS)
    z = np.concatenate([t[:, None], np.cos(bands[None, :] * ang[:, None]), -np.sin(bands[None, :] * ang[:, None])],
                       axis=1)
    zp = np.zeros((2 * L, LANES))
    zp[:, : z.shape[1]] = z
    return jnp.asarray(zp, F32)


def _hy_mlp_kernel(z_ref, w1_ref, b1_ref, w2_ref, b2_ref, w3_ref, fr_ref, dec_ref, taps_ref, asum_ref, *, L, tr):
    i = pl.program_id(0)
    z = z_ref[...]
    fr = fr_ref[...]
    h = jnp.sin(fr * (_dot3(z, w1_ref[...]) + b1_ref[...]))
    h = jnp.sin(fr * (_dot3(h, w2_ref[...]) + b2_ref[...]))
    h = _dot3(h, w3_ref[...])
    t = z[:, 0:1]
    taps = h * jnp.exp(-t * jnp.abs(dec_ref[...]))
    row = i * tr + lax.broadcasted_iota(jnp.int32, (tr, 1), 0)
    taps = jnp.where(row == L, 0.0, taps)
    taps_ref[...] = taps

    @pl.when(i == 0)
    def _():
        asum_ref[...] = jnp.zeros_like(asum_ref)

    asum_ref[...] += jnp.sum(jnp.abs(taps), axis=0, keepdims=True)


def _hy_filter_taps(L, w1, b1, w2, b2, w3, freq, decay):
    W = decay.shape[1]
    E, O = w1.shape
    tr = 512
    half = L // tr
    z = _hy_positions(L)
    w1p = jnp.zeros((LANES, O), F32).at[:E].set(w1)
    return pl.pallas_call(
        functools.partial(_hy_mlp_kernel, L=L, tr=tr),
        grid=(2 * L // tr,),
        in_specs=[pl.BlockSpec((tr, LANES), lambda i: (i, 0)),
                  pl.BlockSpec((LANES, O), lambda i: (0, 0)),
                  pl.BlockSpec((1, O), lambda i: (0, 0)),
                  pl.BlockSpec((O, O), lambda i: (0, 0)),
                  pl.BlockSpec((1, O), lambda i: (0, 0)),
                  pl.BlockSpec((O, W), lambda i: (0, i // half)),
                  pl.BlockSpec((1, O), lambda i: (0, 0)),
                  pl.BlockSpec((None, 1, W), lambda i: (i // half, 0, 0))],
        out_specs=[pl.BlockSpec((tr, W), lambda i: (i, 0)),
                   pl.BlockSpec((1, W), lambda i: (0, 0))],
        out_shape=[jax.ShapeDtypeStruct((2 * L, W), F32), jax.ShapeDtypeStruct((1, W), F32)],
        compiler_params=_cparams(("arbitrary",), [((tr, W), F32), ((O, W), F32), ((tr, LANES), F32)],
                                 [((tr, W), F32), ((tr, W), F32)]),
        name="hyena_filter_mlp",
    )(z, w1p, b1.reshape(1, O), w2, b2.reshape(1, O), w3, freq.reshape(1, O), decay.reshape(2, 1, W))


FFT_UNROLL = 16


def _fft_stage1(src_ref, src_off, a_ref, s1_ref, n1, n2, k1p, sign=None, src_off2=None):
    slab = _slab_rows(k1p)

    def body(j, carry):
        d = src_ref[pl.ds(src_off + j, n1 // 2, stride=n2), :]
        r = _dot_tab(s1_ref[j], d)
        if src_off2 is not None:
            d2 = src_ref[pl.ds(src_off2 + j, n1 // 2, stride=n2), :]
            r = r + sign * _dot_tab(s1_ref[j], d2)
        a_ref[pl.ds(pl.multiple_of(j * slab, 8), 2 * k1p), :] = r
        return carry

    lax.fori_loop(0, n2, body, 0, unroll=FFT_UNROLL)


def _load_k1(a_ref, k, n2, k1p):
    br = a_ref[pl.ds(k, n2, stride=_slab_rows(k1p)), :]
    bi = a_ref[pl.ds(k1p + k, n2, stride=_slab_rows(k1p)), :]
    return jnp.concatenate([br, bi], axis=0)


def _hy_spec_kernel(taps_ref, asum_ref, s1_ref, f2_ref, h_ref, a_ref, *, L, n1, n2, k1p):
    r = lax.broadcasted_iota(jnp.int32, (2 * k1p, 1), 0)
    k1 = jnp.where(r < k1p, r, r - k1p)
    sign = (1 - 2 * (k1 & 1)).astype(F32)
    _fft_stage1(taps_ref, 0, a_ref, s1_ref, n1, n2, k1p, sign=sign, src_off2=L)
    inv = 1.0 / asum_ref[...]

    def body(k, carry):
        h_ref[k] = _dot_tab(f2_ref[...], _load_k1(a_ref, k, n2, k1p)) * inv
        return carry

    lax.fori_loop(0, n1 // 2, body, 0, unroll=2)
    body(n1 // 2, 0)


def _hy_spectrum(taps, asum, tabs):
    n1, n2, k1p = tabs["n1"], tabs["n2"], tabs["k1p"]
    N, W = taps.shape
    L = N // 2
    cb = LANES
    k1v = n1 // 2 + 1
    s1, f2 = tabs["s1"], tabs["f2"]
    return pl.pallas_call(
        functools.partial(_hy_spec_kernel, L=L, n1=n1, n2=n2, k1p=k1p),
        grid=(W // cb,),
        in_specs=[pl.BlockSpec((N, cb), lambda c: (0, c)),
                  pl.BlockSpec((1, cb), lambda c: (0, c)),
                  pl.BlockSpec(s1.shape, lambda c: (0, 0, 0), pipeline_mode=pl.Buffered(1)),
                  pl.BlockSpec(f2.shape, lambda c: (0, 0), pipeline_mode=pl.Buffered(1))],
        out_specs=pl.BlockSpec((k1v, 2 * n2, cb), lambda c: (0, 0, c)),
        out_shape=jax.ShapeDtypeStruct((k1v, 2 * n2, W), F32),
        scratch_shapes=[pltpu.VMEM((n2 * _slab_rows(k1p), cb), F32)],
        compiler_params=_cparams(("parallel",),
                                 [((N, cb), F32), ((k1v, 2 * n2, cb), F32)],
                                 [((n2 * _slab_rows(k1p), cb), F32), (s1.shape, BF16), (f2.shape, BF16)]),
        name="hyena_filter_spectrum",
    )(taps, asum, s1, f2)


def _hy_conv_kernel(x1_ref, v_ref, skip_ref, h_ref, s1_ref, f2_ref, f2c_ref, gl_ref,
                    o_ref, u_ref, a_ref, *, n1, n2, k1p):
    u_ref[...] = x1_ref[...].astype(F32) * v_ref[...].astype(F32)
    _fft_stage1(u_ref, 0, a_ref, s1_ref, n1, n2, k1p)

    cb = u_ref.shape[1]

    def mid(groups):
        loaded = [jnp.concatenate([_load_k1(a_ref, k, n2, k1p) for k in ks], axis=1) for ks in groups]
        results = []
        for ks, rhs in zip(groups, loaded):
            x = _dot_tab(f2_ref[...], rhs)
            xr, xi = x[:n2], x[n2:]
            hr = jnp.concatenate([h_ref[k, :n2, :] for k in ks], axis=1)
            hi = jnp.concatenate([h_ref[k, n2:, :] for k in ks], axis=1)
            y = jnp.concatenate([xr * hr - xi * hi, xr * hi + xi * hr], axis=0)
            results.append(_dot_tab(f2c_ref[...], y))
        for ks, c in zip(groups, results):
            for p, k in enumerate(ks):
                a_ref[pl.ds(k, n2, stride=_slab_rows(k1p)), :] = c[:n2, p * cb:(p + 1) * cb]
                a_ref[pl.ds(k1p + k, n2, stride=_slab_rows(k1p)), :] = c[n2:, p * cb:(p + 1) * cb]

    def mid_oct(ko, carry):
        k = 8 * ko
        mid(((k, k + 1, k + 2, k + 3), (k + 4, k + 5, k + 6, k + 7)))
        return carry

    lax.fori_loop(0, n1 // 16, mid_oct, 0)
    mid(((n1 // 2,),))
    skip = skip_ref[...]

    def last(j, carry):
        rhs = a_ref[pl.ds(pl.multiple_of(j * _slab_rows(k1p), 8), 2 * k1p), :]
        y = _dot_tab(gl_ref[j], rhs)
        rows = pl.ds(j, n1 // 2, stride=n2)
        u_ref[rows, :] = y + u_ref[rows, :] * skip
        return carry

    lax.fori_loop(0, n2, last, 0, unroll=FFT_UNROLL)
    o_ref[...] = u_ref[...].astype(o_ref.dtype)


def _hy_conv(z, skip, spec, tabs):
    n1, n2, k1p = tabs["n1"], tabs["n2"], tabs["k1p"]
    B, nc3, L, cb = z.shape
    nc = nc3 // 3
    W = nc * cb
    k1v = n1 // 2 + 1
    tables = [tabs["s1"], tabs["f2"], tabs["f2c"], tabs["gl"]]

    def const_spec(t):
        return pl.BlockSpec(t.shape, (lambda c, b: (0, 0, 0)) if t.ndim == 3 else (lambda c, b: (0, 0)),
                            pipeline_mode=pl.Buffered(1))

    return pl.pallas_call(
        functools.partial(_hy_conv_kernel, n1=n1, n2=n2, k1p=k1p),
        grid=(nc, B),
        in_specs=[pl.BlockSpec((None, None, L, cb), lambda c, b: (b, nc + c, 0, 0)),
                  pl.BlockSpec((None, None, L, cb), lambda c, b: (b, 2 * nc + c, 0, 0)),
                  pl.BlockSpec((1, cb), lambda c, b: (0, c)),
                  pl.BlockSpec((k1v, 2 * n2, cb), lambda c, b: (0, 0, c), pipeline_mode=pl.Buffered(1))]
        + [const_spec(t) for t in tables],
        out_specs=pl.BlockSpec((None, None, L, cb), lambda c, b: (b, c, 0, 0)),
        out_shape=jax.ShapeDtypeStruct((B, nc, L, cb), BF16),
        scratch_shapes=[pltpu.VMEM((L, cb), F32), pltpu.VMEM((n2 * _slab_rows(k1p), cb), F32)],
        compiler_params=_cparams(("parallel", "arbitrary"),
                                 [((L, cb), BF16)] * 3,
                                 [((L, cb), F32), ((n2 * _slab_rows(k1p), cb), F32), ((k1v, 2 * n2, cb), F32)]
                                 + [(t.shape, BF16) for t in tables]),
        name="hyena_long_conv",
    )(z, z, skip.reshape(1, W), spec, *tables)


def _ret_fwd_kernel(lg_ref, q_ref, k_ref, v_ref, o_ref, s_ref, *, T, C, H, DK, DV):
    @pl.when(pl.program_id(1) == 0)
    def _():
        s_ref[...] = jnp.zeros_like(s_ref)

    row = lax.broadcasted_iota(jnp.int32, (C, C), 0)
    col = lax.broadcasted_iota(jnp.int32, (C, C), 1)
    rel = (row - col).astype(F32)
    ridx = lax.broadcasted_iota(jnp.int32, (C, 1), 0).astype(F32)
    consts = []
    for h in range(H):
        lgf = lg_ref[0, h]
        lgb = lg_ref[1, h]
        dmat = jnp.where(rel > 0.0, jnp.exp(lgf * jnp.maximum(rel, 0.0)),
                         jnp.where(rel < 0.0, jnp.exp(lgb * jnp.maximum(-rel, 0.0)), 2.0))
        consts.append((dmat, jnp.exp(lgf * (ridx + 1.0)), jnp.exp(lgf * (C - 1.0 - ridx)),
                       jnp.exp(jnp.full((1, 1), C, F32) * lgf)))

    def chunk(c, carry):
        rows = pl.ds(pl.multiple_of(c * C, C), C)
        for h in range(H):
            dmat, q_scale, k_scale, s_decay = consts[h]
            ck = slice(h * DK, (h + 1) * DK)
            cv = slice(h * DV, (h + 1) * DV)
            q = q_ref[rows, ck]
            k = k_ref[rows, ck]
            v = v_ref[rows, cv]
            s = lax.dot_general(q, k, NT_DIMS, preferred_element_type=F32) * dmat
            st = s_ref[h]
            o_ref[rows, cv] = (_dot(s.astype(BF16), v) + _dot((q.astype(F32) * q_scale).astype(BF16),
                                                              st.astype(BF16))).astype(o_ref.dtype)
            s_ref[h] = s_decay * st + lax.dot_general((k.astype(F32) * k_scale).astype(BF16), v, TN_DIMS,
                                                      preferred_element_type=F32)
        return carry

    lax.fori_loop(0, T // C, chunk, 0)


def _ret_bwd_kernel(lg_ref, q_ref, k_ref, v_ref, g_ref, o1_ref, w_ref, x_ref, gate_ref, o_ref, s_ref, a_ref,
                    *, T, C, H, DK, DV):
    @pl.when(pl.program_id(1) == 0)
    def _():
        s_ref[...] = jnp.zeros_like(s_ref)

    ridx = lax.broadcasted_iota(jnp.int32, (C, 1), 0).astype(F32)
    nc = T // C
    consts = []
    for h in range(H):
        lgb = lg_ref[1, h]
        consts.append((jnp.exp(lgb * (C - ridx)), jnp.exp(lgb * ridx), jnp.exp(jnp.full((1, 1), C, F32) * lgb)))

    for cc in range(nc):
        rows = slice((nc - 1 - cc) * C, (nc - cc) * C)
        for h in range(H):
            q_scale, k_scale, s_decay = consts[h]
            ck = slice(h * DK, (h + 1) * DK)
            cv = slice(h * DV, (h + 1) * DV)
            q = q_ref[rows, ck].astype(F32)
            k = k_ref[rows, ck].astype(F32)
            v = v_ref[rows, cv]
            st = s_ref[h]
            o = o1_ref[rows, cv].astype(F32) + _dot((q * q_scale).astype(BF16), st.astype(BF16))
            y = o * lax.rsqrt(jnp.mean(o * o, axis=-1, keepdims=True) + NORM_EPS)
            a_ref[rows, cv] = (y * _silu(g_ref[rows, cv].astype(F32))).astype(a_ref.dtype)
            s_ref[h] = s_decay * st + lax.dot_general((k * k_scale).astype(BF16), v, TN_DIMS,
                                                      preferred_element_type=F32)
        o_ref[rows, :] = x_ref[rows, :] + gate_ref[...] * _dot(a_ref[rows, :], w_ref[...])


def _retention(z, log_gamma, w_out, x, gate):
    B, L, W = z.shape
    D = w_out.shape[1]
    H, C = RET_HEADS, RET_CHUNK
    DK = W // (6 * H)
    DV = 2 * DK
    T = min(L, 512)
    nT = L // T
    kw = dict(T=T, C=C, H=H, DK=DK, DV=DV)
    smem = pl.BlockSpec(memory_space=pltpu.SMEM)
    blocks = [((T, H * DK), BF16)] * 2 + [((T, H * DV), BF16)] * 4
    state = [((H, DK, DV), F32), ((DK, DV), F32), ((DK, DV), F32)]
    o1 = pl.pallas_call(
        functools.partial(_ret_fwd_kernel, **kw),
        grid=(B, nT),
        in_specs=[smem,
                  pl.BlockSpec((None, T, H * DK), lambda b, i: (b, i, 0)),
                  pl.BlockSpec((None, T, H * DK), lambda b, i: (b, i, 1)),
                  pl.BlockSpec((None, T, H * DV), lambda b, i: (b, i, 1))],
        out_specs=pl.BlockSpec((None, T, H * DV), lambda b, i: (b, i, 0)),
        out_shape=jax.ShapeDtypeStruct((B, L, H * DV), BF16),
        scratch_shapes=[pltpu.VMEM((H, DK, DV), F32)],
        compiler_params=_cparams(("parallel", "arbitrary"), blocks, state),
        name="retention_fwd",
    )(log_gamma, z, z, z)
    return pl.pallas_call(
        functools.partial(_ret_bwd_kernel, **kw),
        grid=(B, nT),
        in_specs=[smem,
                  pl.BlockSpec((None, T, H * DK), lambda b, i: (b, nT - 1 - i, 0)),
                  pl.BlockSpec((None, T, H * DK), lambda b, i: (b, nT - 1 - i, 1)),
                  pl.BlockSpec((None, T, H * DV), lambda b, i: (b, nT - 1 - i, 1)),
                  pl.BlockSpec((None, T, H * DV), lambda b, i: (b, nT - 1 - i, 2)),
                  pl.BlockSpec((None, T, H * DV), lambda b, i: (b, nT - 1 - i, 0)),
                  pl.BlockSpec((H * DV, D), lambda b, i: (0, 0), pipeline_mode=pl.Buffered(1)),
                  pl.BlockSpec((None, T, D), lambda b, i: (b, nT - 1 - i, 0)),
                  pl.BlockSpec((None, 1, D), lambda b, i: (b, 0, 0))],
        out_specs=pl.BlockSpec((None, T, D), lambda b, i: (b, nT - 1 - i, 0)),
        out_shape=jax.ShapeDtypeStruct((B, L, D), F32),
        scratch_shapes=[pltpu.VMEM((H, DK, DV), F32), pltpu.VMEM((T, H * DV), BF16)],
        compiler_params=_cparams(("parallel", "arbitrary"), blocks[:-1] + [((T, D), F32)] * 2,
                                 state + [((H * DV, D), BF16), ((T, H * DV), BF16), ((T, D), F32)]),
        name="retention_bwd",
    )(log_gamma, z, z, z, z, o1, w_out, x, gate)


def _swa_kernel(sink_ref, q_ref, kp_ref, kc_ref, kn_ref, vp_ref, vc_ref, vn_ref, o_ref, *, HKV, G, BLK):
    i = pl.program_id(1)
    last = pl.num_programs(1) - 1
    r = lax.broadcasted_iota(jnp.int32, (BLK, 3 * BLK), 0)
    c = lax.broadcasted_iota(jnp.int32, (BLK, 3 * BLK), 1)
    rel = r - (c - BLK)
    lo = jnp.where(i == 0, BLK, 0)
    hi = jnp.where(i == last, 2 * BLK, 3 * BLK)
    valid = (jnp.abs(rel) <= WINDOW) & (c >= lo) & (c < hi)
    bias = jnp.where(valid, 0.0, NEG_INF)
    for j in range(HKV):
        sl = slice(j * LANES, (j + 1) * LANES)
        k = jnp.concatenate([kp_ref[:, sl], kc_ref[:, sl], kn_ref[:, sl]], axis=0)
        v = jnp.concatenate([vp_ref[:, sl], vc_ref[:, sl], vn_ref[:, sl]], axis=0)
        q = jnp.concatenate([q_ref[:, (j * G + g) * LANES:(j * G + g + 1) * LANES] for g in range(G)], axis=0)
        s = lax.dot_general(q, k, NT_DIMS, preferred_element_type=F32)
        ps, inv = [], []
        for g in range(G):
            sg = s[g * BLK:(g + 1) * BLK] + bias
            sink = sink_ref[0, j * G + g]
            m = jnp.maximum(jnp.max(sg, axis=-1, keepdims=True), sink)
            p = jnp.exp(sg - m)
            inv.append(1.0 / (jnp.sum(p, axis=-1, keepdims=True) + jnp.exp(sink - m)))
            ps.append(p.astype(BF16))
        o = _dot(jnp.concatenate(ps, axis=0), v)
        for g in range(G):
            o_ref[:, (j * G + g) * LANES:(j * G + g + 1) * LANES] = (o[g * BLK:(g + 1) * BLK] * inv[g]).astype(
                o_ref.dtype)


def _swa_attention(z, sink):
    B, L, _ = z.shape
    BLK = ATTN_BLOCK
    nb = L // BLK
    G = SWA_HQ // SWA_HKV
    qw = SWA_HQ * LANES
    kvw = SWA_HKV * LANES
    kcol = qw // kvw
    prev = lambda b, i: (b, jnp.maximum(i - 1, 0), kcol)
    cur = lambda b, i: (b, i, kcol)
    nxt = lambda b, i: (b, jnp.minimum(i + 1, nb - 1), kcol)
    vprev = lambda b, i: (b, jnp.maximum(i - 1, 0), kcol + 1)
    vcur = lambda b, i: (b, i, kcol + 1)
    vnxt = lambda b, i: (b, jnp.minimum(i + 1, nb - 1), kcol + 1)
    kv = lambda f: pl.BlockSpec((None, BLK, kvw), f)
    return pl.pallas_call(
        functools.partial(_swa_kernel, HKV=SWA_HKV, G=G, BLK=BLK),
        grid=(B, nb),
        in_specs=[pl.BlockSpec(memory_space=pltpu.SMEM),
                  pl.BlockSpec((None, BLK, qw), lambda b, i: (b, i, 0)),
                  kv(prev), kv(cur), kv(nxt), kv(vprev), kv(vcur), kv(vnxt)],
        out_specs=pl.BlockSpec((None, BLK, qw), lambda b, i: (b, i, 0)),
        out_shape=jax.ShapeDtypeStruct((B, L, qw), BF16),
        compiler_params=_cparams(("parallel", "parallel"),
                                 [((BLK, qw), BF16)] * 2 + [((BLK, kvw), BF16)] * 6,
                                 [((G * BLK, 3 * BLK), F32)] * 4),
        name="swa_attention",
    )(sink.reshape(1, SWA_HQ), z, z, z, z, z, z, z)


def _swa_layout(w_in, q_gain, k_gain, w_out, L):
    D = w_in.shape[0]
    dh, hq, hkv = SWA_DH, SWA_HQ, SWA_HKV
    hf = dh // 2
    q_end, k_end = hq * dh, (hq + hkv) * dh

    def rot_slots(w, n):
        w = w.reshape(D, n, 2, hf)
        return jnp.pad(w, ((0, 0), (0, 0), (0, 0), (0, LANES // 2 - hf))).reshape(D, n * LANES)

    def val_slots(w, n):
        return jnp.pad(w.reshape(D, n, dh), ((0, 0), (0, 0), (0, LANES - dh))).reshape(D, n * LANES)

    w = jnp.concatenate([rot_slots(w_in[:, :q_end], hq), rot_slots(w_in[:, q_end:k_end], hkv),
                         val_slots(w_in[:, k_end:], hkv)], axis=1)

    def gain_slot(g):
        return jnp.pad(g.reshape(2, hf), ((0, 0), (0, LANES // 2 - hf))).reshape(1, LANES)

    flag = jnp.concatenate([jnp.ones(((hq + hkv) * LANES,), F32), jnp.zeros((hkv * LANES,), F32)]).reshape(1, -1)
    wo = jnp.pad(w_out.reshape(hq, dh, -1), ((0, 0), (0, LANES - dh), (0, 0))).reshape(hq * LANES, -1)

    inv = ROPE_THETA ** (-np.arange(0, dh, 2) / dh)
    ang = np.arange(L)[:, None] * inv[None, :]
    zero = np.zeros((L, LANES // 2 - hf))
    cos = jnp.asarray(np.concatenate([np.cos(ang), zero, np.cos(ang), zero], axis=1), F32)
    sin = jnp.asarray(np.concatenate([-np.sin(ang), zero, np.sin(ang), zero], axis=1), F32)
    gq = gain_slot(q_gain) * (dh ** -0.5)
    gk = gain_slot(k_gain)
    roll = lambda g: jnp.roll(g, LANES // 2, axis=1)
    cos3 = jnp.stack([cos * gq, cos * gk, jnp.ones_like(cos)])
    sin3 = jnp.stack([sin * roll(gq), sin * roll(gk), jnp.zeros_like(sin)])
    return w, flag, wo, cos3, sin3


def _hg_decays(f_ref, q_ref, rows, lb, tri, total_row, mid_row):
    f = lb + (1.0 - lb) * jax.nn.sigmoid(f_ref[rows, :].astype(F32))
    gl = jnp.log(f)
    cum = _cumdot(tri, gl)
    total = cum[total_row:total_row + 1, :]
    mid = cum[mid_row:mid_row + 1, :]
    q_dec = _silu(q_ref[rows, :].astype(F32)) * jnp.exp(cum - mid)
    k_inv = (1.0 - f) * jnp.exp(mid - cum)
    q_full = (q_dec * jnp.exp(mid)).astype(BF16)
    k_end = (k_inv * jnp.exp(total - mid)).astype(BF16)
    return q_dec.astype(BF16), k_inv.astype(BF16), q_full, k_end, jnp.exp(total)


def _hg_fwd_kernel(lb_ref, q_ref, i_ref, f_ref, o_ref, st_ref, *, T, C, H, DK, DV):
    @pl.when(pl.program_id(1) == 0)
    def _():
        st_ref[...] = jnp.zeros_like(st_ref)

    row = lax.broadcasted_iota(jnp.int32, (C, C), 0)
    col = lax.broadcasted_iota(jnp.int32, (C, C), 1)
    causal = row >= col
    tri = jnp.concatenate([causal.astype(BF16)] * 2, axis=1)
    lb = lb_ref[...]

    def chunk(c, carry):
        rows = pl.ds(pl.multiple_of(c * C, C), C)
        q_dec, k_inv, q_full, k_end, s_decay = _hg_decays(f_ref, q_ref, rows, lb, tri, C - 1, C // 2 - 1)
        v = i_ref[rows, :]
        for h in range(H):
            ck = slice(h * DK, (h + 1) * DK)
            cv = slice(h * DV, (h + 1) * DV)
            s = lax.dot_general(q_dec[:, ck], k_inv[:, ck], NT_DIMS, preferred_element_type=F32)
            s = jnp.where(causal, s, 0.0)
            st = st_ref[h]
            o_ref[rows, cv] = _dot(s.astype(BF16), v[:, cv]) + lax.dot_general(
                q_full[:, ck], st.astype(BF16), NT_DIMS, preferred_element_type=F32)
            st_ref[h] = st * s_decay[:, ck] + lax.dot_general(v[:, cv], k_end[:, ck], TN_DIMS,
                                                              preferred_element_type=F32)
        return carry

    lax.fori_loop(0, T // C, chunk, 0)


def _hg_bwd_kernel(lb_ref, gain_ref, q_ref, i_ref, f_ref, gate_ref, o1_ref, w_ref, x_ref, xgate_ref,
                   o_ref, st_ref, a_ref, *, T, C, H, DK, DV):
    @pl.when(pl.program_id(1) == 0)
    def _():
        st_ref[...] = jnp.zeros_like(st_ref)

    row = lax.broadcasted_iota(jnp.int32, (C, C), 0)
    col = lax.broadcasted_iota(jnp.int32, (C, C), 1)
    anti = col >= row
    tri = jnp.concatenate([anti.astype(BF16)] * 2, axis=1)
    lb = lb_ref[...]
    gain = gain_ref[...]
    nc = T // C

    group = 2 if nc % 2 == 0 else 1
    for cc in range(nc):
        rows = slice((nc - 1 - cc) * C, (nc - cc) * C)
        q_dec, k_inv, q_full, k_end, s_decay = _hg_decays(f_ref, q_ref, rows, lb, tri, 0, C // 2)
        v = i_ref[rows, :]
        for h in range(H):
            ck = slice(h * DK, (h + 1) * DK)
            cv = slice(h * DV, (h + 1) * DV)
            s = lax.dot_general(q_dec[:, ck], k_inv[:, ck], NT_DIMS, preferred_element_type=F32)
            s = jnp.where(anti, s, 0.0)
            st = st_ref[h]
            o = o1_ref[rows, cv] + _dot(s.astype(BF16), v[:, cv]) + lax.dot_general(
                q_full[:, ck], st.astype(BF16), NT_DIMS, preferred_element_type=F32)
            y = o * lax.rsqrt(jnp.mean(o * o, axis=-1, keepdims=True) + NORM_EPS) * gain
            a_ref[rows, cv] = (y * _silu(gate_ref[rows, cv].astype(F32))).astype(a_ref.dtype)
            st_ref[h] = st * s_decay[:, ck] + lax.dot_general(v[:, cv], k_end[:, ck], TN_DIMS,
                                                              preferred_element_type=F32)
        if (cc + 1) % group == 0:
            done = slice((nc - 1 - cc) * C, (nc - 1 - cc + group) * C)
            o_ref[done, :] = x_ref[done, :] + xgate_ref[...] * _dot(a_ref[done, :], w_ref[...])


def _hgrn(z, lb, gain, w_out, x, xgate):
    B, L, _ = z.shape
    D = w_out.shape[1]
    H, C = HG_HEADS, HG_CHUNK
    HD = lb.shape[1]
    DK = HD // H
    DV = gain.shape[0]
    T = min(L, 512)
    nT = L // T
    kw = dict(T=T, C=C, H=H, DK=DK, DV=DV)
    blocks = [((T, HD), BF16)] * 5 + [((T, H * DV), F32)] * 2
    state = [((H, DV, DK), F32)] + [((C, HD), F32)] * 12
    lb3 = lb.reshape(2, 1, HD)
    o1 = pl.pallas_call(
        functools.partial(_hg_fwd_kernel, **kw),
        grid=(B, nT),
        in_specs=[pl.BlockSpec((None, 1, HD), lambda b, i: (0, 0, 0)),
                  pl.BlockSpec((None, T, HD), lambda b, i: (b, i, 0)),
                  pl.BlockSpec((None, T, H * DV), lambda b, i: (b, i, 1)),
                  pl.BlockSpec((None, T, HD), lambda b, i: (b, i, 2))],
        out_specs=pl.BlockSpec((None, T, H * DV), lambda b, i: (b, i, 0)),
        out_shape=jax.ShapeDtypeStruct((B, L, H * DV), F32),
        scratch_shapes=[pltpu.VMEM((H, DV, DK), F32)],
        compiler_params=_cparams(("parallel", "arbitrary"), blocks, state),
        name="hgrn_fwd",
    )(lb3, z, z, z)
    return pl.pallas_call(
        functools.partial(_hg_bwd_kernel, **kw),
        grid=(B, nT),
        in_specs=[pl.BlockSpec((None, 1, HD), lambda b, i: (1, 0, 0)),
                  pl.BlockSpec((1, DV), lambda b, i: (0, 0)),
                  pl.BlockSpec((None, T, HD), lambda b, i: (b, nT - 1 - i, 0)),
                  pl.BlockSpec((None, T, H * DV), lambda b, i: (b, nT - 1 - i, 1)),
                  pl.BlockSpec((None, T, HD), lambda b, i: (b, nT - 1 - i, 3)),
                  pl.BlockSpec((None, T, H * DV), lambda b, i: (b, nT - 1 - i, 4)),
                  pl.BlockSpec((None, T, H * DV), lambda b, i: (b, nT - 1 - i, 0)),
                  pl.BlockSpec((H * DV, D), lambda b, i: (0, 0), pipeline_mode=pl.Buffered(1)),
                  pl.BlockSpec((None, T, D), lambda b, i: (b, nT - 1 - i, 0)),
                  pl.BlockSpec((None, 1, D), lambda b, i: (b, 0, 0))],
        out_specs=pl.BlockSpec((None, T, D), lambda b, i: (b, nT - 1 - i, 0)),
        out_shape=jax.ShapeDtypeStruct((B, L, D), F32),
        scratch_shapes=[pltpu.VMEM((H, DV, DK), F32), pltpu.VMEM((T, H * DV), BF16)],
        compiler_params=_cparams(("parallel", "arbitrary"), blocks + [((T, D), F32)],
                                 state + [((H * DV, D), BF16), ((T, H * DV), BF16), ((T, D), F32)]),
        name="hgrn_bwd",
    )(lb3, gain.reshape(1, DV), z, z, z, z, o1, w_out, x, xgate)


def _ret_rope_tables(L, dk):
    inv = ROPE_THETA ** (-np.arange(0, dk, 2) / dk)
    ang = np.arange(L)[:, None] * inv[None, :]
    cos = np.stack([np.cos(ang), np.cos(ang) * dk ** -0.5, np.ones_like(ang)])
    sin = np.stack([np.sin(ang), np.sin(ang) * dk ** -0.5, np.zeros_like(ang)])
    return jnp.asarray(cos, F32), jnp.asarray(sin, F32)


def kernel(x_prompt, x_sample, c_prompt, c_sample, ada_w, ada_b, norm_g, hy_w_in, hy_conv_w, hy_conv_b, hy_w1, hy_b1, hy_w2, hy_b2, hy_w3, hy_freq, hy_decay, hy_skip, hy_w_out, ret_w_in, ret_decay, ret_w_out, swa_w_in, swa_q_gain, swa_k_gain, swa_sink, swa_w_out, hg_w_in, hg_lb, hg_gain, hg_w_out, ffn_w_gate, ffn_w_val, ffn_conv_w, ffn_conv_b, ffn_w_down):
    depth, D = norm_g.shape[0], norm_g.shape[2]
    groups = [(x_prompt, c_prompt), (x_sample, c_sample)]
    mods = _ada_mod(jnp.concatenate([c for _, c in groups], axis=0), ada_w, ada_b)

    bf = lambda w: w.astype(BF16)
    hy_w_in_b, hy_w_out_b = bf(hy_w_in), bf(hy_w_out)
    ret_w_in_b, ret_w_out_b = bf(ret_w_in), bf(ret_w_out)
    hg_w_in_b, hg_w_out_b = bf(hg_w_in), bf(hg_w_out)
    wg_b, wv_b, wd_b = bf(ffn_w_gate), bf(ffn_w_val), bf(ffn_w_down)
    hg_sm = jax.nn.softmax(hg_lb.astype(F32), axis=1)
    hg_lower = jnp.cumsum(hg_sm, axis=1) - hg_sm
    ret_log_gamma = -jnp.exp(ret_decay.astype(F32))

    outs = []
    row0 = 0
    for x, c in groups:
        B, L, _ = x.shape
        for layer in range(depth):
            kind, j = layer % N_MIXERS, layer // N_MIXERS
            mod = mods[layer, row0:row0 + B].reshape(B, N_MOD, 1, D)
            sh1, sc1, g1, sh2, sc2, g2 = (mod[:, m] for m in range(N_MOD))
            gn1 = norm_g[layer, 0].reshape(1, D)
            gn2 = norm_g[layer, 1].reshape(1, D)
            if kind == 0:
                tabs = _fft_tables(L)
                z = _proj_in_conv(x, gn1, sc1, sh1, hy_w_in_b[j], hy_conv_w[j], hy_conv_b[j], tn=1024)
                taps, asum = _hy_filter_taps(L, hy_w1[j], hy_b1[j], hy_w2[j], hy_b2[j], hy_w3[j], hy_freq[j],
                                             hy_decay[j])
                spec = _hy_spectrum(taps, asum, tabs)
                a = _hy_conv(z, hy_skip[j], spec, tabs)
                x = _proj_out(a, hy_w_out_b[j], x, g1, mult=z)
            elif kind == 1:
                dk = D // RET_HEADS
                cos, sin = _ret_rope_tables(L, dk)
                z = _proj_in_halfrope(x, gn1, sc1, sh1, ret_w_in_b[j], cos, sin, tn=RET_HEADS * dk, rope_tiles=2,
                                      dk=dk)
                x = _retention(z, ret_log_gamma[j], ret_w_out_b[j], x, g1)
            elif kind == 2:
                w, flag, wo, cos, sin = _swa_layout(swa_w_in[j], swa_q_gain[j], swa_k_gain[j], swa_w_out[j], L)
                z = _proj_in_rope(x, gn1, sc1, sh1, bf(w), flag, cos, sin, tn=SWA_HKV * LANES, dh=SWA_DH,
                                  q_tiles=SWA_HQ // SWA_HKV)
                a = _swa_attention(z, swa_sink[j])
                x = _proj_out(a, bf(wo), x, g1)
            else:
                z = _proj_in(x, gn1, sc1, sh1, hg_w_in_b[j], tn=1024)
                x = _hgrn(z, hg_lower[:, layer], hg_gain[j], hg_w_out_b[j], x, g1)
            x = _ffn(x, gn2, sc2, sh2, g2, wg_b[layer], wv_b[layer], ffn_conv_w[layer], ffn_conv_b[layer],
                     wd_b[layer])
        outs.append(x)
        row0 += B
    return tuple(outs)
```
